```python
import math, functools
import jax, jax.numpy as jnp
from jax import lax
import numpy as np

D_MODEL = 1024
BATCH = 2
SEQ = 8192
DEPTH = 2

GRID_W = 64
CTX_LEN = 256
CHUNK = 128
Q_BLOCK = 128
RET_HEADS = 4
RET_DK = 32
RET_DV = 64
GLA_HEADS = 4
GLA_DK = 32
GLA_DV = 64
GLA_LOWRANK = 16
GLA_TAU = 16.0
DIFF_HEADS = 4
DIFF_DH = 64
ROPE_BASE = 10000.0
D_MIX = RET_HEADS * RET_DV + GLA_HEADS * GLA_DV + 2 * DIFF_HEADS * DIFF_DH
IN_WIDTHS = (RET_HEADS * RET_DK, RET_HEADS * RET_DK, RET_HEADS * RET_DV, RET_HEADS * RET_DV,
             GLA_HEADS * GLA_DK, GLA_HEADS * GLA_DK, GLA_HEADS * GLA_DV, GLA_HEADS * GLA_DV, GLA_LOWRANK,
             2 * DIFF_HEADS * DIFF_DH, 2 * DIFF_HEADS * DIFF_DH, 2 * DIFF_HEADS * DIFF_DH)
N_IN = 3088
N_EXPERTS = 64
TOP_K = 6
N_GROUPS = 8
TOPK_GROUPS = 4
D_EXPERT = 256
D_SHARED = 256
ROUTED_SCALE = 2.5
MOE_BLOCK = 128
EPS = 1e-6

kernel_name = 'hybrid_ret_gla_diffattn_moe_dit_block'


def rms_norm(x, g=None):
    xf = x.astype(jnp.float32)
    y = xf * lax.rsqrt(jnp.mean(xf * xf, axis=-1, keepdims=True) + EPS)
    if g is not None:
        y = y * g.astype(jnp.float32)
    return y.astype(x.dtype)


def modulate(x, shift, scale):
    return x * (1.0 + scale) + shift


def split_cols(p):
    offs, acc = [], 0
    for w in IN_WIDTHS[:-1]:
        acc += w
        offs.append(acc)
    return jnp.split(p, offs, axis=-1)


def split_heads(a, h):
    b, t, _ = a.shape
    return a.reshape(b, t, h, -1).transpose(0, 2, 1, 3)


def merge_heads(a):
    b, h, t, d = a.shape
    return a.transpose(0, 2, 1, 3).reshape(b, t, h * d)


def diff_heads(a):
    b, t, _ = a.shape
    return a.reshape(b, t, DIFF_HEADS, 2, DIFF_DH).transpose(0, 2, 3, 1, 4)


def rope_2d(x, row, col):
    d = x.shape[-1]
    half, quarter = d // 2, d // 4
    freqs = ROPE_BASE ** (-jnp.arange(quarter, dtype=jnp.float32) / quarter)

    def rot(xp, pos):
        ang = pos[:, None] * freqs[None, :]
        cos, sin = jnp.cos(ang).astype(x.dtype), jnp.sin(ang).astype(x.dtype)
        x1, x2 = xp[..., :quarter], xp[..., quarter:]
        return jnp.concatenate([x1 * cos - x2 * sin, x1 * sin + x2 * cos], axis=-1)

    return jnp.concatenate([rot(x[..., :half], row), rot(x[..., half:], col)], axis=-1)


def to_chunks(a):
    b, h, t, d = a.shape
    return jnp.moveaxis(a.reshape(b, h, t // CHUNK, CHUNK, d), 2, 0)


def from_chunks(a):
    n, b, h, c, d = a.shape
    return jnp.moveaxis(a, 0, 2).reshape(b, h, n * c, d)


def retention_scan(q, k, v, s0, log_g):
    f32 = jnp.float32
    q, k, v = q.astype(f32), k.astype(f32), v.astype(f32)
    idx = jnp.arange(CHUNK, dtype=f32)
    diff = idx[:, None] - idx[None, :]
    lower = diff >= 0
    dmat = jnp.where(lower, jnp.exp(jnp.where(lower, diff, 0.0)[None] * log_g[:, None, None]), 0.0)
    xi = jnp.exp((idx + 1.0)[None, :] * log_g[:, None])
    zeta = jnp.exp((CHUNK - 1.0 - idx)[None, :] * log_g[:, None])
    g_chunk = jnp.exp(CHUNK * log_g)

    def step(s, inp):
        qc, kc, vc = inp
        att = jnp.einsum('bhid,bhjd->bhij', qc, kc) * dmat
        o = jnp.einsum('bhij,bhje->bhie', att, vc) + jnp.einsum('bhid,bhde->bhie', qc, s) * xi[:, :, None]
        s = s * g_chunk[:, None, None] + jnp.einsum('bhjd,bhje->bhde', kc * zeta[:, :, None], vc)
        return s, o

    s, o = lax.scan(step, s0, (to_chunks(q), to_chunks(k), to_chunks(v)))
    return from_chunks(o), s


def gla_scan(q, k, v, log_a, s0):
    f32 = jnp.float32
    q, k, v, log_a = q.astype(f32), k.astype(f32), v.astype(f32), log_a.astype(f32)
    idx = jnp.arange(CHUNK)
    lower = (idx[:, None] >= idx[None, :])[:, :, None]

    def step(s, inp):
        qc, kc, vc, ac = inp
        b = jnp.cumsum(ac, axis=2)
        rel = b[:, :, :, None, :] - b[:, :, None, :, :]
        decay = jnp.where(lower, jnp.exp(jnp.where(lower, rel, 0.0)), 0.0)
        att = jnp.einsum('bhik,bhjk,bhijk->bhij', qc, kc, decay)
        o = jnp.einsum('bhij,bhje->bhie', att, vc) + jnp.einsum('bhik,bhke->bhie', qc * jnp.exp(b), s)
        b_last = b[:, :, -1:, :]
        s = s * jnp.swapaxes(jnp.exp(b_last), -1, -2) + jnp.einsum(
            'bhjk,bhje->bhke', kc * jnp.exp(b_last - b), vc)
        return s, o

    s, o = lax.scan(step, s0, (to_chunks(q), to_chunks(k), to_chunks(v), to_chunks(log_a)))
    return from_chunks(o), s


def bidirectional_scan(scan_f, scan_b, lat_f, lat_b, ctx_f, ctx_b, s0):
    flip = lambda arrs: tuple(a[:, :, ::-1] for a in arrs)
    o_cf, s_cf = scan_f(*ctx_f, s0)
    o_cb, s_cb = scan_b(*flip(ctx_b), s0)
    o_lf, _ = scan_f(*lat_f, s_cf)
    o_lb, _ = scan_b(*flip(lat_b), s_cb)
    return o_lf + o_lb[:, :, ::-1], o_cf + o_cb[:, :, ::-1]


def diff_attend(q, k, v, lam):
    s = jnp.einsum('bhmqd,bhmkd->bhmqk', q, k).astype(jnp.float32) * (DIFF_DH ** -0.5)
    a = jax.nn.softmax(s, axis=-1)
    w = a[:, :, 0] - lam * a[:, :, 1]
    return jnp.einsum('bhqk,bhke->bhqe', w.astype(v.dtype), v)


def token_mixers(h_lat, h_ctx, w_in, w_out, ret_logit, gla_w, gla_b, gla_g, diff_lam, diff_g,
                 row, col, lam_init, need_ctx):
    f32 = jnp.float32
    dt = h_lat.dtype
    bsz, seq, _ = h_lat.shape
    (rq_l, rk_l, rv_l, rg_l, gq_l, gk_l, gv_l, gr_l, glr_l, dq_l, dk_l, dv_l) = split_cols(h_lat @ w_in)
    (rq_c, rk_c, rv_c, rg_c, gq_c, gk_c, gv_c, gr_c, glr_c, dq_c, dk_c, dv_c) = split_cols(h_ctx @ w_in)
    rope = lambda a: rope_2d(a, row, col)

    log_g = jax.nn.log_sigmoid(ret_logit.astype(f32))
    r_lat = (rope(split_heads(rq_l, RET_HEADS)),
             rope(split_heads(rk_l, RET_HEADS)) * RET_DK ** -0.5,
             split_heads(rv_l, RET_HEADS))
    r_ctx = (split_heads(rq_c, RET_HEADS), split_heads(rk_c, RET_HEADS) * RET_DK ** -0.5,
             split_heads(rv_c, RET_HEADS))
    s0_r = jnp.zeros((bsz, RET_HEADS, RET_DK, RET_DV), f32)
    o_rl, o_rc = bidirectional_scan(functools.partial(retention_scan, log_g=log_g[0]),
                                    functools.partial(retention_scan, log_g=log_g[1]),
                                    r_lat, r_lat, r_ctx, r_ctx, s0_r)

    def log_alpha(lr, direction):
        z = (lr @ gla_w[direction] + gla_b[direction]).astype(f32)
        return split_heads(jax.nn.log_sigmoid(z) / GLA_TAU, GLA_HEADS)

    g_ql, g_kl, g_vl = (split_heads(gq_l, GLA_HEADS) * GLA_DK ** -0.5, split_heads(gk_l, GLA_HEADS),
                        split_heads(gv_l, GLA_HEADS))
    g_qc, g_kc, g_vc = (split_heads(gq_c, GLA_HEADS) * GLA_DK ** -0.5, split_heads(gk_c, GLA_HEADS),
                        split_heads(gv_c, GLA_HEADS))
    s0_g = jnp.zeros((bsz, GLA_HEADS, GLA_DK, GLA_DV), f32)
    o_gl, o_gc = bidirectional_scan(gla_scan, gla_scan,
                                    (g_ql, g_kl, g_vl, log_alpha(glr_l, 0)),
                                    (g_ql, g_kl, g_vl, log_alpha(glr_l, 1)),
                                    (g_qc, g_kc, g_vc, log_alpha(glr_c, 0)),
                                    (g_qc, g_kc, g_vc, log_alpha(glr_c, 1)), s0_g)

    lp = diff_lam.astype(f32)
    lam = jnp.exp(jnp.sum(lp[0] * lp[1])) - jnp.exp(jnp.sum(lp[2] * lp[3])) + lam_init
    dq_lh, dk_lh, dv_lh = rope(diff_heads(dq_l)), rope(diff_heads(dk_l)), split_heads(dv_l, DIFF_HEADS)
    dq_ch, dk_ch, dv_ch = diff_heads(dq_c), diff_heads(dk_c), split_heads(dv_c, DIFF_HEADS)
    k_all = jnp.concatenate([dk_ch, dk_lh], axis=3)
    v_all = jnp.concatenate([dv_ch, dv_lh], axis=2)
    n_qb = seq // Q_BLOCK
    q_blocks = jnp.moveaxis(dq_lh.reshape(bsz, DIFF_HEADS, 2, n_qb, Q_BLOCK, DIFF_DH), 3, 0)
    o_blocks = lax.map(lambda qb: diff_attend(qb, k_all, v_all, lam), q_blocks)
    o_dl = jnp.moveaxis(o_blocks, 0, 2).reshape(bsz, DIFF_HEADS, seq, 2 * DIFF_DH)

    def merge(o_r, g_r, o_g, g_g, o_d):
        ret = merge_heads(rms_norm(o_r)).astype(dt) * jax.nn.silu(g_r)
        gla = merge_heads(rms_norm(o_g, gla_g)).astype(dt) * jax.nn.silu(g_g)
        dif = merge_heads(rms_norm(o_d, diff_g) * (1.0 - lam_init)).astype(dt)
        return jnp.concatenate([ret, gla, dif], axis=-1) @ w_out

    m_lat = merge(o_rl, rg_l, o_gl, gr_l, o_dl)
    m_ctx = merge(o_rc, rg_c, o_gc, gr_c, diff_attend(dq_ch, dk_ch, dv_ch, lam)) if need_ctx else None
    return m_lat, m_ctx


def moe_ffn(t, w_router, router_bias, w_eg, w_eu, w_ed, w_sg, w_su, w_sd):
    n_tok, d = t.shape
    scores = jax.nn.sigmoid(t.astype(jnp.float32) @ w_router.astype(jnp.float32))
    sel = scores + router_bias.astype(jnp.float32)
    grp = sel.reshape(n_tok, N_GROUPS, N_EXPERTS // N_GROUPS)
    grp_score = lax.top_k(grp, 2)[0].sum(-1)
    _, g_idx = lax.top_k(grp_score, TOPK_GROUPS)
    g_mask = jax.nn.one_hot(g_idx, N_GROUPS, dtype=jnp.float32).sum(-2) > 0
    e_mask = jnp.repeat(g_mask, N_EXPERTS // N_GROUPS, axis=-1)
    _, e_idx = lax.top_k(jnp.where(e_mask, sel, -jnp.inf), TOP_K)
    w = jnp.take_along_axis(scores, e_idx, axis=-1)
    w = w / jnp.sum(w, axis=-1, keepdims=True) * ROUTED_SCALE

    n_pairs = n_tok * TOP_K
    flat_e = e_idx.reshape(-1)
    flat_tok = jnp.repeat(jnp.arange(n_tok, dtype=jnp.int32), TOP_K)
    order = jnp.argsort(flat_e, stable=True)
    s_e, s_tok, s_w = flat_e[order], flat_tok[order], w.reshape(-1)[order]
    counts = jnp.bincount(flat_e, length=N_EXPERTS)
    padded = (counts + MOE_BLOCK - 1) // MOE_BLOCK * MOE_BLOCK
    p_end = jnp.cumsum(padded)
    p_start = p_end - padded
    c_start = jnp.cumsum(counts) - counts
    dest = p_start[s_e] + jnp.arange(n_pairs, dtype=jnp.int32) - c_start[s_e]
    n_blocks = (n_pairs + N_EXPERTS * (MOE_BLOCK - 1) + MOE_BLOCK - 1) // MOE_BLOCK
    n_rows = n_blocks * MOE_BLOCK
    row_tok = jnp.full((n_rows,), n_tok, jnp.int32).at[dest].set(s_tok)
    row_w = jnp.zeros((n_rows,), jnp.float32).at[dest].set(s_w)
    block_e = jnp.minimum(jnp.searchsorted(p_end, jnp.arange(n_blocks) * MOE_BLOCK, side='right'),
                          N_EXPERTS - 1).astype(jnp.int32)
    t_pad = jnp.concatenate([t, jnp.zeros((1, d), t.dtype)], axis=0)

    def block_ffn(args):
        b, e = args
        rows = lax.dynamic_slice(row_tok, (b * MOE_BLOCK,), (MOE_BLOCK,))
        xb = t_pad[rows]
        hb = jax.nn.silu(xb @ w_eg[e]) * (xb @ w_eu[e])
        return hb @ w_ed[e]

    y = lax.map(block_ffn, (jnp.arange(n_blocks, dtype=jnp.int32), block_e)).reshape(n_rows, d)
    routed = jax.ops.segment_sum(y * row_w[:, None].astype(y.dtype), row_tok,
                                 num_segments=n_tok + 1)[:n_tok]
    shared = (jax.nn.silu(t @ w_sg) * (t @ w_su)) @ w_sd
    return shared + routed


def setup_inputs(seed: int = 0) -> dict:
    key = jax.random.key(seed)
    ks = jax.random.split(key, 32)
    f32 = jnp.float32
    L, D = DEPTH, D_MODEL
    nrm = lambda k, shape, s: jax.random.normal(k, shape, f32) * s
    gamma = 1.0 - 2.0 ** (-5.0 - jnp.arange(RET_HEADS, dtype=f32))
    ret_logit0 = jnp.log(gamma) - jnp.log1p(-gamma)
    return {
        'x': nrm(ks[0], (BATCH, SEQ, D), 1.0),
        'c': nrm(ks[1], (BATCH, D), 1.0),
        'ctx': nrm(ks[2], (BATCH, CTX_LEN, D), 1.0),
        'c_ctx': nrm(ks[3], (D,), 1.0),
        'w_mod': nrm(ks[4], (L, D, 6 * D), 0.5 * D ** -0.5),
        'b_mod': nrm(ks[5], (L, 6 * D), 0.01),
        'g_pre_mix': 1.0 + nrm(ks[6], (L, D), 0.05),
        'g_post_mix': 1.0 + nrm(ks[7], (L, D), 0.05),
        'g_pre_ffn': 1.0 + nrm(ks[8], (L, D), 0.05),
        'g_post_ffn': 1.0 + nrm(ks[9], (L, D), 0.05),
        'w_in': nrm(ks[10], (L, D, N_IN), D ** -0.5),
        'w_out': nrm(ks[11], (L, D_MIX, D), D_MIX ** -0.5),
        'ret_decay_logit': ret_logit0[None, None, :] + nrm(ks[12], (L, 2, RET_HEADS), 0.1),
        'gla_w_gate': nrm(ks[13], (L, 2, GLA_LOWRANK, GLA_HEADS * GLA_DK), GLA_LOWRANK ** -0.5),
        'gla_b_gate': nrm(ks[14], (L, 2, GLA_HEADS * GLA_DK), 0.1),
        'gla_norm_g': 1.0 + nrm(ks[15], (L, GLA_DV), 0.05),
        'diff_lambda': nrm(ks[16], (L, 4, DIFF_DH), 0.1),
        'diff_norm_g': 1.0 + nrm(ks[17], (L, 2 * DIFF_DH), 0.05),
        'w_router': nrm(ks[18], (L, D, N_EXPERTS), D ** -0.5),
        'router_bias': nrm(ks[19], (L, N_EXPERTS), 0.01),
        'w_exp_gate': nrm(ks[20], (L, N_EXPERTS, D, D_EXPERT), D ** -0.5),
        'w_exp_up': nrm(ks[21], (L, N_EXPERTS, D, D_EXPERT), D ** -0.5),
        'w_exp_down': nrm(ks[22], (L, N_EXPERTS, D_EXPERT, D), D_EXPERT ** -0.5),
        'w_sh_gate': nrm(ks[23], (L, D, D_SHARED), D ** -0.5),
        'w_sh_up': nrm(ks[24], (L, D, D_SHARED), D ** -0.5),
        'w_sh_down': nrm(ks[25], (L, D_SHARED, D), D_SHARED ** -0.5),
    }


def reference(x, c, ctx, c_ctx, w_mod, b_mod, g_pre_mix, g_post_mix, g_pre_ffn, g_post_ffn, w_in, w_out,
              ret_decay_logit, gla_w_gate, gla_b_gate, gla_norm_g, diff_lambda, diff_norm_g,
              w_router, router_bias, w_exp_gate, w_exp_up, w_exp_down, w_sh_gate, w_sh_up, w_sh_down):
    bsz, seq, d = x.shape
    rows = seq // GRID_W
    row = jnp.broadcast_to(jnp.arange(rows, dtype=jnp.float32)[:, None], (rows, GRID_W)).reshape(-1)
    col = jnp.broadcast_to(jnp.arange(GRID_W, dtype=jnp.float32)[None, :], (rows, GRID_W)).reshape(-1)
    cond_l = jax.nn.silu(c)
    cond_c = jax.nn.silu(c_ctx)
    h_ctx = ctx
    for layer in range(DEPTH):
        need_ctx = layer < DEPTH - 1
        lam_init = 0.8 - 0.6 * math.exp(-0.3 * layer)
        mod_l = (cond_l @ w_mod[layer] + b_mod[layer])[:, None, :]
        mod_c = cond_c @ w_mod[layer] + b_mod[layer]
        sh1_l, sc1_l, gt1_l, sh2_l, sc2_l, gt2_l = jnp.split(mod_l, 6, axis=-1)
        sh1_c, sc1_c, gt1_c, sh2_c, sc2_c, gt2_c = jnp.split(mod_c, 6, axis=-1)

        hl = modulate(rms_norm(x, g_pre_mix[layer]), sh1_l, sc1_l)
        hc = modulate(rms_norm(h_ctx, g_pre_mix[layer]), sh1_c, sc1_c)
        m_l, m_c = token_mixers(hl, hc, w_in[layer], w_out[layer], ret_decay_logit[layer],
                                gla_w_gate[layer], gla_b_gate[layer], gla_norm_g[layer],
                                diff_lambda[layer], diff_norm_g[layer], row, col, lam_init, need_ctx)
        x = x + gt1_l * rms_norm(m_l, g_post_mix[layer])
        hl2 = modulate(rms_norm(x, g_pre_ffn[layer]), sh2_l, sc2_l)
        if need_ctx:
            h_ctx = h_ctx + gt1_c * rms_norm(m_c, g_post_mix[layer])
            hc2 = modulate(rms_norm(h_ctx, g_pre_ffn[layer]), sh2_c, sc2_c)
            tokens = jnp.concatenate([hl2.reshape(-1, d), hc2.reshape(-1, d)], axis=0)
        else:
            tokens = hl2.reshape(-1, d)
        f = moe_ffn(tokens, w_router[layer], router_bias[layer], w_exp_gate[layer], w_exp_up[layer],
                    w_exp_down[layer], w_sh_gate[layer], w_sh_up[layer], w_sh_down[layer])
        x = x + gt2_l * rms_norm(f[:bsz * seq].reshape(bsz, seq, d), g_post_ffn[layer])
        if need_ctx:
            h_ctx = h_ctx + gt2_c * rms_norm(f[bsz * seq:].reshape(bsz, -1, d), g_post_ffn[layer])
    return x
```

```python
import functools
import math

import jax
import jax.numpy as jnp
from jax import lax
from jax.experimental import pallas as pl
from jax.experimental.pallas import tpu as pltpu

F32 = jnp.float32
BF16 = jnp.bfloat16
I32 = jnp.int32
U32 = jnp.uint32

GRID_W = 64
CHUNK = 128
N_HEADS = 4
RET_DK, RET_DV = 32, 64
GLA_DK, GLA_DV = 32, 64
GLA_LOWRANK = 16
GLA_TAU = 16.0
DIFF_DH = 64
ROPE_BASE = 10000.0
N_EXPERTS = 64
TOP_K = 6
N_GROUPS = 8
TOPK_GROUPS = 4
GROUP_SIZE = N_EXPERTS // N_GROUPS
ROUTED_SCALE = 2.5
EPS = 1e-6
GLA_SUB = 16

LANES = 128
SUBLANES = 8
VMEM_LIMIT_BYTES = 56 * 1024 * 1024

TOKEN_TILE = 512
ATTN_TQ = 256
ATTN_TK = 512
MOE_TILES = (1024, 512)
MOE_BLOCK = 128

HIGHEST = lax.Precision.HIGHEST
NT_DIMS = (((1,), (1,)), ((), ()))
TN_DIMS = (((0,), (0,)), ((), ()))


def _cparams(*sem):
    return pltpu.CompilerParams(dimension_semantics=sem, vmem_limit_bytes=VMEM_LIMIT_BYTES)


def _log_sigmoid(x):
    return jnp.minimum(x, 0.0) - jnp.log(1.0 + jnp.exp(-jnp.abs(x)))


def _silu(x):
    return x * (1.0 / (1.0 + jnp.exp(-x)))


def _rms(x):
    return x * lax.rsqrt(jnp.mean(x * x, axis=-1, keepdims=True) + EPS)


def _pack_halves(a, b):
    ua = lax.bitcast_convert_type(a.astype(BF16).astype(F32), U32)
    ub = lax.bitcast_convert_type(b.astype(BF16).astype(F32), U32)
    return (ua & jnp.uint32(0xFFFF0000)) | (ub >> 16)


def _unpack_halves(w):
    a = lax.bitcast_convert_type(w & jnp.uint32(0xFFFF0000), F32)
    b = lax.bitcast_convert_type(w << 16, F32)
    return a, b


def _mod_kernel(cond_ref, w_ref, b_ref, o_ref):
    a = _silu(cond_ref[...])
    o_ref[0] = jnp.dot(a, w_ref[0], precision=HIGHEST, preferred_element_type=F32) + b_ref[0]


def _modulation(cond, w_mod, b_mod):
    n_layers, d, d6 = w_mod.shape
    tn = 1024
    return pl.pallas_call(
        _mod_kernel,
        out_shape=jax.ShapeDtypeStruct((n_layers, SUBLANES, d6), F32),
        grid=(n_layers, d6 // tn),
        in_specs=[
            pl.BlockSpec((SUBLANES, d), lambda l, j: (0, 0)),
            pl.BlockSpec((1, d, tn), lambda l, j: (l, 0, j)),
            pl.BlockSpec((1, 1, tn), lambda l, j: (l, 0, j)),
        ],
        out_specs=pl.BlockSpec((1, SUBLANES, tn), lambda l, j: (l, 0, j)),
        compiler_params=_cparams("parallel", "parallel"),
        name="modulation",
    )(cond, w_mod, b_mod.reshape(n_layers, 1, d6))


def _rope(x, cos, sin, quarter):
    lane = lax.broadcasted_iota(I32, (1, LANES), 1)
    first = (lane % (2 * quarter)) < quarter
    outs = []
    for c in range(x.shape[-1] // LANES):
        xc = x[:, c * LANES:(c + 1) * LANES]
        partner = jnp.where(first, pltpu.roll(xc, LANES - quarter, 1), pltpu.roll(xc, quarter, 1))
        outs.append(xc * cos + partner * sin)
    return outs[0] if len(outs) == 1 else jnp.concatenate(outs, axis=-1)


def _inproj_kernel(tiles_per_batch, n_batch, x_ref, mod_ref, g_ref, w_ref, c32_ref, s32_ref, c64_ref, s64_ref,
                   ret_ref, gla_ref, glr_ref, dq_ref, dk_ref, dv_ref):
    d = x_ref.shape[-1]
    r = jnp.minimum(pl.program_id(0) // tiles_per_batch, n_batch)
    shift = mod_ref[pl.ds(r, 1), 0:d]
    scale = mod_ref[pl.ds(r, 1), d:2 * d]
    h = (_rms(x_ref[...]) * g_ref[...] * (1.0 + scale) + shift).astype(BF16)

    def proj(lo, hi):
        return jnp.dot(h, w_ref[:, lo:hi], preferred_element_type=F32)

    c32, s32 = c32_ref[...], s32_ref[...]
    c64, s64 = c64_ref[...], s64_ref[...]
    ret = proj(0, 768)
    ret_ref[:, 0:128] = _rope(ret[:, 0:128], c32, s32, RET_DK // 4).astype(BF16)
    ret_ref[:, 128:256] = (_rope(ret[:, 128:256], c32, s32, RET_DK // 4) * RET_DK ** -0.5).astype(BF16)
    ret_ref[:, 256:768] = ret[:, 256:768].astype(BF16)
    gla = proj(768, 1536)
    gla_ref[:, 0:128] = (gla[:, 0:128] * GLA_DK ** -0.5).astype(BF16)
    gla_ref[:, 128:768] = gla[:, 128:768].astype(BF16)
    glr_ref[...] = proj(1536, 1664)
    dq_ref[...] = (_rope(proj(1664, 2176), c64, s64, DIFF_DH // 4) * DIFF_DH ** -0.5).astype(BF16)
    dk_ref[...] = _rope(proj(2176, 2688), c64, s64, DIFF_DH // 4).astype(BF16)
    dv_ref[...] = proj(2688, 3200).astype(BF16)


def _in_projection(xs, mod, g_pre, w_r, tables, n_batch, seq):
    n, d = xs.shape
    tm = TOKEN_TILE
    tiles_per_batch = seq // tm
    n_lat_tiles = n_batch * tiles_per_batch
    c32, s32, c64, s64 = tables

    def tab_map(i):
        return (jnp.where(i < n_lat_tiles, i % tiles_per_batch, tiles_per_batch), 0)

    row = lambda i: (i, 0)
    const = lambda i: (0, 0)
    tab_spec = pl.BlockSpec((tm, LANES), tab_map)
    return pl.pallas_call(
        functools.partial(_inproj_kernel, tiles_per_batch, n_batch),
        out_shape=(
            jax.ShapeDtypeStruct((n, 768), BF16), jax.ShapeDtypeStruct((n, 768), BF16),
            jax.ShapeDtypeStruct((n, LANES), F32),
            jax.ShapeDtypeStruct((n, 512), BF16), jax.ShapeDtypeStruct((n, 512), BF16),
            jax.ShapeDtypeStruct((n, 512), BF16)),
        grid=(n // tm,),
        in_specs=[
            pl.BlockSpec((tm, d), row),
            pl.BlockSpec(mod.shape, const),
            pl.BlockSpec((1, d), const),
            pl.BlockSpec(w_r.shape, const),
            tab_spec, tab_spec, tab_spec, tab_spec,
        ],
        out_specs=(
            pl.BlockSpec((tm, 768), row), pl.BlockSpec((tm, 768), row), pl.BlockSpec((tm, LANES), row),
            pl.BlockSpec((tm, 512), row), pl.BlockSpec((tm, 512), row), pl.BlockSpec((tm, 512), row)),
        compiler_params=_cparams("parallel"),
        name="in_projection",
    )(xs, mod, g_pre, w_r, c32, s32, c64, s64)


def _rope_tables(seq, head_dim, extra_rows):
    half, quarter = head_dim // 2, head_dim // 4
    freqs = ROPE_BASE ** (-jnp.arange(quarter, dtype=F32) / quarter)
    t = jnp.arange(seq)
    row = (t // GRID_W).astype(F32)
    col = (t % GRID_W).astype(F32)
    j = jnp.arange(LANES) % head_dim
    jj = j % half
    pos = jnp.where((j < half)[None, :], row[:, None], col[:, None])
    ang = pos * freqs[jj % quarter][None, :]
    cos = jnp.cos(ang)
    sin = jnp.sin(ang) * jnp.where(jj < quarter, -1.0, 1.0)[None, :]
    cos = jnp.concatenate([cos, jnp.ones((extra_rows, LANES), F32)], axis=0)
    sin = jnp.concatenate([sin, jnp.zeros((extra_rows, LANES), F32)], axis=0)
    return cos, sin


def _head_stack(x, width):
    lane = lax.broadcasted_iota(I32, (1, x.shape[-1]), 1)
    zero = jnp.zeros_like(x)
    return jnp.concatenate([jnp.where(lane // width == h, x, zero) for h in range(N_HEADS)], axis=0)


def _head_select(x4, rows):
    lane = lax.broadcasted_iota(I32, (1, x4.shape[-1]), 1)
    out = jnp.zeros((rows, x4.shape[-1]), F32)
    for h in range(N_HEADS):
        out = out + jnp.where(lane // RET_DV == h, x4[h * rows:(h + 1) * rows], 0.0)
    return out


def _scan_kernel(rev, ret_ref, gla_ref, glr_ref, rl_lane_ref, rl_rows_ref, rl_col_ref, gw_ref, gb_ref,
                 ore_ref, ogl_ref, sr_ref, sg_ref):
    c = CHUNK

    @pl.when(pl.program_id(1) == 0)
    def _():
        sr_ref[...] = jnp.zeros_like(sr_ref)
        sg_ref[...] = jnp.zeros_like(sg_ref)

    ri = lax.broadcasted_iota(I32, (c, c), 0)
    ci = lax.broadcasted_iota(I32, (c, c), 1)
    attends = (ri <= ci) if rev else (ri >= ci)
    idx = lax.broadcasted_iota(I32, (c, 1), 0).astype(F32)
    bd = (lax.broadcasted_iota(I32, (LANES, 2 * LANES), 0) // RET_DK
          == lax.broadcasted_iota(I32, (LANES, 2 * LANES), 1) // RET_DV)

    q = ret_ref[:, 0:128]
    k = ret_ref[:, 128:256]
    v = ret_ref[:, 256:512]
    lg_lane = _log_sigmoid(rl_lane_ref[...])
    lg_rows = _log_sigmoid(rl_rows_ref[...])
    lg_col = _log_sigmoid(rl_col_ref[...])
    ri4 = lax.broadcasted_iota(I32, (N_HEADS * c, c), 0) % c
    ci4 = lax.broadcasted_iota(I32, (N_HEADS * c, c), 1)
    att4 = (ri4 <= ci4) if rev else (ri4 >= ci4)
    dmat = jnp.where(att4, jnp.exp(jnp.abs(ri4 - ci4).astype(F32) * lg_rows), 0.0)
    s = lax.dot_general(_head_stack(q, RET_DK), k, NT_DIMS, preferred_element_type=F32)
    o = _head_select(jnp.dot((s * dmat).astype(BF16), v, preferred_element_type=F32), c)
    if rev:
        xi = jnp.exp((c - idx) * lg_lane)
        zeta = jnp.exp(idx * lg_lane)
    else:
        xi = jnp.exp((idx + 1.0) * lg_lane)
        zeta = jnp.exp((c - 1.0 - idx) * lg_lane)
    sr = sr_ref[...]
    o = o + jnp.dot((q.astype(F32) * xi).astype(BF16), sr.astype(BF16), preferred_element_type=F32)
    kz = (k.astype(F32) * zeta).astype(BF16)
    u = lax.dot_general(kz, v, TN_DIMS, preferred_element_type=F32)
    g_chunk = jnp.exp(float(c) * lg_col)
    sr_ref[...] = sr * jnp.concatenate([g_chunk, g_chunk], axis=1) + jnp.where(bd, u, 0.0)
    ore_ref[...] = o

    gq = gla_ref[:, 0:128].astype(F32)
    gk = gla_ref[:, 128:256].astype(F32)
    gv = gla_ref[:, 256:512]
    z = jnp.dot(glr_ref[...], gw_ref[...], precision=HIGHEST, preferred_element_type=F32) + gb_ref[...]
    la = _log_sigmoid(z) * (1.0 / GLA_TAU)
    b = jnp.dot(attends.astype(F32), la, precision=HIGHEST, preferred_element_type=F32)
    if rev:
        first = (ri // GLA_SUB) * GLA_SUB + (GLA_SUB - 1)
        ref_sel = ci >= first
    else:
        first = (ri // GLA_SUB) * GLA_SUB
        ref_sel = ci <= first
    refrow = jnp.dot(ref_sel.astype(F32), la, precision=HIGHEST, preferred_element_type=F32)
    qs = gq * jnp.exp(b - refrow)
    jcol = lax.broadcasted_iota(I32, (c, 1), 0)
    rr = lax.broadcasted_iota(I32, (N_HEADS * GLA_SUB, c), 0) % GLA_SUB
    cc = lax.broadcasted_iota(I32, (N_HEADS * GLA_SUB, c), 1)
    pieces = []
    for blk in range(c // GLA_SUB):
        lo = blk * GLA_SUB
        ref_b = refrow[lo:lo + 1]
        seen = (jcol >= lo) if rev else (jcol < lo + GLA_SUB)
        ks = (gk * jnp.exp(jnp.where(seen, ref_b - b, -jnp.inf))).astype(BF16)
        qz = _head_stack(qs[lo:lo + GLA_SUB], GLA_DK).astype(BF16)
        att = lax.dot_general(qz, ks, NT_DIMS, preferred_element_type=F32)
        ok = (cc >= rr + lo) if rev else (cc <= rr + lo)
        att = jnp.where(ok, att, 0.0).astype(BF16)
        pieces.append(_head_select(jnp.dot(att, gv, preferred_element_type=F32), GLA_SUB))
    og = jnp.concatenate(pieces, axis=0)
    sg = sg_ref[...]
    og = og + jnp.dot((gq * jnp.exp(b)).astype(BF16), sg.astype(BF16), preferred_element_type=F32)
    b_last = b[0:1] if rev else b[c - 1:c]
    kz = (gk * jnp.exp(b_last - b)).astype(BF16)
    u = lax.dot_general(kz, gv, TN_DIMS, preferred_element_type=F32)
    eye = lax.broadcasted_iota(I32, (LANES, LANES), 0) == lax.broadcasted_iota(I32, (LANES, LANES), 1)
    g_col = jnp.sum(jnp.where(eye, jnp.exp(b_last), 0.0), axis=1, keepdims=True)
    sg_ref[...] = sg * g_col + jnp.where(bd, u, 0.0)
    ogl_ref[...] = og


def _scan(rev, ret, gla, glr, ret_logit, gla_w, gla_b, n_batch, seq, ctx_len):
    n = ret.shape[0]
    c = CHUNK
    nc_ctx, nc_lat = ctx_len // c, seq // c
    n_steps = nc_ctx + nc_lat
    ctx_base = n_batch * nc_lat

    def blk(b, s):
        if rev:
            return (jnp.where(s < nc_ctx, ctx_base + b * nc_ctx + (nc_ctx - 1 - s), b * nc_lat + (n_steps - 1 - s)), 0)
        return (jnp.where(s < nc_ctx, ctx_base + b * nc_ctx + s, b * nc_lat + (s - nc_ctx)), 0)

    const = lambda b, s: (0, 0)
    rl_lane = jnp.repeat(ret_logit, RET_DK)[None, :]
    rl_rows = jnp.broadcast_to(jnp.repeat(ret_logit, c)[:, None], (N_HEADS * c, c))
    rl_col = jnp.broadcast_to(jnp.repeat(ret_logit, RET_DK)[:, None], (LANES, LANES))
    gw = jnp.zeros((LANES, LANES), F32).at[:GLA_LOWRANK].set(gla_w)
    return pl.pallas_call(
        functools.partial(_scan_kernel, rev),
        out_shape=(jax.ShapeDtypeStruct((n, 256), F32), jax.ShapeDtypeStruct((n, 256), F32)),
        grid=(n_batch, n_steps),
        in_specs=[
            pl.BlockSpec((c, 768), blk), pl.BlockSpec((c, 768), blk), pl.BlockSpec((c, LANES), blk),
            pl.BlockSpec(rl_lane.shape, const), pl.BlockSpec(rl_rows.shape, const),
            pl.BlockSpec(rl_col.shape, const), pl.BlockSpec(gw.shape, const), pl.BlockSpec((1, LANES), const),
        ],
        out_specs=(pl.BlockSpec((c, 256), blk), pl.BlockSpec((c, 256), blk)),
        scratch_shapes=[pltpu.VMEM((LANES, 2 * LANES), F32), pltpu.VMEM((LANES, 2 * LANES), F32)],
        compiler_params=_cparams("arbitrary", "arbitrary"),
        name="scan_bwd" if rev else "scan_fwd",
    )(ret, gla, glr, rl_lane, rl_rows, rl_col, gw, gla_b[None, :])


def _attn_kernel(n_lat_blocks, lam_init, *refs):
    if n_lat_blocks:
        q_ref, kc_ref, vc_ref, kl_ref, vl_ref, lp_ref, g_ref, o_ref, m_ref, l_ref, acc_ref = refs
    else:
        q_ref, kc_ref, vc_ref, lp_ref, g_ref, o_ref, m_ref, l_ref, acc_ref = refs
    tq = q_ref.shape[0]
    q = q_ref[...]
    lane = lax.broadcasted_iota(I32, (1, LANES), 1)
    zero = jnp.zeros_like(q)
    qz = jnp.concatenate([jnp.where(lane < DIFF_DH, q, zero), jnp.where(lane >= DIFF_DH, q, zero)], axis=0)
    m_ref[...] = jnp.full(m_ref.shape, -jnp.inf, F32)
    l_ref[...] = jnp.zeros_like(l_ref)
    acc_ref[...] = jnp.zeros_like(acc_ref)

    def step(kb, vb):
        s = lax.dot_general(qz, kb, NT_DIMS, preferred_element_type=F32)
        m_prev = m_ref[...]
        m_new = jnp.maximum(m_prev, jnp.max(s, axis=1, keepdims=True))
        alpha = jnp.exp(m_prev - m_new)
        p = jnp.exp(s - m_new)
        l_ref[...] = alpha * l_ref[...] + jnp.sum(p, axis=1, keepdims=True)
        acc_ref[...] = alpha * acc_ref[...] + jnp.dot(p.astype(BF16), vb, preferred_element_type=F32)
        m_ref[...] = m_new

    step(kc_ref[...], vc_ref[...])
    if n_lat_blocks:
        def body(j, carry):
            off = pl.multiple_of(j * ATTN_TK, ATTN_TK)
            step(kl_ref[pl.ds(off, ATTN_TK), :], vl_ref[pl.ds(off, ATTN_TK), :])
            return carry
        lax.fori_loop(0, n_lat_blocks, body, 0)

    lp = lp_ref[...]
    lam = (jnp.exp(jnp.sum(lp[0:1] * lp[1:2], axis=1, keepdims=True))
           - jnp.exp(jnp.sum(lp[2:3] * lp[3:4], axis=1, keepdims=True)) + lam_init)
    acc = acc_ref[...]
    l = l_ref[...]
    o = acc[:tq] / l[:tq] - lam * (acc[tq:] / l[tq:])
    o_ref[...] = (_rms(o) * g_ref[...] * (1.0 - lam_init)).astype(o_ref.dtype)


def _diff_attention(dq, dk, dv, lp, g, lam_init, n_batch, seq, ctx_len, latent):
    ctx_blk0 = (n_batch * seq) // ctx_len
    kc_spec = pl.BlockSpec((ctx_len, LANES), lambda b, h, i: (ctx_blk0 + b, h))
    const = lambda b, h, i: (0, 0)
    if latent:
        tq = ATTN_TQ
        n_q = seq // tq
        q_map = lambda b, h, i: (b * n_q + i, h)
        lat_spec = pl.BlockSpec((seq, LANES), lambda b, h, i: (b, h))
        in_specs = [pl.BlockSpec((tq, LANES), q_map), kc_spec, kc_spec, lat_spec, lat_spec]
        args = (dq, dk, dv, dk, dv)
        n_rows, o_map, n_lat_blocks = n_batch * seq, q_map, seq // ATTN_TK
    else:
        tq, n_q = ctx_len, 1
        in_specs = [kc_spec, kc_spec, kc_spec]
        args = (dq, dk, dv)
        n_rows, o_map, n_lat_blocks = n_batch * ctx_len, (lambda b, h, i: (b, h)), 0
    in_specs += [pl.BlockSpec(lp.shape, const), pl.BlockSpec((1, LANES), const)]
    return pl.pallas_call(
        functools.partial(_attn_kernel, n_lat_blocks, lam_init),
        out_shape=jax.ShapeDtypeStruct((n_rows, N_HEADS * LANES), BF16),
        grid=(n_batch, N_HEADS, n_q),
        in_specs=in_specs,
        out_specs=pl.BlockSpec((tq, LANES), o_map),
        scratch_shapes=[pltpu.VMEM((2 * tq, 1), F32), pltpu.VMEM((2 * tq, 1), F32),
                        pltpu.VMEM((2 * tq, LANES), F32)],
        compiler_params=_cparams("parallel", "parallel", "parallel"),
        name="diff_attention_lat" if latent else "diff_attention_ctx",
    )(*args, lp, g)


def _merge_kernel(tiles_per_batch, n_batch, orf_ref, orb_ref, ogf_ref, ogb_ref, rg_ref, gr_ref, dif_ref,
                  x_ref, mod_ref, gpost_ref, gpre_ref, glag_ref, wout_ref, wrt_ref,
                  xo_ref, hp_ref, sc_ref):
    d = x_ref.shape[-1]
    r = jnp.minimum(pl.program_id(0) // tiles_per_batch, n_batch)
    gate1 = mod_ref[pl.ds(r, 1), 2 * d:3 * d]
    shift2 = mod_ref[pl.ds(r, 1), 3 * d:4 * d]
    scale2 = mod_ref[pl.ds(r, 1), 4 * d:5 * d]
    gi = lax.broadcasted_iota(I32, (256, 256), 0) // RET_DV
    gj = lax.broadcasted_iota(I32, (256, 256), 1) // RET_DV
    group_mean = jnp.where(gi == gj, 1.0 / RET_DV, 0.0).astype(BF16)

    def head_norm(o):
        ms = jnp.dot((o * o).astype(BF16), group_mean, preferred_element_type=F32)
        return o * lax.rsqrt(ms + EPS)

    ret = head_norm(orf_ref[...] + orb_ref[...]) * _silu(rg_ref[...].astype(F32))
    gla = head_norm(ogf_ref[...] + ogb_ref[...]) * glag_ref[...] * _silu(gr_ref[...].astype(F32))
    m = (jnp.dot(ret.astype(BF16), wout_ref[0:256, :], preferred_element_type=F32)
         + jnp.dot(gla.astype(BF16), wout_ref[256:512, :], preferred_element_type=F32)
         + jnp.dot(dif_ref[...], wout_ref[512:1024, :], preferred_element_type=F32))
    x_new = x_ref[...] + gate1 * (_rms(m) * gpost_ref[...])
    xo_ref[...] = x_new
    h2 = _rms(x_new) * gpre_ref[...] * (1.0 + scale2) + shift2
    hp_ref[...] = _pack_halves(h2[:, :d // 2], h2[:, d // 2:])
    logits = lax.dot_general(wrt_ref[...], h2, NT_DIMS, precision=HIGHEST, preferred_element_type=F32)
    sc_ref[...] = 1.0 / (1.0 + jnp.exp(-logits))


def _merge(o_rf, o_rb, o_gf, o_gb, ret, gla, dif, xs, mod, g_post, g_pre_ffn, gla_g, w_out, w_rt,
           n_rows, n_batch, seq):
    d = xs.shape[-1]
    tm = TOKEN_TILE
    row = lambda i: (i, 0)
    gate_col = lambda i: (i, 2)
    const = lambda i: (0, 0)
    return pl.pallas_call(
        functools.partial(_merge_kernel, seq // tm, n_batch),
        out_shape=(jax.ShapeDtypeStruct((n_rows, d), F32), jax.ShapeDtypeStruct((n_rows, d // 2), U32),
                   jax.ShapeDtypeStruct((N_EXPERTS, n_rows), F32)),
        grid=(n_rows // tm,),
        in_specs=[
            pl.BlockSpec((tm, 256), row), pl.BlockSpec((tm, 256), row),
            pl.BlockSpec((tm, 256), row), pl.BlockSpec((tm, 256), row),
            pl.BlockSpec((tm, 256), gate_col), pl.BlockSpec((tm, 256), gate_col),
            pl.BlockSpec((tm, 512), row), pl.BlockSpec((tm, d), row),
            pl.BlockSpec(mod.shape, const), pl.BlockSpec((1, d), const), pl.BlockSpec((1, d), const),
            pl.BlockSpec((1, 256), const), pl.BlockSpec(w_out.shape, const), pl.BlockSpec(w_rt.shape, const),
        ],
        out_specs=(pl.BlockSpec((tm, d), row), pl.BlockSpec((tm, d // 2), row),
                   pl.BlockSpec((N_EXPERTS, tm), lambda i: (0, i))),
        compiler_params=_cparams("parallel"),
        name="merge",
    )(o_rf, o_rb, o_gf, o_gb, ret, gla, dif, xs, mod, g_post, g_pre_ffn, gla_g, w_out, w_rt)


def _route_kernel(sc_ref, bias_ref, eidx_ref, wts_ref, pos_ref, meta_ref):
    tt = sc_ref.shape[-1]
    scores = sc_ref[...]
    sel = scores + bias_ref[...][:, 0:1]
    sub = lax.broadcasted_iota(I32, (GROUP_SIZE, tt), 0)
    neg = -jnp.inf
    gscore = []
    for g in range(N_GROUPS):
        xg = sel[g * GROUP_SIZE:(g + 1) * GROUP_SIZE]
        m1 = jnp.max(xg, axis=0, keepdims=True)
        i1 = jnp.min(jnp.where(xg == m1, sub, GROUP_SIZE), axis=0, keepdims=True)
        m2 = jnp.max(jnp.where(sub == i1, neg, xg), axis=0, keepdims=True)
        gscore.append(m1 + m2)
    rows = []
    for g in range(N_GROUPS):
        rank = jnp.zeros((1, tt), I32)
        for o in range(N_GROUPS):
            if o == g:
                continue
            ahead = (gscore[o] >= gscore[g]) if o < g else (gscore[o] > gscore[g])
            rank = rank + ahead.astype(I32)
        rows.append(jnp.where(rank < TOPK_GROUPS, sel[g * GROUP_SIZE:(g + 1) * GROUP_SIZE], neg))
    masked = jnp.concatenate(rows, axis=0)
    eio = lax.broadcasted_iota(I32, (N_EXPERTS, tt), 0)
    member = jnp.zeros((N_EXPERTS, tt), F32)
    idxs, ws = [], []
    for _ in range(TOP_K):
        m = jnp.max(masked, axis=0, keepdims=True)
        i = jnp.min(jnp.where(masked == m, eio, N_EXPERTS), axis=0, keepdims=True)
        hit = eio == i
        idxs.append(i)
        ws.append(jnp.sum(jnp.where(hit, scores, 0.0), axis=0, keepdims=True))
        member = jnp.where(hit, 1.0, member)
        masked = jnp.where(hit, neg, masked)
    wsum = ws[0]
    for w in ws[1:]:
        wsum = wsum + w
    ti = lax.broadcasted_iota(I32, (tt, tt), 0)
    tj = lax.broadcasted_iota(I32, (tt, tt), 1)
    before = jnp.where(ti < tj, 1.0, 0.0).astype(BF16)
    rank_in_e = jnp.dot(member.astype(BF16), before, preferred_element_type=F32)
    cnt = jnp.sum(member, axis=1, keepdims=True)
    padded = jnp.floor((cnt + (SUBLANES - 1)) * (1.0 / SUBLANES)) * SUBLANES
    ei = lax.broadcasted_iota(I32, (N_EXPERTS, N_EXPERTS), 0)
    ej = lax.broadcasted_iota(I32, (N_EXPERTS, N_EXPERTS), 1)
    lower = jnp.where(ej < ei, 1.0, 0.0)
    off = jnp.dot(lower, jnp.broadcast_to(padded, (N_EXPERTS, LANES)), precision=HIGHEST,
                  preferred_element_type=F32)
    slot = rank_in_e + off[:, 0:1]
    zrow_i = jnp.zeros((SUBLANES - TOP_K, tt), I32)
    zrow_f = jnp.zeros((SUBLANES - TOP_K, tt), F32)
    pos = [jnp.sum(jnp.where(eio == i, slot, 0.0), axis=0, keepdims=True).astype(I32) for i in idxs]
    eidx_ref[...] = jnp.concatenate(idxs + [zrow_i], axis=0)
    wts_ref[...] = jnp.concatenate([w / wsum * ROUTED_SCALE for w in ws] + [zrow_f], axis=0)
    pos_ref[...] = jnp.concatenate(pos + [zrow_i], axis=0)
    meta_ref[0] = jnp.concatenate([jnp.broadcast_to(cnt, (N_EXPERTS, LANES)), off], axis=1).astype(I32)


def _route(scores_t, bias, tt, row0, n):
    tile0 = row0 // tt
    tok = lambda i: (0, i)
    return pl.pallas_call(
        _route_kernel,
        out_shape=(jax.ShapeDtypeStruct((SUBLANES, n), I32), jax.ShapeDtypeStruct((SUBLANES, n), F32),
                   jax.ShapeDtypeStruct((SUBLANES, n), I32),
                   jax.ShapeDtypeStruct((n // tt, N_EXPERTS, 2 * LANES), I32)),
        grid=(n // tt,),
        in_specs=[pl.BlockSpec((N_EXPERTS, tt), lambda i: (0, i + tile0)),
                  pl.BlockSpec((N_EXPERTS, LANES), lambda i: (0, 0))],
        out_specs=(pl.BlockSpec((SUBLANES, tt), tok), pl.BlockSpec((SUBLANES, tt), tok),
                   pl.BlockSpec((SUBLANES, tt), tok),
                   pl.BlockSpec((1, N_EXPERTS, 2 * LANES), lambda i: (i, 0, 0))),
        compiler_params=_cparams("parallel"),
        name="route",
    )(scores_t, jnp.broadcast_to(bias[:, None], (N_EXPERTS, LANES)))


def _slot_tokens(e_idx, cnt, off, tt, n_slots):
    n_tiles = cnt.shape[0]
    keys = e_idx[:TOP_K].T.reshape(n_tiles, tt * TOP_K)
    order = jnp.argsort(keys, axis=1, stable=True)
    sorted_tok = (order // TOP_K).astype(I32)
    cstart = jnp.cumsum(cnt, axis=1) - cnt
    s = jnp.arange(n_slots, dtype=I32)

    def one(sorted_tok_t, cnt_t, off_t, cstart_t):
        e = jnp.clip(jnp.searchsorted(off_t, s, side="right") - 1, 0, N_EXPERTS - 1)
        r = s - off_t[e]
        valid = r < cnt_t[e]
        c = jnp.clip(cstart_t[e] + r, 0, tt * TOP_K - 1)
        return jnp.where(valid, sorted_tok_t[c], 0)

    return jax.vmap(one)(sorted_tok, cnt, off, cstart)


def _moe_kernel(seq, n_batch, tile0, has_prev, rt_ref, pos_ref, wts_ref, meta_ref,
                hp_ref, x_ref, mod_ref, gpost_ref, weg_ref, weu_ref, wed_ref, wsg_ref, wsu_ref, wsd_ref, *refs):
    o_ref, xg_ref, y_ref = refs[1:] if has_prev else refs
    tt, half = hp_ref.shape
    d = 2 * half
    e = pl.program_id(1)
    cnt = meta_ref[0, e]
    off = meta_ref[1, e]

    def expert_block(j, carry):
        base = pl.multiple_of(off + j * MOE_BLOCK, SUBLANES)
        for i in range(MOE_BLOCK):
            s = base + i
            t = rt_ref[s // LANES, s % LANES]
            xg_ref[pl.ds(i, 1), :] = hp_ref[pl.ds(t, 1), :]
        xa, xb = _unpack_halves(xg_ref[...])
        xa, xb = xa.astype(BF16), xb.astype(BF16)
        hg = (jnp.dot(xa, weg_ref[0, 0:half, :], preferred_element_type=F32)
              + jnp.dot(xb, weg_ref[0, half:d, :], preferred_element_type=F32))
        hu = (jnp.dot(xa, weu_ref[0, 0:half, :], preferred_element_type=F32)
              + jnp.dot(xb, weu_ref[0, half:d, :], preferred_element_type=F32))
        y = jnp.dot((_silu(hg) * hu).astype(BF16), wed_ref[0], preferred_element_type=F32)
        y_ref[pl.ds(base, MOE_BLOCK), :] = _pack_halves(y[:, :half], y[:, half:])
        return carry

    lax.fori_loop(0, (cnt + (MOE_BLOCK - 1)) // MOE_BLOCK, expert_block, 0)

    @pl.when(e == N_EXPERTS - 1)
    def _():
        rows_per_k = tt // LANES

        def token(t, carry):
            acc_a = jnp.zeros((1, half), F32)
            acc_b = jnp.zeros((1, half), F32)
            for k in range(TOP_K):
                r = pos_ref[k * rows_per_k + t // LANES, t % LANES]
                w = wts_ref[k * rows_per_k + t // LANES, t % LANES]
                ya, yb = _unpack_halves(y_ref[pl.ds(r, 1), :])
                acc_a = acc_a + w * ya
                acc_b = acc_b + w * yb
            o_ref[pl.ds(t, 1), 0:half] = acc_a
            o_ref[pl.ds(t, 1), half:d] = acc_b
            return carry

        lax.fori_loop(0, tt, token, 0)
        rows = (pl.program_id(0) + tile0) * tt + lax.broadcasted_iota(I32, (tt, 1), 0)
        rb = jnp.minimum(rows // seq, n_batch)
        gate2 = jnp.zeros((tt, d), F32)
        for bi in range(n_batch + 1):
            gate2 = jnp.where(rb == bi, mod_ref[bi:bi + 1, 5 * d:6 * d], gate2)
        xa, xb = _unpack_halves(hp_ref[...])
        xa, xb = xa.astype(BF16), xb.astype(BF16)
        hg = (jnp.dot(xa, wsg_ref[0:half, :], preferred_element_type=F32)
              + jnp.dot(xb, wsg_ref[half:d, :], preferred_element_type=F32))
        hu = (jnp.dot(xa, wsu_ref[0:half, :], preferred_element_type=F32)
              + jnp.dot(xb, wsu_ref[half:d, :], preferred_element_type=F32))
        f = jnp.dot((_silu(hg) * hu).astype(BF16), wsd_ref[...], preferred_element_type=F32) + o_ref[...]
        o_ref[...] = x_ref[...] + gate2 * (_rms(f) * gpost_ref[...])


def _moe_segment(prev, scores_t, hp, xs, mod, router_bias, g_post, weights, row0, n, tt, n_batch, seq):
    n_all, half = hp.shape
    d = 2 * half
    w_eg, w_eu, w_ed, w_sg, w_su, w_sd = weights
    de = w_eg.shape[-1]
    n_tiles, tile0 = n // tt, row0 // tt
    n_slots = -(-(tt * TOP_K + N_EXPERTS * (SUBLANES - 1) + MOE_BLOCK) // (SUBLANES * LANES)) * SUBLANES * LANES
    e_idx, wts, pos, meta = _route(scores_t, router_bias, tt, row0, n)
    cnt, off = meta[:, :, 0], meta[:, :, LANES]
    rt = _slot_tokens(e_idx, cnt, off, tt, n_slots).reshape(n_tiles * n_slots // LANES, LANES)

    def per_tile(a):
        return a.reshape(SUBLANES, n_tiles, tt // LANES, LANES).transpose(1, 0, 2, 3).reshape(-1, LANES)

    meta_s = jnp.zeros((n_tiles, SUBLANES, LANES), I32)
    meta_s = meta_s.at[:, 0, :N_EXPERTS].set(cnt).at[:, 1, :N_EXPERTS].set(off).reshape(-1, LANES)

    local = lambda i, e: (i, 0)
    tile = lambda i, e: (i + tile0, 0)
    const = lambda i, e: (0, 0)
    expert = lambda i, e: (e, 0, 0)
    smem = functools.partial(pl.BlockSpec, memory_space=pltpu.SMEM)
    rows_k = SUBLANES * tt // LANES
    in_specs = [
        smem((n_slots // LANES, LANES), local), smem((rows_k, LANES), local), smem((rows_k, LANES), local),
        smem((SUBLANES, LANES), local),
        pl.BlockSpec((tt, half), tile), pl.BlockSpec((tt, d), tile),
        pl.BlockSpec(mod.shape, const), pl.BlockSpec((1, d), const),
        pl.BlockSpec((1, d, de), expert), pl.BlockSpec((1, d, de), expert), pl.BlockSpec((1, de, d), expert),
        pl.BlockSpec(w_sg.shape, const), pl.BlockSpec(w_su.shape, const), pl.BlockSpec(w_sd.shape, const),
    ]
    args = [rt, per_tile(pos), per_tile(wts), meta_s, hp, xs, mod, g_post, w_eg, w_eu, w_ed, w_sg, w_su, w_sd]
    aliases = {}
    if prev is not None:
        in_specs.append(pl.BlockSpec(memory_space=pl.ANY))
        args.append(prev)
        aliases = {len(args) - 1: 0}
    return pl.pallas_call(
        functools.partial(_moe_kernel, seq, n_batch, tile0, prev is not None),
        out_shape=jax.ShapeDtypeStruct((n_all, d), F32),
        grid=(n_tiles, N_EXPERTS),
        in_specs=in_specs,
        out_specs=pl.BlockSpec((tt, d), tile),
        scratch_shapes=[pltpu.VMEM((MOE_BLOCK, half), U32), pltpu.VMEM((n_slots, half), U32)],
        input_output_aliases=aliases,
        compiler_params=_cparams("parallel", "arbitrary"),
        name="moe",
    )(*args)


def _moe_layer(scores_t, hp, xs, mod, router_bias, g_post, weights, n_lat, n_batch, seq):
    pick = lambda rows: next(t for t in MOE_TILES if rows % t == 0)
    out = _moe_segment(None, scores_t, hp, xs, mod, router_bias, g_post, weights, 0, n_lat, pick(n_lat),
                       n_batch, seq)
    n_ctx = hp.shape[0] - n_lat
    if n_ctx:
        tt = next(t for t in MOE_TILES if n_ctx % t == 0 and n_lat % t == 0)
        out = _moe_segment(out, scores_t, hp, xs, mod, router_bias, g_post, weights, n_lat, n_ctx, tt,
                           n_batch, seq)
    return out


def kernel(x, c, ctx, c_ctx, w_mod, b_mod, g_pre_mix, g_post_mix, g_pre_ffn, g_post_ffn, w_in, w_out,
           ret_decay_logit, gla_w_gate, gla_b_gate, gla_norm_g, diff_lambda, diff_norm_g,
           w_router, router_bias, w_exp_gate, w_exp_up, w_exp_down, w_sh_gate, w_sh_up, w_sh_down):
    n_batch, seq, d = x.shape
    ctx_len = ctx.shape[1]
    depth = w_mod.shape[0]
    n_lat = n_batch * seq
    assert seq % TOKEN_TILE == 0 and (n_batch * ctx_len) % TOKEN_TILE == 0 and n_batch < SUBLANES
    assert seq % ATTN_TK == 0 and seq % CHUNK == 0 and ctx_len % CHUNK == 0 and n_lat % ctx_len == 0

    xs = jnp.concatenate([x.reshape(n_lat, d), ctx.reshape(n_batch * ctx_len, d)], axis=0)
    cond = jnp.zeros((SUBLANES, d), F32).at[:n_batch].set(c).at[n_batch].set(c_ctx)
    mods = _modulation(cond, w_mod, b_mod)
    tables = _rope_tables(seq, RET_DK, TOKEN_TILE) + _rope_tables(seq, DIFF_DH, TOKEN_TILE)
    lr0 = N_HEADS * (2 * RET_DK + 2 * RET_DV + 2 * GLA_DK + 2 * GLA_DV)
    row = lambda a: a[None, :]

    for layer in range(depth):
        need_ctx = layer < depth - 1
        lam_init = 0.8 - 0.6 * math.exp(-0.3 * layer)
        mod = mods[layer]
        wl = w_in[layer]
        w_r = jnp.concatenate([wl[:, :lr0], wl[:, lr0:lr0 + GLA_LOWRANK],
                               jnp.zeros((d, LANES - GLA_LOWRANK), F32), wl[:, lr0 + GLA_LOWRANK:]],
                              axis=1).astype(BF16)
        ret, gla, glr, dq, dk, dv = _in_projection(xs, mod, row(g_pre_mix[layer]), w_r, tables, n_batch, seq)

        scan = functools.partial(_scan, ret=ret, gla=gla, glr=glr, n_batch=n_batch, seq=seq, ctx_len=ctx_len)
        o_rf, o_gf = scan(False, ret_logit=ret_decay_logit[layer, 0], gla_w=gla_w_gate[layer, 0],
                          gla_b=gla_b_gate[layer, 0])
        o_rb, o_gb = scan(True, ret_logit=ret_decay_logit[layer, 1], gla_w=gla_w_gate[layer, 1],
                          gla_b=gla_b_gate[layer, 1])

        lp = jnp.zeros((SUBLANES, LANES), F32).at[:4, :DIFF_DH].set(diff_lambda[layer])
        attn = functools.partial(_diff_attention, dq, dk, dv, lp, row(diff_norm_g[layer]), lam_init,
                                 n_batch, seq, ctx_len)
        dif = attn(True)
        n_rows = n_lat
        if need_ctx:
            dif = jnp.concatenate([dif, attn(False)], axis=0)
            n_rows = xs.shape[0]

        xs, hp, scores_t = _merge(o_rf, o_rb, o_gf, o_gb, ret, gla, dif, xs, mod, row(g_post_mix[layer]),
                                  row(g_pre_ffn[layer]), row(jnp.tile(gla_norm_g[layer], N_HEADS)),
                                  w_out[layer].astype(BF16), w_router[layer].T, n_rows, n_batch, seq)
        weights = tuple(w[layer].astype(BF16) for w in
                        (w_exp_gate, w_exp_up, w_exp_down, w_sh_gate, w_sh_up, w_sh_down))
        xs = _moe_layer(scores_t, hp, xs, mod, router_bias[layer], row(g_post_ffn[layer]), weights,
                        n_lat, n_batch, seq)
    return xs[:n_lat].reshape(n_batch, seq, d)
```

```python
import functools
import math

import jax
import jax.numpy as jnp
from jax import lax
from jax.experimental import pallas as pl
from jax.experimental.pallas import tpu as pltpu

F32 = jnp.float32
BF16 = jnp.bfloat16
I32 = jnp.int32
U32 = jnp.uint32

GRID_W = 64
CHUNK = 128
N_HEADS = 4
RET_DK, RET_DV = 32, 64
GLA_DK, GLA_DV = 32, 64
GLA_LOWRANK = 16
GLA_TAU = 16.0
DIFF_DH = 64
ROPE_BASE = 10000.0
N_EXPERTS = 64
TOP_K = 6
N_GROUPS = 8
TOPK_GROUPS = 4
GROUP_SIZE = N_EXPERTS // N_GROUPS
ROUTED_SCALE = 2.5
EPS = 1e-6
GLA_SUB = 16

LANES = 128
SUBLANES = 8
VMEM_LIMIT_BYTES = 56 * 1024 * 1024

TOKEN_TILE = 512
ATTN_TQ = 256
ATTN_TK = 512
ATTN_ONES = 16
MOE_TILES = (1024, 512)
MOE_BLOCK = 128
COMBINE_UNROLL = 4

HIGHEST = lax.Precision.HIGHEST
NT_DIMS = (((1,), (1,)), ((), ()))
TN_DIMS = (((0,), (0,)), ((), ()))


def _cparams(*sem):
    return pltpu.CompilerParams(dimension_semantics=sem, vmem_limit_bytes=VMEM_LIMIT_BYTES)


def _log_sigmoid(x):
    return jnp.minimum(x, 0.0) - jnp.log(1.0 + jnp.exp(-jnp.abs(x)))


def _silu(x):
    return x * (1.0 / (1.0 + jnp.exp(-x)))


def _rms(x):
    return x * lax.rsqrt(jnp.mean(x * x, axis=-1, keepdims=True) + EPS)


def _pack_halves(a, b):
    ua = lax.bitcast_convert_type(a.astype(BF16).astype(F32), U32)
    ub = lax.bitcast_convert_type(b.astype(BF16).astype(F32), U32)
    return (ua & jnp.uint32(0xFFFF0000)) | (ub >> 16)


def _unpack_halves(w):
    a = lax.bitcast_convert_type(w & jnp.uint32(0xFFFF0000), F32)
    b = lax.bitcast_convert_type(w << 16, F32)
    return a, b


def _mod_kernel(cond_ref, w_ref, b_ref, o_ref):
    a = _silu(cond_ref[...])
    o_ref[0] = jnp.dot(a, w_ref[0], precision=HIGHEST, preferred_element_type=F32) + b_ref[0]


def _modulation(cond, w_mod, b_mod):
    n_layers, d, d6 = w_mod.shape
    tn = 1024
    return pl.pallas_call(
        _mod_kernel,
        out_shape=jax.ShapeDtypeStruct((n_layers, SUBLANES, d6), F32),
        grid=(n_layers, d6 // tn),
        in_specs=[
            pl.BlockSpec((SUBLANES, d), lambda l, j: (0, 0)),
            pl.BlockSpec((1, d, tn), lambda l, j: (l, 0, j)),
            pl.BlockSpec((1, 1, tn), lambda l, j: (l, 0, j)),
        ],
        out_specs=pl.BlockSpec((1, SUBLANES, tn), lambda l, j: (l, 0, j)),
        compiler_params=_cparams("parallel", "parallel"),
        name="modulation",
    )(cond, w_mod, b_mod.reshape(n_layers, 1, d6))


def _rope(x, cos, sin, quarter):
    lane = lax.broadcasted_iota(I32, (1, LANES), 1)
    first = (lane % (2 * quarter)) < quarter
    outs = []
    for c in range(x.shape[-1] // LANES):
        xc = x[:, c * LANES:(c + 1) * LANES]
        partner = jnp.where(first, pltpu.roll(xc, LANES - quarter, 1), pltpu.roll(xc, quarter, 1))
        outs.append(xc * cos + partner * sin)
    return outs[0] if len(outs) == 1 else jnp.concatenate(outs, axis=-1)


def _inproj_kernel(tiles_per_batch, n_batch, x_ref, mod_ref, g_ref, w_ref, c32_ref, s32_ref, c64_ref, s64_ref,
                   ret_ref, gla_ref, glr_ref, dq_ref, dk_ref, dv_ref):
    d = x_ref.shape[-1]
    r = jnp.minimum(pl.program_id(0) // tiles_per_batch, n_batch)
    shift = mod_ref[pl.ds(r, 1), 0:d]
    scale = mod_ref[pl.ds(r, 1), d:2 * d]
    h = (_rms(x_ref[...]) * g_ref[...] * (1.0 + scale) + shift).astype(BF16)

    def proj(lo, hi):
        return jnp.dot(h, w_ref[:, lo:hi], preferred_element_type=F32)

    c32, s32 = c32_ref[...], s32_ref[...]
    c64, s64 = c64_ref[...], s64_ref[...]
    ret = proj(0, 768)
    ret_ref[:, 0:128] = _rope(ret[:, 0:128], c32, s32, RET_DK // 4).astype(BF16)
    ret_ref[:, 128:256] = (_rope(ret[:, 128:256], c32, s32, RET_DK // 4) * RET_DK ** -0.5).astype(BF16)
    ret_ref[:, 256:768] = ret[:, 256:768].astype(BF16)
    gla = proj(768, 1536)
    gla_ref[:, 0:128] = (gla[:, 0:128] * GLA_DK ** -0.5).astype(BF16)
    gla_ref[:, 128:768] = gla[:, 128:768].astype(BF16)
    glr_ref[...] = proj(1536, 1664)
    dq_ref[...] = (_rope(proj(1664, 2176), c64, s64, DIFF_DH // 4) * DIFF_DH ** -0.5).astype(BF16)
    dk_ref[...] = _rope(proj(2176, 2688), c64, s64, DIFF_DH // 4).astype(BF16)
    dv_ref[...] = proj(2688, 3200).astype(BF16)


def _in_projection(xs, mod, g_pre, w_r, tables, n_batch, seq):
    n, d = xs.shape
    tm = TOKEN_TILE
    tiles_per_batch = seq // tm
    n_lat_tiles = n_batch * tiles_per_batch
    c32, s32, c64, s64 = tables

    def tab_map(i):
        return (jnp.where(i < n_lat_tiles, i % tiles_per_batch, tiles_per_batch), 0)

    row = lambda i: (i, 0)
    const = lambda i: (0, 0)
    tab_spec = pl.BlockSpec((tm, LANES), tab_map)
    return pl.pallas_call(
        functools.partial(_inproj_kernel, tiles_per_batch, n_batch),
        out_shape=(
            jax.ShapeDtypeStruct((n, 768), BF16), jax.ShapeDtypeStruct((n, 768), BF16),
            jax.ShapeDtypeStruct((n, LANES), F32),
            jax.ShapeDtypeStruct((n, 512), BF16), jax.ShapeDtypeStruct((n, 512), BF16),
            jax.ShapeDtypeStruct((n, 512), BF16)),
        grid=(n // tm,),
        in_specs=[
            pl.BlockSpec((tm, d), row),
            pl.BlockSpec(mod.shape, const),
            pl.BlockSpec((1, d), const),
            pl.BlockSpec(w_r.shape, const),
            tab_spec, tab_spec, tab_spec, tab_spec,
        ],
        out_specs=(
            pl.BlockSpec((tm, 768), row), pl.BlockSpec((tm, 768), row), pl.BlockSpec((tm, LANES), row),
            pl.BlockSpec((tm, 512), row), pl.BlockSpec((tm, 512), row), pl.BlockSpec((tm, 512), row)),
        compiler_params=_cparams("parallel"),
        name="in_projection",
    )(xs, mod, g_pre, w_r, c32, s32, c64, s64)


def _rope_tables(seq, head_dim, extra_rows):
    half, quarter = head_dim // 2, head_dim // 4
    freqs = ROPE_BASE ** (-jnp.arange(quarter, dtype=F32) / quarter)
    t = jnp.arange(seq)
    row = (t // GRID_W).astype(F32)
    col = (t % GRID_W).astype(F32)
    j = jnp.arange(LANES) % head_dim
    jj = j % half
    pos = jnp.where((j < half)[None, :], row[:, None], col[:, None])
    ang = pos * freqs[jj % quarter][None, :]
    cos = jnp.cos(ang)
    sin = jnp.sin(ang) * jnp.where(jj < quarter, -1.0, 1.0)[None, :]
    cos = jnp.concatenate([cos, jnp.ones((extra_rows, LANES), F32)], axis=0)
    sin = jnp.concatenate([sin, jnp.zeros((extra_rows, LANES), F32)], axis=0)
    return cos, sin


def _head_stack(x, width):
    lane = lax.broadcasted_iota(I32, (1, x.shape[-1]), 1)
    zero = jnp.zeros_like(x)
    return jnp.concatenate([jnp.where(lane // width == h, x, zero) for h in range(N_HEADS)], axis=0)


def _head_select(x4, rows):
    lane = lax.broadcasted_iota(I32, (1, x4.shape[-1]), 1)
    out = jnp.zeros((rows, x4.shape[-1]), F32)
    for h in range(N_HEADS):
        out = out + jnp.where(lane // RET_DV == h, x4[h * rows:(h + 1) * rows], 0.0)
    return out


def _scan_kernel(rev, ret_ref, gla_ref, glr_ref, rl_lane_ref, rl_rows_ref, rl_col_ref, gw_ref, gb_ref,
                 ore_ref, ogl_ref, sr_ref, sg_ref):
    c = CHUNK

    @pl.when(pl.program_id(1) == 0)
    def _():
        sr_ref[...] = jnp.zeros_like(sr_ref)
        sg_ref[...] = jnp.zeros_like(sg_ref)

    ri = lax.broadcasted_iota(I32, (c, c), 0)
    ci = lax.broadcasted_iota(I32, (c, c), 1)
    attends = (ri <= ci) if rev else (ri >= ci)
    idx = lax.broadcasted_iota(I32, (c, 1), 0).astype(F32)
    bd = (lax.broadcasted_iota(I32, (LANES, 2 * LANES), 0) // RET_DK
          == lax.broadcasted_iota(I32, (LANES, 2 * LANES), 1) // RET_DV)

    q = ret_ref[:, 0:128]
    k = ret_ref[:, 128:256]
    v = ret_ref[:, 256:512]
    lg_lane = _log_sigmoid(rl_lane_ref[...])
    lg_rows = _log_sigmoid(rl_rows_ref[...])
    lg_col = _log_sigmoid(rl_col_ref[...])
    ri4 = lax.broadcasted_iota(I32, (N_HEADS * c, c), 0) % c
    ci4 = lax.broadcasted_iota(I32, (N_HEADS * c, c), 1)
    att4 = (ri4 <= ci4) if rev else (ri4 >= ci4)
    dmat = jnp.where(att4, jnp.exp(jnp.abs(ri4 - ci4).astype(F32) * lg_rows), 0.0)
    s = lax.dot_general(_head_stack(q, RET_DK), k, NT_DIMS, preferred_element_type=F32)
    o = _head_select(jnp.dot((s * dmat).astype(BF16), v, preferred_element_type=F32), c)
    if rev:
        xi = jnp.exp((c - idx) * lg_lane)
        zeta = jnp.exp(idx * lg_lane)
    else:
        xi = jnp.exp((idx + 1.0) * lg_lane)
        zeta = jnp.exp((c - 1.0 - idx) * lg_lane)
    sr = sr_ref[...]
    o = o + jnp.dot((q.astype(F32) * xi).astype(BF16), sr.astype(BF16), preferred_element_type=F32)
    kz = (k.astype(F32) * zeta).astype(BF16)
    u = lax.dot_general(kz, v, TN_DIMS, preferred_element_type=F32)
    g_chunk = jnp.exp(float(c) * lg_col)
    sr_ref[...] = sr * jnp.concatenate([g_chunk, g_chunk], axis=1) + jnp.where(bd, u, 0.0)
    ore_ref[...] = o

    gq = gla_ref[:, 0:128].astype(F32)
    gk = gla_ref[:, 128:256].astype(F32)
    gv = gla_ref[:, 256:512]
    z = jnp.dot(glr_ref[...], gw_ref[...], precision=HIGHEST, preferred_element_type=F32) + gb_ref[...]
    la = _log_sigmoid(z) * (1.0 / GLA_TAU)
    b = jnp.dot(attends.astype(F32), la, precision=HIGHEST, preferred_element_type=F32)
    if rev:
        first = (ri // GLA_SUB) * GLA_SUB + (GLA_SUB - 1)
        ref_sel = ci >= first
    else:
        first = (ri // GLA_SUB) * GLA_SUB
        ref_sel = ci <= first
    refrow = jnp.dot(ref_sel.astype(F32), la, precision=HIGHEST, preferred_element_type=F32)
    qs = gq * jnp.exp(b - refrow)
    jcol = lax.broadcasted_iota(I32, (c, 1), 0)
    rr = lax.broadcasted_iota(I32, (N_HEADS * GLA_SUB, c), 0) % GLA_SUB
    cc = lax.broadcasted_iota(I32, (N_HEADS * GLA_SUB, c), 1)
    pieces = []
    for blk in range(c // GLA_SUB):
        lo = blk * GLA_SUB
        ref_b = refrow[lo:lo + 1]
        seen = (jcol >= lo) if rev else (jcol < lo + GLA_SUB)
        ks = (gk * jnp.exp(jnp.where(seen, ref_b - b, -jnp.inf))).astype(BF16)
        qz = _head_stack(qs[lo:lo + GLA_SUB], GLA_DK).astype(BF16)
        att = lax.dot_general(qz, ks, NT_DIMS, preferred_element_type=F32)
        ok = (cc >= rr + lo) if rev else (cc <= rr + lo)
        att = jnp.where(ok, att, 0.0).astype(BF16)
        pieces.append(_head_select(jnp.dot(att, gv, preferred_element_type=F32), GLA_SUB))
    og = jnp.concatenate(pieces, axis=0)
    sg = sg_ref[...]
    og = og + jnp.dot((gq * jnp.exp(b)).astype(BF16), sg.astype(BF16), preferred_element_type=F32)
    b_last = b[0:1] if rev else b[c - 1:c]
    kz = (gk * jnp.exp(b_last - b)).astype(BF16)
    u = lax.dot_general(kz, gv, TN_DIMS, preferred_element_type=F32)
    eye = lax.broadcasted_iota(I32, (LANES, LANES), 0) == lax.broadcasted_iota(I32, (LANES, LANES), 1)
    g_col = jnp.sum(jnp.where(eye, jnp.exp(b_last), 0.0), axis=1, keepdims=True)
    sg_ref[...] = sg * g_col + jnp.where(bd, u, 0.0)
    ogl_ref[...] = og


def _scan(rev, ret, gla, glr, ret_logit, gla_w, gla_b, n_batch, seq, ctx_len):
    n = ret.shape[0]
    c = CHUNK
    nc_ctx, nc_lat = ctx_len // c, seq // c
    n_steps = nc_ctx + nc_lat
    ctx_base = n_batch * nc_lat

    def blk(b, s):
        if rev:
            return (jnp.where(s < nc_ctx, ctx_base + b * nc_ctx + (nc_ctx - 1 - s), b * nc_lat + (n_steps - 1 - s)), 0)
        return (jnp.where(s < nc_ctx, ctx_base + b * nc_ctx + s, b * nc_lat + (s - nc_ctx)), 0)

    const = lambda b, s: (0, 0)
    rl_lane = jnp.repeat(ret_logit, RET_DK)[None, :]
    rl_rows = jnp.broadcast_to(jnp.repeat(ret_logit, c)[:, None], (N_HEADS * c, c))
    rl_col = jnp.broadcast_to(jnp.repeat(ret_logit, RET_DK)[:, None], (LANES, LANES))
    gw = jnp.zeros((LANES, LANES), F32).at[:GLA_LOWRANK].set(gla_w)
    return pl.pallas_call(
        functools.partial(_scan_kernel, rev),
        out_shape=(jax.ShapeDtypeStruct((n, 256), F32), jax.ShapeDtypeStruct((n, 256), F32)),
        grid=(n_batch, n_steps),
        in_specs=[
            pl.BlockSpec((c, 768), blk), pl.BlockSpec((c, 768), blk), pl.BlockSpec((c, LANES), blk),
            pl.BlockSpec(rl_lane.shape, const), pl.BlockSpec(rl_rows.shape, const),
            pl.BlockSpec(rl_col.shape, const), pl.BlockSpec(gw.shape, const), pl.BlockSpec((1, LANES), const),
        ],
        out_specs=(pl.BlockSpec((c, 256), blk), pl.BlockSpec((c, 256), blk)),
        scratch_shapes=[pltpu.VMEM((LANES, 2 * LANES), F32), pltpu.VMEM((LANES, 2 * LANES), F32)],
        compiler_params=_cparams("arbitrary", "arbitrary"),
        name="scan_bwd" if rev else "scan_fwd",
    )(ret, gla, glr, rl_lane, rl_rows, rl_col, gw, gla_b[None, :])


def _attn_kernel(n_lat_blocks, lam_init, *refs):
    if n_lat_blocks:
        q_ref, kc_ref, vtc_ref, kl_ref, vtl_ref, lp_ref, g_ref, o_ref, m_ref, acc_ref = refs
    else:
        q_ref, kc_ref, vtc_ref, lp_ref, g_ref, o_ref, m_ref, acc_ref = refs
    tq = q_ref.shape[0]
    q = q_ref[...]
    lane = lax.broadcasted_iota(I32, (1, LANES), 1)
    zero = jnp.zeros_like(q)
    qz = jnp.concatenate([jnp.where(lane < DIFF_DH, q, zero), jnp.where(lane >= DIFF_DH, q, zero)], axis=0)
    m_ref[...] = jnp.full(m_ref.shape, -jnp.inf, F32)
    acc_ref[...] = jnp.zeros_like(acc_ref)

    def step(kb, vtb):
        st = lax.dot_general(kb, qz, NT_DIMS, preferred_element_type=F32)
        m_prev = m_ref[...]
        m_new = jnp.maximum(m_prev, jnp.max(st, axis=0, keepdims=True))
        alpha = jnp.exp(m_prev - m_new)
        p = jnp.exp(st - m_new).astype(BF16)
        acc_ref[...] = alpha * acc_ref[...] + jnp.dot(vtb, p, preferred_element_type=F32)
        m_ref[...] = m_new

    step(kc_ref[...], vtc_ref[...])
    if n_lat_blocks:
        def body(j, carry):
            off = pl.multiple_of(j * ATTN_TK, ATTN_TK)
            step(kl_ref[pl.ds(off, ATTN_TK), :], vtl_ref[:, pl.ds(off, ATTN_TK)])
            return carry
        lax.fori_loop(0, n_lat_blocks, body, 0)

    lp = lp_ref[...]
    lam = (jnp.exp(jnp.sum(lp[0:1] * lp[1:2], axis=1, keepdims=True))
           - jnp.exp(jnp.sum(lp[2:3] * lp[3:4], axis=1, keepdims=True)) + lam_init)
    acc = acc_ref[...]
    o1 = acc[0:LANES, :tq] / acc[LANES:LANES + 1, :tq]
    o2 = acc[0:LANES, tq:] / acc[LANES:LANES + 1, tq:]
    ot = o1 - lam * o2
    ot = ot * lax.rsqrt(jnp.mean(ot * ot, axis=0, keepdims=True) + EPS) * g_ref[...][:, 0:1] * (1.0 - lam_init)
    o_ref[...] = ot.T.astype(o_ref.dtype)


def _diff_attention(dq, dk, vt, lp, g, lam_init, n_batch, seq, ctx_len, latent):
    ctx_blk0 = (n_batch * seq) // ctx_len
    vrows = LANES + ATTN_ONES
    kc_spec = pl.BlockSpec((ctx_len, LANES), lambda b, h, i: (ctx_blk0 + b, h))
    vtc_spec = pl.BlockSpec((vrows, ctx_len), lambda b, h, i: (h, ctx_blk0 + b))
    const = lambda b, h, i: (0, 0)
    if latent:
        tq = ATTN_TQ
        n_q = seq // tq
        q_map = lambda b, h, i: (b * n_q + i, h)
        in_specs = [pl.BlockSpec((tq, LANES), q_map), kc_spec, vtc_spec,
                    pl.BlockSpec((seq, LANES), lambda b, h, i: (b, h)),
                    pl.BlockSpec((vrows, seq), lambda b, h, i: (h, b))]
        args = (dq, dk, vt, dk, vt)
        n_rows, o_map, n_lat_blocks = n_batch * seq, q_map, seq // ATTN_TK
    else:
        tq, n_q = ctx_len, 1
        in_specs = [kc_spec, kc_spec, vtc_spec]
        args = (dq, dk, vt)
        n_rows, o_map, n_lat_blocks = n_batch * ctx_len, (lambda b, h, i: (b, h)), 0
    in_specs += [pl.BlockSpec(lp.shape, const), pl.BlockSpec((LANES, LANES), const)]
    return pl.pallas_call(
        functools.partial(_attn_kernel, n_lat_blocks, lam_init),
        out_shape=jax.ShapeDtypeStruct((n_rows, N_HEADS * LANES), BF16),
        grid=(n_batch, N_HEADS, n_q),
        in_specs=in_specs,
        out_specs=pl.BlockSpec((tq, LANES), o_map),
        scratch_shapes=[pltpu.VMEM((1, 2 * tq), F32), pltpu.VMEM((vrows, 2 * tq), F32)],
        compiler_params=_cparams("parallel", "parallel", "parallel"),
        name="diff_attention_lat" if latent else "diff_attention_ctx",
    )(*args, lp, g)


def _merge_kernel(tiles_per_batch, n_batch, orf_ref, orb_ref, ogf_ref, ogb_ref, rg_ref, gr_ref, dif_ref,
                  x_ref, mod_ref, gpost_ref, gpre_ref, glag_ref, wout_ref, wrt_ref,
                  xo_ref, hp_ref, sc_ref):
    d = x_ref.shape[-1]
    r = jnp.minimum(pl.program_id(0) // tiles_per_batch, n_batch)
    gate1 = mod_ref[pl.ds(r, 1), 2 * d:3 * d]
    shift2 = mod_ref[pl.ds(r, 1), 3 * d:4 * d]
    scale2 = mod_ref[pl.ds(r, 1), 4 * d:5 * d]
    gi = lax.broadcasted_iota(I32, (256, 256), 0) // RET_DV
    gj = lax.broadcasted_iota(I32, (256, 256), 1) // RET_DV
    group_mean = jnp.where(gi == gj, 1.0 / RET_DV, 0.0).astype(BF16)

    def head_norm(o):
        ms = jnp.dot((o * o).astype(BF16), group_mean, preferred_element_type=F32)
        return o * lax.rsqrt(ms + EPS)

    ret = head_norm(orf_ref[...] + orb_ref[...]) * _silu(rg_ref[...].astype(F32))
    gla = head_norm(ogf_ref[...] + ogb_ref[...]) * glag_ref[...] * _silu(gr_ref[...].astype(F32))
    m = (jnp.dot(ret.astype(BF16), wout_ref[0:256, :], preferred_element_type=F32)
         + jnp.dot(gla.astype(BF16), wout_ref[256:512, :], preferred_element_type=F32)
         + jnp.dot(dif_ref[...], wout_ref[512:1024, :], preferred_element_type=F32))
    x_new = x_ref[...] + gate1 * (_rms(m) * gpost_ref[...])
    xo_ref[...] = x_new
    h2 = _rms(x_new) * gpre_ref[...] * (1.0 + scale2) + shift2
    hp_ref[...] = _pack_halves(h2[:, :d // 2], h2[:, d // 2:])
    logits = lax.dot_general(wrt_ref[...], h2, NT_DIMS, precision=HIGHEST, preferred_element_type=F32)
    sc_ref[...] = 1.0 / (1.0 + jnp.exp(-logits))


def _merge(o_rf, o_rb, o_gf, o_gb, ret, gla, dif, xs, mod, g_post, g_pre_ffn, gla_g, w_out, w_rt,
           n_rows, n_batch, seq):
    d = xs.shape[-1]
    tm = TOKEN_TILE
    row = lambda i: (i, 0)
    gate_col = lambda i: (i, 2)
    const = lambda i: (0, 0)
    return pl.pallas_call(
        functools.partial(_merge_kernel, seq // tm, n_batch),
        out_shape=(jax.ShapeDtypeStruct((n_rows, d), F32), jax.ShapeDtypeStruct((n_rows, d // 2), U32),
                   jax.ShapeDtypeStruct((N_EXPERTS, n_rows), F32)),
        grid=(n_rows // tm,),
        in_specs=[
            pl.BlockSpec((tm, 256), row), pl.BlockSpec((tm, 256), row),
            pl.BlockSpec((tm, 256), row), pl.BlockSpec((tm, 256), row),
            pl.BlockSpec((tm, 256), gate_col), pl.BlockSpec((tm, 256), gate_col),
            pl.BlockSpec((tm, 512), row), pl.BlockSpec((tm, d), row),
            pl.BlockSpec(mod.shape, const), pl.BlockSpec((1, d), const), pl.BlockSpec((1, d), const),
            pl.BlockSpec((1, 256), const), pl.BlockSpec(w_out.shape, const), pl.BlockSpec(w_rt.shape, const),
        ],
        out_specs=(pl.BlockSpec((tm, d), row), pl.BlockSpec((tm, d // 2), row),
                   pl.BlockSpec((N_EXPERTS, tm), lambda i: (0, i))),
        compiler_params=_cparams("parallel"),
        name="merge",
    )(o_rf, o_rb, o_gf, o_gb, ret, gla, dif, xs, mod, g_post, g_pre_ffn, gla_g, w_out, w_rt)


def _route_kernel(sc_ref, bias_ref, eidx_ref, wts_ref, pos_ref, meta_ref):
    tt = sc_ref.shape[-1]
    scores = sc_ref[...]
    sel = scores + bias_ref[...][:, 0:1]
    sub = lax.broadcasted_iota(I32, (GROUP_SIZE, tt), 0)
    neg = -jnp.inf
    gscore = []
    for g in range(N_GROUPS):
        xg = sel[g * GROUP_SIZE:(g + 1) * GROUP_SIZE]
        m1 = jnp.max(xg, axis=0, keepdims=True)
        i1 = jnp.min(jnp.where(xg == m1, sub, GROUP_SIZE), axis=0, keepdims=True)
        m2 = jnp.max(jnp.where(sub == i1, neg, xg), axis=0, keepdims=True)
        gscore.append(m1 + m2)
    rows = []
    for g in range(N_GROUPS):
        rank = jnp.zeros((1, tt), I32)
        for o in range(N_GROUPS):
            if o == g:
                continue
            ahead = (gscore[o] >= gscore[g]) if o < g else (gscore[o] > gscore[g])
            rank = rank + ahead.astype(I32)
        rows.append(jnp.where(rank < TOPK_GROUPS, sel[g * GROUP_SIZE:(g + 1) * GROUP_SIZE], neg))
    masked = jnp.concatenate(rows, axis=0)
    eio = lax.broadcasted_iota(I32, (N_EXPERTS, tt), 0)
    member = jnp.zeros((N_EXPERTS, tt), F32)
    idxs, ws = [], []
    for _ in range(TOP_K):
        m = jnp.max(masked, axis=0, keepdims=True)
        i = jnp.min(jnp.where(masked == m, eio, N_EXPERTS), axis=0, keepdims=True)
        hit = eio == i
        idxs.append(i)
        ws.append(jnp.sum(jnp.where(hit, scores, 0.0), axis=0, keepdims=True))
        member = jnp.where(hit, 1.0, member)
        masked = jnp.where(hit, neg, masked)
    wsum = ws[0]
    for w in ws[1:]:
        wsum = wsum + w
    ti = lax.broadcasted_iota(I32, (tt, tt), 0)
    tj = lax.broadcasted_iota(I32, (tt, tt), 1)
    before = jnp.where(ti < tj, 1.0, 0.0).astype(BF16)
    rank_in_e = jnp.dot(member.astype(BF16), before, preferred_element_type=F32)
    cnt = jnp.sum(member, axis=1, keepdims=True)
    padded = jnp.floor((cnt + (SUBLANES - 1)) * (1.0 / SUBLANES)) * SUBLANES
    ei = lax.broadcasted_iota(I32, (N_EXPERTS, N_EXPERTS), 0)
    ej = lax.broadcasted_iota(I32, (N_EXPERTS, N_EXPERTS), 1)
    lower = jnp.where(ej < ei, 1.0, 0.0)
    off = jnp.dot(lower, jnp.broadcast_to(padded, (N_EXPERTS, LANES)), precision=HIGHEST,
                  preferred_element_type=F32)
    slot = rank_in_e + off[:, 0:1]
    zrow_i = jnp.zeros((SUBLANES - TOP_K, tt), I32)
    zrow_f = jnp.zeros((SUBLANES - TOP_K, tt), F32)
    pos = [jnp.sum(jnp.where(eio == i, slot, 0.0), axis=0, keepdims=True).astype(I32) for i in idxs]
    eidx_ref[...] = jnp.concatenate(idxs + [zrow_i], axis=0)
    wts_ref[...] = jnp.concatenate([w / wsum * ROUTED_SCALE for w in ws] + [zrow_f], axis=0)
    pos_ref[...] = jnp.concatenate(pos + [zrow_i], axis=0)
    meta_ref[0] = jnp.concatenate([jnp.broadcast_to(cnt, (N_EXPERTS, LANES)), off], axis=1).astype(I32)


def _route(scores_t, bias, tt, row0, n):
    tile0 = row0 // tt
    tok = lambda i: (0, i)
    return pl.pallas_call(
        _route_kernel,
        out_shape=(jax.ShapeDtypeStruct((SUBLANES, n), I32), jax.ShapeDtypeStruct((SUBLANES, n), F32),
                   jax.ShapeDtypeStruct((SUBLANES, n), I32),
                   jax.ShapeDtypeStruct((n // tt, N_EXPERTS, 2 * LANES), I32)),
        grid=(n // tt,),
        in_specs=[pl.BlockSpec((N_EXPERTS, tt), lambda i: (0, i + tile0)),
                  pl.BlockSpec((N_EXPERTS, LANES), lambda i: (0, 0))],
        out_specs=(pl.BlockSpec((SUBLANES, tt), tok), pl.BlockSpec((SUBLANES, tt), tok),
                   pl.BlockSpec((SUBLANES, tt), tok),
                   pl.BlockSpec((1, N_EXPERTS, 2 * LANES), lambda i: (i, 0, 0))),
        compiler_params=_cparams("parallel"),
        name="route",
    )(scores_t, jnp.broadcast_to(bias[:, None], (N_EXPERTS, LANES)))


def _moe_kernel(seq, n_batch, tile0, has_prev, pos_ref, wts_ref, meta_ref,
                hp_ref, x_ref, mod_ref, gpost_ref, weg_ref, weu_ref, wed_ref, wsg_ref, wsu_ref, wsd_ref, *refs):
    o_ref, xg_ref, y_ref, rt_ref = refs[1:] if has_prev else refs
    tt, half = hp_ref.shape
    d = 2 * half
    e = pl.program_id(1)
    cnt = meta_ref[0, e]
    off = meta_ref[1, e]

    @pl.when((pl.program_id(0) == 0) & (e == 0))
    def _():
        def zero(i, carry):
            for u in range(SUBLANES):
                rt_ref[i * SUBLANES + u] = 0
            return carry
        lax.fori_loop(0, rt_ref.shape[0] // SUBLANES, zero, 0)

    @pl.when(e == 0)
    def _():
        def scatter(t, carry):
            for k in range(TOP_K):
                rt_ref[pos_ref[t * SUBLANES + k]] = t
            return carry
        lax.fori_loop(0, tt, scatter, 0)

    def expert_block(j, carry):
        base = pl.multiple_of(off + j * MOE_BLOCK, SUBLANES)
        for i in range(MOE_BLOCK):
            t = rt_ref[base + i]
            xg_ref[pl.ds(i, 1), :] = hp_ref[pl.ds(t, 1), :]
        xa, xb = _unpack_halves(xg_ref[...])
        xa, xb = xa.astype(BF16), xb.astype(BF16)
        hg = (jnp.dot(xa, weg_ref[0, 0:half, :], preferred_element_type=F32)
              + jnp.dot(xb, weg_ref[0, half:d, :], preferred_element_type=F32))
        hu = (jnp.dot(xa, weu_ref[0, 0:half, :], preferred_element_type=F32)
              + jnp.dot(xb, weu_ref[0, half:d, :], preferred_element_type=F32))
        y = jnp.dot((_silu(hg) * hu).astype(BF16), wed_ref[0], preferred_element_type=F32)
        y_ref[pl.ds(base, MOE_BLOCK), :] = _pack_halves(y[:, :half], y[:, half:])
        return carry

    lax.fori_loop(0, (cnt + (MOE_BLOCK - 1)) // MOE_BLOCK, expert_block, 0)

    @pl.when(e == N_EXPERTS - 1)
    def _():
        def tokens(g, carry):
            for u in range(COMBINE_UNROLL):
                t = g * COMBINE_UNROLL + u
                acc_a = jnp.zeros((1, half), F32)
                acc_b = jnp.zeros((1, half), F32)
                for k in range(TOP_K):
                    r = pos_ref[t * SUBLANES + k]
                    w = wts_ref[t * SUBLANES + k]
                    ya, yb = _unpack_halves(y_ref[pl.ds(r, 1), :])
                    acc_a = acc_a + w * ya
                    acc_b = acc_b + w * yb
                o_ref[pl.ds(t, 1), 0:half] = acc_a
                o_ref[pl.ds(t, 1), half:d] = acc_b
            return carry

        lax.fori_loop(0, tt // COMBINE_UNROLL, tokens, 0)
        rows = (pl.program_id(0) + tile0) * tt + lax.broadcasted_iota(I32, (tt, 1), 0)
        rb = jnp.minimum(rows // seq, n_batch)
        gate2 = jnp.zeros((tt, d), F32)
        for bi in range(n_batch + 1):
            gate2 = jnp.where(rb == bi, mod_ref[bi:bi + 1, 5 * d:6 * d], gate2)
        xa, xb = _unpack_halves(hp_ref[...])
        xa, xb = xa.astype(BF16), xb.astype(BF16)
        hg = (jnp.dot(xa, wsg_ref[0:half, :], preferred_element_type=F32)
              + jnp.dot(xb, wsg_ref[half:d, :], preferred_element_type=F32))
        hu = (jnp.dot(xa, wsu_ref[0:half, :], preferred_element_type=F32)
              + jnp.dot(xb, wsu_ref[half:d, :], preferred_element_type=F32))
        f = jnp.dot((_silu(hg) * hu).astype(BF16), wsd_ref[...], preferred_element_type=F32) + o_ref[...]
        o_ref[...] = x_ref[...] + gate2 * (_rms(f) * gpost_ref[...])


def _moe_segment(prev, scores_t, hp, xs, mod, router_bias, g_post, weights, row0, n, tt, n_batch, seq):
    n_all, half = hp.shape
    d = 2 * half
    w_eg, w_eu, w_ed, w_sg, w_su, w_sd = weights
    de = w_eg.shape[-1]
    n_tiles, tile0 = n // tt, row0 // tt
    n_slots = -(-(tt * TOP_K + N_EXPERTS * (SUBLANES - 1) + MOE_BLOCK) // (SUBLANES * LANES)) * SUBLANES * LANES
    e_idx, wts, pos, meta = _route(scores_t, router_bias, tt, row0, n)
    cnt, off = meta[:, :, 0], meta[:, :, LANES]

    def per_token(a):
        return a.T.reshape(-1)

    meta_s = jnp.zeros((n_tiles, SUBLANES, LANES), I32)
    meta_s = meta_s.at[:, 0, :N_EXPERTS].set(cnt).at[:, 1, :N_EXPERTS].set(off).reshape(-1, LANES)

    tile = lambda i, e: (i + tile0, 0)
    const = lambda i, e: (0, 0)
    expert = lambda i, e: (e, 0, 0)
    smem = functools.partial(pl.BlockSpec, memory_space=pltpu.SMEM)
    in_specs = [
        smem((SUBLANES * tt,), lambda i, e: (i,)), smem((SUBLANES * tt,), lambda i, e: (i,)),
        smem((SUBLANES, LANES), lambda i, e: (i, 0)),
        pl.BlockSpec((tt, half), tile), pl.BlockSpec((tt, d), tile),
        pl.BlockSpec(mod.shape, const), pl.BlockSpec((1, d), const),
        pl.BlockSpec((1, d, de), expert), pl.BlockSpec((1, d, de), expert), pl.BlockSpec((1, de, d), expert),
        pl.BlockSpec(w_sg.shape, const), pl.BlockSpec(w_su.shape, const), pl.BlockSpec(w_sd.shape, const),
    ]
    args = [per_token(pos), per_token(wts), meta_s, hp, xs, mod, g_post, w_eg, w_eu, w_ed, w_sg, w_su, w_sd]
    aliases = {}
    if prev is not None:
        in_specs.append(pl.BlockSpec(memory_space=pl.ANY))
        args.append(prev)
        aliases = {len(args) - 1: 0}
    return pl.pallas_call(
        functools.partial(_moe_kernel, seq, n_batch, tile0, prev is not None),
        out_shape=jax.ShapeDtypeStruct((n_all, d), F32),
        grid=(n_tiles, N_EXPERTS),
        in_specs=in_specs,
        out_specs=pl.BlockSpec((tt, d), tile),
        scratch_shapes=[pltpu.VMEM((MOE_BLOCK, half), U32), pltpu.VMEM((n_slots, half), U32),
                        pltpu.SMEM((n_slots,), I32)],
        input_output_aliases=aliases,
        compiler_params=_cparams("arbitrary", "arbitrary"),
        name="moe",
    )(*args)


def _moe_layer(scores_t, hp, xs, mod, router_bias, g_post, weights, n_lat, n_batch, seq):
    pick = lambda rows: next(t for t in MOE_TILES if rows % t == 0)
    out = _moe_segment(None, scores_t, hp, xs, mod, router_bias, g_post, weights, 0, n_lat, pick(n_lat),
                       n_batch, seq)
    n_ctx = hp.shape[0] - n_lat
    if n_ctx:
        tt = next(t for t in MOE_TILES if n_ctx % t == 0 and n_lat % t == 0)
        out = _moe_segment(out, scores_t, hp, xs, mod, router_bias, g_post, weights, n_lat, n_ctx, tt,
                           n_batch, seq)
    return out


def kernel(x, c, ctx, c_ctx, w_mod, b_mod, g_pre_mix, g_post_mix, g_pre_ffn, g_post_ffn, w_in, w_out,
           ret_decay_logit, gla_w_gate, gla_b_gate, gla_norm_g, diff_lambda, diff_norm_g,
           w_router, router_bias, w_exp_gate, w_exp_up, w_exp_down, w_sh_gate, w_sh_up, w_sh_down):
    n_batch, seq, d = x.shape
    ctx_len = ctx.shape[1]
    depth = w_mod.shape[0]
    n_lat = n_batch * seq
    assert seq % TOKEN_TILE == 0 and (n_batch * ctx_len) % TOKEN_TILE == 0 and n_batch < SUBLANES
    assert seq % ATTN_TK == 0 and seq % CHUNK == 0 and ctx_len % CHUNK == 0 and n_lat % ctx_len == 0

    xs = jnp.concatenate([x.reshape(n_lat, d), ctx.reshape(n_batch * ctx_len, d)], axis=0)
    cond = jnp.zeros((SUBLANES, d), F32).at[:n_batch].set(c).at[n_batch].set(c_ctx)
    mods = _modulation(cond, w_mod, b_mod)
    tables = _rope_tables(seq, RET_DK, TOKEN_TILE) + _rope_tables(seq, DIFF_DH, TOKEN_TILE)
    lr0 = N_HEADS * (2 * RET_DK + 2 * RET_DV + 2 * GLA_DK + 2 * GLA_DV)
    row = lambda a: a[None, :]

    for layer in range(depth):
        need_ctx = layer < depth - 1
        lam_init = 0.8 - 0.6 * math.exp(-0.3 * layer)
        mod = mods[layer]
        wl = w_in[layer]
        w_r = jnp.concatenate([wl[:, :lr0], wl[:, lr0:lr0 + GLA_LOWRANK],
                               jnp.zeros((d, LANES - GLA_LOWRANK), F32), wl[:, lr0 + GLA_LOWRANK:]],
                              axis=1).astype(BF16)
        ret, gla, glr, dq, dk, dv = _in_projection(xs, mod, row(g_pre_mix[layer]), w_r, tables, n_batch, seq)

        scan = functools.partial(_scan, ret=ret, gla=gla, glr=glr, n_batch=n_batch, seq=seq, ctx_len=ctx_len)
        o_rf, o_gf = scan(False, ret_logit=ret_decay_logit[layer, 0], gla_w=gla_w_gate[layer, 0],
                          gla_b=gla_b_gate[layer, 0])
        o_rb, o_gb = scan(True, ret_logit=ret_decay_logit[layer, 1], gla_w=gla_w_gate[layer, 1],
                          gla_b=gla_b_gate[layer, 1])

        lp = jnp.zeros((SUBLANES, LANES), F32).at[:4, :DIFF_DH].set(diff_lambda[layer])
        n_all = dv.shape[0]
        vt = jnp.concatenate([dv.T.reshape(N_HEADS, LANES, n_all), jnp.ones((N_HEADS, ATTN_ONES, n_all), BF16)],
                             axis=1).reshape(N_HEADS * (LANES + ATTN_ONES), n_all)
        g_col = jnp.broadcast_to(diff_norm_g[layer][:, None], (LANES, LANES))
        attn = functools.partial(_diff_attention, dq, dk, vt, lp, g_col, lam_init, n_batch, seq, ctx_len)
        dif = attn(True)
        n_rows = n_lat
        if need_ctx:
            dif = jnp.concatenate([dif, attn(False)], axis=0)
            n_rows = xs.shape[0]

        xs, hp, scores_t = _merge(o_rf, o_rb, o_gf, o_gb, ret, gla, dif, xs, mod, row(g_post_mix[layer]),
                                  row(g_pre_ffn[layer]), row(jnp.tile(gla_norm_g[layer], N_HEADS)),
                                  w_out[layer].astype(BF16), w_router[layer].T, n_rows, n_batch, seq)
        weights = tuple(w[layer].astype(BF16) for w in
                        (w_exp_gate, w_exp_up, w_exp_down, w_sh_gate, w_sh_up, w_sh_down))
        xs = _moe_layer(scores_t, hp, xs, mod, router_bias[layer], row(g_post_ffn[layer]), weights,
                        n_lat, n_batch, seq)
    return xs[:n_lat].reshape(n_batch, seq, d)
```

```python
import functools
import math

import jax
import jax.numpy as jnp
from jax import lax
from jax.experimental import pallas as pl
from jax.experimental.pallas import tpu as pltpu

F32 = jnp.float32
BF16 = jnp.bfloat16
I32 = jnp.int32
U32 = jnp.uint32

GRID_W = 64
CHUNK = 128
N_HEADS = 4
RET_DK, RET_DV = 32, 64
GLA_DK, GLA_DV = 32, 64
GLA_LOWRANK = 16
GLA_TAU = 16.0
DIFF_DH = 64
ROPE_BASE = 10000.0
N_EXPERTS = 64
TOP_K = 6
N_GROUPS = 8
TOPK_GROUPS = 4
GROUP_SIZE = N_EXPERTS // N_GROUPS
ROUTED_SCALE = 2.5
EPS = 1e-6
GLA_SUB = 16

LANES = 128
SUBLANES = 8
VMEM_LIMIT_BYTES = 56 * 1024 * 1024

TOKEN_TILE = 512
ATTN_TQ = 256
ATTN_TK = 512
ATTN_ONES = 16
MOE_TILES = (1024, 512)
MOE_BLOCK = 128
EXPERT_GROUP = 4

HIGHEST = lax.Precision.HIGHEST
NT_DIMS = (((1,), (1,)), ((), ()))
TN_DIMS = (((0,), (0,)), ((), ()))


def _cparams(*sem):
    return pltpu.CompilerParams(dimension_semantics=sem, vmem_limit_bytes=VMEM_LIMIT_BYTES)


def _log_sigmoid(x):
    return jnp.minimum(x, 0.0) - jnp.log(1.0 + jnp.exp(-jnp.abs(x)))


def _silu(x):
    return x * (1.0 / (1.0 + jnp.exp(-x)))


def _rms(x):
    return x * lax.rsqrt(jnp.mean(x * x, axis=-1, keepdims=True) + EPS)


def _pack_halves(a, b):
    ua = lax.bitcast_convert_type(a.astype(BF16).astype(F32), U32)
    ub = lax.bitcast_convert_type(b.astype(BF16).astype(F32), U32)
    return (ua & jnp.uint32(0xFFFF0000)) | (ub >> 16)


def _unpack_halves(w):
    a = lax.bitcast_convert_type(w & jnp.uint32(0xFFFF0000), F32)
    b = lax.bitcast_convert_type(w << 16, F32)
    return a, b


def _mod_kernel(cond_ref, w_ref, b_ref, o_ref):
    a = _silu(cond_ref[...])
    o_ref[0] = jnp.dot(a, w_ref[0], precision=HIGHEST, preferred_element_type=F32) + b_ref[0]


def _modulation(cond, w_mod, b_mod):
    n_layers, d, d6 = w_mod.shape
    tn = 1024
    return pl.pallas_call(
        _mod_kernel,
        out_shape=jax.ShapeDtypeStruct((n_layers, SUBLANES, d6), F32),
        grid=(n_layers, d6 // tn),
        in_specs=[
            pl.BlockSpec((SUBLANES, d), lambda l, j: (0, 0)),
            pl.BlockSpec((1, d, tn), lambda l, j: (l, 0, j)),
            pl.BlockSpec((1, 1, tn), lambda l, j: (l, 0, j)),
        ],
        out_specs=pl.BlockSpec((1, SUBLANES, tn), lambda l, j: (l, 0, j)),
        compiler_params=_cparams("parallel", "parallel"),
        name="modulation",
    )(cond, w_mod, b_mod.reshape(n_layers, 1, d6))


def _rope(x, cos, sin, quarter):
    lane = lax.broadcasted_iota(I32, (1, LANES), 1)
    first = (lane % (2 * quarter)) < quarter
    outs = []
    for c in range(x.shape[-1] // LANES):
        xc = x[:, c * LANES:(c + 1) * LANES]
        partner = jnp.where(first, pltpu.roll(xc, LANES - quarter, 1), pltpu.roll(xc, quarter, 1))
        outs.append(xc * cos + partner * sin)
    return outs[0] if len(outs) == 1 else jnp.concatenate(outs, axis=-1)


def _inproj_kernel(tiles_per_batch, n_batch, x_ref, mod_ref, g_ref, w_ref, c32_ref, s32_ref, c64_ref, s64_ref,
                   ret_ref, gla_ref, glr_ref, dq_ref, dk_ref, dv_ref):
    d = x_ref.shape[-1]
    r = jnp.minimum(pl.program_id(0) // tiles_per_batch, n_batch)
    shift = mod_ref[pl.ds(r, 1), 0:d]
    scale = mod_ref[pl.ds(r, 1), d:2 * d]
    h = (_rms(x_ref[...]) * g_ref[...] * (1.0 + scale) + shift).astype(BF16)

    def proj(lo, hi):
        return jnp.dot(h, w_ref[:, lo:hi], preferred_element_type=F32)

    c32, s32 = c32_ref[...], s32_ref[...]
    c64, s64 = c64_ref[...], s64_ref[...]
    ret = proj(0, 768)
    ret_ref[:, 0:128] = _rope(ret[:, 0:128], c32, s32, RET_DK // 4).astype(BF16)
    ret_ref[:, 128:256] = (_rope(ret[:, 128:256], c32, s32, RET_DK // 4) * RET_DK ** -0.5).astype(BF16)
    ret_ref[:, 256:768] = ret[:, 256:768].astype(BF16)
    gla = proj(768, 1536)
    gla_ref[:, 0:128] = (gla[:, 0:128] * GLA_DK ** -0.5).astype(BF16)
    gla_ref[:, 128:768] = gla[:, 128:768].astype(BF16)
    glr_ref[...] = proj(1536, 1664)
    dq_ref[...] = (_rope(proj(1664, 2176), c64, s64, DIFF_DH // 4) * DIFF_DH ** -0.5).astype(BF16)
    dk_ref[...] = _rope(proj(2176, 2688), c64, s64, DIFF_DH // 4).astype(BF16)
    dv_ref[...] = proj(2688, 3200).astype(BF16)


def _in_projection(xs, mod, g_pre, w_r, tables, n_batch, seq):
    n, d = xs.shape
    tm = TOKEN_TILE
    tiles_per_batch = seq // tm
    n_lat_tiles = n_batch * tiles_per_batch
    c32, s32, c64, s64 = tables

    def tab_map(i):
        return (jnp.where(i < n_lat_tiles, i % tiles_per_batch, tiles_per_batch), 0)

    row = lambda i: (i, 0)
    const = lambda i: (0, 0)
    tab_spec = pl.BlockSpec((tm, LANES), tab_map)
    return pl.pallas_call(
        functools.partial(_inproj_kernel, tiles_per_batch, n_batch),
        out_shape=(
            jax.ShapeDtypeStruct((n, 768), BF16), jax.ShapeDtypeStruct((n, 768), BF16),
            jax.ShapeDtypeStruct((n, LANES), F32),
            jax.ShapeDtypeStruct((n, 512), BF16), jax.ShapeDtypeStruct((n, 512), BF16),
            jax.ShapeDtypeStruct((n, 512), BF16)),
        grid=(n // tm,),
        in_specs=[
            pl.BlockSpec((tm, d), row),
            pl.BlockSpec(mod.shape, const),
            pl.BlockSpec((1, d), const),
            pl.BlockSpec(w_r.shape, const),
            tab_spec, tab_spec, tab_spec, tab_spec,
        ],
        out_specs=(
            pl.BlockSpec((tm, 768), row), pl.BlockSpec((tm, 768), row), pl.BlockSpec((tm, LANES), row),
            pl.BlockSpec((tm, 512), row), pl.BlockSpec((tm, 512), row), pl.BlockSpec((tm, 512), row)),
        compiler_params=_cparams("parallel"),
        name="in_projection",
    )(xs, mod, g_pre, w_r, c32, s32, c64, s64)


def _rope_tables(seq, head_dim, extra_rows):
    half, quarter = head_dim // 2, head_dim // 4
    freqs = ROPE_BASE ** (-jnp.arange(quarter, dtype=F32) / quarter)
    t = jnp.arange(seq)
    row = (t // GRID_W).astype(F32)
    col = (t % GRID_W).astype(F32)
    j = jnp.arange(LANES) % head_dim
    jj = j % half
    pos = jnp.where((j < half)[None, :], row[:, None], col[:, None])
    ang = pos * freqs[jj % quarter][None, :]
    cos = jnp.cos(ang)
    sin = jnp.sin(ang) * jnp.where(jj < quarter, -1.0, 1.0)[None, :]
    cos = jnp.concatenate([cos, jnp.ones((extra_rows, LANES), F32)], axis=0)
    sin = jnp.concatenate([sin, jnp.zeros((extra_rows, LANES), F32)], axis=0)
    return cos, sin


def _head_stack(x, width):
    lane = lax.broadcasted_iota(I32, (1, x.shape[-1]), 1)
    zero = jnp.zeros_like(x)
    return jnp.concatenate([jnp.where(lane // width == h, x, zero) for h in range(N_HEADS)], axis=0)


def _head_select(x4, rows):
    lane = lax.broadcasted_iota(I32, (1, x4.shape[-1]), 1)
    out = jnp.zeros((rows, x4.shape[-1]), F32)
    for h in range(N_HEADS):
        out = out + jnp.where(lane // RET_DV == h, x4[h * rows:(h + 1) * rows], 0.0)
    return out


def _scan_kernel(rev, ret_ref, gla_ref, glr_ref, rl_lane_ref, rl_rows_ref, rl_col_ref, gw_ref, gb_ref,
                 ore_ref, ogl_ref, sr_ref, sg_ref):
    c = CHUNK

    @pl.when(pl.program_id(1) == 0)
    def _():
        sr_ref[...] = jnp.zeros_like(sr_ref)
        sg_ref[...] = jnp.zeros_like(sg_ref)

    ri = lax.broadcasted_iota(I32, (c, c), 0)
    ci = lax.broadcasted_iota(I32, (c, c), 1)
    attends = (ri <= ci) if rev else (ri >= ci)
    idx = lax.broadcasted_iota(I32, (c, 1), 0).astype(F32)
    bd = (lax.broadcasted_iota(I32, (LANES, 2 * LANES), 0) // RET_DK
          == lax.broadcasted_iota(I32, (LANES, 2 * LANES), 1) // RET_DV)

    q = ret_ref[:, 0:128]
    k = ret_ref[:, 128:256]
    v = ret_ref[:, 256:512]
    lg_lane = _log_sigmoid(rl_lane_ref[...])
    lg_rows = _log_sigmoid(rl_rows_ref[...])
    lg_col = _log_sigmoid(rl_col_ref[...])
    ri4 = lax.broadcasted_iota(I32, (N_HEADS * c, c), 0) % c
    ci4 = lax.broadcasted_iota(I32, (N_HEADS * c, c), 1)
    att4 = (ri4 <= ci4) if rev else (ri4 >= ci4)
    dmat = jnp.where(att4, jnp.exp(jnp.abs(ri4 - ci4).astype(F32) * lg_rows), 0.0)
    s = lax.dot_general(_head_stack(q, RET_DK), k, NT_DIMS, preferred_element_type=F32)
    o = _head_select(jnp.dot((s * dmat).astype(BF16), v, preferred_element_type=F32), c)
    if rev:
        xi = jnp.exp((c - idx) * lg_lane)
        zeta = jnp.exp(idx * lg_lane)
    else:
        xi = jnp.exp((idx + 1.0) * lg_lane)
        zeta = jnp.exp((c - 1.0 - idx) * lg_lane)
    sr = sr_ref[...]
    o = o + jnp.dot((q.astype(F32) * xi).astype(BF16), sr.astype(BF16), preferred_element_type=F32)
    kz = (k.astype(F32) * zeta).astype(BF16)
    u = lax.dot_general(kz, v, TN_DIMS, preferred_element_type=F32)
    g_chunk = jnp.exp(float(c) * lg_col)
    sr_ref[...] = sr * jnp.concatenate([g_chunk, g_chunk], axis=1) + jnp.where(bd, u, 0.0)
    ore_ref[...] = o

    gq = gla_ref[:, 0:128].astype(F32)
    gk = gla_ref[:, 128:256].astype(F32)
    gv = gla_ref[:, 256:512]
    z = jnp.dot(glr_ref[...], gw_ref[...], precision=HIGHEST, preferred_element_type=F32) + gb_ref[...]
    la = _log_sigmoid(z) * (1.0 / GLA_TAU)
    b = jnp.dot(attends.astype(F32), la, precision=HIGHEST, preferred_element_type=F32)
    if rev:
        first = (ri // GLA_SUB) * GLA_SUB + (GLA_SUB - 1)
        ref_sel = ci >= first
    else:
        first = (ri // GLA_SUB) * GLA_SUB
        ref_sel = ci <= first
    refrow = jnp.dot(ref_sel.astype(F32), la, precision=HIGHEST, preferred_element_type=F32)
    qs = gq * jnp.exp(b - refrow)
    jcol = lax.broadcasted_iota(I32, (c, 1), 0)
    rr = lax.broadcasted_iota(I32, (N_HEADS * GLA_SUB, c), 0) % GLA_SUB
    cc = lax.broadcasted_iota(I32, (N_HEADS * GLA_SUB, c), 1)
    pieces = []
    for blk in range(c // GLA_SUB):
        lo = blk * GLA_SUB
        ref_b = refrow[lo:lo + 1]
        seen = (jcol >= lo) if rev else (jcol < lo + GLA_SUB)
        ks = (gk * jnp.exp(jnp.where(seen, ref_b - b, -jnp.inf))).astype(BF16)
        qz = _head_stack(qs[lo:lo + GLA_SUB], GLA_DK).astype(BF16)
        att = lax.dot_general(qz, ks, NT_DIMS, preferred_element_type=F32)
        ok = (cc >= rr + lo) if rev else (cc <= rr + lo)
        att = jnp.where(ok, att, 0.0).astype(BF16)
        pieces.append(_head_select(jnp.dot(att, gv, preferred_element_type=F32), GLA_SUB))
    og = jnp.concatenate(pieces, axis=0)
    sg = sg_ref[...]
    og = og + jnp.dot((gq * jnp.exp(b)).astype(BF16), sg.astype(BF16), preferred_element_type=F32)
    b_last = b[0:1] if rev else b[c - 1:c]
    kz = (gk * jnp.exp(b_last - b)).astype(BF16)
    u = lax.dot_general(kz, gv, TN_DIMS, preferred_element_type=F32)
    eye = lax.broadcasted_iota(I32, (LANES, LANES), 0) == lax.broadcasted_iota(I32, (LANES, LANES), 1)
    g_col = jnp.sum(jnp.where(eye, jnp.exp(b_last), 0.0), axis=1, keepdims=True)
    sg_ref[...] = sg * g_col + jnp.where(bd, u, 0.0)
    ogl_ref[...] = og


def _scan(rev, ret, gla, glr, ret_logit, gla_w, gla_b, n_batch, seq, ctx_len):
    n = ret.shape[0]
    c = CHUNK
    nc_ctx, nc_lat = ctx_len // c, seq // c
    n_steps = nc_ctx + nc_lat
    ctx_base = n_batch * nc_lat

    def blk(b, s):
        if rev:
            return (jnp.where(s < nc_ctx, ctx_base + b * nc_ctx + (nc_ctx - 1 - s), b * nc_lat + (n_steps - 1 - s)), 0)
        return (jnp.where(s < nc_ctx, ctx_base + b * nc_ctx + s, b * nc_lat + (s - nc_ctx)), 0)

    const = lambda b, s: (0, 0)
    rl_lane = jnp.repeat(ret_logit, RET_DK)[None, :]
    rl_rows = jnp.broadcast_to(jnp.repeat(ret_logit, c)[:, None], (N_HEADS * c, c))
    rl_col = jnp.broadcast_to(jnp.repeat(ret_logit, RET_DK)[:, None], (LANES, LANES))
    gw = jnp.zeros((LANES, LANES), F32).at[:GLA_LOWRANK].set(gla_w)
    return pl.pallas_call(
        functools.partial(_scan_kernel, rev),
        out_shape=(jax.ShapeDtypeStruct((n, 256), F32), jax.ShapeDtypeStruct((n, 256), F32)),
        grid=(n_batch, n_steps),
        in_specs=[
            pl.BlockSpec((c, 768), blk), pl.BlockSpec((c, 768), blk), pl.BlockSpec((c, LANES), blk),
            pl.BlockSpec(rl_lane.shape, const), pl.BlockSpec(rl_rows.shape, const),
            pl.BlockSpec(rl_col.shape, const), pl.BlockSpec(gw.shape, const), pl.BlockSpec((1, LANES), const),
        ],
        out_specs=(pl.BlockSpec((c, 256), blk), pl.BlockSpec((c, 256), blk)),
        scratch_shapes=[pltpu.VMEM((LANES, 2 * LANES), F32), pltpu.VMEM((LANES, 2 * LANES), F32)],
        compiler_params=_cparams("arbitrary", "arbitrary"),
        name="scan_bwd" if rev else "scan_fwd",
    )(ret, gla, glr, rl_lane, rl_rows, rl_col, gw, gla_b[None, :])


def _attn_kernel(n_lat_blocks, lam_init, *refs):
    if n_lat_blocks:
        q_ref, kc_ref, vtc_ref, kl_ref, vtl_ref, lp_ref, g_ref, o_ref, m_ref, acc_ref, st_ref = refs
    else:
        q_ref, kc_ref, vtc_ref, lp_ref, g_ref, o_ref, m_ref, acc_ref, st_ref = refs
    tq = q_ref.shape[0]
    q = q_ref[...].astype(F32)
    lane = lax.broadcasted_iota(I32, (1, LANES), 1)
    qt = jnp.concatenate([jnp.where(lane < DIFF_DH, q, 0.0).T, jnp.where(lane >= DIFF_DH, q, 0.0).T],
                         axis=1).astype(BF16)
    m_ref[...] = jnp.full(m_ref.shape, -jnp.inf, F32)
    acc_ref[...] = jnp.zeros_like(acc_ref)

    def scores(slot, kb):
        st_ref[slot, 0:kb.shape[0], :] = jnp.dot(kb, qt, preferred_element_type=F32)

    def absorb(slot, vtb):
        st = st_ref[slot, 0:vtb.shape[1], :]
        m_prev = m_ref[...]
        m_new = jnp.maximum(m_prev, jnp.max(st, axis=0, keepdims=True))
        alpha = jnp.exp(m_prev - m_new)
        p = jnp.exp(st - m_new).astype(BF16)
        acc_ref[...] = alpha * acc_ref[...] + jnp.dot(vtb, p, preferred_element_type=F32)
        m_ref[...] = m_new

    def k_lat(blk):
        return kl_ref[pl.ds(pl.multiple_of(blk * ATTN_TK, ATTN_TK), ATTN_TK), :]

    def vt_lat(blk):
        return vtl_ref[:, pl.ds(pl.multiple_of(blk * ATTN_TK, ATTN_TK), ATTN_TK)]

    scores(0, kc_ref[...])
    if not n_lat_blocks:
        absorb(0, vtc_ref[...])
    else:
        scores(1, k_lat(0))
        absorb(0, vtc_ref[...])
        n_pairs = (n_lat_blocks - 1) // 2

        def body(i, carry):
            scores(0, k_lat(2 * i + 1))
            absorb(1, vt_lat(2 * i))
            scores(1, k_lat(2 * i + 2))
            absorb(0, vt_lat(2 * i + 1))
            return carry
        lax.fori_loop(0, n_pairs, body, 0)
        done = 2 * n_pairs
        if n_lat_blocks - done == 2:
            scores(0, k_lat(done + 1))
            absorb(1, vt_lat(done))
            absorb(0, vt_lat(done + 1))
        else:
            absorb(1, vt_lat(done))

    lp = lp_ref[...]
    lam = (jnp.exp(jnp.sum(lp[0:1] * lp[1:2], axis=1, keepdims=True))
           - jnp.exp(jnp.sum(lp[2:3] * lp[3:4], axis=1, keepdims=True)) + lam_init)
    acc = acc_ref[...]
    o1 = acc[0:LANES, :tq] / acc[LANES:LANES + 1, :tq]
    o2 = acc[0:LANES, tq:] / acc[LANES:LANES + 1, tq:]
    ot = o1 - lam * o2
    ot = ot * lax.rsqrt(jnp.mean(ot * ot, axis=0, keepdims=True) + EPS) * g_ref[...][:, 0:1] * (1.0 - lam_init)
    o_ref[...] = ot.T.astype(o_ref.dtype)


def _diff_attention(dq, dk, vt, lp, g, lam_init, n_batch, seq, ctx_len, latent):
    ctx_blk0 = (n_batch * seq) // ctx_len
    vrows = LANES + ATTN_ONES
    kc_spec = pl.BlockSpec((ctx_len, LANES), lambda b, h, i: (ctx_blk0 + b, h))
    vtc_spec = pl.BlockSpec((vrows, ctx_len), lambda b, h, i: (h, ctx_blk0 + b))
    const = lambda b, h, i: (0, 0)
    if latent:
        tq = ATTN_TQ
        n_q = seq // tq
        q_map = lambda b, h, i: (b * n_q + i, h)
        in_specs = [pl.BlockSpec((tq, LANES), q_map), kc_spec, vtc_spec,
                    pl.BlockSpec((seq, LANES), lambda b, h, i: (b, h)),
                    pl.BlockSpec((vrows, seq), lambda b, h, i: (h, b))]
        args = (dq, dk, vt, dk, vt)
        n_rows, o_map, n_lat_blocks = n_batch * seq, q_map, seq // ATTN_TK
    else:
        tq, n_q = ctx_len, 1
        in_specs = [kc_spec, kc_spec, vtc_spec]
        args = (dq, dk, vt)
        n_rows, o_map, n_lat_blocks = n_batch * ctx_len, (lambda b, h, i: (b, h)), 0
    in_specs += [pl.BlockSpec(lp.shape, const), pl.BlockSpec((LANES, LANES), const)]
    return pl.pallas_call(
        functools.partial(_attn_kernel, n_lat_blocks, lam_init),
        out_shape=jax.ShapeDtypeStruct((n_rows, N_HEADS * LANES), BF16),
        grid=(n_batch, N_HEADS, n_q),
        in_specs=in_specs,
        out_specs=pl.BlockSpec((tq, LANES), o_map),
        scratch_shapes=[pltpu.VMEM((1, 2 * tq), F32), pltpu.VMEM((vrows, 2 * tq), F32),
                        pltpu.VMEM((2, max(ATTN_TK, ctx_len), 2 * tq), F32)],
        compiler_params=_cparams("parallel", "parallel", "parallel"),
        name="diff_attention_lat" if latent else "diff_attention_ctx",
    )(*args, lp, g)


def _merge_kernel(tiles_per_batch, n_batch, orf_ref, orb_ref, ogf_ref, ogb_ref, rg_ref, gr_ref, dif_ref,
                  x_ref, mod_ref, gpost_ref, gpre_ref, glag_ref, wout_ref, wrt_ref,
                  xo_ref, hp_ref, sc_ref):
    d = x_ref.shape[-1]
    r = jnp.minimum(pl.program_id(0) // tiles_per_batch, n_batch)
    gate1 = mod_ref[pl.ds(r, 1), 2 * d:3 * d]
    shift2 = mod_ref[pl.ds(r, 1), 3 * d:4 * d]
    scale2 = mod_ref[pl.ds(r, 1), 4 * d:5 * d]
    gi = lax.broadcasted_iota(I32, (256, 256), 0) // RET_DV
    gj = lax.broadcasted_iota(I32, (256, 256), 1) // RET_DV
    group_mean = jnp.where(gi == gj, 1.0 / RET_DV, 0.0).astype(BF16)

    def head_norm(o):
        ms = jnp.dot((o * o).astype(BF16), group_mean, preferred_element_type=F32)
        return o * lax.rsqrt(ms + EPS)

    ret = head_norm(orf_ref[...] + orb_ref[...]) * _silu(rg_ref[...].astype(F32))
    gla = head_norm(ogf_ref[...] + ogb_ref[...]) * glag_ref[...] * _silu(gr_ref[...].astype(F32))
    m = (jnp.dot(ret.astype(BF16), wout_ref[0:256, :], preferred_element_type=F32)
         + jnp.dot(gla.astype(BF16), wout_ref[256:512, :], preferred_element_type=F32)
         + jnp.dot(dif_ref[...], wout_ref[512:1024, :], preferred_element_type=F32))
    x_new = x_ref[...] + gate1 * (_rms(m) * gpost_ref[...])
    xo_ref[...] = x_new
    h2 = _rms(x_new) * gpre_ref[...] * (1.0 + scale2) + shift2
    hp_ref[...] = _pack_halves(h2[:, :d // 2], h2[:, d // 2:])
    logits = lax.dot_general(wrt_ref[...], h2, NT_DIMS, precision=HIGHEST, preferred_element_type=F32)
    sc_ref[...] = 1.0 / (1.0 + jnp.exp(-logits))


def _merge(o_rf, o_rb, o_gf, o_gb, ret, gla, dif, xs, mod, g_post, g_pre_ffn, gla_g, w_out, w_rt,
           n_rows, n_batch, seq):
    d = xs.shape[-1]
    tm = TOKEN_TILE
    row = lambda i: (i, 0)
    gate_col = lambda i: (i, 2)
    const = lambda i: (0, 0)
    return pl.pallas_call(
        functools.partial(_merge_kernel, seq // tm, n_batch),
        out_shape=(jax.ShapeDtypeStruct((n_rows, d), F32), jax.ShapeDtypeStruct((n_rows, d // 2), U32),
                   jax.ShapeDtypeStruct((N_EXPERTS, n_rows), F32)),
        grid=(n_rows // tm,),
        in_specs=[
            pl.BlockSpec((tm, 256), row), pl.BlockSpec((tm, 256), row),
            pl.BlockSpec((tm, 256), row), pl.BlockSpec((tm, 256), row),
            pl.BlockSpec((tm, 256), gate_col), pl.BlockSpec((tm, 256), gate_col),
            pl.BlockSpec((tm, 512), row), pl.BlockSpec((tm, d), row),
            pl.BlockSpec(mod.shape, const), pl.BlockSpec((1, d), const), pl.BlockSpec((1, d), const),
            pl.BlockSpec((1, 256), const), pl.BlockSpec(w_out.shape, const), pl.BlockSpec(w_rt.shape, const),
        ],
        out_specs=(pl.BlockSpec((tm, d), row), pl.BlockSpec((tm, d // 2), row),
                   pl.BlockSpec((N_EXPERTS, tm), lambda i: (0, i))),
        compiler_params=_cparams("parallel"),
        name="merge",
    )(o_rf, o_rb, o_gf, o_gb, ret, gla, dif, xs, mod, g_post, g_pre_ffn, gla_g, w_out, w_rt)


def _route_kernel(sc_ref, bias_ref, eidx_ref, wts_ref, pos_ref, meta_ref):
    tt = sc_ref.shape[-1]
    scores = sc_ref[...]
    sel = scores + bias_ref[...][:, 0:1]
    sub = lax.broadcasted_iota(I32, (GROUP_SIZE, tt), 0)
    neg = -jnp.inf
    gscore = []
    for g in range(N_GROUPS):
        xg = sel[g * GROUP_SIZE:(g + 1) * GROUP_SIZE]
        m1 = jnp.max(xg, axis=0, keepdims=True)
        i1 = jnp.min(jnp.where(xg == m1, sub, GROUP_SIZE), axis=0, keepdims=True)
        m2 = jnp.max(jnp.where(sub == i1, neg, xg), axis=0, keepdims=True)
        gscore.append(m1 + m2)
    rows = []
    for g in range(N_GROUPS):
        rank = jnp.zeros((1, tt), I32)
        for o in range(N_GROUPS):
            if o == g:
                continue
            ahead = (gscore[o] >= gscore[g]) if o < g else (gscore[o] > gscore[g])
            rank = rank + ahead.astype(I32)
        rows.append(jnp.where(rank < TOPK_GROUPS, sel[g * GROUP_SIZE:(g + 1) * GROUP_SIZE], neg))
    masked = jnp.concatenate(rows, axis=0)
    eio = lax.broadcasted_iota(I32, (N_EXPERTS, tt), 0)
    member = jnp.zeros((N_EXPERTS, tt), F32)
    idxs, ws = [], []
    for _ in range(TOP_K):
        m = jnp.max(masked, axis=0, keepdims=True)
        i = jnp.min(jnp.where(masked == m, eio, N_EXPERTS), axis=0, keepdims=True)
        hit = eio == i
        idxs.append(i)
        ws.append(jnp.sum(jnp.where(hit, scores, 0.0), axis=0, keepdims=True))
        member = jnp.where(hit, 1.0, member)
        masked = jnp.where(hit, neg, masked)
    wsum = ws[0]
    for w in ws[1:]:
        wsum = wsum + w
    ti = lax.broadcasted_iota(I32, (tt, tt), 0)
    tj = lax.broadcasted_iota(I32, (tt, tt), 1)
    before = jnp.where(ti < tj, 1.0, 0.0).astype(BF16)
    rank_in_e = jnp.dot(member.astype(BF16), before, preferred_element_type=F32)
    cnt = jnp.sum(member, axis=1, keepdims=True)
    padded = jnp.floor((cnt + (SUBLANES - 1)) * (1.0 / SUBLANES)) * SUBLANES
    ei = lax.broadcasted_iota(I32, (N_EXPERTS, N_EXPERTS), 0)
    ej = lax.broadcasted_iota(I32, (N_EXPERTS, N_EXPERTS), 1)
    lower = jnp.where(ej < ei, 1.0, 0.0)
    off = jnp.dot(lower, jnp.broadcast_to(padded, (N_EXPERTS, LANES)), precision=HIGHEST,
                  preferred_element_type=F32)
    slot = rank_in_e + off[:, 0:1]
    zrow_i = jnp.zeros((SUBLANES - TOP_K, tt), I32)
    zrow_f = jnp.zeros((SUBLANES - TOP_K, tt), F32)
    pos = [jnp.sum(jnp.where(eio == i, slot, 0.0), axis=0, keepdims=True).astype(I32) for i in idxs]
    eidx_ref[...] = jnp.concatenate(idxs + [zrow_i], axis=0)
    wts_ref[...] = jnp.concatenate([w / wsum * ROUTED_SCALE for w in ws] + [zrow_f], axis=0)
    pos_ref[...] = jnp.concatenate(pos + [zrow_i], axis=0)
    meta_ref[0] = jnp.concatenate([jnp.broadcast_to(cnt, (N_EXPERTS, LANES)), off], axis=1).astype(I32)


def _route(scores_t, bias, tt, row0, n):
    tile0 = row0 // tt
    tok = lambda i: (0, i)
    return pl.pallas_call(
        _route_kernel,
        out_shape=(jax.ShapeDtypeStruct((SUBLANES, n), I32), jax.ShapeDtypeStruct((SUBLANES, n), F32),
                   jax.ShapeDtypeStruct((SUBLANES, n), I32),
                   jax.ShapeDtypeStruct((n // tt, N_EXPERTS, 2 * LANES), I32)),
        grid=(n // tt,),
        in_specs=[pl.BlockSpec((N_EXPERTS, tt), lambda i: (0, i + tile0)),
                  pl.BlockSpec((N_EXPERTS, LANES), lambda i: (0, 0))],
        out_specs=(pl.BlockSpec((SUBLANES, tt), tok), pl.BlockSpec((SUBLANES, tt), tok),
                   pl.BlockSpec((SUBLANES, tt), tok),
                   pl.BlockSpec((1, N_EXPERTS, 2 * LANES), lambda i: (i, 0, 0))),
        compiler_params=_cparams("parallel"),
        name="route",
    )(scores_t, jnp.broadcast_to(bias[:, None], (N_EXPERTS, LANES)))


def _moe_kernel(seq, n_batch, tile0, has_prev, pos_ref, wts_ref, meta_ref,
                hp_ref, x_ref, mod_ref, gpost_ref, weg_ref, weu_ref, wed_ref, wsg_ref, wsu_ref, wsd_ref, *refs):
    o_ref, xg_ref, y_ref, rt_ref = refs[1:] if has_prev else refs
    tt, half = hp_ref.shape
    d = 2 * half
    eg = pl.program_id(1)

    @pl.when((pl.program_id(0) == 0) & (eg == 0))
    def _():
        def zero(i, carry):
            for u in range(SUBLANES):
                rt_ref[i * SUBLANES + u] = 0
            return carry
        lax.fori_loop(0, rt_ref.shape[0] // SUBLANES, zero, 0)

    @pl.when(eg == 0)
    def _():
        def scatter(t, carry):
            for k in range(TOP_K):
                rt_ref[pos_ref[t * SUBLANES + k]] = t
            return carry
        lax.fori_loop(0, tt, scatter, 0)

    sub = lax.broadcasted_iota(I32, (SUBLANES, half), 0)

    def gather8(src_ref, row_of):
        buf = jnp.zeros((SUBLANES, half), U32)
        for u in range(SUBLANES):
            buf = jnp.where(sub == u, src_ref[pl.ds(row_of(u), 1), :], buf)
        return buf

    def expert_block(ge, off, j, carry):
        base = pl.multiple_of(off + j * MOE_BLOCK, SUBLANES)
        for g in range(MOE_BLOCK // SUBLANES):
            xg_ref[g * SUBLANES:(g + 1) * SUBLANES, :] = gather8(
                hp_ref, lambda u, g=g: rt_ref[base + g * SUBLANES + u])
        xa, xb = _unpack_halves(xg_ref[...])
        xa, xb = xa.astype(BF16), xb.astype(BF16)
        hg = (jnp.dot(xa, weg_ref[ge, 0:half, :], preferred_element_type=F32)
              + jnp.dot(xb, weg_ref[ge, half:d, :], preferred_element_type=F32))
        hu = (jnp.dot(xa, weu_ref[ge, 0:half, :], preferred_element_type=F32)
              + jnp.dot(xb, weu_ref[ge, half:d, :], preferred_element_type=F32))
        y = jnp.dot((_silu(hg) * hu).astype(BF16), wed_ref[ge], preferred_element_type=F32)
        y_ref[pl.ds(base, MOE_BLOCK), :] = _pack_halves(y[:, :half], y[:, half:])
        return carry

    for ge in range(EXPERT_GROUP):
        e = eg * EXPERT_GROUP + ge
        n_blocks = (meta_ref[0, e] + (MOE_BLOCK - 1)) // MOE_BLOCK
        lax.fori_loop(0, n_blocks, functools.partial(expert_block, ge, meta_ref[1, e]), 0)

    @pl.when(eg == N_EXPERTS // EXPERT_GROUP - 1)
    def _():
        def tokens(g, carry):
            t0 = pl.multiple_of(g * SUBLANES, SUBLANES)
            wg = wts_ref[pl.ds(t0, SUBLANES), :]
            acc_a = jnp.zeros((SUBLANES, half), F32)
            acc_b = jnp.zeros((SUBLANES, half), F32)
            for k in range(TOP_K):
                ya, yb = _unpack_halves(gather8(y_ref, lambda u, k=k: pos_ref[(t0 + u) * SUBLANES + k]))
                acc_a = acc_a + wg[:, k:k + 1] * ya
                acc_b = acc_b + wg[:, k:k + 1] * yb
            o_ref[pl.ds(t0, SUBLANES), 0:half] = acc_a
            o_ref[pl.ds(t0, SUBLANES), half:d] = acc_b
            return carry

        lax.fori_loop(0, tt // SUBLANES, tokens, 0)
        rows = (pl.program_id(0) + tile0) * tt + lax.broadcasted_iota(I32, (tt, 1), 0)
        rb = jnp.minimum(rows // seq, n_batch)
        gate2 = jnp.zeros((tt, d), F32)
        for bi in range(n_batch + 1):
            gate2 = jnp.where(rb == bi, mod_ref[bi:bi + 1, 5 * d:6 * d], gate2)
        xa, xb = _unpack_halves(hp_ref[...])
        xa, xb = xa.astype(BF16), xb.astype(BF16)
        hg = (jnp.dot(xa, wsg_ref[0:half, :], preferred_element_type=F32)
              + jnp.dot(xb, wsg_ref[half:d, :], preferred_element_type=F32))
        hu = (jnp.dot(xa, wsu_ref[0:half, :], preferred_element_type=F32)
              + jnp.dot(xb, wsu_ref[half:d, :], preferred_element_type=F32))
        f = jnp.dot((_silu(hg) * hu).astype(BF16), wsd_ref[...], preferred_element_type=F32) + o_ref[...]
        o_ref[...] = x_ref[...] + gate2 * (_rms(f) * gpost_ref[...])


def _moe_segment(prev, scores_t, hp, xs, mod, router_bias, g_post, weights, row0, n, tt, n_batch, seq):
    n_all, half = hp.shape
    d = 2 * half
    w_eg, w_eu, w_ed, w_sg, w_su, w_sd = weights
    de = w_eg.shape[-1]
    n_tiles, tile0 = n // tt, row0 // tt
    n_slots = -(-(tt * TOP_K + N_EXPERTS * (SUBLANES - 1) + MOE_BLOCK) // (SUBLANES * LANES)) * SUBLANES * LANES
    e_idx, wts, pos, meta = _route(scores_t, router_bias, tt, row0, n)
    cnt, off = meta[:, :, 0], meta[:, :, LANES]

    def per_token(a):
        return a.T.reshape(-1)

    meta_s = jnp.zeros((n_tiles, SUBLANES, LANES), I32)
    meta_s = meta_s.at[:, 0, :N_EXPERTS].set(cnt).at[:, 1, :N_EXPERTS].set(off).reshape(-1, LANES)

    tile = lambda i, e: (i + tile0, 0)
    const = lambda i, e: (0, 0)
    expert = lambda i, e: (e, 0, 0)
    smem = functools.partial(pl.BlockSpec, memory_space=pltpu.SMEM)
    eg = EXPERT_GROUP
    in_specs = [
        smem((SUBLANES * tt,), lambda i, e: (i,)), pl.BlockSpec((tt, SUBLANES), lambda i, e: (i, 0)),
        smem((SUBLANES, LANES), lambda i, e: (i, 0)),
        pl.BlockSpec((tt, half), tile), pl.BlockSpec((tt, d), tile),
        pl.BlockSpec(mod.shape, const), pl.BlockSpec((1, d), const),
        pl.BlockSpec((eg, d, de), expert), pl.BlockSpec((eg, d, de), expert), pl.BlockSpec((eg, de, d), expert),
        pl.BlockSpec(w_sg.shape, const), pl.BlockSpec(w_su.shape, const), pl.BlockSpec(w_sd.shape, const),
    ]
    args = [per_token(pos), wts.T, meta_s, hp, xs, mod, g_post, w_eg, w_eu, w_ed, w_sg, w_su, w_sd]
    aliases = {}
    if prev is not None:
        in_specs.append(pl.BlockSpec(memory_space=pl.ANY))
        args.append(prev)
        aliases = {len(args) - 1: 0}
    return pl.pallas_call(
        functools.partial(_moe_kernel, seq, n_batch, tile0, prev is not None),
        out_shape=jax.ShapeDtypeStruct((n_all, d), F32),
        grid=(n_tiles, N_EXPERTS // EXPERT_GROUP),
        in_specs=in_specs,
        out_specs=pl.BlockSpec((tt, d), tile),
        scratch_shapes=[pltpu.VMEM((MOE_BLOCK, half), U32), pltpu.VMEM((n_slots, half), U32),
                        pltpu.SMEM((n_slots,), I32)],
        input_output_aliases=aliases,
        compiler_params=_cparams("arbitrary", "arbitrary"),
        name="moe",
    )(*args)


def _moe_layer(scores_t, hp, xs, mod, router_bias, g_post, weights, n_lat, n_batch, seq):
    pick = lambda rows: next(t for t in MOE_TILES if rows % t == 0)
    out = _moe_segment(None, scores_t, hp, xs, mod, router_bias, g_post, weights, 0, n_lat, pick(n_lat),
                       n_batch, seq)
    n_ctx = hp.shape[0] - n_lat
    if n_ctx:
        tt = next(t for t in MOE_TILES if n_ctx % t == 0 and n_lat % t == 0)
        out = _moe_segment(out, scores_t, hp, xs, mod, router_bias, g_post, weights, n_lat, n_ctx, tt,
                           n_batch, seq)
    return out


def kernel(x, c, ctx, c_ctx, w_mod, b_mod, g_pre_mix, g_post_mix, g_pre_ffn, g_post_ffn, w_in, w_out,
           ret_decay_logit, gla_w_gate, gla_b_gate, gla_norm_g, diff_lambda, diff_norm_g,
           w_router, router_bias, w_exp_gate, w_exp_up, w_exp_down, w_sh_gate, w_sh_up, w_sh_down):
    n_batch, seq, d = x.shape
    ctx_len = ctx.shape[1]
    depth = w_mod.shape[0]
    n_lat = n_batch * seq
    assert seq % TOKEN_TILE == 0 and (n_batch * ctx_len) % TOKEN_TILE == 0 and n_batch < SUBLANES
    assert seq % ATTN_TK == 0 and seq % CHUNK == 0 and ctx_len % CHUNK == 0 and n_lat % ctx_len == 0

    xs = jnp.concatenate([x.reshape(n_lat, d), ctx.reshape(n_batch * ctx_len, d)], axis=0)
    cond = jnp.zeros((SUBLANES, d), F32).at[:n_batch].set(c).at[n_batch].set(c_ctx)
    mods = _modulation(cond, w_mod, b_mod)
    tables = _rope_tables(seq, RET_DK, TOKEN_TILE) + _rope_tables(seq, DIFF_DH, TOKEN_TILE)
    lr0 = N_HEADS * (2 * RET_DK + 2 * RET_DV + 2 * GLA_DK + 2 * GLA_DV)
    row = lambda a: a[None, :]

    for layer in range(depth):
        need_ctx = layer < depth - 1
        lam_init = 0.8 - 0.6 * math.exp(-0.3 * layer)
        mod = mods[layer]
        wl = w_in[layer]
        w_r = jnp.concatenate([wl[:, :lr0], wl[:, lr0:lr0 + GLA_LOWRANK],
                               jnp.zeros((d, LANES - GLA_LOWRANK), F32), wl[:, lr0 + GLA_LOWRANK:]],
                              axis=1).astype(BF16)
        ret, gla, glr, dq, dk, dv = _in_projection(xs, mod, row(g_pre_mix[layer]), w_r, tables, n_batch, seq)

        scan = functools.partial(_scan, ret=ret, gla=gla, glr=glr, n_batch=n_batch, seq=seq, ctx_len=ctx_len)
        o_rf, o_gf = scan(False, ret_logit=ret_decay_logit[layer, 0], gla_w=gla_w_gate[layer, 0],
                          gla_b=gla_b_gate[layer, 0])
        o_rb, o_gb = scan(True, ret_logit=ret_decay_logit[layer, 1], gla_w=gla_w_gate[layer, 1],
                          gla_b=gla_b_gate[layer, 1])

        lp = jnp.zeros((SUBLANES, LANES), F32).at[:4, :DIFF_DH].set(diff_lambda[layer])
        n_all = dv.shape[0]
        vt = jnp.concatenate([dv.T.reshape(N_HEADS, LANES, n_all), jnp.ones((N_HEADS, ATTN_ONES, n_all), BF16)],
                             axis=1).reshape(N_HEADS * (LANES + ATTN_ONES), n_all)
        g_col = jnp.broadcast_to(diff_norm_g[layer][:, None], (LANES, LANES))
        attn = functools.partial(_diff_attention, dq, dk, vt, lp, g_col, lam_init, n_batch, seq, ctx_len)
        dif = attn(True)
        n_rows = n_lat
        if need_ctx:
            dif = jnp.concatenate([dif, attn(False)], axis=0)
            n_rows = xs.shape[0]

        xs, hp, scores_t = _merge(o_rf, o_rb, o_gf, o_gb, ret, gla, dif, xs, mod, row(g_post_mix[layer]),
                                  row(g_pre_ffn[layer]), row(jnp.tile(gla_norm_g[layer], N_HEADS)),
                                  w_out[layer].astype(BF16), w_router[layer].T, n_rows, n_batch, seq)
        weights = tuple(w[layer].astype(BF16) for w in
                        (w_exp_gate, w_exp_up, w_exp_down, w_sh_gate, w_sh_up, w_sh_down))
        xs = _moe_layer(scores_t, hp, xs, mod, router_bias[layer], row(g_post_ffn[layer]), weights,
                        n_lat, n_batch, seq)
    return xs[:n_lat].reshape(n_batch, seq, d)
```

```python
import functools
import math

import jax
import jax.numpy as jnp
from jax import lax
from jax.experimental import pallas as pl
from jax.experimental.pallas import tpu as pltpu

F32 = jnp.float32
BF16 = jnp.bfloat16
I32 = jnp.int32
U32 = jnp.uint32

GRID_W = 64
CHUNK = 128
N_HEADS = 4
RET_DK, RET_DV = 32, 64
GLA_DK, GLA_DV = 32, 64
GLA_LOWRANK = 16
GLA_TAU = 16.0
DIFF_DH = 64
ROPE_BASE = 10000.0
N_EXPERTS = 64
TOP_K = 6
N_GROUPS = 8
TOPK_GROUPS = 4
GROUP_SIZE = N_EXPERTS // N_GROUPS
ROUTED_SCALE = 2.5
EPS = 1e-6
GLA_SUB = 16

LANES = 128
SUBLANES = 8
VMEM_LIMIT_BYTES = 56 * 1024 * 1024

TOKEN_TILE = 512
ATTN_TQ = 512
ATTN_TK = 512
ATTN_ONES = 16
MOE_TILES = (1024, 512)
MOE_BLOCK = 128
EXPERT_GROUP = 4

HIGHEST = lax.Precision.HIGHEST
NT_DIMS = (((1,), (1,)), ((), ()))
TN_DIMS = (((0,), (0,)), ((), ()))


def _cparams(*sem):
    return pltpu.CompilerParams(dimension_semantics=sem, vmem_limit_bytes=VMEM_LIMIT_BYTES)


def _log_sigmoid(x):
    return jnp.minimum(x, 0.0) - jnp.log(1.0 + jnp.exp(-jnp.abs(x)))


def _silu(x):
    return x * (1.0 / (1.0 + jnp.exp(-x)))


def _rms(x):
    return x * lax.rsqrt(jnp.mean(x * x, axis=-1, keepdims=True) + EPS)


def _pack_halves(a, b):
    ua = lax.bitcast_convert_type(a.astype(BF16).astype(F32), U32)
    ub = lax.bitcast_convert_type(b.astype(BF16).astype(F32), U32)
    return (ua & jnp.uint32(0xFFFF0000)) | (ub >> 16)


def _unpack_halves(w):
    a = lax.bitcast_convert_type(w & jnp.uint32(0xFFFF0000), F32)
    b = lax.bitcast_convert_type(w << 16, F32)
    return a, b


def _mod_kernel(cond_ref, w_ref, b_ref, o_ref):
    a = _silu(cond_ref[...])
    o_ref[0] = jnp.dot(a, w_ref[0], precision=HIGHEST, preferred_element_type=F32) + b_ref[0]


def _modulation(cond, w_mod, b_mod):
    n_layers, d, d6 = w_mod.shape
    tn = 1024
    return pl.pallas_call(
        _mod_kernel,
        out_shape=jax.ShapeDtypeStruct((n_layers, SUBLANES, d6), F32),
        grid=(n_layers, d6 // tn),
        in_specs=[
            pl.BlockSpec((SUBLANES, d), lambda l, j: (0, 0)),
            pl.BlockSpec((1, d, tn), lambda l, j: (l, 0, j)),
            pl.BlockSpec((1, 1, tn), lambda l, j: (l, 0, j)),
        ],
        out_specs=pl.BlockSpec((1, SUBLANES, tn), lambda l, j: (l, 0, j)),
        compiler_params=_cparams("parallel", "parallel"),
        name="modulation",
    )(cond, w_mod, b_mod.reshape(n_layers, 1, d6))


def _rope(x, cos, sin, quarter):
    lane = lax.broadcasted_iota(I32, (1, LANES), 1)
    first = (lane % (2 * quarter)) < quarter
    outs = []
    for c in range(x.shape[-1] // LANES):
        xc = x[:, c * LANES:(c + 1) * LANES]
        partner = jnp.where(first, pltpu.roll(xc, LANES - quarter, 1), pltpu.roll(xc, quarter, 1))
        outs.append(xc * cos + partner * sin)
    return outs[0] if len(outs) == 1 else jnp.concatenate(outs, axis=-1)


def _inproj_kernel(tiles_per_batch, n_batch, x_ref, mod_ref, g_ref, w_ref, c32_ref, s32_ref, c64_ref, s64_ref,
                   ret_ref, gla_ref, glr_ref, dq_ref, dk_ref, dv_ref):
    d = x_ref.shape[-1]
    r = jnp.minimum(pl.program_id(0) // tiles_per_batch, n_batch)
    shift = mod_ref[pl.ds(r, 1), 0:d]
    scale = mod_ref[pl.ds(r, 1), d:2 * d]
    h = (_rms(x_ref[...]) * g_ref[...] * (1.0 + scale) + shift).astype(BF16)

    def proj(lo, hi):
        return jnp.dot(h, w_ref[:, lo:hi], preferred_element_type=F32)

    c32, s32 = c32_ref[...], s32_ref[...]
    c64, s64 = c64_ref[...], s64_ref[...]
    ret = proj(0, 768)
    ret_ref[:, 0:128] = _rope(ret[:, 0:128], c32, s32, RET_DK // 4).astype(BF16)
    ret_ref[:, 128:256] = (_rope(ret[:, 128:256], c32, s32, RET_DK // 4) * RET_DK ** -0.5).astype(BF16)
    ret_ref[:, 256:768] = ret[:, 256:768].astype(BF16)
    gla = proj(768, 1536)
    gla_ref[:, 0:128] = (gla[:, 0:128] * GLA_DK ** -0.5).astype(BF16)
    gla_ref[:, 128:768] = gla[:, 128:768].astype(BF16)
    glr_ref[...] = proj(1536, 1664)
    dq_ref[...] = (_rope(proj(1664, 2176), c64, s64, DIFF_DH // 4) * DIFF_DH ** -0.5).astype(BF16)
    dk_ref[...] = _rope(proj(2176, 2688), c64, s64, DIFF_DH // 4).astype(BF16)
    dv_ref[...] = proj(2688, 3200).astype(BF16)


def _in_projection(xs, mod, g_pre, w_r, tables, n_batch, seq):
    n, d = xs.shape
    tm = TOKEN_TILE
    tiles_per_batch = seq // tm
    n_lat_tiles = n_batch * tiles_per_batch
    c32, s32, c64, s64 = tables

    def tab_map(i):
        return (jnp.where(i < n_lat_tiles, i % tiles_per_batch, tiles_per_batch), 0)

    row = lambda i: (i, 0)
    const = lambda i: (0, 0)
    tab_spec = pl.BlockSpec((tm, LANES), tab_map)
    return pl.pallas_call(
        functools.partial(_inproj_kernel, tiles_per_batch, n_batch),
        out_shape=(
            jax.ShapeDtypeStruct((n, 768), BF16), jax.ShapeDtypeStruct((n, 768), BF16),
            jax.ShapeDtypeStruct((n, LANES), F32),
            jax.ShapeDtypeStruct((n, 512), BF16), jax.ShapeDtypeStruct((n, 512), BF16),
            jax.ShapeDtypeStruct((n, 512), BF16)),
        grid=(n // tm,),
        in_specs=[
            pl.BlockSpec((tm, d), row),
            pl.BlockSpec(mod.shape, const),
            pl.BlockSpec((1, d), const),
            pl.BlockSpec(w_r.shape, const),
            tab_spec, tab_spec, tab_spec, tab_spec,
        ],
        out_specs=(
            pl.BlockSpec((tm, 768), row), pl.BlockSpec((tm, 768), row), pl.BlockSpec((tm, LANES), row),
            pl.BlockSpec((tm, 512), row), pl.BlockSpec((tm, 512), row), pl.BlockSpec((tm, 512), row)),
        compiler_params=_cparams("parallel"),
        name="in_projection",
    )(xs, mod, g_pre, w_r, c32, s32, c64, s64)


def _rope_tables(seq, head_dim, extra_rows):
    half, quarter = head_dim // 2, head_dim // 4
    freqs = ROPE_BASE ** (-jnp.arange(quarter, dtype=F32) / quarter)
    t = jnp.arange(seq)
    row = (t // GRID_W).astype(F32)
    col = (t % GRID_W).astype(F32)
    j = jnp.arange(LANES) % head_dim
    jj = j % half
    pos = jnp.where((j < half)[None, :], row[:, None], col[:, None])
    ang = pos * freqs[jj % quarter][None, :]
    cos = jnp.cos(ang)
    sin = jnp.sin(ang) * jnp.where(jj < quarter, -1.0, 1.0)[None, :]
    cos = jnp.concatenate([cos, jnp.ones((extra_rows, LANES), F32)], axis=0)
    sin = jnp.concatenate([sin, jnp.zeros((extra_rows, LANES), F32)], axis=0)
    return cos, sin


def _head_stack(x, width):
    lane = lax.broadcasted_iota(I32, (1, x.shape[-1]), 1)
    zero = jnp.zeros_like(x)
    return jnp.concatenate([jnp.where(lane // width == h, x, zero) for h in range(N_HEADS)], axis=0)


def _head_select(x4, rows):
    lane = lax.broadcasted_iota(I32, (1, x4.shape[-1]), 1)
    out = jnp.zeros((rows, x4.shape[-1]), F32)
    for h in range(N_HEADS):
        out = out + jnp.where(lane // RET_DV == h, x4[h * rows:(h + 1) * rows], 0.0)
    return out


def _scan_tables_kernel(rl_lane_ref, rl_rows_ref, rl_col_ref, dmat_ref, xi_ref, zeta_ref, gchunk_ref):
    c = CHUNK
    idx = lax.broadcasted_iota(I32, (c, 1), 0).astype(F32)
    ri4 = lax.broadcasted_iota(I32, (N_HEADS * c, c), 0) % c
    ci4 = lax.broadcasted_iota(I32, (N_HEADS * c, c), 1)
    dist = jnp.abs(ri4 - ci4).astype(F32)
    for d in range(2):
        lg_lane = _log_sigmoid(rl_lane_ref[d])
        lg_rows = _log_sigmoid(rl_rows_ref[d])
        lg_col = _log_sigmoid(rl_col_ref[d])
        att4 = (ri4 <= ci4) if d else (ri4 >= ci4)
        dmat_ref[d] = jnp.where(att4, jnp.exp(dist * lg_rows), 0.0)
        xi_ref[d] = jnp.exp(((c - idx) if d else (idx + 1.0)) * lg_lane)
        zeta_ref[d] = jnp.exp((idx if d else (c - 1.0 - idx)) * lg_lane)
        g_chunk = jnp.exp(float(c) * lg_col)
        gchunk_ref[d] = jnp.concatenate([g_chunk, g_chunk], axis=1)


def _scan_tables(ret_logit):
    c = CHUNK
    rl_lane = jnp.repeat(ret_logit, RET_DK, axis=1)[:, None, :]
    rl_rows = jnp.broadcast_to(jnp.repeat(ret_logit, c, axis=1)[:, :, None], (2, N_HEADS * c, c))
    rl_col = jnp.broadcast_to(jnp.repeat(ret_logit, RET_DK, axis=1)[:, :, None], (2, LANES, LANES))
    return pl.pallas_call(
        _scan_tables_kernel,
        out_shape=(jax.ShapeDtypeStruct((2, N_HEADS * c, c), F32), jax.ShapeDtypeStruct((2, c, LANES), F32),
                   jax.ShapeDtypeStruct((2, c, LANES), F32), jax.ShapeDtypeStruct((2, LANES, 2 * LANES), F32)),
        name="scan_tables",
    )(rl_lane, rl_rows, rl_col)


def _state_block_mask():
    return (lax.broadcasted_iota(I32, (LANES, 2 * LANES), 0) // RET_DK
            == lax.broadcasted_iota(I32, (LANES, 2 * LANES), 1) // RET_DV)


def _ret_chain(ret_ref, dmat, xi, zeta, g_chunk, ore_ref, sr_ref):
    c = CHUNK
    q = ret_ref[:, 0:128]
    k = ret_ref[:, 128:256]
    v = ret_ref[:, 256:512]
    s = lax.dot_general(_head_stack(q, RET_DK), k, NT_DIMS, preferred_element_type=F32)
    sr = sr_ref[...]
    inter = jnp.dot((q.astype(F32) * xi).astype(BF16), sr.astype(BF16), preferred_element_type=F32)
    kz = (k.astype(F32) * zeta).astype(BF16)
    u = lax.dot_general(kz, v, TN_DIMS, preferred_element_type=F32)
    yield
    o4 = jnp.dot((s * dmat).astype(BF16), v, preferred_element_type=F32)
    sr_ref[...] = sr * g_chunk + jnp.where(_state_block_mask(), u, 0.0)
    yield
    ore_ref[...] = _head_select(o4, c) + inter


def _gla_chain(rev, gla_ref, glr_ref, gw, gb, ogl_ref, sg_ref):
    c = CHUNK
    ri = lax.broadcasted_iota(I32, (c, c), 0)
    ci = lax.broadcasted_iota(I32, (c, c), 1)
    attends = (ri <= ci) if rev else (ri >= ci)
    bd = _state_block_mask()
    gq = gla_ref[:, 0:128].astype(F32)
    gk = gla_ref[:, 128:256].astype(F32)
    gv = gla_ref[:, 256:512]
    z = jnp.dot(glr_ref[...], gw, precision=HIGHEST, preferred_element_type=F32) + gb
    yield
    la = _log_sigmoid(z) * (1.0 / GLA_TAU)
    if rev:
        first = (ri // GLA_SUB) * GLA_SUB + (GLA_SUB - 1)
        ref_sel = ci >= first
    else:
        first = (ri // GLA_SUB) * GLA_SUB
        ref_sel = ci <= first
    sel = jnp.concatenate([jnp.where(attends, 1.0, 0.0), jnp.where(ref_sel, 1.0, 0.0)], axis=0).astype(BF16)
    la_hi = la.astype(BF16)
    la_lo = (la - la_hi.astype(F32)).astype(BF16)
    sums = (jnp.dot(sel, la_hi, preferred_element_type=F32) + jnp.dot(sel, la_lo, preferred_element_type=F32))
    b, refrow = sums[0:c], sums[c:2 * c]
    yield
    qs = gq * jnp.exp(b - refrow)
    sg = sg_ref[...]
    inter = jnp.dot((gq * jnp.exp(b)).astype(BF16), sg.astype(BF16), preferred_element_type=F32)
    b_last = b[0:1] if rev else b[c - 1:c]
    kz = (gk * jnp.exp(b_last - b)).astype(BF16)
    u = lax.dot_general(kz, gv, TN_DIMS, preferred_element_type=F32)
    eye = lax.broadcasted_iota(I32, (LANES, LANES), 0) == lax.broadcasted_iota(I32, (LANES, LANES), 1)
    g_col = jnp.sum(jnp.where(eye, jnp.exp(b_last), 0.0), axis=1, keepdims=True)
    jcol = lax.broadcasted_iota(I32, (c, 1), 0)
    rr = lax.broadcasted_iota(I32, (N_HEADS * GLA_SUB, c), 0) % GLA_SUB
    cc = lax.broadcasted_iota(I32, (N_HEADS * GLA_SUB, c), 1)
    pieces = []
    for blk in range(c // GLA_SUB):
        lo = blk * GLA_SUB
        ref_b = refrow[lo:lo + 1]
        seen = (jcol >= lo) if rev else (jcol < lo + GLA_SUB)
        ks = (gk * jnp.exp(jnp.where(seen, ref_b - b, -jnp.inf))).astype(BF16)
        qz = _head_stack(qs[lo:lo + GLA_SUB], GLA_DK).astype(BF16)
        att = lax.dot_general(qz, ks, NT_DIMS, preferred_element_type=F32)
        ok = (cc >= rr + lo) if rev else (cc <= rr + lo)
        att = jnp.where(ok, att, 0.0).astype(BF16)
        pieces.append(_head_select(jnp.dot(att, gv, preferred_element_type=F32), GLA_SUB))
        if blk % 2:
            yield
    sg_ref[...] = sg * g_col + jnp.where(bd, u, 0.0)
    ogl_ref[...] = jnp.concatenate(pieces, axis=0) + inter


def _scan_kernel(ret_f, gla_f, glr_f, ret_b, gla_b, glr_b, dmat_ref, xi_ref, zeta_ref, gchunk_ref, gw_ref, gb_ref,
                 orf_ref, ogf_ref, orb_ref, ogb_ref, sr_ref, sg_ref):
    @pl.when(pl.program_id(1) == 0)
    def _():
        sr_ref[...] = jnp.zeros_like(sr_ref)
        sg_ref[...] = jnp.zeros_like(sg_ref)

    chains = []
    for d, (ret_ref, gla_ref, glr_ref, ore_ref, ogl_ref) in enumerate(
            ((ret_f, gla_f, glr_f, orf_ref, ogf_ref), (ret_b, gla_b, glr_b, orb_ref, ogb_ref))):
        chains.append(_gla_chain(bool(d), gla_ref, glr_ref, gw_ref[d], gb_ref[d], ogl_ref, sg_ref.at[d]))
        chains.append(_ret_chain(ret_ref, dmat_ref[d], xi_ref[d], zeta_ref[d], gchunk_ref[d], ore_ref,
                                 sr_ref.at[d]))
    while chains:
        chains = [ch for ch in chains if next(ch, True) is None]


def _scan(ret, gla, glr, tables, gla_w, gla_b, n_batch, seq, ctx_len):
    n = ret.shape[0]
    c = CHUNK
    nc_ctx, nc_lat = ctx_len // c, seq // c
    n_steps = nc_ctx + nc_lat
    ctx_base = n_batch * nc_lat

    def fwd(b, s):
        return (jnp.where(s < nc_ctx, ctx_base + b * nc_ctx + s, b * nc_lat + (s - nc_ctx)), 0)

    def bwd(b, s):
        return (jnp.where(s < nc_ctx, ctx_base + b * nc_ctx + (nc_ctx - 1 - s), b * nc_lat + (n_steps - 1 - s)), 0)

    const3 = lambda b, s: (0, 0, 0)
    gw = jnp.zeros((2, LANES, LANES), F32).at[:, :GLA_LOWRANK].set(gla_w)
    gb = gla_b[:, None, :]
    chain_in = lambda m: [pl.BlockSpec((c, 768), m), pl.BlockSpec((c, 768), m), pl.BlockSpec((c, LANES), m)]
    o_sds = jax.ShapeDtypeStruct((n, 256), F32)
    return pl.pallas_call(
        _scan_kernel,
        out_shape=(o_sds, o_sds, o_sds, o_sds),
        grid=(n_batch, n_steps),
        in_specs=chain_in(fwd) + chain_in(bwd) + [pl.BlockSpec(t.shape, const3) for t in tables]
        + [pl.BlockSpec(gw.shape, const3), pl.BlockSpec(gb.shape, const3)],
        out_specs=(pl.BlockSpec((c, 256), fwd), pl.BlockSpec((c, 256), fwd),
                   pl.BlockSpec((c, 256), bwd), pl.BlockSpec((c, 256), bwd)),
        scratch_shapes=[pltpu.VMEM((2, LANES, 2 * LANES), F32), pltpu.VMEM((2, LANES, 2 * LANES), F32)],
        compiler_params=_cparams("arbitrary", "arbitrary"),
        name="scan",
    )(ret, gla, glr, ret, gla, glr, *tables, gw, gb)


def _attn_kernel(n_lat_blocks, lam_init, *refs):
    if n_lat_blocks:
        q_ref, kc_ref, vtc_ref, kl_ref, vtl_ref, lp_ref, g_ref, o_ref, m_ref, acc_ref, st_ref = refs
    else:
        q_ref, kc_ref, vtc_ref, lp_ref, g_ref, o_ref, m_ref, acc_ref, st_ref = refs
    tq = q_ref.shape[0]
    q = q_ref[...].astype(F32)
    lane = lax.broadcasted_iota(I32, (1, LANES), 1)
    qt = jnp.concatenate([jnp.where(lane < DIFF_DH, q, 0.0).T, jnp.where(lane >= DIFF_DH, q, 0.0).T],
                         axis=1).astype(BF16)
    m_ref[...] = jnp.full(m_ref.shape, -jnp.inf, F32)
    acc_ref[...] = jnp.zeros_like(acc_ref)

    def scores(slot, kb):
        st_ref[slot, 0:kb.shape[0], :] = jnp.dot(kb, qt, preferred_element_type=F32)

    def absorb(slot, vtb):
        st = st_ref[slot, 0:vtb.shape[1], :]
        m_prev = m_ref[...]
        m_new = jnp.maximum(m_prev, jnp.max(st, axis=0, keepdims=True))
        alpha = jnp.exp(m_prev - m_new)
        p = jnp.exp(st - m_new).astype(BF16)
        acc_ref[...] = alpha * acc_ref[...] + jnp.dot(vtb, p, preferred_element_type=F32)
        m_ref[...] = m_new

    def k_lat(blk):
        return kl_ref[pl.ds(pl.multiple_of(blk * ATTN_TK, ATTN_TK), ATTN_TK), :]

    def vt_lat(blk):
        return vtl_ref[:, pl.ds(pl.multiple_of(blk * ATTN_TK, ATTN_TK), ATTN_TK)]

    scores(0, kc_ref[...])
    if not n_lat_blocks:
        absorb(0, vtc_ref[...])
    else:
        scores(1, k_lat(0))
        absorb(0, vtc_ref[...])
        n_pairs = (n_lat_blocks - 1) // 2

        def body(i, carry):
            scores(0, k_lat(2 * i + 1))
            absorb(1, vt_lat(2 * i))
            scores(1, k_lat(2 * i + 2))
            absorb(0, vt_lat(2 * i + 1))
            return carry
        lax.fori_loop(0, n_pairs, body, 0)
        done = 2 * n_pairs
        if n_lat_blocks - done == 2:
            scores(0, k_lat(done + 1))
            absorb(1, vt_lat(done))
            absorb(0, vt_lat(done + 1))
        else:
            absorb(1, vt_lat(done))

    lp = lp_ref[...]
    lam = (jnp.exp(jnp.sum(lp[0:1] * lp[1:2], axis=1, keepdims=True))
           - jnp.exp(jnp.sum(lp[2:3] * lp[3:4], axis=1, keepdims=True)) + lam_init)
    acc = acc_ref[...]
    o1 = acc[0:LANES, :tq] / acc[LANES:LANES + 1, :tq]
    o2 = acc[0:LANES, tq:] / acc[LANES:LANES + 1, tq:]
    ot = o1 - lam * o2
    ot = ot * lax.rsqrt(jnp.mean(ot * ot, axis=0, keepdims=True) + EPS) * g_ref[...][:, 0:1] * (1.0 - lam_init)
    o_ref[...] = ot.T.astype(o_ref.dtype)


def _diff_attention(dq, dk, vt, lp, g, lam_init, n_batch, seq, ctx_len, latent):
    ctx_blk0 = (n_batch * seq) // ctx_len
    vrows = LANES + ATTN_ONES
    kc_spec = pl.BlockSpec((ctx_len, LANES), lambda b, h, i: (ctx_blk0 + b, h))
    vtc_spec = pl.BlockSpec((vrows, ctx_len), lambda b, h, i: (h, ctx_blk0 + b))
    const = lambda b, h, i: (0, 0)
    if latent:
        tq = ATTN_TQ
        n_q = seq // tq
        q_map = lambda b, h, i: (b * n_q + i, h)
        in_specs = [pl.BlockSpec((tq, LANES), q_map), kc_spec, vtc_spec,
                    pl.BlockSpec((seq, LANES), lambda b, h, i: (b, h)),
                    pl.BlockSpec((vrows, seq), lambda b, h, i: (h, b))]
        args = (dq, dk, vt, dk, vt)
        n_rows, o_map, n_lat_blocks = n_batch * seq, q_map, seq // ATTN_TK
    else:
        tq, n_q = ctx_len, 1
        in_specs = [kc_spec, kc_spec, vtc_spec]
        args = (dq, dk, vt)
        n_rows, o_map, n_lat_blocks = n_batch * ctx_len, (lambda b, h, i: (b, h)), 0
    in_specs += [pl.BlockSpec(lp.shape, const), pl.BlockSpec((LANES, LANES), const)]
    return pl.pallas_call(
        functools.partial(_attn_kernel, n_lat_blocks, lam_init),
        out_shape=jax.ShapeDtypeStruct((n_rows, N_HEADS * LANES), BF16),
        grid=(n_batch, N_HEADS, n_q),
        in_specs=in_specs,
        out_specs=pl.BlockSpec((tq, LANES), o_map),
        scratch_shapes=[pltpu.VMEM((1, 2 * tq), F32), pltpu.VMEM((vrows, 2 * tq), F32),
                        pltpu.VMEM((2, max(ATTN_TK, ctx_len), 2 * tq), F32)],
        compiler_params=_cparams("parallel", "parallel", "parallel"),
        name="diff_attention_lat" if latent else "diff_attention_ctx",
    )(*args, lp, g)


def _merge_kernel(tiles_per_batch, n_batch, orf_ref, orb_ref, ogf_ref, ogb_ref, rg_ref, gr_ref, dif_ref,
                  x_ref, mod_ref, gpost_ref, gpre_ref, glag_ref, wout_ref, wrt_ref,
                  xo_ref, hp_ref, sc_ref):
    d = x_ref.shape[-1]
    r = jnp.minimum(pl.program_id(0) // tiles_per_batch, n_batch)
    gate1 = mod_ref[pl.ds(r, 1), 2 * d:3 * d]
    shift2 = mod_ref[pl.ds(r, 1), 3 * d:4 * d]
    scale2 = mod_ref[pl.ds(r, 1), 4 * d:5 * d]
    gi = lax.broadcasted_iota(I32, (256, 256), 0) // RET_DV
    gj = lax.broadcasted_iota(I32, (256, 256), 1) // RET_DV
    group_mean = jnp.where(gi == gj, 1.0 / RET_DV, 0.0).astype(BF16)

    def head_norm(o):
        ms = jnp.dot((o * o).astype(BF16), group_mean, preferred_element_type=F32)
        return o * lax.rsqrt(ms + EPS)

    ret = head_norm(orf_ref[...] + orb_ref[...]) * _silu(rg_ref[...].astype(F32))
    gla = head_norm(ogf_ref[...] + ogb_ref[...]) * glag_ref[...] * _silu(gr_ref[...].astype(F32))
    m = (jnp.dot(ret.astype(BF16), wout_ref[0:256, :], preferred_element_type=F32)
         + jnp.dot(gla.astype(BF16), wout_ref[256:512, :], preferred_element_type=F32)
         + jnp.dot(dif_ref[...], wout_ref[512:1024, :], preferred_element_type=F32))
    x_new = x_ref[...] + gate1 * (_rms(m) * gpost_ref[...])
    xo_ref[...] = x_new
    h2 = _rms(x_new) * gpre_ref[...] * (1.0 + scale2) + shift2
    hp_ref[...] = _pack_halves(h2[:, :d // 2], h2[:, d // 2:])
    logits = lax.dot_general(wrt_ref[...], h2, NT_DIMS, precision=HIGHEST, preferred_element_type=F32)
    sc_ref[...] = 1.0 / (1.0 + jnp.exp(-logits))


def _merge(o_rf, o_rb, o_gf, o_gb, ret, gla, dif, xs, mod, g_post, g_pre_ffn, gla_g, w_out, w_rt,
           n_rows, n_batch, seq):
    d = xs.shape[-1]
    tm = TOKEN_TILE
    row = lambda i: (i, 0)
    gate_col = lambda i: (i, 2)
    const = lambda i: (0, 0)
    return pl.pallas_call(
        functools.partial(_merge_kernel, seq // tm, n_batch),
        out_shape=(jax.ShapeDtypeStruct((n_rows, d), F32), jax.ShapeDtypeStruct((n_rows, d // 2), U32),
                   jax.ShapeDtypeStruct((N_EXPERTS, n_rows), F32)),
        grid=(n_rows // tm,),
        in_specs=[
            pl.BlockSpec((tm, 256), row), pl.BlockSpec((tm, 256), row),
            pl.BlockSpec((tm, 256), row), pl.BlockSpec((tm, 256), row),
            pl.BlockSpec((tm, 256), gate_col), pl.BlockSpec((tm, 256), gate_col),
            pl.BlockSpec((tm, 512), row), pl.BlockSpec((tm, d), row),
            pl.BlockSpec(mod.shape, const), pl.BlockSpec((1, d), const), pl.BlockSpec((1, d), const),
            pl.BlockSpec((1, 256), const), pl.BlockSpec(w_out.shape, const), pl.BlockSpec(w_rt.shape, const),
        ],
        out_specs=(pl.BlockSpec((tm, d), row), pl.BlockSpec((tm, d // 2), row),
                   pl.BlockSpec((N_EXPERTS, tm), lambda i: (0, i))),
        compiler_params=_cparams("parallel"),
        name="merge",
    )(o_rf, o_rb, o_gf, o_gb, ret, gla, dif, xs, mod, g_post, g_pre_ffn, gla_g, w_out, w_rt)


def _route_kernel(sc_ref, bias_ref, eidx_ref, wts_ref, pos_ref, meta_ref):
    tt = sc_ref.shape[-1]
    scores = sc_ref[...]
    sel = scores + bias_ref[...][:, 0:1]
    sub = lax.broadcasted_iota(I32, (GROUP_SIZE, tt), 0)
    neg = -jnp.inf
    gscore = []
    for g in range(N_GROUPS):
        xg = sel[g * GROUP_SIZE:(g + 1) * GROUP_SIZE]
        m1 = jnp.max(xg, axis=0, keepdims=True)
        i1 = jnp.min(jnp.where(xg == m1, sub, GROUP_SIZE), axis=0, keepdims=True)
        m2 = jnp.max(jnp.where(sub == i1, neg, xg), axis=0, keepdims=True)
        gscore.append(m1 + m2)
    rows = []
    for g in range(N_GROUPS):
        rank = jnp.zeros((1, tt), I32)
        for o in range(N_GROUPS):
            if o == g:
                continue
            ahead = (gscore[o] >= gscore[g]) if o < g else (gscore[o] > gscore[g])
            rank = rank + ahead.astype(I32)
        rows.append(jnp.where(rank < TOPK_GROUPS, sel[g * GROUP_SIZE:(g + 1) * GROUP_SIZE], neg))
    masked = jnp.concatenate(rows, axis=0)
    eio = lax.broadcasted_iota(I32, (N_EXPERTS, tt), 0)
    member = jnp.zeros((N_EXPERTS, tt), F32)
    idxs, ws = [], []
    for _ in range(TOP_K):
        m = jnp.max(masked, axis=0, keepdims=True)
        i = jnp.min(jnp.where(masked == m, eio, N_EXPERTS), axis=0, keepdims=True)
        hit = eio == i
        idxs.append(i)
        ws.append(jnp.sum(jnp.where(hit, scores, 0.0), axis=0, keepdims=True))
        member = jnp.where(hit, 1.0, member)
        masked = jnp.where(hit, neg, masked)
    wsum = ws[0]
    for w in ws[1:]:
        wsum = wsum + w
    ti = lax.broadcasted_iota(I32, (tt, tt), 0)
    tj = lax.broadcasted_iota(I32, (tt, tt), 1)
    before = jnp.where(ti < tj, 1.0, 0.0).astype(BF16)
    rank_in_e = jnp.dot(member.astype(BF16), before, preferred_element_type=F32)
    cnt = jnp.sum(member, axis=1, keepdims=True)
    padded = jnp.floor((cnt + (SUBLANES - 1)) * (1.0 / SUBLANES)) * SUBLANES
    ei = lax.broadcasted_iota(I32, (N_EXPERTS, N_EXPERTS), 0)
    ej = lax.broadcasted_iota(I32, (N_EXPERTS, N_EXPERTS), 1)
    lower = jnp.where(ej < ei, 1.0, 0.0)
    off = jnp.dot(lower, jnp.broadcast_to(padded, (N_EXPERTS, LANES)), precision=HIGHEST,
                  preferred_element_type=F32)
    slot = rank_in_e + off[:, 0:1]
    zrow_i = jnp.zeros((SUBLANES - TOP_K, tt), I32)
    zrow_f = jnp.zeros((SUBLANES - TOP_K, tt), F32)
    pos = [jnp.sum(jnp.where(eio == i, slot, 0.0), axis=0, keepdims=True).astype(I32) for i in idxs]
    eidx_ref[...] = jnp.concatenate(idxs + [zrow_i], axis=0)
    wts_ref[...] = jnp.concatenate([w / wsum * ROUTED_SCALE for w in ws] + [zrow_f], axis=0)
    pos_ref[...] = jnp.concatenate(pos + [zrow_i], axis=0)
    meta_ref[0] = jnp.concatenate([jnp.broadcast_to(cnt, (N_EXPERTS, LANES)), off], axis=1).astype(I32)


def _route(scores_t, bias, tt, row0, n):
    tile0 = row0 // tt
    tok = lambda i: (0, i)
    return pl.pallas_call(
        _route_kernel,
        out_shape=(jax.ShapeDtypeStruct((SUBLANES, n), I32), jax.ShapeDtypeStruct((SUBLANES, n), F32),
                   jax.ShapeDtypeStruct((SUBLANES, n), I32),
                   jax.ShapeDtypeStruct((n // tt, N_EXPERTS, 2 * LANES), I32)),
        grid=(n // tt,),
        in_specs=[pl.BlockSpec((N_EXPERTS, tt), lambda i: (0, i + tile0)),
                  pl.BlockSpec((N_EXPERTS, LANES), lambda i: (0, 0))],
        out_specs=(pl.BlockSpec((SUBLANES, tt), tok), pl.BlockSpec((SUBLANES, tt), tok),
                   pl.BlockSpec((SUBLANES, tt), tok),
                   pl.BlockSpec((1, N_EXPERTS, 2 * LANES), lambda i: (i, 0, 0))),
        compiler_params=_cparams("parallel"),
        name="route",
    )(scores_t, jnp.broadcast_to(bias[:, None], (N_EXPERTS, LANES)))


def _moe_kernel(seq, n_batch, tile0, has_prev, pos_ref, wts_ref, meta_ref,
                hp_ref, x_ref, mod_ref, gpost_ref, weg_ref, weu_ref, wed_ref, wsg_ref, wsu_ref, wsd_ref, *refs):
    o_ref, xg_ref, y_ref, rt_ref = refs[1:] if has_prev else refs
    tt, half = hp_ref.shape
    d = 2 * half
    eg = pl.program_id(1)

    @pl.when((pl.program_id(0) == 0) & (eg == 0))
    def _():
        def zero(i, carry):
            for u in range(SUBLANES):
                rt_ref[i * SUBLANES + u] = 0
            return carry
        lax.fori_loop(0, rt_ref.shape[0] // SUBLANES, zero, 0)

    @pl.when(eg == 0)
    def _():
        def scatter(t, carry):
            for k in range(TOP_K):
                rt_ref[pos_ref[t * SUBLANES + k]] = t
            return carry
        lax.fori_loop(0, tt, scatter, 0)

    sub = lax.broadcasted_iota(I32, (SUBLANES, half), 0)

    def gather8(src_ref, row_of):
        buf = jnp.zeros((SUBLANES, half), U32)
        for u in range(SUBLANES):
            buf = jnp.where(sub == u, src_ref[pl.ds(row_of(u), 1), :], buf)
        return buf

    def expert_block(ge, off, j, carry):
        base = pl.multiple_of(off + j * MOE_BLOCK, SUBLANES)
        for g in range(MOE_BLOCK // SUBLANES):
            xg_ref[g * SUBLANES:(g + 1) * SUBLANES, :] = gather8(
                hp_ref, lambda u, g=g: rt_ref[base + g * SUBLANES + u])
        xa, xb = _unpack_halves(xg_ref[...])
        xa, xb = xa.astype(BF16), xb.astype(BF16)
        hg = (jnp.dot(xa, weg_ref[ge, 0:half, :], preferred_element_type=F32)
              + jnp.dot(xb, weg_ref[ge, half:d, :], preferred_element_type=F32))
        hu = (jnp.dot(xa, weu_ref[ge, 0:half, :], preferred_element_type=F32)
              + jnp.dot(xb, weu_ref[ge, half:d, :], preferred_element_type=F32))
        y = jnp.dot((_silu(hg) * hu).astype(BF16), wed_ref[ge], preferred_element_type=F32)
        y_ref[pl.ds(base, MOE_BLOCK), :] = _pack_halves(y[:, :half], y[:, half:])
        return carry

    for ge in range(EXPERT_GROUP):
        e = eg * EXPERT_GROUP + ge
        n_blocks = (meta_ref[0, e] + (MOE_BLOCK - 1)) // MOE_BLOCK
        lax.fori_loop(0, n_blocks, functools.partial(expert_block, ge, meta_ref[1, e]), 0)

    @pl.when(eg == N_EXPERTS // EXPERT_GROUP - 1)
    def _():
        def tokens(g, carry):
            t0 = pl.multiple_of(g * SUBLANES, SUBLANES)
            wg = wts_ref[pl.ds(t0, SUBLANES), :]
            acc_a = jnp.zeros((SUBLANES, half), F32)
            acc_b = jnp.zeros((SUBLANES, half), F32)
            for k in range(TOP_K):
                ya, yb = _unpack_halves(gather8(y_ref, lambda u, k=k: pos_ref[(t0 + u) * SUBLANES + k]))
                acc_a = acc_a + wg[:, k:k + 1] * ya
                acc_b = acc_b + wg[:, k:k + 1] * yb
            o_ref[pl.ds(t0, SUBLANES), 0:half] = acc_a
            o_ref[pl.ds(t0, SUBLANES), half:d] = acc_b
            return carry

        lax.fori_loop(0, tt // SUBLANES, tokens, 0)
        rows = (pl.program_id(0) + tile0) * tt + lax.broadcasted_iota(I32, (tt, 1), 0)
        rb = jnp.minimum(rows // seq, n_batch)
        gate2 = jnp.zeros((tt, d), F32)
        for bi in range(n_batch + 1):
            gate2 = jnp.where(rb == bi, mod_ref[bi:bi + 1, 5 * d:6 * d], gate2)
        xa, xb = _unpack_halves(hp_ref[...])
        xa, xb = xa.astype(BF16), xb.astype(BF16)
        hg = (jnp.dot(xa, wsg_ref[0:half, :], preferred_element_type=F32)
              + jnp.dot(xb, wsg_ref[half:d, :], preferred_element_type=F32))
        hu = (jnp.dot(xa, wsu_ref[0:half, :], preferred_element_type=F32)
              + jnp.dot(xb, wsu_ref[half:d, :], preferred_element_type=F32))
        f = jnp.dot((_silu(hg) * hu).astype(BF16), wsd_ref[...], preferred_element_type=F32) + o_ref[...]
        o_ref[...] = x_ref[...] + gate2 * (_rms(f) * gpost_ref[...])


def _moe_segment(prev, scores_t, hp, xs, mod, router_bias, g_post, weights, row0, n, tt, n_batch, seq):
    n_all, half = hp.shape
    d = 2 * half
    w_eg, w_eu, w_ed, w_sg, w_su, w_sd = weights
    de = w_eg.shape[-1]
    n_tiles, tile0 = n // tt, row0 // tt
    n_slots = -(-(tt * TOP_K + N_EXPERTS * (SUBLANES - 1) + MOE_BLOCK) // (SUBLANES * LANES)) * SUBLANES * LANES
    e_idx, wts, pos, meta = _route(scores_t, router_bias, tt, row0, n)
    cnt, off = meta[:, :, 0], meta[:, :, LANES]

    def per_token(a):
        return a.T.reshape(-1)

    meta_s = jnp.zeros((n_tiles, SUBLANES, LANES), I32)
    meta_s = meta_s.at[:, 0, :N_EXPERTS].set(cnt).at[:, 1, :N_EXPERTS].set(off).reshape(-1, LANES)

    tile = lambda i, e: (i + tile0, 0)
    const = lambda i, e: (0, 0)
    expert = lambda i, e: (e, 0, 0)
    smem = functools.partial(pl.BlockSpec, memory_space=pltpu.SMEM)
    eg = EXPERT_GROUP
    in_specs = [
        smem((SUBLANES * tt,), lambda i, e: (i,)), pl.BlockSpec((tt, SUBLANES), lambda i, e: (i, 0)),
        smem((SUBLANES, LANES), lambda i, e: (i, 0)),
        pl.BlockSpec((tt, half), tile), pl.BlockSpec((tt, d), tile),
        pl.BlockSpec(mod.shape, const), pl.BlockSpec((1, d), const),
        pl.BlockSpec((eg, d, de), expert), pl.BlockSpec((eg, d, de), expert), pl.BlockSpec((eg, de, d), expert),
        pl.BlockSpec(w_sg.shape, const), pl.BlockSpec(w_su.shape, const), pl.BlockSpec(w_sd.shape, const),
    ]
    args = [per_token(pos), wts.T, meta_s, hp, xs, mod, g_post, w_eg, w_eu, w_ed, w_sg, w_su, w_sd]
    aliases = {}
    if prev is not None:
        in_specs.append(pl.BlockSpec(memory_space=pl.ANY))
        args.append(prev)
        aliases = {len(args) - 1: 0}
    return pl.pallas_call(
        functools.partial(_moe_kernel, seq, n_batch, tile0, prev is not None),
        out_shape=jax.ShapeDtypeStruct((n_all, d), F32),
        grid=(n_tiles, N_EXPERTS // EXPERT_GROUP),
        in_specs=in_specs,
        out_specs=pl.BlockSpec((tt, d), tile),
        scratch_shapes=[pltpu.VMEM((MOE_BLOCK, half), U32), pltpu.VMEM((n_slots, half), U32),
                        pltpu.SMEM((n_slots,), I32)],
        input_output_aliases=aliases,
        compiler_params=_cparams("arbitrary", "arbitrary"),
        name="moe",
    )(*args)


def _moe_layer(scores_t, hp, xs, mod, router_bias, g_post, weights, n_lat, n_batch, seq):
    pick = lambda rows: next(t for t in MOE_TILES if rows % t == 0)
    out = _moe_segment(None, scores_t, hp, xs, mod, router_bias, g_post, weights, 0, n_lat, pick(n_lat),
                       n_batch, seq)
    n_ctx = hp.shape[0] - n_lat
    if n_ctx:
        tt = next(t for t in MOE_TILES if n_ctx % t == 0 and n_lat % t == 0)
        out = _moe_segment(out, scores_t, hp, xs, mod, router_bias, g_post, weights, n_lat, n_ctx, tt,
                           n_batch, seq)
    return out


def kernel(x, c, ctx, c_ctx, w_mod, b_mod, g_pre_mix, g_post_mix, g_pre_ffn, g_post_ffn, w_in, w_out,
           ret_decay_logit, gla_w_gate, gla_b_gate, gla_norm_g, diff_lambda, diff_norm_g,
           w_router, router_bias, w_exp_gate, w_exp_up, w_exp_down, w_sh_gate, w_sh_up, w_sh_down):
    n_batch, seq, d = x.shape
    ctx_len = ctx.shape[1]
    depth = w_mod.shape[0]
    n_lat = n_batch * seq
    assert seq % TOKEN_TILE == 0 and (n_batch * ctx_len) % TOKEN_TILE == 0 and n_batch < SUBLANES
    assert seq % ATTN_TK == 0 and seq % CHUNK == 0 and ctx_len % CHUNK == 0 and n_lat % ctx_len == 0

    xs = jnp.concatenate([x.reshape(n_lat, d), ctx.reshape(n_batch * ctx_len, d)], axis=0)
    cond = jnp.zeros((SUBLANES, d), F32).at[:n_batch].set(c).at[n_batch].set(c_ctx)
    mods = _modulation(cond, w_mod, b_mod)
    tables = _rope_tables(seq, RET_DK, TOKEN_TILE) + _rope_tables(seq, DIFF_DH, TOKEN_TILE)
    lr0 = N_HEADS * (2 * RET_DK + 2 * RET_DV + 2 * GLA_DK + 2 * GLA_DV)
    row = lambda a: a[None, :]

    for layer in range(depth):
        need_ctx = layer < depth - 1
        lam_init = 0.8 - 0.6 * math.exp(-0.3 * layer)
        mod = mods[layer]
        wl = w_in[layer]
        w_r = jnp.concatenate([wl[:, :lr0], wl[:, lr0:lr0 + GLA_LOWRANK],
                               jnp.zeros((d, LANES - GLA_LOWRANK), F32), wl[:, lr0 + GLA_LOWRANK:]],
                              axis=1).astype(BF16)
        ret, gla, glr, dq, dk, dv = _in_projection(xs, mod, row(g_pre_mix[layer]), w_r, tables, n_batch, seq)

        o_rf, o_gf, o_rb, o_gb = _scan(ret, gla, glr, _scan_tables(ret_decay_logit[layer]), gla_w_gate[layer],
                                       gla_b_gate[layer], n_batch, seq, ctx_len)

        lp = jnp.zeros((SUBLANES, LANES), F32).at[:4, :DIFF_DH].set(diff_lambda[layer])
        n_all = dv.shape[0]
        vt = jnp.concatenate([dv.T.reshape(N_HEADS, LANES, n_all), jnp.ones((N_HEADS, ATTN_ONES, n_all), BF16)],
                             axis=1).reshape(N_HEADS * (LANES + ATTN_ONES), n_all)
        g_col = jnp.broadcast_to(diff_norm_g[layer][:, None], (LANES, LANES))
        attn = functools.partial(_diff_attention, dq, dk, vt, lp, g_col, lam_init, n_batch, seq, ctx_len)
        dif = attn(True)
        n_rows = n_lat
        if need_ctx:
            dif = jnp.concatenate([dif, attn(False)], axis=0)
            n_rows = xs.shape[0]

        xs, hp, scores_t = _merge(o_rf, o_rb, o_gf, o_gb, ret, gla, dif, xs, mod, row(g_post_mix[layer]),
                                  row(g_pre_ffn[layer]), row(jnp.tile(gla_norm_g[layer], N_HEADS)),
                                  w_out[layer].astype(BF16), w_router[layer].T, n_rows, n_batch, seq)
        weights = tuple(w[layer].astype(BF16) for w in
                        (w_exp_gate, w_exp_up, w_exp_down, w_sh_gate, w_sh_up, w_sh_down))
        xs = _moe_layer(scores_t, hp, xs, mod, router_bias[layer], row(g_post_ffn[layer]), weights,
                        n_lat, n_batch, seq)
    return xs[:n_lat].reshape(n_batch, seq, d)
```

```python
import functools
import math

import jax
import jax.numpy as jnp
from jax import lax
from jax.experimental import pallas as pl
from jax.experimental.pallas import tpu as pltpu

F32 = jnp.float32
BF16 = jnp.bfloat16
I32 = jnp.int32
U32 = jnp.uint32

GRID_W = 64
CHUNK = 128
N_HEADS = 4
RET_DK, RET_DV = 32, 64
GLA_DK, GLA_DV = 32, 64
GLA_LOWRANK = 16
GLA_TAU = 16.0
DIFF_DH = 64
ROPE_BASE = 10000.0
N_EXPERTS = 64
TOP_K = 6
N_GROUPS = 8
TOPK_GROUPS = 4
GROUP_SIZE = N_EXPERTS // N_GROUPS
ROUTED_SCALE = 2.5
EPS = 1e-6
GLA_SUB = 16

LANES = 128
SUBLANES = 8
VMEM_LIMIT_BYTES = 56 * 1024 * 1024

TOKEN_TILE = 512
ATTN_TQ = 512
ATTN_TK = 512
ATTN_ONES = 16
MOE_TILES = (1024, 512)
MOE_BLOCK = 128
EXPERT_GROUP = 4

HIGHEST = lax.Precision.HIGHEST
NT_DIMS = (((1,), (1,)), ((), ()))
TN_DIMS = (((0,), (0,)), ((), ()))


def _cparams(*sem):
    return pltpu.CompilerParams(dimension_semantics=sem, vmem_limit_bytes=VMEM_LIMIT_BYTES)


def _log_sigmoid(x):
    return jnp.minimum(x, 0.0) - jnp.log(1.0 + jnp.exp(-jnp.abs(x)))


def _silu(x):
    return x * (1.0 / (1.0 + jnp.exp(-x)))


def _rms(x):
    return x * lax.rsqrt(jnp.mean(x * x, axis=-1, keepdims=True) + EPS)


def _pack_halves(a, b):
    ua = lax.bitcast_convert_type(a.astype(BF16).astype(F32), U32)
    ub = lax.bitcast_convert_type(b.astype(BF16).astype(F32), U32)
    return (ua & jnp.uint32(0xFFFF0000)) | (ub >> 16)


def _unpack_halves(w):
    a = lax.bitcast_convert_type(w & jnp.uint32(0xFFFF0000), F32)
    b = lax.bitcast_convert_type(w << 16, F32)
    return a, b


def _mod_kernel(cond_ref, w_ref, b_ref, o_ref):
    a = _silu(cond_ref[...])
    o_ref[0] = jnp.dot(a, w_ref[0], precision=HIGHEST, preferred_element_type=F32) + b_ref[0]


def _modulation(cond, w_mod, b_mod):
    n_layers, d, d6 = w_mod.shape
    tn = 1024
    return pl.pallas_call(
        _mod_kernel,
        out_shape=jax.ShapeDtypeStruct((n_layers, SUBLANES, d6), F32),
        grid=(n_layers, d6 // tn),
        in_specs=[
            pl.BlockSpec((SUBLANES, d), lambda l, j: (0, 0)),
            pl.BlockSpec((1, d, tn), lambda l, j: (l, 0, j)),
            pl.BlockSpec((1, 1, tn), lambda l, j: (l, 0, j)),
        ],
        out_specs=pl.BlockSpec((1, SUBLANES, tn), lambda l, j: (l, 0, j)),
        compiler_params=_cparams("parallel", "parallel"),
        name="modulation",
    )(cond, w_mod, b_mod.reshape(n_layers, 1, d6))


def _rope(x, cos, sin, quarter):
    lane = lax.broadcasted_iota(I32, (1, LANES), 1)
    first = (lane % (2 * quarter)) < quarter
    outs = []
    for c in range(x.shape[-1] // LANES):
        xc = x[:, c * LANES:(c + 1) * LANES]
        partner = jnp.where(first, pltpu.roll(xc, LANES - quarter, 1), pltpu.roll(xc, quarter, 1))
        outs.append(xc * cos + partner * sin)
    return outs[0] if len(outs) == 1 else jnp.concatenate(outs, axis=-1)


def _inproj_kernel(tiles_per_batch, n_batch, x_ref, mod_ref, g_ref, w_ref, c32_ref, s32_ref, c64_ref, s64_ref,
                   ret_ref, gla_ref, glr_ref, dq_ref, dk_ref, dv_ref):
    d = x_ref.shape[-1]
    r = jnp.minimum(pl.program_id(0) // tiles_per_batch, n_batch)
    shift = mod_ref[pl.ds(r, 1), 0:d]
    scale = mod_ref[pl.ds(r, 1), d:2 * d]
    h = (_rms(x_ref[...]) * g_ref[...] * (1.0 + scale) + shift).astype(BF16)

    def proj(lo, hi):
        return jnp.dot(h, w_ref[:, lo:hi], preferred_element_type=F32)

    c32, s32 = c32_ref[...], s32_ref[...]
    c64, s64 = c64_ref[...], s64_ref[...]
    ret = proj(0, 768)
    ret_ref[:, 0:128] = _rope(ret[:, 0:128], c32, s32, RET_DK // 4).astype(BF16)
    ret_ref[:, 128:256] = (_rope(ret[:, 128:256], c32, s32, RET_DK // 4) * RET_DK ** -0.5).astype(BF16)
    ret_ref[:, 256:768] = ret[:, 256:768].astype(BF16)
    gla = proj(768, 1536)
    gla_ref[:, 0:128] = (gla[:, 0:128] * GLA_DK ** -0.5).astype(BF16)
    gla_ref[:, 128:768] = gla[:, 128:768].astype(BF16)
    glr_ref[...] = proj(1536, 1664)
    dq_ref[...] = (_rope(proj(1664, 2176), c64, s64, DIFF_DH // 4) * DIFF_DH ** -0.5).astype(BF16)
    dk_ref[...] = _rope(proj(2176, 2688), c64, s64, DIFF_DH // 4).astype(BF16)
    dv_ref[...] = proj(2688, 3200).astype(BF16)


def _in_projection(xs, mod, g_pre, w_r, tables, n_batch, seq):
    n, d = xs.shape
    tm = TOKEN_TILE
    tiles_per_batch = seq // tm
    n_lat_tiles = n_batch * tiles_per_batch
    c32, s32, c64, s64 = tables

    def tab_map(i):
        return (jnp.where(i < n_lat_tiles, i % tiles_per_batch, tiles_per_batch), 0)

    row = lambda i: (i, 0)
    const = lambda i: (0, 0)
    tab_spec = pl.BlockSpec((tm, LANES), tab_map)
    return pl.pallas_call(
        functools.partial(_inproj_kernel, tiles_per_batch, n_batch),
        out_shape=(
            jax.ShapeDtypeStruct((n, 768), BF16), jax.ShapeDtypeStruct((n, 768), BF16),
            jax.ShapeDtypeStruct((n, LANES), F32),
            jax.ShapeDtypeStruct((n, 512), BF16), jax.ShapeDtypeStruct((n, 512), BF16),
            jax.ShapeDtypeStruct((n, 512), BF16)),
        grid=(n // tm,),
        in_specs=[
            pl.BlockSpec((tm, d), row),
            pl.BlockSpec(mod.shape, const),
            pl.BlockSpec((1, d), const),
            pl.BlockSpec(w_r.shape, const),
            tab_spec, tab_spec, tab_spec, tab_spec,
        ],
        out_specs=(
            pl.BlockSpec((tm, 768), row), pl.BlockSpec((tm, 768), row), pl.BlockSpec((tm, LANES), row),
            pl.BlockSpec((tm, 512), row), pl.BlockSpec((tm, 512), row), pl.BlockSpec((tm, 512), row)),
        compiler_params=_cparams("parallel"),
        name="in_projection",
    )(xs, mod, g_pre, w_r, c32, s32, c64, s64)


def _rope_tables(seq, head_dim, extra_rows):
    half, quarter = head_dim // 2, head_dim // 4
    freqs = ROPE_BASE ** (-jnp.arange(quarter, dtype=F32) / quarter)
    t = jnp.arange(seq)
    row = (t // GRID_W).astype(F32)
    col = (t % GRID_W).astype(F32)
    j = jnp.arange(LANES) % head_dim
    jj = j % half
    pos = jnp.where((j < half)[None, :], row[:, None], col[:, None])
    ang = pos * freqs[jj % quarter][None, :]
    cos = jnp.cos(ang)
    sin = jnp.sin(ang) * jnp.where(jj < quarter, -1.0, 1.0)[None, :]
    cos = jnp.concatenate([cos, jnp.ones((extra_rows, LANES), F32)], axis=0)
    sin = jnp.concatenate([sin, jnp.zeros((extra_rows, LANES), F32)], axis=0)
    return cos, sin


def _head_stack(x, width):
    lane = lax.broadcasted_iota(I32, (1, x.shape[-1]), 1)
    zero = jnp.zeros_like(x)
    return jnp.concatenate([jnp.where(lane // width == h, x, zero) for h in range(N_HEADS)], axis=0)


def _head_select(x4, rows):
    lane = lax.broadcasted_iota(I32, (1, x4.shape[-1]), 1)
    out = jnp.zeros((rows, x4.shape[-1]), F32)
    for h in range(N_HEADS):
        out = out + jnp.where(lane // RET_DV == h, x4[h * rows:(h + 1) * rows], 0.0)
    return out


def _scan_tables_kernel(rl_lane_ref, rl_rows_ref, rl_col_ref, dmat_ref, xi_ref, zeta_ref, gchunk_ref):
    c = CHUNK
    idx = lax.broadcasted_iota(I32, (c, 1), 0).astype(F32)
    ri4 = lax.broadcasted_iota(I32, (N_HEADS * c, c), 0) % c
    ci4 = lax.broadcasted_iota(I32, (N_HEADS * c, c), 1)
    dist = jnp.abs(ri4 - ci4).astype(F32)
    for d in range(2):
        lg_lane = _log_sigmoid(rl_lane_ref[d])
        lg_rows = _log_sigmoid(rl_rows_ref[d])
        lg_col = _log_sigmoid(rl_col_ref[d])
        att4 = (ri4 <= ci4) if d else (ri4 >= ci4)
        dmat_ref[d] = jnp.where(att4, jnp.exp(dist * lg_rows), 0.0)
        xi_ref[d] = jnp.exp(((c - idx) if d else (idx + 1.0)) * lg_lane)
        zeta_ref[d] = jnp.exp((idx if d else (c - 1.0 - idx)) * lg_lane)
        g_chunk = jnp.exp(float(c) * lg_col)
        gchunk_ref[d] = jnp.concatenate([g_chunk, g_chunk], axis=1)


def _scan_tables(ret_logit):
    c = CHUNK
    rl_lane = jnp.repeat(ret_logit, RET_DK, axis=1)[:, None, :]
    rl_rows = jnp.broadcast_to(jnp.repeat(ret_logit, c, axis=1)[:, :, None], (2, N_HEADS * c, c))
    rl_col = jnp.broadcast_to(jnp.repeat(ret_logit, RET_DK, axis=1)[:, :, None], (2, LANES, LANES))
    return pl.pallas_call(
        _scan_tables_kernel,
        out_shape=(jax.ShapeDtypeStruct((2, N_HEADS * c, c), F32), jax.ShapeDtypeStruct((2, c, LANES), F32),
                   jax.ShapeDtypeStruct((2, c, LANES), F32), jax.ShapeDtypeStruct((2, LANES, 2 * LANES), F32)),
        name="scan_tables",
    )(rl_lane, rl_rows, rl_col)


def _state_block_mask():
    return (lax.broadcasted_iota(I32, (LANES, 2 * LANES), 0) // RET_DK
            == lax.broadcasted_iota(I32, (LANES, 2 * LANES), 1) // RET_DV)


def _ret_chain(ret_ref, dmat, xi, zeta, g_chunk, ore_ref, sr_ref):
    c = CHUNK
    q = ret_ref[:, 0:128]
    k = ret_ref[:, 128:256]
    v = ret_ref[:, 256:512]
    s = lax.dot_general(_head_stack(q, RET_DK), k, NT_DIMS, preferred_element_type=F32)
    sr = sr_ref[...]
    inter = jnp.dot((q.astype(F32) * xi).astype(BF16), sr.astype(BF16), preferred_element_type=F32)
    kz = (k.astype(F32) * zeta).astype(BF16)
    u = lax.dot_general(kz, v, TN_DIMS, preferred_element_type=F32)
    yield
    o4 = jnp.dot((s * dmat).astype(BF16), v, preferred_element_type=F32)
    sr_ref[...] = sr * g_chunk + jnp.where(_state_block_mask(), u, 0.0)
    yield
    ore_ref[...] = _head_select(o4, c) + inter


def _gla_chain(rev, gla_ref, glr_ref, gw, gb, ogl_ref, sg_ref):
    c = CHUNK
    ri = lax.broadcasted_iota(I32, (c, c), 0)
    ci = lax.broadcasted_iota(I32, (c, c), 1)
    attends = (ri <= ci) if rev else (ri >= ci)
    bd = _state_block_mask()
    gq = gla_ref[:, 0:128].astype(F32)
    gk = gla_ref[:, 128:256].astype(F32)
    gv = gla_ref[:, 256:512]
    z = jnp.dot(glr_ref[...], gw, precision=HIGHEST, preferred_element_type=F32) + gb
    yield
    la = _log_sigmoid(z) * (1.0 / GLA_TAU)
    if rev:
        first = (ri // GLA_SUB) * GLA_SUB + (GLA_SUB - 1)
        ref_sel = ci >= first
    else:
        first = (ri // GLA_SUB) * GLA_SUB
        ref_sel = ci <= first
    sel = jnp.concatenate([jnp.where(attends, 1.0, 0.0), jnp.where(ref_sel, 1.0, 0.0)], axis=0).astype(BF16)
    la_hi = la.astype(BF16)
    la_lo = (la - la_hi.astype(F32)).astype(BF16)
    sums = (jnp.dot(sel, la_hi, preferred_element_type=F32) + jnp.dot(sel, la_lo, preferred_element_type=F32))
    b, refrow = sums[0:c], sums[c:2 * c]
    yield
    qs = gq * jnp.exp(b - refrow)
    sg = sg_ref[...]
    inter = jnp.dot((gq * jnp.exp(b)).astype(BF16), sg.astype(BF16), preferred_element_type=F32)
    b_last = b[0:1] if rev else b[c - 1:c]
    kz = (gk * jnp.exp(b_last - b)).astype(BF16)
    u = lax.dot_general(kz, gv, TN_DIMS, preferred_element_type=F32)
    eye = lax.broadcasted_iota(I32, (LANES, LANES), 0) == lax.broadcasted_iota(I32, (LANES, LANES), 1)
    g_col = jnp.sum(jnp.where(eye, jnp.exp(b_last), 0.0), axis=1, keepdims=True)
    jcol = lax.broadcasted_iota(I32, (c, 1), 0)
    rr = lax.broadcasted_iota(I32, (N_HEADS * GLA_SUB, c), 0) % GLA_SUB
    cc = lax.broadcasted_iota(I32, (N_HEADS * GLA_SUB, c), 1)
    pieces = []
    for blk in range(c // GLA_SUB):
        lo = blk * GLA_SUB
        ref_b = refrow[lo:lo + 1]
        seen = (jcol >= lo) if rev else (jcol < lo + GLA_SUB)
        ks = (gk * jnp.exp(jnp.where(seen, ref_b - b, -jnp.inf))).astype(BF16)
        qz = _head_stack(qs[lo:lo + GLA_SUB], GLA_DK).astype(BF16)
        att = lax.dot_general(qz, ks, NT_DIMS, preferred_element_type=F32)
        ok = (cc >= rr + lo) if rev else (cc <= rr + lo)
        att = jnp.where(ok, att, 0.0).astype(BF16)
        pieces.append(_head_select(jnp.dot(att, gv, preferred_element_type=F32), GLA_SUB))
        if blk % 2:
            yield
    sg_ref[...] = sg * g_col + jnp.where(bd, u, 0.0)
    ogl_ref[...] = jnp.concatenate(pieces, axis=0) + inter


def _scan_kernel(ret_f, gla_f, glr_f, ret_b, gla_b, glr_b, dmat_ref, xi_ref, zeta_ref, gchunk_ref, gw_ref, gb_ref,
                 orf_ref, ogf_ref, orb_ref, ogb_ref, sr_ref, sg_ref):
    @pl.when(pl.program_id(1) == 0)
    def _():
        sr_ref[...] = jnp.zeros_like(sr_ref)
        sg_ref[...] = jnp.zeros_like(sg_ref)

    chains = []
    for d, (ret_ref, gla_ref, glr_ref, ore_ref, ogl_ref) in enumerate(
            ((ret_f, gla_f, glr_f, orf_ref, ogf_ref), (ret_b, gla_b, glr_b, orb_ref, ogb_ref))):
        chains.append(_gla_chain(bool(d), gla_ref, glr_ref, gw_ref[d], gb_ref[d], ogl_ref, sg_ref.at[d]))
        chains.append(_ret_chain(ret_ref, dmat_ref[d], xi_ref[d], zeta_ref[d], gchunk_ref[d], ore_ref,
                                 sr_ref.at[d]))
    while chains:
        chains = [ch for ch in chains if next(ch, True) is None]


def _scan(ret, gla, glr, tables, gla_w, gla_b, n_batch, seq, ctx_len):
    n = ret.shape[0]
    c = CHUNK
    nc_ctx, nc_lat = ctx_len // c, seq // c
    n_steps = nc_ctx + nc_lat
    ctx_base = n_batch * nc_lat

    def fwd(b, s):
        return (jnp.where(s < nc_ctx, ctx_base + b * nc_ctx + s, b * nc_lat + (s - nc_ctx)), 0)

    def bwd(b, s):
        return (jnp.where(s < nc_ctx, ctx_base + b * nc_ctx + (nc_ctx - 1 - s), b * nc_lat + (n_steps - 1 - s)), 0)

    const3 = lambda b, s: (0, 0, 0)
    gw = jnp.zeros((2, LANES, LANES), F32).at[:, :GLA_LOWRANK].set(gla_w)
    gb = gla_b[:, None, :]
    chain_in = lambda m: [pl.BlockSpec((c, 768), m), pl.BlockSpec((c, 768), m), pl.BlockSpec((c, LANES), m)]
    o_sds = jax.ShapeDtypeStruct((n, 256), F32)
    return pl.pallas_call(
        _scan_kernel,
        out_shape=(o_sds, o_sds, o_sds, o_sds),
        grid=(n_batch, n_steps),
        in_specs=chain_in(fwd) + chain_in(bwd) + [pl.BlockSpec(t.shape, const3) for t in tables]
        + [pl.BlockSpec(gw.shape, const3), pl.BlockSpec(gb.shape, const3)],
        out_specs=(pl.BlockSpec((c, 256), fwd), pl.BlockSpec((c, 256), fwd),
                   pl.BlockSpec((c, 256), bwd), pl.BlockSpec((c, 256), bwd)),
        scratch_shapes=[pltpu.VMEM((2, LANES, 2 * LANES), F32), pltpu.VMEM((2, LANES, 2 * LANES), F32)],
        compiler_params=_cparams("arbitrary", "arbitrary"),
        name="scan",
    )(ret, gla, glr, ret, gla, glr, *tables, gw, gb)


def _attn_kernel(n_lat_blocks, lam_init, *refs):
    if n_lat_blocks:
        q_ref, kc_ref, vtc_ref, kl_ref, vtl_ref, lp_ref, g_ref, o_ref, m_ref, acc_ref, st_ref = refs
    else:
        q_ref, kc_ref, vtc_ref, lp_ref, g_ref, o_ref, m_ref, acc_ref, st_ref = refs
    tq = q_ref.shape[0]
    q = q_ref[...].astype(F32)
    lane = lax.broadcasted_iota(I32, (1, LANES), 1)
    qt = jnp.concatenate([jnp.where(lane < DIFF_DH, q, 0.0).T, jnp.where(lane >= DIFF_DH, q, 0.0).T],
                         axis=1).astype(BF16)
    m_ref[...] = jnp.full(m_ref.shape, -jnp.inf, F32)
    acc_ref[...] = jnp.zeros_like(acc_ref)

    def scores(slot, kb):
        st_ref[slot, 0:kb.shape[0], :] = jnp.dot(kb, qt, preferred_element_type=F32)

    def absorb(slot, vtb):
        st = st_ref[slot, 0:vtb.shape[1], :]
        m_prev = m_ref[...]
        m_new = jnp.maximum(m_prev, jnp.max(st, axis=0, keepdims=True))
        alpha = jnp.exp(m_prev - m_new)
        p = jnp.exp(st - m_new).astype(BF16)
        acc_ref[...] = alpha * acc_ref[...] + jnp.dot(vtb, p, preferred_element_type=F32)
        m_ref[...] = m_new

    def k_lat(blk):
        return kl_ref[pl.ds(pl.multiple_of(blk * ATTN_TK, ATTN_TK), ATTN_TK), :]

    def vt_lat(blk):
        return vtl_ref[:, pl.ds(pl.multiple_of(blk * ATTN_TK, ATTN_TK), ATTN_TK)]

    scores(0, kc_ref[...])
    if not n_lat_blocks:
        absorb(0, vtc_ref[...])
    else:
        scores(1, k_lat(0))
        absorb(0, vtc_ref[...])
        n_pairs = (n_lat_blocks - 1) // 2

        def body(i, carry):
            scores(0, k_lat(2 * i + 1))
            absorb(1, vt_lat(2 * i))
            scores(1, k_lat(2 * i + 2))
            absorb(0, vt_lat(2 * i + 1))
            return carry
        lax.fori_loop(0, n_pairs, body, 0)
        done = 2 * n_pairs
        if n_lat_blocks - done == 2:
            scores(0, k_lat(done + 1))
            absorb(1, vt_lat(done))
            absorb(0, vt_lat(done + 1))
        else:
            absorb(1, vt_lat(done))

    lp = lp_ref[...]
    lam = (jnp.exp(jnp.sum(lp[0:1] * lp[1:2], axis=1, keepdims=True))
           - jnp.exp(jnp.sum(lp[2:3] * lp[3:4], axis=1, keepdims=True)) + lam_init)
    acc = acc_ref[...]
    o1 = acc[0:LANES, :tq] / acc[LANES:LANES + 1, :tq]
    o2 = acc[0:LANES, tq:] / acc[LANES:LANES + 1, tq:]
    ot = o1 - lam * o2
    ot = ot * lax.rsqrt(jnp.mean(ot * ot, axis=0, keepdims=True) + EPS) * g_ref[...][:, 0:1] * (1.0 - lam_init)
    o_ref[...] = ot.T.astype(o_ref.dtype)


def _diff_attention(dq, dk, vt, lp, g, lam_init, n_batch, seq, ctx_len, latent):
    ctx_blk0 = (n_batch * seq) // ctx_len
    vrows = LANES + ATTN_ONES
    kc_spec = pl.BlockSpec((ctx_len, LANES), lambda b, h, i: (ctx_blk0 + b, h))
    vtc_spec = pl.BlockSpec((vrows, ctx_len), lambda b, h, i: (h, ctx_blk0 + b))
    const = lambda b, h, i: (0, 0)
    if latent:
        tq = ATTN_TQ
        n_q = seq // tq
        q_map = lambda b, h, i: (b * n_q + i, h)
        in_specs = [pl.BlockSpec((tq, LANES), q_map), kc_spec, vtc_spec,
                    pl.BlockSpec((seq, LANES), lambda b, h, i: (b, h)),
                    pl.BlockSpec((vrows, seq), lambda b, h, i: (h, b))]
        args = (dq, dk, vt, dk, vt)
        n_rows, o_map, n_lat_blocks = n_batch * seq, q_map, seq // ATTN_TK
    else:
        tq, n_q = ctx_len, 1
        in_specs = [kc_spec, kc_spec, vtc_spec]
        args = (dq, dk, vt)
        n_rows, o_map, n_lat_blocks = n_batch * ctx_len, (lambda b, h, i: (b, h)), 0
    in_specs += [pl.BlockSpec(lp.shape, const), pl.BlockSpec((LANES, LANES), const)]
    return pl.pallas_call(
        functools.partial(_attn_kernel, n_lat_blocks, lam_init),
        out_shape=jax.ShapeDtypeStruct((n_rows, N_HEADS * LANES), BF16),
        grid=(n_batch, N_HEADS, n_q),
        in_specs=in_specs,
        out_specs=pl.BlockSpec((tq, LANES), o_map),
        scratch_shapes=[pltpu.VMEM((1, 2 * tq), F32), pltpu.VMEM((vrows, 2 * tq), F32),
                        pltpu.VMEM((2, max(ATTN_TK, ctx_len), 2 * tq), F32)],
        compiler_params=_cparams("parallel", "parallel", "parallel"),
        name="diff_attention_lat" if latent else "diff_attention_ctx",
    )(*args, lp, g)


def _merge_kernel(tiles_per_batch, n_batch, orf_ref, orb_ref, ogf_ref, ogb_ref, rg_ref, gr_ref, dif_ref,
                  x_ref, mod_ref, gpost_ref, gpre_ref, glag_ref, wout_ref, wrt_ref,
                  xo_ref, hp_ref, sc_ref):
    d = x_ref.shape[-1]
    r = jnp.minimum(pl.program_id(0) // tiles_per_batch, n_batch)
    gate1 = mod_ref[pl.ds(r, 1), 2 * d:3 * d]
    shift2 = mod_ref[pl.ds(r, 1), 3 * d:4 * d]
    scale2 = mod_ref[pl.ds(r, 1), 4 * d:5 * d]
    gi = lax.broadcasted_iota(I32, (256, 256), 0) // RET_DV
    gj = lax.broadcasted_iota(I32, (256, 256), 1) // RET_DV
    group_mean = jnp.where(gi == gj, 1.0 / RET_DV, 0.0).astype(BF16)

    def head_norm(o):
        ms = jnp.dot((o * o).astype(BF16), group_mean, preferred_element_type=F32)
        return o * lax.rsqrt(ms + EPS)

    ret = head_norm(orf_ref[...] + orb_ref[...]) * _silu(rg_ref[...].astype(F32))
    gla = head_norm(ogf_ref[...] + ogb_ref[...]) * glag_ref[...] * _silu(gr_ref[...].astype(F32))
    m = (jnp.dot(ret.astype(BF16), wout_ref[0:256, :], preferred_element_type=F32)
         + jnp.dot(gla.astype(BF16), wout_ref[256:512, :], preferred_element_type=F32)
         + jnp.dot(dif_ref[...], wout_ref[512:1024, :], preferred_element_type=F32))
    x_new = x_ref[...] + gate1 * (_rms(m) * gpost_ref[...])
    xo_ref[...] = x_new
    h2 = _rms(x_new) * gpre_ref[...] * (1.0 + scale2) + shift2
    hp_ref[...] = _pack_halves(h2[:, :d // 2], h2[:, d // 2:])
    logits = lax.dot_general(wrt_ref[...], h2, NT_DIMS, precision=HIGHEST, preferred_element_type=F32)
    sc_ref[...] = 1.0 / (1.0 + jnp.exp(-logits))


def _merge(o_rf, o_rb, o_gf, o_gb, ret, gla, dif, xs, mod, g_post, g_pre_ffn, gla_g, w_out, w_rt,
           n_rows, n_batch, seq):
    d = xs.shape[-1]
    tm = TOKEN_TILE
    row = lambda i: (i, 0)
    gate_col = lambda i: (i, 2)
    const = lambda i: (0, 0)
    return pl.pallas_call(
        functools.partial(_merge_kernel, seq // tm, n_batch),
        out_shape=(jax.ShapeDtypeStruct((n_rows, d), F32), jax.ShapeDtypeStruct((n_rows, d // 2), U32),
                   jax.ShapeDtypeStruct((N_EXPERTS, n_rows), F32)),
        grid=(n_rows // tm,),
        in_specs=[
            pl.BlockSpec((tm, 256), row), pl.BlockSpec((tm, 256), row),
            pl.BlockSpec((tm, 256), row), pl.BlockSpec((tm, 256), row),
            pl.BlockSpec((tm, 256), gate_col), pl.BlockSpec((tm, 256), gate_col),
            pl.BlockSpec((tm, 512), row), pl.BlockSpec((tm, d), row),
            pl.BlockSpec(mod.shape, const), pl.BlockSpec((1, d), const), pl.BlockSpec((1, d), const),
            pl.BlockSpec((1, 256), const), pl.BlockSpec(w_out.shape, const), pl.BlockSpec(w_rt.shape, const),
        ],
        out_specs=(pl.BlockSpec((tm, d), row), pl.BlockSpec((tm, d // 2), row),
                   pl.BlockSpec((N_EXPERTS, tm), lambda i: (0, i))),
        compiler_params=_cparams("parallel"),
        name="merge",
    )(o_rf, o_rb, o_gf, o_gb, ret, gla, dif, xs, mod, g_post, g_pre_ffn, gla_g, w_out, w_rt)


def _route_kernel(sc_ref, bias_ref, eidx_ref, wts_ref, pos_ref, meta_ref):
    tt = sc_ref.shape[-1]
    scores = sc_ref[...]
    sel = scores + bias_ref[...][:, 0:1]
    sub = lax.broadcasted_iota(I32, (GROUP_SIZE, tt), 0)
    neg = -jnp.inf
    gscore = []
    for g in range(N_GROUPS):
        xg = sel[g * GROUP_SIZE:(g + 1) * GROUP_SIZE]
        m1 = jnp.max(xg, axis=0, keepdims=True)
        i1 = jnp.min(jnp.where(xg == m1, sub, GROUP_SIZE), axis=0, keepdims=True)
        m2 = jnp.max(jnp.where(sub == i1, neg, xg), axis=0, keepdims=True)
        gscore.append(m1 + m2)
    rows = []
    for g in range(N_GROUPS):
        rank = jnp.zeros((1, tt), I32)
        for o in range(N_GROUPS):
            if o == g:
                continue
            ahead = (gscore[o] >= gscore[g]) if o < g else (gscore[o] > gscore[g])
            rank = rank + ahead.astype(I32)
        rows.append(jnp.where(rank < TOPK_GROUPS, sel[g * GROUP_SIZE:(g + 1) * GROUP_SIZE], neg))
    masked = jnp.concatenate(rows, axis=0)
    eio = lax.broadcasted_iota(I32, (N_EXPERTS, tt), 0)
    member = jnp.zeros((N_EXPERTS, tt), F32)
    idxs, ws = [], []
    for _ in range(TOP_K):
        m = jnp.max(masked, axis=0, keepdims=True)
        i = jnp.min(jnp.where(masked == m, eio, N_EXPERTS), axis=0, keepdims=True)
        hit = eio == i
        idxs.append(i)
        ws.append(jnp.sum(jnp.where(hit, scores, 0.0), axis=0, keepdims=True))
        member = jnp.where(hit, 1.0, member)
        masked = jnp.where(hit, neg, masked)
    wsum = ws[0]
    for w in ws[1:]:
        wsum = wsum + w
    ti = lax.broadcasted_iota(I32, (tt, tt), 0)
    tj = lax.broadcasted_iota(I32, (tt, tt), 1)
    before = jnp.where(ti < tj, 1.0, 0.0).astype(BF16)
    rank_in_e = jnp.dot(member.astype(BF16), before, preferred_element_type=F32)
    cnt = jnp.sum(member, axis=1, keepdims=True)
    padded = jnp.floor((cnt + (SUBLANES - 1)) * (1.0 / SUBLANES)) * SUBLANES
    ei = lax.broadcasted_iota(I32, (N_EXPERTS, N_EXPERTS), 0)
    ej = lax.broadcasted_iota(I32, (N_EXPERTS, N_EXPERTS), 1)
    lower = jnp.where(ej < ei, 1.0, 0.0)
    off = jnp.dot(lower, jnp.broadcast_to(padded, (N_EXPERTS, LANES)), precision=HIGHEST,
                  preferred_element_type=F32)
    slot = rank_in_e + off[:, 0:1]
    zrow_i = jnp.zeros((SUBLANES - TOP_K, tt), I32)
    zrow_f = jnp.zeros((SUBLANES - TOP_K, tt), F32)
    pos = [jnp.sum(jnp.where(eio == i, slot, 0.0), axis=0, keepdims=True).astype(I32) for i in idxs]
    eidx_ref[...] = jnp.concatenate(idxs + [zrow_i], axis=0)
    wts_ref[...] = jnp.concatenate([w / wsum * ROUTED_SCALE for w in ws] + [zrow_f], axis=0)
    pos_ref[...] = jnp.concatenate(pos + [zrow_i], axis=0)
    meta_ref[0] = jnp.concatenate([jnp.broadcast_to(cnt, (N_EXPERTS, LANES)), off], axis=1).astype(I32)


def _route(scores_t, bias, tt, row0, n):
    tile0 = row0 // tt
    tok = lambda i: (0, i)
    return pl.pallas_call(
        _route_kernel,
        out_shape=(jax.ShapeDtypeStruct((SUBLANES, n), I32), jax.ShapeDtypeStruct((SUBLANES, n), F32),
                   jax.ShapeDtypeStruct((SUBLANES, n), I32),
                   jax.ShapeDtypeStruct((n // tt, N_EXPERTS, 2 * LANES), I32)),
        grid=(n // tt,),
        in_specs=[pl.BlockSpec((N_EXPERTS, tt), lambda i: (0, i + tile0)),
                  pl.BlockSpec((N_EXPERTS, LANES), lambda i: (0, 0))],
        out_specs=(pl.BlockSpec((SUBLANES, tt), tok), pl.BlockSpec((SUBLANES, tt), tok),
                   pl.BlockSpec((SUBLANES, tt), tok),
                   pl.BlockSpec((1, N_EXPERTS, 2 * LANES), lambda i: (i, 0, 0))),
        compiler_params=_cparams("parallel"),
        name="route",
    )(scores_t, jnp.broadcast_to(bias[:, None], (N_EXPERTS, LANES)))


def _moe_kernel(seq, n_batch, tile0, has_prev, pos_ref, wts_ref, meta_ref,
                hp_ref, x_ref, mod_ref, gpost_ref, weg_ref, weu_ref, wed_ref, wsg_ref, wsu_ref, wsd_ref, *refs):
    o_ref, xg_ref, y_ref, rt_ref = refs[1:] if has_prev else refs
    tt, half = hp_ref.shape
    d = 2 * half
    eg = pl.program_id(1)

    @pl.when((pl.program_id(0) == 0) & (eg == 0))
    def _():
        def zero(i, carry):
            for u in range(SUBLANES):
                rt_ref[i * SUBLANES + u] = 0
            return carry
        lax.fori_loop(0, rt_ref.shape[0] // SUBLANES, zero, 0)

    @pl.when(eg == 0)
    def _():
        def scatter(t, carry):
            for k in range(TOP_K):
                rt_ref[pos_ref[t * SUBLANES + k]] = t
            return carry
        lax.fori_loop(0, tt, scatter, 0)

    sub = lax.broadcasted_iota(I32, (SUBLANES, half), 0)

    def gather8(src_ref, row_of):
        buf = jnp.zeros((SUBLANES, half), U32)
        for u in range(SUBLANES):
            buf = jnp.where(sub == u, src_ref[pl.ds(row_of(u), 1), :], buf)
        return buf

    row = lax.broadcasted_iota(I32, (MOE_BLOCK, half), 0)

    def expert_block(ge, base, n_valid):
        base = pl.multiple_of(base, SUBLANES)
        for i in range(MOE_BLOCK):
            xg_ref[ge, i:i + 1, :] = hp_ref[pl.ds(rt_ref[base + i], 1), :]
        yield
        xa, xb = _unpack_halves(xg_ref[ge])
        xa, xb = xa.astype(BF16), xb.astype(BF16)
        hg = (jnp.dot(xa, weg_ref[ge, 0:half, :], preferred_element_type=F32)
              + jnp.dot(xb, weg_ref[ge, half:d, :], preferred_element_type=F32))
        hu = (jnp.dot(xa, weu_ref[ge, 0:half, :], preferred_element_type=F32)
              + jnp.dot(xb, weu_ref[ge, half:d, :], preferred_element_type=F32))
        yield
        y = jnp.dot((_silu(hg) * hu).astype(BF16), wed_ref[ge], preferred_element_type=F32)
        yield
        pltpu.store(y_ref.at[pl.ds(base, MOE_BLOCK), :], _pack_halves(y[:, :half], y[:, half:]),
                    mask=row < n_valid)

    cnts = [meta_ref[0, eg * EXPERT_GROUP + ge] for ge in range(EXPERT_GROUP)]
    offs = [meta_ref[1, eg * EXPERT_GROUP + ge] for ge in range(EXPERT_GROUP)]
    chains = [expert_block(ge, offs[ge], cnts[ge]) for ge in range(EXPERT_GROUP)]
    while chains:
        chains = [ch for ch in chains if next(ch, True) is None]
    for ge in range(EXPERT_GROUP):
        def more(j, carry, ge=ge):
            for _ in expert_block(ge, offs[ge] + j * MOE_BLOCK, cnts[ge] - j * MOE_BLOCK):
                pass
            return carry
        lax.fori_loop(1, (cnts[ge] + (MOE_BLOCK - 1)) // MOE_BLOCK, more, 0)

    @pl.when(eg == N_EXPERTS // EXPERT_GROUP - 1)
    def _():
        def tokens(g, carry):
            t0 = pl.multiple_of(g * SUBLANES, SUBLANES)
            wg = wts_ref[pl.ds(t0, SUBLANES), :]
            acc_a = jnp.zeros((SUBLANES, half), F32)
            acc_b = jnp.zeros((SUBLANES, half), F32)
            for k in range(TOP_K):
                ya, yb = _unpack_halves(gather8(y_ref, lambda u, k=k: pos_ref[(t0 + u) * SUBLANES + k]))
                acc_a = acc_a + wg[:, k:k + 1] * ya
                acc_b = acc_b + wg[:, k:k + 1] * yb
            o_ref[pl.ds(t0, SUBLANES), 0:half] = acc_a
            o_ref[pl.ds(t0, SUBLANES), half:d] = acc_b
            return carry

        lax.fori_loop(0, tt // SUBLANES, tokens, 0)
        rows = (pl.program_id(0) + tile0) * tt + lax.broadcasted_iota(I32, (tt, 1), 0)
        rb = jnp.minimum(rows // seq, n_batch)
        gate2 = jnp.zeros((tt, d), F32)
        for bi in range(n_batch + 1):
            gate2 = jnp.where(rb == bi, mod_ref[bi:bi + 1, 5 * d:6 * d], gate2)
        xa, xb = _unpack_halves(hp_ref[...])
        xa, xb = xa.astype(BF16), xb.astype(BF16)
        hg = (jnp.dot(xa, wsg_ref[0:half, :], preferred_element_type=F32)
              + jnp.dot(xb, wsg_ref[half:d, :], preferred_element_type=F32))
        hu = (jnp.dot(xa, wsu_ref[0:half, :], preferred_element_type=F32)
              + jnp.dot(xb, wsu_ref[half:d, :], preferred_element_type=F32))
        f = jnp.dot((_silu(hg) * hu).astype(BF16), wsd_ref[...], preferred_element_type=F32) + o_ref[...]
        o_ref[...] = x_ref[...] + gate2 * (_rms(f) * gpost_ref[...])


def _moe_segment(prev, scores_t, hp, xs, mod, router_bias, g_post, weights, row0, n, tt, n_batch, seq):
    n_all, half = hp.shape
    d = 2 * half
    w_eg, w_eu, w_ed, w_sg, w_su, w_sd = weights
    de = w_eg.shape[-1]
    n_tiles, tile0 = n // tt, row0 // tt
    n_slots = -(-(tt * TOP_K + N_EXPERTS * (SUBLANES - 1) + MOE_BLOCK) // (SUBLANES * LANES)) * SUBLANES * LANES
    e_idx, wts, pos, meta = _route(scores_t, router_bias, tt, row0, n)
    cnt, off = meta[:, :, 0], meta[:, :, LANES]

    def per_token(a):
        return a.T.reshape(-1)

    meta_s = jnp.zeros((n_tiles, SUBLANES, LANES), I32)
    meta_s = meta_s.at[:, 0, :N_EXPERTS].set(cnt).at[:, 1, :N_EXPERTS].set(off).reshape(-1, LANES)

    tile = lambda i, e: (i + tile0, 0)
    const = lambda i, e: (0, 0)
    expert = lambda i, e: (e, 0, 0)
    smem = functools.partial(pl.BlockSpec, memory_space=pltpu.SMEM)
    eg = EXPERT_GROUP
    in_specs = [
        smem((SUBLANES * tt,), lambda i, e: (i,)), pl.BlockSpec((tt, SUBLANES), lambda i, e: (i, 0)),
        smem((SUBLANES, LANES), lambda i, e: (i, 0)),
        pl.BlockSpec((tt, half), tile), pl.BlockSpec((tt, d), tile),
        pl.BlockSpec(mod.shape, const), pl.BlockSpec((1, d), const),
        pl.BlockSpec((eg, d, de), expert), pl.BlockSpec((eg, d, de), expert), pl.BlockSpec((eg, de, d), expert),
        pl.BlockSpec(w_sg.shape, const), pl.BlockSpec(w_su.shape, const), pl.BlockSpec(w_sd.shape, const),
    ]
    args = [per_token(pos), wts.T, meta_s, hp, xs, mod, g_post, w_eg, w_eu, w_ed, w_sg, w_su, w_sd]
    aliases = {}
    if prev is not None:
        in_specs.append(pl.BlockSpec(memory_space=pl.ANY))
        args.append(prev)
        aliases = {len(args) - 1: 0}
    return pl.pallas_call(
        functools.partial(_moe_kernel, seq, n_batch, tile0, prev is not None),
        out_shape=jax.ShapeDtypeStruct((n_all, d), F32),
        grid=(n_tiles, N_EXPERTS // EXPERT_GROUP),
        in_specs=in_specs,
        out_specs=pl.BlockSpec((tt, d), tile),
        scratch_shapes=[pltpu.VMEM((EXPERT_GROUP, MOE_BLOCK, half), U32), pltpu.VMEM((n_slots, half), U32),
                        pltpu.SMEM((n_slots,), I32)],
        input_output_aliases=aliases,
        compiler_params=_cparams("arbitrary", "arbitrary"),
        name="moe",
    )(*args)


def _moe_layer(scores_t, hp, xs, mod, router_bias, g_post, weights, n_lat, n_batch, seq):
    pick = lambda rows: next(t for t in MOE_TILES if rows % t == 0)
    out = _moe_segment(None, scores_t, hp, xs, mod, router_bias, g_post, weights, 0, n_lat, pick(n_lat),
                       n_batch, seq)
    n_ctx = hp.shape[0] - n_lat
    if n_ctx:
        tt = next(t for t in MOE_TILES if n_ctx % t == 0 and n_lat % t == 0)
        out = _moe_segment(out, scores_t, hp, xs, mod, router_bias, g_post, weights, n_lat, n_ctx, tt,
                           n_batch, seq)
    return out


def kernel(x, c, ctx, c_ctx, w_mod, b_mod, g_pre_mix, g_post_mix, g_pre_ffn, g_post_ffn, w_in, w_out,
           ret_decay_logit, gla_w_gate, gla_b_gate, gla_norm_g, diff_lambda, diff_norm_g,
           w_router, router_bias, w_exp_gate, w_exp_up, w_exp_down, w_sh_gate, w_sh_up, w_sh_down):
    n_batch, seq, d = x.shape
    ctx_len = ctx.shape[1]
    depth = w_mod.shape[0]
    n_lat = n_batch * seq
    assert seq % TOKEN_TILE == 0 and (n_batch * ctx_len) % TOKEN_TILE == 0 and n_batch < SUBLANES
    assert seq % ATTN_TK == 0 and seq % CHUNK == 0 and ctx_len % CHUNK == 0 and n_lat % ctx_len == 0

    xs = jnp.concatenate([x.reshape(n_lat, d), ctx.reshape(n_batch * ctx_len, d)], axis=0)
    cond = jnp.zeros((SUBLANES, d), F32).at[:n_batch].set(c).at[n_batch].set(c_ctx)
    mods = _modulation(cond, w_mod, b_mod)
    tables = _rope_tables(seq, RET_DK, TOKEN_TILE) + _rope_tables(seq, DIFF_DH, TOKEN_TILE)
    lr0 = N_HEADS * (2 * RET_DK + 2 * RET_DV + 2 * GLA_DK + 2 * GLA_DV)
    row = lambda a: a[None, :]

    for layer in range(depth):
        need_ctx = layer < depth - 1
        lam_init = 0.8 - 0.6 * math.exp(-0.3 * layer)
        mod = mods[layer]
        wl = w_in[layer]
        w_r = jnp.concatenate([wl[:, :lr0], wl[:, lr0:lr0 + GLA_LOWRANK],
                               jnp.zeros((d, LANES - GLA_LOWRANK), F32), wl[:, lr0 + GLA_LOWRANK:]],
                              axis=1).astype(BF16)
        ret, gla, glr, dq, dk, dv = _in_projection(xs, mod, row(g_pre_mix[layer]), w_r, tables, n_batch, seq)

        o_rf, o_gf, o_rb, o_gb = _scan(ret, gla, glr, _scan_tables(ret_decay_logit[layer]), gla_w_gate[layer],
                                       gla_b_gate[layer], n_batch, seq, ctx_len)

        lp = jnp.zeros((SUBLANES, LANES), F32).at[:4, :DIFF_DH].set(diff_lambda[layer])
        n_all = dv.shape[0]
        vt = jnp.concatenate([dv.T.reshape(N_HEADS, LANES, n_all), jnp.ones((N_HEADS, ATTN_ONES, n_all), BF16)],
                             axis=1).reshape(N_HEADS * (LANES + ATTN_ONES), n_all)
        g_col = jnp.broadcast_to(diff_norm_g[layer][:, None], (LANES, LANES))
        attn = functools.partial(_diff_attention, dq, dk, vt, lp, g_col, lam_init, n_batch, seq, ctx_len)
        dif = attn(True)
        n_rows = n_lat
        if need_ctx:
            dif = jnp.concatenate([dif, attn(False)], axis=0)
            n_rows = xs.shape[0]

        xs, hp, scores_t = _merge(o_rf, o_rb, o_gf, o_gb, ret, gla, dif, xs, mod, row(g_post_mix[layer]),
                                  row(g_pre_ffn[layer]), row(jnp.tile(gla_norm_g[layer], N_HEADS)),
                                  w_out[layer].astype(BF16), w_router[layer].T, n_rows, n_batch, seq)
        weights = tuple(w[layer].astype(BF16) for w in
                        (w_exp_gate, w_exp_up, w_exp_down, w_sh_gate, w_sh_up, w_sh_down))
        xs = _moe_layer(scores_t, hp, xs, mod, router_bias[layer], row(g_post_ffn[layer]), weights,
                        n_lat, n_batch, seq)
    return xs[:n_lat].reshape(n_batch, seq, d)
```

```python
import functools
import math

import jax
import jax.numpy as jnp
import numpy as np
from jax import lax
from jax.experimental import pallas as pl
from jax.experimental.pallas import tpu as pltpu

F32 = jnp.float32
BF16 = jnp.bfloat16
I32 = jnp.int32
U32 = jnp.uint32

GRID_W = 64
CHUNK = 128
N_HEADS = 4
RET_DK, RET_DV = 32, 64
GLA_DK, GLA_DV = 32, 64
GLA_LOWRANK = 16
GLA_TAU = 16.0
DIFF_DH = 64
ROPE_BASE = 10000.0
N_EXPERTS = 64
TOP_K = 6
N_GROUPS = 8
TOPK_GROUPS = 4
GROUP_SIZE = N_EXPERTS // N_GROUPS
ROUTED_SCALE = 2.5
EPS = 1e-6
GLA_SUB = 16

LANES = 128
SUBLANES = 8
VMEM_LIMIT_BYTES = 56 * 1024 * 1024

TOKEN_TILE = 512
ATTN_TQ = 512
ATTN_TK = 512
ATTN_ONES = 16
MOE_TILES = (1024, 512)
MOE_BLOCK = 128
EXPERT_GROUP = 4
COMBINE_SELECTS = 3

HIGHEST = lax.Precision.HIGHEST
NT_DIMS = (((1,), (1,)), ((), ()))
TN_DIMS = (((0,), (0,)), ((), ()))


def _cparams(*sem):
    return pltpu.CompilerParams(dimension_semantics=sem, vmem_limit_bytes=VMEM_LIMIT_BYTES)


def _log_sigmoid(x):
    return jnp.minimum(x, 0.0) - jnp.log(1.0 + jnp.exp(-jnp.abs(x)))


def _silu(x):
    return x * (1.0 / (1.0 + jnp.exp(-x)))


def _rms(x):
    return x * lax.rsqrt(jnp.mean(x * x, axis=-1, keepdims=True) + EPS)


def _pack_halves(a, b):
    ua = lax.bitcast_convert_type(a.astype(BF16).astype(F32), U32)
    ub = lax.bitcast_convert_type(b.astype(BF16).astype(F32), U32)
    return (ua & jnp.uint32(0xFFFF0000)) | (ub >> 16)


def _unpack_halves(w):
    a = lax.bitcast_convert_type(w & jnp.uint32(0xFFFF0000), F32)
    b = lax.bitcast_convert_type(w << 16, F32)
    return a, b


def _mod_kernel(cond_ref, w_ref, b_ref, o_ref):
    a = _silu(cond_ref[...])
    o_ref[0] = jnp.dot(a, w_ref[0], precision=HIGHEST, preferred_element_type=F32) + b_ref[0]


def _modulation(cond, w_mod, b_mod):
    n_layers, d, d6 = w_mod.shape
    tn = 1024
    return pl.pallas_call(
        _mod_kernel,
        out_shape=jax.ShapeDtypeStruct((n_layers, SUBLANES, d6), F32),
        grid=(n_layers, d6 // tn),
        in_specs=[
            pl.BlockSpec((SUBLANES, d), lambda l, j: (0, 0)),
            pl.BlockSpec((1, d, tn), lambda l, j: (l, 0, j)),
            pl.BlockSpec((1, 1, tn), lambda l, j: (l, 0, j)),
        ],
        out_specs=pl.BlockSpec((1, SUBLANES, tn), lambda l, j: (l, 0, j)),
        compiler_params=_cparams("parallel", "parallel"),
        name="modulation",
    )(cond, w_mod, b_mod.reshape(n_layers, 1, d6))


def _rope(x, cos, sin, quarter):
    lane = lax.broadcasted_iota(I32, (1, LANES), 1)
    first = (lane % (2 * quarter)) < quarter
    outs = []
    for c in range(x.shape[-1] // LANES):
        xc = x[:, c * LANES:(c + 1) * LANES]
        partner = jnp.where(first, pltpu.roll(xc, LANES - quarter, 1), pltpu.roll(xc, quarter, 1))
        outs.append(xc * cos + partner * sin)
    return outs[0] if len(outs) == 1 else jnp.concatenate(outs, axis=-1)


def _inproj_kernel(tiles_per_batch, n_batch, x_ref, mod_ref, g_ref, w_ref, c32_ref, s32_ref, c64_ref, s64_ref,
                   ret_ref, gla_ref, glr_ref, dq_ref, dk_ref, vt_ref):
    d = x_ref.shape[-1]
    r = jnp.minimum(pl.program_id(0) // tiles_per_batch, n_batch)
    shift = mod_ref[pl.ds(r, 1), 0:d]
    scale = mod_ref[pl.ds(r, 1), d:2 * d]
    h = (_rms(x_ref[...]) * g_ref[...] * (1.0 + scale) + shift).astype(BF16)

    def proj(lo, hi):
        return jnp.dot(h, w_ref[:, lo:hi], preferred_element_type=F32)

    c32, s32 = c32_ref[...], s32_ref[...]
    c64, s64 = c64_ref[...], s64_ref[...]
    ret = proj(0, 768)
    ret_ref[:, 0:128] = _rope(ret[:, 0:128], c32, s32, RET_DK // 4).astype(BF16)
    ret_ref[:, 128:256] = (_rope(ret[:, 128:256], c32, s32, RET_DK // 4) * RET_DK ** -0.5).astype(BF16)
    ret_ref[:, 256:768] = ret[:, 256:768].astype(BF16)
    gla = proj(768, 1536)
    gla_ref[:, 0:128] = (gla[:, 0:128] * GLA_DK ** -0.5).astype(BF16)
    gla_ref[:, 128:768] = gla[:, 128:768].astype(BF16)
    glr_ref[...] = proj(1536, 1664)
    dq_ref[...] = (_rope(proj(1664, 2176), c64, s64, DIFF_DH // 4) * DIFF_DH ** -0.5).astype(BF16)
    dk_ref[...] = _rope(proj(2176, 2688), c64, s64, DIFF_DH // 4).astype(BF16)
    dv = proj(2688, 3200)
    vrows = LANES + ATTN_ONES
    for hd in range(N_HEADS):
        vt_ref[hd * vrows:hd * vrows + LANES, :] = dv[:, hd * LANES:(hd + 1) * LANES].T.astype(BF16)
        vt_ref[hd * vrows + LANES:(hd + 1) * vrows, :] = jnp.ones((ATTN_ONES, dv.shape[0]), BF16)


def _in_projection(xs, mod, g_pre, w_r, tables, n_batch, seq):
    n, d = xs.shape
    tm = TOKEN_TILE
    tiles_per_batch = seq // tm
    n_lat_tiles = n_batch * tiles_per_batch
    c32, s32, c64, s64 = tables

    def tab_map(i):
        return (jnp.where(i < n_lat_tiles, i % tiles_per_batch, tiles_per_batch), 0)

    row = lambda i: (i, 0)
    const = lambda i: (0, 0)
    tab_spec = pl.BlockSpec((tm, LANES), tab_map)
    return pl.pallas_call(
        functools.partial(_inproj_kernel, tiles_per_batch, n_batch),
        out_shape=(
            jax.ShapeDtypeStruct((n, 768), BF16), jax.ShapeDtypeStruct((n, 768), BF16),
            jax.ShapeDtypeStruct((n, LANES), F32),
            jax.ShapeDtypeStruct((n, 512), BF16), jax.ShapeDtypeStruct((n, 512), BF16),
            jax.ShapeDtypeStruct((N_HEADS * (LANES + ATTN_ONES), n), BF16)),
        grid=(n // tm,),
        in_specs=[
            pl.BlockSpec((tm, d), row),
            pl.BlockSpec(mod.shape, const),
            pl.BlockSpec((1, d), const),
            pl.BlockSpec(w_r.shape, const),
            tab_spec, tab_spec, tab_spec, tab_spec,
        ],
        out_specs=(
            pl.BlockSpec((tm, 768), row), pl.BlockSpec((tm, 768), row), pl.BlockSpec((tm, LANES), row),
            pl.BlockSpec((tm, 512), row), pl.BlockSpec((tm, 512), row),
            pl.BlockSpec((N_HEADS * (LANES + ATTN_ONES), tm), lambda i: (0, i))),
        compiler_params=_cparams("parallel"),
        name="in_projection",
    )(xs, mod, g_pre, w_r, c32, s32, c64, s64)


def _rope_tables(seq, head_dim, extra_rows):
    half, quarter = head_dim // 2, head_dim // 4
    freqs = (ROPE_BASE ** (-np.arange(quarter, dtype=np.float32) / quarter)).astype(np.float32)
    t = np.arange(seq)
    row = (t // GRID_W).astype(np.float32)
    col = (t % GRID_W).astype(np.float32)
    j = np.arange(LANES) % head_dim
    jj = j % half
    pos = np.where((j < half)[None, :], row[:, None], col[:, None])
    ang = (pos * freqs[jj % quarter][None, :]).astype(np.float32)
    cos = np.cos(ang)
    sin = np.sin(ang) * np.where(jj < quarter, -1.0, 1.0)[None, :]
    cos = np.concatenate([cos, np.ones((extra_rows, LANES))], axis=0).astype(np.float32)
    sin = np.concatenate([sin, np.zeros((extra_rows, LANES))], axis=0).astype(np.float32)
    return jnp.asarray(cos), jnp.asarray(sin)


def _head_stack(x, width):
    lane = lax.broadcasted_iota(I32, (1, x.shape[-1]), 1)
    zero = jnp.zeros_like(x)
    return jnp.concatenate([jnp.where(lane // width == h, x, zero) for h in range(N_HEADS)], axis=0)


def _head_select(x4, rows):
    lane = lax.broadcasted_iota(I32, (1, x4.shape[-1]), 1)
    out = jnp.zeros((rows, x4.shape[-1]), F32)
    for h in range(N_HEADS):
        out = out + jnp.where(lane // RET_DV == h, x4[h * rows:(h + 1) * rows], 0.0)
    return out


def _scan_tables_kernel(rl_lane_ref, rl_rows_ref, rl_col_ref, dmat_ref, xi_ref, zeta_ref, gchunk_ref):
    c = CHUNK
    idx = lax.broadcasted_iota(I32, (c, 1), 0).astype(F32)
    ri4 = lax.broadcasted_iota(I32, (N_HEADS * c, c), 0) % c
    ci4 = lax.broadcasted_iota(I32, (N_HEADS * c, c), 1)
    dist = jnp.abs(ri4 - ci4).astype(F32)
    for d in range(2):
        lg_lane = _log_sigmoid(rl_lane_ref[d])
        lg_rows = _log_sigmoid(rl_rows_ref[d])
        lg_col = _log_sigmoid(rl_col_ref[d])
        att4 = (ri4 <= ci4) if d else (ri4 >= ci4)
        dmat_ref[d] = jnp.where(att4, jnp.exp(dist * lg_rows), 0.0)
        xi_ref[d] = jnp.exp(((c - idx) if d else (idx + 1.0)) * lg_lane)
        zeta_ref[d] = jnp.exp((idx if d else (c - 1.0 - idx)) * lg_lane)
        g_chunk = jnp.exp(float(c) * lg_col)
        gchunk_ref[d] = jnp.concatenate([g_chunk, g_chunk], axis=1)


def _scan_tables(ret_logit):
    c = CHUNK
    rl_lane = jnp.repeat(ret_logit, RET_DK, axis=1)[:, None, :]
    rl_rows = jnp.broadcast_to(jnp.repeat(ret_logit, c, axis=1)[:, :, None], (2, N_HEADS * c, c))
    rl_col = jnp.broadcast_to(jnp.repeat(ret_logit, RET_DK, axis=1)[:, :, None], (2, LANES, LANES))
    return pl.pallas_call(
        _scan_tables_kernel,
        out_shape=(jax.ShapeDtypeStruct((2, N_HEADS * c, c), F32), jax.ShapeDtypeStruct((2, c, LANES), F32),
                   jax.ShapeDtypeStruct((2, c, LANES), F32), jax.ShapeDtypeStruct((2, LANES, 2 * LANES), F32)),
        name="scan_tables",
    )(rl_lane, rl_rows, rl_col)


def _state_block_mask():
    return (lax.broadcasted_iota(I32, (LANES, 2 * LANES), 0) // RET_DK
            == lax.broadcasted_iota(I32, (LANES, 2 * LANES), 1) // RET_DV)


def _ret_chain(ret_ref, dmat, xi, zeta, g_chunk, ore_ref, sr_ref):
    c = CHUNK
    q = ret_ref[:, 0:128]
    k = ret_ref[:, 128:256]
    v = ret_ref[:, 256:512]
    s = lax.dot_general(_head_stack(q, RET_DK), k, NT_DIMS, preferred_element_type=F32)
    sr = sr_ref[...]
    inter = jnp.dot((q.astype(F32) * xi).astype(BF16), sr.astype(BF16), preferred_element_type=F32)
    kz = (k.astype(F32) * zeta).astype(BF16)
    u = lax.dot_general(kz, v, TN_DIMS, preferred_element_type=F32)
    yield
    o4 = jnp.dot((s * dmat).astype(BF16), v, preferred_element_type=F32)
    sr_ref[...] = sr * g_chunk + jnp.where(_state_block_mask(), u, 0.0)
    yield
    ore_ref[...] = _head_select(o4, c) + inter


def _gla_chain(rev, gla_ref, glr_ref, gw, gb, ogl_ref, sg_ref):
    c = CHUNK
    ri = lax.broadcasted_iota(I32, (c, c), 0)
    ci = lax.broadcasted_iota(I32, (c, c), 1)
    attends = (ri <= ci) if rev else (ri >= ci)
    bd = _state_block_mask()
    gq = gla_ref[:, 0:128].astype(F32)
    gk = gla_ref[:, 128:256].astype(F32)
    gv = gla_ref[:, 256:512]
    z = jnp.dot(glr_ref[...], gw, precision=HIGHEST, preferred_element_type=F32) + gb
    yield
    la = _log_sigmoid(z) * (1.0 / GLA_TAU)
    if rev:
        first = (ri // GLA_SUB) * GLA_SUB + (GLA_SUB - 1)
        ref_sel = ci >= first
    else:
        first = (ri // GLA_SUB) * GLA_SUB
        ref_sel = ci <= first
    sel = jnp.concatenate([jnp.where(attends, 1.0, 0.0), jnp.where(ref_sel, 1.0, 0.0)], axis=0).astype(BF16)
    la_hi = la.astype(BF16)
    la_lo = (la - la_hi.astype(F32)).astype(BF16)
    sums = (jnp.dot(sel, la_hi, preferred_element_type=F32) + jnp.dot(sel, la_lo, preferred_element_type=F32))
    b, refrow = sums[0:c], sums[c:2 * c]
    yield
    qs = gq * jnp.exp(b - refrow)
    sg = sg_ref[...]
    inter = jnp.dot((gq * jnp.exp(b)).astype(BF16), sg.astype(BF16), preferred_element_type=F32)
    b_last = b[0:1] if rev else b[c - 1:c]
    kz = (gk * jnp.exp(b_last - b)).astype(BF16)
    u = lax.dot_general(kz, gv, TN_DIMS, preferred_element_type=F32)
    eye = lax.broadcasted_iota(I32, (LANES, LANES), 0) == lax.broadcasted_iota(I32, (LANES, LANES), 1)
    g_col = jnp.sum(jnp.where(eye, jnp.exp(b_last), 0.0), axis=1, keepdims=True)
    jcol = lax.broadcasted_iota(I32, (c, 1), 0)
    rr = lax.broadcasted_iota(I32, (N_HEADS * GLA_SUB, c), 0) % GLA_SUB
    cc = lax.broadcasted_iota(I32, (N_HEADS * GLA_SUB, c), 1)
    pieces = []
    for blk in range(c // GLA_SUB):
        lo = blk * GLA_SUB
        ref_b = refrow[lo:lo + 1]
        seen = (jcol >= lo) if rev else (jcol < lo + GLA_SUB)
        ks = (gk * jnp.exp(jnp.where(seen, ref_b - b, -jnp.inf))).astype(BF16)
        qz = _head_stack(qs[lo:lo + GLA_SUB], GLA_DK).astype(BF16)
        att = lax.dot_general(qz, ks, NT_DIMS, preferred_element_type=F32)
        ok = (cc >= rr + lo) if rev else (cc <= rr + lo)
        att = jnp.where(ok, att, 0.0).astype(BF16)
        pieces.append(_head_select(jnp.dot(att, gv, preferred_element_type=F32), GLA_SUB))
        if blk % 2:
            yield
    sg_ref[...] = sg * g_col + jnp.where(bd, u, 0.0)
    ogl_ref[...] = jnp.concatenate(pieces, axis=0) + inter


def _scan_kernel(ret_f, gla_f, glr_f, ret_b, gla_b, glr_b, dmat_ref, xi_ref, zeta_ref, gchunk_ref, gw_ref, gb_ref,
                 orf_ref, ogf_ref, orb_ref, ogb_ref, sr_ref, sg_ref):
    @pl.when(pl.program_id(1) == 0)
    def _():
        sr_ref[...] = jnp.zeros_like(sr_ref)
        sg_ref[...] = jnp.zeros_like(sg_ref)

    chains = []
    for d, (ret_ref, gla_ref, glr_ref, ore_ref, ogl_ref) in enumerate(
            ((ret_f, gla_f, glr_f, orf_ref, ogf_ref), (ret_b, gla_b, glr_b, orb_ref, ogb_ref))):
        chains.append(_gla_chain(bool(d), gla_ref, glr_ref, gw_ref[d], gb_ref[d], ogl_ref, sg_ref.at[d]))
        chains.append(_ret_chain(ret_ref, dmat_ref[d], xi_ref[d], zeta_ref[d], gchunk_ref[d], ore_ref,
                                 sr_ref.at[d]))
    while chains:
        chains = [ch for ch in chains if next(ch, True) is None]


def _scan(ret, gla, glr, tables, gla_w, gla_b, n_batch, seq, ctx_len):
    n = ret.shape[0]
    c = CHUNK
    nc_ctx, nc_lat = ctx_len // c, seq // c
    n_steps = nc_ctx + nc_lat
    ctx_base = n_batch * nc_lat

    def fwd(b, s):
        return (jnp.where(s < nc_ctx, ctx_base + b * nc_ctx + s, b * nc_lat + (s - nc_ctx)), 0)

    def bwd(b, s):
        return (jnp.where(s < nc_ctx, ctx_base + b * nc_ctx + (nc_ctx - 1 - s), b * nc_lat + (n_steps - 1 - s)), 0)

    const3 = lambda b, s: (0, 0, 0)
    gw = jnp.zeros((2, LANES, LANES), F32).at[:, :GLA_LOWRANK].set(gla_w)
    gb = gla_b[:, None, :]
    chain_in = lambda m: [pl.BlockSpec((c, 768), m), pl.BlockSpec((c, 768), m), pl.BlockSpec((c, LANES), m)]
    o_sds = jax.ShapeDtypeStruct((n, 256), F32)
    return pl.pallas_call(
        _scan_kernel,
        out_shape=(o_sds, o_sds, o_sds, o_sds),
        grid=(n_batch, n_steps),
        in_specs=chain_in(fwd) + chain_in(bwd) + [pl.BlockSpec(t.shape, const3) for t in tables]
        + [pl.BlockSpec(gw.shape, const3), pl.BlockSpec(gb.shape, const3)],
        out_specs=(pl.BlockSpec((c, 256), fwd), pl.BlockSpec((c, 256), fwd),
                   pl.BlockSpec((c, 256), bwd), pl.BlockSpec((c, 256), bwd)),
        scratch_shapes=[pltpu.VMEM((2, LANES, 2 * LANES), F32), pltpu.VMEM((2, LANES, 2 * LANES), F32)],
        compiler_params=_cparams("arbitrary", "arbitrary"),
        name="scan",
    )(ret, gla, glr, ret, gla, glr, *tables, gw, gb)


def _attn_kernel(n_lat_blocks, lam_init, *refs):
    if n_lat_blocks:
        q_ref, kc_ref, vtc_ref, kl_ref, vtl_ref, lp_ref, g_ref, o_ref, m_ref, acc_ref, st_ref = refs
    else:
        q_ref, kc_ref, vtc_ref, lp_ref, g_ref, o_ref, m_ref, acc_ref, st_ref = refs
    tq = q_ref.shape[0]
    q = q_ref[...].astype(F32)
    lane = lax.broadcasted_iota(I32, (1, LANES), 1)
    qt = jnp.concatenate([jnp.where(lane < DIFF_DH, q, 0.0).T, jnp.where(lane >= DIFF_DH, q, 0.0).T],
                         axis=1).astype(BF16)
    m_ref[...] = jnp.full(m_ref.shape, -jnp.inf, F32)
    acc_ref[...] = jnp.zeros_like(acc_ref)

    def scores(slot, kb):
        st_ref[slot, 0:kb.shape[0], :] = jnp.dot(kb, qt, preferred_element_type=F32)

    def absorb(slot, vtb):
        st = st_ref[slot, 0:vtb.shape[1], :]
        m_prev = m_ref[...]
        m_new = jnp.maximum(m_prev, jnp.max(st, axis=0, keepdims=True))
        alpha = jnp.exp(m_prev - m_new)
        p = jnp.exp(st - m_new).astype(BF16)
        acc_ref[...] = alpha * acc_ref[...] + jnp.dot(vtb, p, preferred_element_type=F32)
        m_ref[...] = m_new

    def k_lat(blk):
        return kl_ref[pl.ds(pl.multiple_of(blk * ATTN_TK, ATTN_TK), ATTN_TK), :]

    def vt_lat(blk):
        return vtl_ref[:, pl.ds(pl.multiple_of(blk * ATTN_TK, ATTN_TK), ATTN_TK)]

    scores(0, kc_ref[...])
    if not n_lat_blocks:
        absorb(0, vtc_ref[...])
    else:
        scores(1, k_lat(0))
        absorb(0, vtc_ref[...])
        n_pairs = (n_lat_blocks - 1) // 2

        def body(i, carry):
            scores(0, k_lat(2 * i + 1))
            absorb(1, vt_lat(2 * i))
            scores(1, k_lat(2 * i + 2))
            absorb(0, vt_lat(2 * i + 1))
            return carry
        lax.fori_loop(0, n_pairs, body, 0)
        done = 2 * n_pairs
        if n_lat_blocks - done == 2:
            scores(0, k_lat(done + 1))
            absorb(1, vt_lat(done))
            absorb(0, vt_lat(done + 1))
        else:
            absorb(1, vt_lat(done))

    lp = lp_ref[...]
    lam = (jnp.exp(jnp.sum(lp[0:1] * lp[1:2], axis=1, keepdims=True))
           - jnp.exp(jnp.sum(lp[2:3] * lp[3:4], axis=1, keepdims=True)) + lam_init)
    acc = acc_ref[...]
    o1 = acc[0:LANES, :tq] / acc[LANES:LANES + 1, :tq]
    o2 = acc[0:LANES, tq:] / acc[LANES:LANES + 1, tq:]
    ot = o1 - lam * o2
    ot = ot * lax.rsqrt(jnp.mean(ot * ot, axis=0, keepdims=True) + EPS) * g_ref[...][:, 0:1] * (1.0 - lam_init)
    o_ref[...] = ot.T.astype(o_ref.dtype)


def _diff_attention(dq, dk, vt, lp, g, lam_init, n_batch, seq, ctx_len, latent):
    ctx_blk0 = (n_batch * seq) // ctx_len
    vrows = LANES + ATTN_ONES
    kc_spec = pl.BlockSpec((ctx_len, LANES), lambda b, h, i: (ctx_blk0 + b, h))
    vtc_spec = pl.BlockSpec((vrows, ctx_len), lambda b, h, i: (h, ctx_blk0 + b))
    const = lambda b, h, i: (0, 0)
    if latent:
        tq = ATTN_TQ
        n_q = seq // tq
        q_map = lambda b, h, i: (b * n_q + i, h)
        in_specs = [pl.BlockSpec((tq, LANES), q_map), kc_spec, vtc_spec,
                    pl.BlockSpec((seq, LANES), lambda b, h, i: (b, h)),
                    pl.BlockSpec((vrows, seq), lambda b, h, i: (h, b))]
        args = (dq, dk, vt, dk, vt)
        n_rows, o_map, n_lat_blocks = n_batch * seq, q_map, seq // ATTN_TK
    else:
        tq, n_q = ctx_len, 1
        in_specs = [kc_spec, kc_spec, vtc_spec]
        args = (dq, dk, vt)
        n_rows, o_map, n_lat_blocks = n_batch * ctx_len, (lambda b, h, i: (b, h)), 0
    in_specs += [pl.BlockSpec(lp.shape, const), pl.BlockSpec((LANES, LANES), const)]
    return pl.pallas_call(
        functools.partial(_attn_kernel, n_lat_blocks, lam_init),
        out_shape=jax.ShapeDtypeStruct((n_rows, N_HEADS * LANES), BF16),
        grid=(n_batch, N_HEADS, n_q),
        in_specs=in_specs,
        out_specs=pl.BlockSpec((tq, LANES), o_map),
        scratch_shapes=[pltpu.VMEM((1, 2 * tq), F32), pltpu.VMEM((vrows, 2 * tq), F32),
                        pltpu.VMEM((2, max(ATTN_TK, ctx_len), 2 * tq), F32)],
        compiler_params=_cparams("parallel", "parallel", "parallel"),
        name="diff_attention_lat" if latent else "diff_attention_ctx",
    )(*args, lp, g)


def _merge_kernel(tiles_per_batch, n_batch, orf_ref, orb_ref, ogf_ref, ogb_ref, rg_ref, gr_ref, dif_ref,
                  x_ref, mod_ref, gpost_ref, gpre_ref, glag_ref, wout_ref, wrt_ref,
                  xo_ref, hp_ref, sc_ref):
    d = x_ref.shape[-1]
    r = jnp.minimum(pl.program_id(0) // tiles_per_batch, n_batch)
    gate1 = mod_ref[pl.ds(r, 1), 2 * d:3 * d]
    shift2 = mod_ref[pl.ds(r, 1), 3 * d:4 * d]
    scale2 = mod_ref[pl.ds(r, 1), 4 * d:5 * d]
    gi = lax.broadcasted_iota(I32, (256, 256), 0) // RET_DV
    gj = lax.broadcasted_iota(I32, (256, 256), 1) // RET_DV
    group_mean = jnp.where(gi == gj, 1.0 / RET_DV, 0.0).astype(BF16)

    def head_norm(o):
        ms = jnp.dot((o * o).astype(BF16), group_mean, preferred_element_type=F32)
        return o * lax.rsqrt(ms + EPS)

    ret = head_norm(orf_ref[...] + orb_ref[...]) * _silu(rg_ref[...].astype(F32))
    gla = head_norm(ogf_ref[...] + ogb_ref[...]) * glag_ref[...] * _silu(gr_ref[...].astype(F32))
    m = (jnp.dot(ret.astype(BF16), wout_ref[0:256, :], preferred_element_type=F32)
         + jnp.dot(gla.astype(BF16), wout_ref[256:512, :], preferred_element_type=F32)
         + jnp.dot(dif_ref[...], wout_ref[512:1024, :], preferred_element_type=F32))
    x_new = x_ref[...] + gate1 * (_rms(m) * gpost_ref[...])
    xo_ref[...] = x_new
    h2 = _rms(x_new) * gpre_ref[...] * (1.0 + scale2) + shift2
    hp_ref[...] = _pack_halves(h2[:, :d // 2], h2[:, d // 2:])
    logits = lax.dot_general(wrt_ref[...], h2, NT_DIMS, precision=HIGHEST, preferred_element_type=F32)
    sc_ref[...] = 1.0 / (1.0 + jnp.exp(-logits))


def _merge(o_rf, o_rb, o_gf, o_gb, ret, gla, dif, xs, mod, g_post, g_pre_ffn, gla_g, w_out, w_rt,
           n_rows, n_batch, seq):
    d = xs.shape[-1]
    tm = TOKEN_TILE
    row = lambda i: (i, 0)
    gate_col = lambda i: (i, 2)
    const = lambda i: (0, 0)
    return pl.pallas_call(
        functools.partial(_merge_kernel, seq // tm, n_batch),
        out_shape=(jax.ShapeDtypeStruct((n_rows, d), F32), jax.ShapeDtypeStruct((n_rows, d // 2), U32),
                   jax.ShapeDtypeStruct((N_EXPERTS, n_rows), F32)),
        grid=(n_rows // tm,),
        in_specs=[
            pl.BlockSpec((tm, 256), row), pl.BlockSpec((tm, 256), row),
            pl.BlockSpec((tm, 256), row), pl.BlockSpec((tm, 256), row),
            pl.BlockSpec((tm, 256), gate_col), pl.BlockSpec((tm, 256), gate_col),
            pl.BlockSpec((tm, 512), row), pl.BlockSpec((tm, d), row),
            pl.BlockSpec(mod.shape, const), pl.BlockSpec((1, d), const), pl.BlockSpec((1, d), const),
            pl.BlockSpec((1, 256), const), pl.BlockSpec(w_out.shape, const), pl.BlockSpec(w_rt.shape, const),
        ],
        out_specs=(pl.BlockSpec((tm, d), row), pl.BlockSpec((tm, d // 2), row),
                   pl.BlockSpec((N_EXPERTS, tm), lambda i: (0, i))),
        compiler_params=_cparams("parallel"),
        name="merge",
    )(o_rf, o_rb, o_gf, o_gb, ret, gla, dif, xs, mod, g_post, g_pre_ffn, gla_g, w_out, w_rt)


def _route_kernel(sc_ref, bias_ref, eidx_ref, wts_ref, pos_ref, meta_ref):
    tt = sc_ref.shape[-1]
    scores = sc_ref[...]
    sel = scores + bias_ref[...][:, 0:1]
    sub = lax.broadcasted_iota(I32, (GROUP_SIZE, tt), 0)
    neg = -jnp.inf
    gscore = []
    for g in range(N_GROUPS):
        xg = sel[g * GROUP_SIZE:(g + 1) * GROUP_SIZE]
        m1 = jnp.max(xg, axis=0, keepdims=True)
        i1 = jnp.min(jnp.where(xg == m1, sub, GROUP_SIZE), axis=0, keepdims=True)
        m2 = jnp.max(jnp.where(sub == i1, neg, xg), axis=0, keepdims=True)
        gscore.append(m1 + m2)
    rows = []
    for g in range(N_GROUPS):
        rank = jnp.zeros((1, tt), I32)
        for o in range(N_GROUPS):
            if o == g:
                continue
            ahead = (gscore[o] >= gscore[g]) if o < g else (gscore[o] > gscore[g])
            rank = rank + ahead.astype(I32)
        rows.append(jnp.where(rank < TOPK_GROUPS, sel[g * GROUP_SIZE:(g + 1) * GROUP_SIZE], neg))
    masked = jnp.concatenate(rows, axis=0)
    eio = lax.broadcasted_iota(I32, (N_EXPERTS, tt), 0)
    member = jnp.zeros((N_EXPERTS, tt), F32)
    idxs, ws = [], []
    for _ in range(TOP_K):
        m = jnp.max(masked, axis=0, keepdims=True)
        i = jnp.min(jnp.where(masked == m, eio, N_EXPERTS), axis=0, keepdims=True)
        hit = eio == i
        idxs.append(i)
        ws.append(jnp.sum(jnp.where(hit, scores, 0.0), axis=0, keepdims=True))
        member = jnp.where(hit, 1.0, member)
        masked = jnp.where(hit, neg, masked)
    wsum = ws[0]
    for w in ws[1:]:
        wsum = wsum + w
    ti = lax.broadcasted_iota(I32, (tt, tt), 0)
    tj = lax.broadcasted_iota(I32, (tt, tt), 1)
    before = jnp.where(ti < tj, 1.0, 0.0).astype(BF16)
    rank_in_e = jnp.dot(member.astype(BF16), before, preferred_element_type=F32)
    cnt = jnp.sum(member, axis=1, keepdims=True)
    padded = jnp.floor((cnt + (SUBLANES - 1)) * (1.0 / SUBLANES)) * SUBLANES
    ei = lax.broadcasted_iota(I32, (N_EXPERTS, N_EXPERTS), 0)
    ej = lax.broadcasted_iota(I32, (N_EXPERTS, N_EXPERTS), 1)
    lower = jnp.where(ej < ei, 1.0, 0.0)
    off = jnp.dot(lower, jnp.broadcast_to(padded, (N_EXPERTS, LANES)), precision=HIGHEST,
                  preferred_element_type=F32)
    slot = rank_in_e + off[:, 0:1]
    zrow_i = jnp.zeros((SUBLANES - TOP_K, tt), I32)
    zrow_f = jnp.zeros((SUBLANES - TOP_K, tt), F32)
    pos = [jnp.sum(jnp.where(eio == i, slot, 0.0), axis=0, keepdims=True).astype(I32) for i in idxs]
    eidx_ref[...] = jnp.concatenate(idxs + [zrow_i], axis=0)
    wts_ref[...] = jnp.concatenate([w / wsum * ROUTED_SCALE for w in ws] + [zrow_f], axis=0)
    pos_ref[...] = jnp.concatenate(pos + [zrow_i], axis=0)
    meta_ref[0] = jnp.concatenate([jnp.broadcast_to(cnt, (N_EXPERTS, LANES)), off], axis=1).astype(I32)


def _route(scores_t, bias, tt, row0, n):
    tile0 = row0 // tt
    tok = lambda i: (0, i)
    return pl.pallas_call(
        _route_kernel,
        out_shape=(jax.ShapeDtypeStruct((SUBLANES, n), I32), jax.ShapeDtypeStruct((SUBLANES, n), F32),
                   jax.ShapeDtypeStruct((SUBLANES, n), I32),
                   jax.ShapeDtypeStruct((n // tt, N_EXPERTS, 2 * LANES), I32)),
        grid=(n // tt,),
        in_specs=[pl.BlockSpec((N_EXPERTS, tt), lambda i: (0, i + tile0)),
                  pl.BlockSpec((N_EXPERTS, LANES), lambda i: (0, 0))],
        out_specs=(pl.BlockSpec((SUBLANES, tt), tok), pl.BlockSpec((SUBLANES, tt), tok),
                   pl.BlockSpec((SUBLANES, tt), tok),
                   pl.BlockSpec((1, N_EXPERTS, 2 * LANES), lambda i: (i, 0, 0))),
        compiler_params=_cparams("parallel"),
        name="route",
    )(scores_t, jnp.broadcast_to(bias[:, None], (N_EXPERTS, LANES)))


def _moe_kernel(seq, n_batch, tile0, has_prev, pos_ref, wts_ref, meta_ref,
                hp_ref, x_ref, mod_ref, gpost_ref, weg_ref, weu_ref, wed_ref, wsg_ref, wsu_ref, wsd_ref, *refs):
    o_ref, xg_ref, y_ref, cg_ref, rt_ref = refs[1:] if has_prev else refs
    tt, half = hp_ref.shape
    d = 2 * half
    eg = pl.program_id(1)

    @pl.when((pl.program_id(0) == 0) & (eg == 0))
    def _():
        def zero(i, carry):
            for u in range(SUBLANES):
                rt_ref[i * SUBLANES + u] = 0
            return carry
        lax.fori_loop(0, rt_ref.shape[0] // SUBLANES, zero, 0)

    @pl.when(eg == 0)
    def _():
        def scatter(t, carry):
            for k in range(TOP_K):
                rt_ref[pos_ref[t * SUBLANES + k]] = t
            return carry
        lax.fori_loop(0, tt, scatter, 0)

    sub = lax.broadcasted_iota(I32, (SUBLANES, half), 0)

    def gather8(src_ref, row_of):
        buf = jnp.zeros((SUBLANES, half), U32)
        for u in range(SUBLANES):
            buf = jnp.where(sub == u, src_ref[pl.ds(row_of(u), 1), :], buf)
        return buf

    row = lax.broadcasted_iota(I32, (MOE_BLOCK, half), 0)

    def expert_block(ge, base, n_valid):
        base = pl.multiple_of(base, SUBLANES)
        for i in range(MOE_BLOCK):
            xg_ref[ge, i:i + 1, :] = hp_ref[pl.ds(rt_ref[base + i], 1), :]
        yield
        xa, xb = _unpack_halves(xg_ref[ge])
        xa, xb = xa.astype(BF16), xb.astype(BF16)
        hg = (jnp.dot(xa, weg_ref[ge, 0:half, :], preferred_element_type=F32)
              + jnp.dot(xb, weg_ref[ge, half:d, :], preferred_element_type=F32))
        hu = (jnp.dot(xa, weu_ref[ge, 0:half, :], preferred_element_type=F32)
              + jnp.dot(xb, weu_ref[ge, half:d, :], preferred_element_type=F32))
        yield
        y = jnp.dot((_silu(hg) * hu).astype(BF16), wed_ref[ge], preferred_element_type=F32)
        yield
        pltpu.store(y_ref.at[pl.ds(base, MOE_BLOCK), :], _pack_halves(y[:, :half], y[:, half:]),
                    mask=row < n_valid)

    cnts = [meta_ref[0, eg * EXPERT_GROUP + ge] for ge in range(EXPERT_GROUP)]
    offs = [meta_ref[1, eg * EXPERT_GROUP + ge] for ge in range(EXPERT_GROUP)]
    chains = [expert_block(ge, offs[ge], cnts[ge]) for ge in range(EXPERT_GROUP)]
    while chains:
        chains = [ch for ch in chains if next(ch, True) is None]
    for ge in range(EXPERT_GROUP):
        def more(j, carry, ge=ge):
            for _ in expert_block(ge, offs[ge] + j * MOE_BLOCK, cnts[ge] - j * MOE_BLOCK):
                pass
            return carry
        lax.fori_loop(1, (cnts[ge] + (MOE_BLOCK - 1)) // MOE_BLOCK, more, 0)

    @pl.when(eg == N_EXPERTS // EXPERT_GROUP - 1)
    def _():
        def tokens(g, carry):
            t0 = pl.multiple_of(g * SUBLANES, SUBLANES)
            wg = wts_ref[pl.ds(t0, SUBLANES), :]
            acc_a = jnp.zeros((SUBLANES, half), F32)
            acc_b = jnp.zeros((SUBLANES, half), F32)
            for k in range(TOP_K):
                slot_of = lambda u, k=k: pos_ref[(t0 + u) * SUBLANES + k]
                if k < COMBINE_SELECTS:
                    rows = gather8(y_ref, slot_of)
                else:
                    for u in range(SUBLANES):
                        cg_ref[k - COMBINE_SELECTS, u:u + 1, :] = y_ref[pl.ds(slot_of(u), 1), :]
                    rows = cg_ref[k - COMBINE_SELECTS]
                ya, yb = _unpack_halves(rows)
                acc_a = acc_a + wg[:, k:k + 1] * ya
                acc_b = acc_b + wg[:, k:k + 1] * yb
            o_ref[pl.ds(t0, SUBLANES), 0:half] = acc_a
            o_ref[pl.ds(t0, SUBLANES), half:d] = acc_b
            return carry

        lax.fori_loop(0, tt // SUBLANES, tokens, 0)
        rows = (pl.program_id(0) + tile0) * tt + lax.broadcasted_iota(I32, (tt, 1), 0)
        rb = jnp.minimum(rows // seq, n_batch)
        gate2 = jnp.zeros((tt, d), F32)
        for bi in range(n_batch + 1):
            gate2 = jnp.where(rb == bi, mod_ref[bi:bi + 1, 5 * d:6 * d], gate2)
        xa, xb = _unpack_halves(hp_ref[...])
        xa, xb = xa.astype(BF16), xb.astype(BF16)
        hg = (jnp.dot(xa, wsg_ref[0:half, :], preferred_element_type=F32)
              + jnp.dot(xb, wsg_ref[half:d, :], preferred_element_type=F32))
        hu = (jnp.dot(xa, wsu_ref[0:half, :], preferred_element_type=F32)
              + jnp.dot(xb, wsu_ref[half:d, :], preferred_element_type=F32))
        f = jnp.dot((_silu(hg) * hu).astype(BF16), wsd_ref[...], preferred_element_type=F32) + o_ref[...]
        o_ref[...] = x_ref[...] + gate2 * (_rms(f) * gpost_ref[...])


def _moe_segment(prev, scores_t, hp, xs, mod, router_bias, g_post, weights, row0, n, tt, n_batch, seq):
    n_all, half = hp.shape
    d = 2 * half
    w_eg, w_eu, w_ed, w_sg, w_su, w_sd = weights
    de = w_eg.shape[-1]
    n_tiles, tile0 = n // tt, row0 // tt
    n_slots = -(-(tt * TOP_K + N_EXPERTS * (SUBLANES - 1) + MOE_BLOCK) // (SUBLANES * LANES)) * SUBLANES * LANES
    e_idx, wts, pos, meta = _route(scores_t, router_bias, tt, row0, n)
    cnt, off = meta[:, :, 0], meta[:, :, LANES]

    def per_token(a):
        return a.T.reshape(-1)

    meta_s = jnp.zeros((n_tiles, SUBLANES, LANES), I32)
    meta_s = meta_s.at[:, 0, :N_EXPERTS].set(cnt).at[:, 1, :N_EXPERTS].set(off).reshape(-1, LANES)

    tile = lambda i, e: (i + tile0, 0)
    const = lambda i, e: (0, 0)
    expert = lambda i, e: (e, 0, 0)
    smem = functools.partial(pl.BlockSpec, memory_space=pltpu.SMEM)
    eg = EXPERT_GROUP
    in_specs = [
        smem((SUBLANES * tt,), lambda i, e: (i,)), pl.BlockSpec((tt, SUBLANES), lambda i, e: (i, 0)),
        smem((SUBLANES, LANES), lambda i, e: (i, 0)),
        pl.BlockSpec((tt, half), tile), pl.BlockSpec((tt, d), tile),
        pl.BlockSpec(mod.shape, const), pl.BlockSpec((1, d), const),
        pl.BlockSpec((eg, d, de), expert), pl.BlockSpec((eg, d, de), expert), pl.BlockSpec((eg, de, d), expert),
        pl.BlockSpec(w_sg.shape, const), pl.BlockSpec(w_su.shape, const), pl.BlockSpec(w_sd.shape, const),
    ]
    args = [per_token(pos), wts.T, meta_s, hp, xs, mod, g_post, w_eg, w_eu, w_ed, w_sg, w_su, w_sd]
    aliases = {}
    if prev is not None:
        in_specs.append(pl.BlockSpec(memory_space=pl.ANY))
        args.append(prev)
        aliases = {len(args) - 1: 0}
    return pl.pallas_call(
        functools.partial(_moe_kernel, seq, n_batch, tile0, prev is not None),
        out_shape=jax.ShapeDtypeStruct((n_all, d), F32),
        grid=(n_tiles, N_EXPERTS // EXPERT_GROUP),
        in_specs=in_specs,
        out_specs=pl.BlockSpec((tt, d), tile),
        scratch_shapes=[pltpu.VMEM((EXPERT_GROUP, MOE_BLOCK, half), U32), pltpu.VMEM((n_slots, half), U32),
                        pltpu.VMEM((TOP_K - COMBINE_SELECTS, SUBLANES, half), U32),
                        pltpu.SMEM((n_slots,), I32)],
        input_output_aliases=aliases,
        compiler_params=_cparams("arbitrary", "arbitrary"),
        name="moe",
    )(*args)


def _moe_layer(scores_t, hp, xs, mod, router_bias, g_post, weights, n_lat, n_batch, seq):
    pick = lambda rows: next(t for t in MOE_TILES if rows % t == 0)
    out = _moe_segment(None, scores_t, hp, xs, mod, router_bias, g_post, weights, 0, n_lat, pick(n_lat),
                       n_batch, seq)
    n_ctx = hp.shape[0] - n_lat
    if n_ctx:
        tt = next(t for t in MOE_TILES if n_ctx % t == 0 and n_lat % t == 0)
        out = _moe_segment(out, scores_t, hp, xs, mod, router_bias, g_post, weights, n_lat, n_ctx, tt,
                           n_batch, seq)
    return out


def kernel(x, c, ctx, c_ctx, w_mod, b_mod, g_pre_mix, g_post_mix, g_pre_ffn, g_post_ffn, w_in, w_out,
           ret_decay_logit, gla_w_gate, gla_b_gate, gla_norm_g, diff_lambda, diff_norm_g,
           w_router, router_bias, w_exp_gate, w_exp_up, w_exp_down, w_sh_gate, w_sh_up, w_sh_down):
    n_batch, seq, d = x.shape
    ctx_len = ctx.shape[1]
    depth = w_mod.shape[0]
    n_lat = n_batch * seq
    assert seq % TOKEN_TILE == 0 and (n_batch * ctx_len) % TOKEN_TILE == 0 and n_batch < SUBLANES
    assert seq % ATTN_TK == 0 and seq % CHUNK == 0 and ctx_len % CHUNK == 0 and n_lat % ctx_len == 0

    xs = jnp.concatenate([x.reshape(n_lat, d), ctx.reshape(n_batch * ctx_len, d)], axis=0)
    cond = jnp.zeros((SUBLANES, d), F32).at[:n_batch].set(c).at[n_batch].set(c_ctx)
    mods = _modulation(cond, w_mod, b_mod)
    tables = _rope_tables(seq, RET_DK, TOKEN_TILE) + _rope_tables(seq, DIFF_DH, TOKEN_TILE)
    lr0 = N_HEADS * (2 * RET_DK + 2 * RET_DV + 2 * GLA_DK + 2 * GLA_DV)
    row = lambda a: a[None, :]

    for layer in range(depth):
        need_ctx = layer < depth - 1
        lam_init = 0.8 - 0.6 * math.exp(-0.3 * layer)
        mod = mods[layer]
        wl = w_in[layer]
        w_r = jnp.concatenate([wl[:, :lr0], wl[:, lr0:lr0 + GLA_LOWRANK],
                               jnp.zeros((d, LANES - GLA_LOWRANK), F32), wl[:, lr0 + GLA_LOWRANK:]],
                              axis=1).astype(BF16)
        ret, gla, glr, dq, dk, vt = _in_projection(xs, mod, row(g_pre_mix[layer]), w_r, tables, n_batch, seq)

        o_rf, o_gf, o_rb, o_gb = _scan(ret, gla, glr, _scan_tables(ret_decay_logit[layer]), gla_w_gate[layer],
                                       gla_b_gate[layer], n_batch, seq, ctx_len)

        lp = jnp.zeros((SUBLANES, LANES), F32).at[:4, :DIFF_DH].set(diff_lambda[layer])
        g_col = jnp.broadcast_to(diff_norm_g[layer][:, None], (LANES, LANES))
        attn = functools.partial(_diff_attention, dq, dk, vt, lp, g_col, lam_init, n_batch, seq, ctx_len)
        dif = attn(True)
        n_rows = n_lat
        if need_ctx:
            dif = jnp.concatenate([dif, attn(False)], axis=0)
            n_rows = xs.shape[0]

        xs, hp, scores_t = _merge(o_rf, o_rb, o_gf, o_gb, ret, gla, dif, xs, mod, row(g_post_mix[layer]),
                                  row(g_pre_ffn[layer]), row(jnp.tile(gla_norm_g[layer], N_HEADS)),
                                  w_out[layer].astype(BF16), w_router[layer].T, n_rows, n_batch, seq)
        weights = tuple(w[layer].astype(BF16) for w in
                        (w_exp_gate, w_exp_up, w_exp_down, w_sh_gate, w_sh_up, w_sh_down))
        xs = _moe_layer(scores_t, hp, xs, mod, router_bias[layer], row(g_post_ffn[layer]), weights,
                        n_lat, n_batch, seq)
    return xs[:n_lat].reshape(n_batch, seq, d)
```

```python
import functools
import math

import jax
import jax.numpy as jnp
import numpy as np
from jax import lax
from jax.experimental import pallas as pl
from jax.experimental.pallas import tpu as pltpu

F32 = jnp.float32
BF16 = jnp.bfloat16
I32 = jnp.int32
U32 = jnp.uint32

GRID_W = 64
CHUNK = 128
N_HEADS = 4
RET_DK, RET_DV = 32, 64
GLA_DK, GLA_DV = 32, 64
GLA_LOWRANK = 16
GLA_TAU = 16.0
DIFF_DH = 64
ROPE_BASE = 10000.0
N_EXPERTS = 64
TOP_K = 6
N_GROUPS = 8
TOPK_GROUPS = 4
GROUP_SIZE = N_EXPERTS // N_GROUPS
ROUTED_SCALE = 2.5
EPS = 1e-6
GLA_SUB = 16

LANES = 128
SUBLANES = 8
VMEM_LIMIT_BYTES = 56 * 1024 * 1024

TOKEN_TILE = 512
ATTN_TQ = 512
ATTN_TK = 512
ATTN_ONES = 16
MOE_TILES = (2048, 1024, 512)
MOE_BLOCK = 128
MOE_CHAINS = 4
COMBINE_SELECTS = 3

HIGHEST = lax.Precision.HIGHEST
NT_DIMS = (((1,), (1,)), ((), ()))
TN_DIMS = (((0,), (0,)), ((), ()))


def _cparams(*sem):
    return pltpu.CompilerParams(dimension_semantics=sem, vmem_limit_bytes=VMEM_LIMIT_BYTES)


def _log_sigmoid(x):
    return jnp.minimum(x, 0.0) - jnp.log(1.0 + jnp.exp(-jnp.abs(x)))


def _silu(x):
    return x * (1.0 / (1.0 + jnp.exp(-x)))


def _rms(x):
    return x * lax.rsqrt(jnp.mean(x * x, axis=-1, keepdims=True) + EPS)


def _pack_halves(a, b):
    ua = lax.bitcast_convert_type(a.astype(BF16).astype(F32), U32)
    ub = lax.bitcast_convert_type(b.astype(BF16).astype(F32), U32)
    return (ua & jnp.uint32(0xFFFF0000)) | (ub >> 16)


def _unpack_halves(w):
    a = lax.bitcast_convert_type(w & jnp.uint32(0xFFFF0000), F32)
    b = lax.bitcast_convert_type(w << 16, F32)
    return a, b


def _mod_kernel(cond_ref, w_ref, b_ref, o_ref):
    a = _silu(cond_ref[...])
    o_ref[0] = jnp.dot(a, w_ref[0], precision=HIGHEST, preferred_element_type=F32) + b_ref[0]


def _modulation(cond, w_mod, b_mod):
    n_layers, d, d6 = w_mod.shape
    tn = 1024
    return pl.pallas_call(
        _mod_kernel,
        out_shape=jax.ShapeDtypeStruct((n_layers, SUBLANES, d6), F32),
        grid=(n_layers, d6 // tn),
        in_specs=[
            pl.BlockSpec((SUBLANES, d), lambda l, j: (0, 0)),
            pl.BlockSpec((1, d, tn), lambda l, j: (l, 0, j)),
            pl.BlockSpec((1, 1, tn), lambda l, j: (l, 0, j)),
        ],
        out_specs=pl.BlockSpec((1, SUBLANES, tn), lambda l, j: (l, 0, j)),
        compiler_params=_cparams("parallel", "parallel"),
        name="modulation",
    )(cond, w_mod, b_mod.reshape(n_layers, 1, d6))


def _rope(x, cos, sin, quarter):
    lane = lax.broadcasted_iota(I32, (1, LANES), 1)
    first = (lane % (2 * quarter)) < quarter
    outs = []
    for c in range(x.shape[-1] // LANES):
        xc = x[:, c * LANES:(c + 1) * LANES]
        partner = jnp.where(first, pltpu.roll(xc, LANES - quarter, 1), pltpu.roll(xc, quarter, 1))
        outs.append(xc * cos + partner * sin)
    return outs[0] if len(outs) == 1 else jnp.concatenate(outs, axis=-1)


def _inproj_kernel(tiles_per_batch, n_batch, x_ref, mod_ref, g_ref, w_ref, c32_ref, s32_ref, c64_ref, s64_ref,
                   ret_ref, gla_ref, glr_ref, dq_ref, dk_ref, vt_ref):
    d = x_ref.shape[-1]
    r = jnp.minimum(pl.program_id(0) // tiles_per_batch, n_batch)
    shift = mod_ref[pl.ds(r, 1), 0:d]
    scale = mod_ref[pl.ds(r, 1), d:2 * d]
    h = (_rms(x_ref[...]) * g_ref[...] * (1.0 + scale) + shift).astype(BF16)

    def proj(lo, hi):
        return jnp.dot(h, w_ref[:, lo:hi], preferred_element_type=F32)

    c32, s32 = c32_ref[...], s32_ref[...]
    c64, s64 = c64_ref[...], s64_ref[...]
    ret = proj(0, 768)
    ret_ref[:, 0:128] = _rope(ret[:, 0:128], c32, s32, RET_DK // 4).astype(BF16)
    ret_ref[:, 128:256] = (_rope(ret[:, 128:256], c32, s32, RET_DK // 4) * RET_DK ** -0.5).astype(BF16)
    ret_ref[:, 256:768] = ret[:, 256:768].astype(BF16)
    gla = proj(768, 1536)
    gla_ref[:, 0:128] = (gla[:, 0:128] * GLA_DK ** -0.5).astype(BF16)
    gla_ref[:, 128:768] = gla[:, 128:768].astype(BF16)
    glr_ref[...] = proj(1536, 1664)
    dq_ref[...] = (_rope(proj(1664, 2176), c64, s64, DIFF_DH // 4) * DIFF_DH ** -0.5).astype(BF16)
    dk_ref[...] = _rope(proj(2176, 2688), c64, s64, DIFF_DH // 4).astype(BF16)
    dv = proj(2688, 3200)
    vrows = LANES + ATTN_ONES
    for hd in range(N_HEADS):
        vt_ref[hd * vrows:hd * vrows + LANES, :] = dv[:, hd * LANES:(hd + 1) * LANES].T.astype(BF16)
        vt_ref[hd * vrows + LANES:(hd + 1) * vrows, :] = jnp.ones((ATTN_ONES, dv.shape[0]), BF16)


def _in_projection(xs, mod, g_pre, w_r, tables, n_batch, seq):
    n, d = xs.shape
    tm = TOKEN_TILE
    tiles_per_batch = seq // tm
    n_lat_tiles = n_batch * tiles_per_batch
    c32, s32, c64, s64 = tables

    def tab_map(i):
        return (jnp.where(i < n_lat_tiles, i % tiles_per_batch, tiles_per_batch), 0)

    row = lambda i: (i, 0)
    const = lambda i: (0, 0)
    tab_spec = pl.BlockSpec((tm, LANES), tab_map)
    return pl.pallas_call(
        functools.partial(_inproj_kernel, tiles_per_batch, n_batch),
        out_shape=(
            jax.ShapeDtypeStruct((n, 768), BF16), jax.ShapeDtypeStruct((n, 768), BF16),
            jax.ShapeDtypeStruct((n, LANES), F32),
            jax.ShapeDtypeStruct((n, 512), BF16), jax.ShapeDtypeStruct((n, 512), BF16),
            jax.ShapeDtypeStruct((N_HEADS * (LANES + ATTN_ONES), n), BF16)),
        grid=(n // tm,),
        in_specs=[
            pl.BlockSpec((tm, d), row),
            pl.BlockSpec(mod.shape, const),
            pl.BlockSpec((1, d), const),
            pl.BlockSpec(w_r.shape, const),
            tab_spec, tab_spec, tab_spec, tab_spec,
        ],
        out_specs=(
            pl.BlockSpec((tm, 768), row), pl.BlockSpec((tm, 768), row), pl.BlockSpec((tm, LANES), row),
            pl.BlockSpec((tm, 512), row), pl.BlockSpec((tm, 512), row),
            pl.BlockSpec((N_HEADS * (LANES + ATTN_ONES), tm), lambda i: (0, i))),
        compiler_params=_cparams("parallel"),
        name="in_projection",
    )(xs, mod, g_pre, w_r, c32, s32, c64, s64)


def _rope_tables(seq, head_dim, extra_rows):
    half, quarter = head_dim // 2, head_dim // 4
    freqs = (ROPE_BASE ** (-np.arange(quarter, dtype=np.float32) / quarter)).astype(np.float32)
    t = np.arange(seq)
    row = (t // GRID_W).astype(np.float32)
    col = (t % GRID_W).astype(np.float32)
    j = np.arange(LANES) % head_dim
    jj = j % half
    pos = np.where((j < half)[None, :], row[:, None], col[:, None])
    ang = (pos * freqs[jj % quarter][None, :]).astype(np.float32)
    cos = np.cos(ang)
    sin = np.sin(ang) * np.where(jj < quarter, -1.0, 1.0)[None, :]
    cos = np.concatenate([cos, np.ones((extra_rows, LANES))], axis=0).astype(np.float32)
    sin = np.concatenate([sin, np.zeros((extra_rows, LANES))], axis=0).astype(np.float32)
    return jnp.asarray(cos), jnp.asarray(sin)


def _head_stack(x, width):
    lane = lax.broadcasted_iota(I32, (1, x.shape[-1]), 1)
    zero = jnp.zeros_like(x)
    return jnp.concatenate([jnp.where(lane // width == h, x, zero) for h in range(N_HEADS)], axis=0)


def _head_select(x4, rows):
    lane = lax.broadcasted_iota(I32, (1, x4.shape[-1]), 1)
    out = jnp.zeros((rows, x4.shape[-1]), F32)
    for h in range(N_HEADS):
        out = out + jnp.where(lane // RET_DV == h, x4[h * rows:(h + 1) * rows], 0.0)
    return out


def _scan_tables_kernel(rl_lane_ref, rl_rows_ref, rl_col_ref, dmat_ref, xi_ref, zeta_ref, gchunk_ref):
    c = CHUNK
    idx = lax.broadcasted_iota(I32, (c, 1), 0).astype(F32)
    ri4 = lax.broadcasted_iota(I32, (N_HEADS * c, c), 0) % c
    ci4 = lax.broadcasted_iota(I32, (N_HEADS * c, c), 1)
    dist = jnp.abs(ri4 - ci4).astype(F32)
    for d in range(2):
        lg_lane = _log_sigmoid(rl_lane_ref[d])
        lg_rows = _log_sigmoid(rl_rows_ref[d])
        lg_col = _log_sigmoid(rl_col_ref[d])
        att4 = (ri4 <= ci4) if d else (ri4 >= ci4)
        dmat_ref[d] = jnp.where(att4, jnp.exp(dist * lg_rows), 0.0)
        xi_ref[d] = jnp.exp(((c - idx) if d else (idx + 1.0)) * lg_lane)
        zeta_ref[d] = jnp.exp((idx if d else (c - 1.0 - idx)) * lg_lane)
        g_chunk = jnp.exp(float(c) * lg_col)
        gchunk_ref[d] = jnp.concatenate([g_chunk, g_chunk], axis=1)


def _scan_tables(ret_logit):
    c = CHUNK
    rl_lane = jnp.repeat(ret_logit, RET_DK, axis=1)[:, None, :]
    rl_rows = jnp.broadcast_to(jnp.repeat(ret_logit, c, axis=1)[:, :, None], (2, N_HEADS * c, c))
    rl_col = jnp.broadcast_to(jnp.repeat(ret_logit, RET_DK, axis=1)[:, :, None], (2, LANES, LANES))
    return pl.pallas_call(
        _scan_tables_kernel,
        out_shape=(jax.ShapeDtypeStruct((2, N_HEADS * c, c), F32), jax.ShapeDtypeStruct((2, c, LANES), F32),
                   jax.ShapeDtypeStruct((2, c, LANES), F32), jax.ShapeDtypeStruct((2, LANES, 2 * LANES), F32)),
        name="scan_tables",
    )(rl_lane, rl_rows, rl_col)


def _state_block_mask():
    return (lax.broadcasted_iota(I32, (LANES, 2 * LANES), 0) // RET_DK
            == lax.broadcasted_iota(I32, (LANES, 2 * LANES), 1) // RET_DV)


def _ret_chain(ret_ref, dmat, xi, zeta, g_chunk, ore_ref, sr_ref):
    c = CHUNK
    q = ret_ref[:, 0:128]
    k = ret_ref[:, 128:256]
    v = ret_ref[:, 256:512]
    s = lax.dot_general(_head_stack(q, RET_DK), k, NT_DIMS, preferred_element_type=F32)
    sr = sr_ref[...]
    inter = jnp.dot((q.astype(F32) * xi).astype(BF16), sr.astype(BF16), preferred_element_type=F32)
    kz = (k.astype(F32) * zeta).astype(BF16)
    u = lax.dot_general(kz, v, TN_DIMS, preferred_element_type=F32)
    yield
    o4 = jnp.dot((s * dmat).astype(BF16), v, preferred_element_type=F32)
    sr_ref[...] = sr * g_chunk + jnp.where(_state_block_mask(), u, 0.0)
    yield
    ore_ref[...] = _head_select(o4, c) + inter


def _gla_chain(rev, gla_ref, glr_ref, gw, gb, ogl_ref, sg_ref):
    c = CHUNK
    ri = lax.broadcasted_iota(I32, (c, c), 0)
    ci = lax.broadcasted_iota(I32, (c, c), 1)
    attends = (ri <= ci) if rev else (ri >= ci)
    bd = _state_block_mask()
    gq = gla_ref[:, 0:128].astype(F32)
    gk = gla_ref[:, 128:256].astype(F32)
    gv = gla_ref[:, 256:512]
    z = jnp.dot(glr_ref[...], gw, precision=HIGHEST, preferred_element_type=F32) + gb
    yield
    la = _log_sigmoid(z) * (1.0 / GLA_TAU)
    if rev:
        first = (ri // GLA_SUB) * GLA_SUB + (GLA_SUB - 1)
        ref_sel = ci >= first
    else:
        first = (ri // GLA_SUB) * GLA_SUB
        ref_sel = ci <= first
    sel = jnp.concatenate([jnp.where(attends, 1.0, 0.0), jnp.where(ref_sel, 1.0, 0.0)], axis=0).astype(BF16)
    la_hi = la.astype(BF16)
    la_lo = (la - la_hi.astype(F32)).astype(BF16)
    sums = (jnp.dot(sel, la_hi, preferred_element_type=F32) + jnp.dot(sel, la_lo, preferred_element_type=F32))
    b, refrow = sums[0:c], sums[c:2 * c]
    yield
    qs = gq * jnp.exp(b - refrow)
    sg = sg_ref[...]
    inter = jnp.dot((gq * jnp.exp(b)).astype(BF16), sg.astype(BF16), preferred_element_type=F32)
    b_last = b[0:1] if rev else b[c - 1:c]
    kz = (gk * jnp.exp(b_last - b)).astype(BF16)
    u = lax.dot_general(kz, gv, TN_DIMS, preferred_element_type=F32)
    eye = lax.broadcasted_iota(I32, (LANES, LANES), 0) == lax.broadcasted_iota(I32, (LANES, LANES), 1)
    g_col = jnp.sum(jnp.where(eye, jnp.exp(b_last), 0.0), axis=1, keepdims=True)
    jcol = lax.broadcasted_iota(I32, (c, 1), 0)
    rr = lax.broadcasted_iota(I32, (N_HEADS * GLA_SUB, c), 0) % GLA_SUB
    cc = lax.broadcasted_iota(I32, (N_HEADS * GLA_SUB, c), 1)
    pieces = []
    for blk in range(c // GLA_SUB):
        lo = blk * GLA_SUB
        ref_b = refrow[lo:lo + 1]
        seen = (jcol >= lo) if rev else (jcol < lo + GLA_SUB)
        ks = (gk * jnp.exp(jnp.where(seen, ref_b - b, -jnp.inf))).astype(BF16)
        qz = _head_stack(qs[lo:lo + GLA_SUB], GLA_DK).astype(BF16)
        att = lax.dot_general(qz, ks, NT_DIMS, preferred_element_type=F32)
        ok = (cc >= rr + lo) if rev else (cc <= rr + lo)
        att = jnp.where(ok, att, 0.0).astype(BF16)
        pieces.append(_head_select(jnp.dot(att, gv, preferred_element_type=F32), GLA_SUB))
        if blk % 2:
            yield
    sg_ref[...] = sg * g_col + jnp.where(bd, u, 0.0)
    ogl_ref[...] = jnp.concatenate(pieces, axis=0) + inter


def _scan_kernel(ret_f, gla_f, glr_f, ret_b, gla_b, glr_b, dmat_ref, xi_ref, zeta_ref, gchunk_ref, gw_ref, gb_ref,
                 orf_ref, ogf_ref, orb_ref, ogb_ref, sr_ref, sg_ref):
    @pl.when(pl.program_id(1) == 0)
    def _():
        sr_ref[...] = jnp.zeros_like(sr_ref)
        sg_ref[...] = jnp.zeros_like(sg_ref)

    chains = []
    for d, (ret_ref, gla_ref, glr_ref, ore_ref, ogl_ref) in enumerate(
            ((ret_f, gla_f, glr_f, orf_ref, ogf_ref), (ret_b, gla_b, glr_b, orb_ref, ogb_ref))):
        chains.append(_gla_chain(bool(d), gla_ref, glr_ref, gw_ref[d], gb_ref[d], ogl_ref, sg_ref.at[d]))
        chains.append(_ret_chain(ret_ref, dmat_ref[d], xi_ref[d], zeta_ref[d], gchunk_ref[d], ore_ref,
                                 sr_ref.at[d]))
    while chains:
        chains = [ch for ch in chains if next(ch, True) is None]


def _scan(ret, gla, glr, tables, gla_w, gla_b, n_batch, seq, ctx_len):
    n = ret.shape[0]
    c = CHUNK
    nc_ctx, nc_lat = ctx_len // c, seq // c
    n_steps = nc_ctx + nc_lat
    ctx_base = n_batch * nc_lat

    def fwd(b, s):
        return (jnp.where(s < nc_ctx, ctx_base + b * nc_ctx + s, b * nc_lat + (s - nc_ctx)), 0)

    def bwd(b, s):
        return (jnp.where(s < nc_ctx, ctx_base + b * nc_ctx + (nc_ctx - 1 - s), b * nc_lat + (n_steps - 1 - s)), 0)

    const3 = lambda b, s: (0, 0, 0)
    gw = jnp.zeros((2, LANES, LANES), F32).at[:, :GLA_LOWRANK].set(gla_w)
    gb = gla_b[:, None, :]
    chain_in = lambda m: [pl.BlockSpec((c, 768), m), pl.BlockSpec((c, 768), m), pl.BlockSpec((c, LANES), m)]
    o_sds = jax.ShapeDtypeStruct((n, 256), F32)
    return pl.pallas_call(
        _scan_kernel,
        out_shape=(o_sds, o_sds, o_sds, o_sds),
        grid=(n_batch, n_steps),
        in_specs=chain_in(fwd) + chain_in(bwd) + [pl.BlockSpec(t.shape, const3) for t in tables]
        + [pl.BlockSpec(gw.shape, const3), pl.BlockSpec(gb.shape, const3)],
        out_specs=(pl.BlockSpec((c, 256), fwd), pl.BlockSpec((c, 256), fwd),
                   pl.BlockSpec((c, 256), bwd), pl.BlockSpec((c, 256), bwd)),
        scratch_shapes=[pltpu.VMEM((2, LANES, 2 * LANES), F32), pltpu.VMEM((2, LANES, 2 * LANES), F32)],
        compiler_params=_cparams("arbitrary", "arbitrary"),
        name="scan",
    )(ret, gla, glr, ret, gla, glr, *tables, gw, gb)


def _attn_kernel(n_lat_blocks, lam_init, *refs):
    if n_lat_blocks:
        q_ref, kc_ref, vtc_ref, kl_ref, vtl_ref, lp_ref, g_ref, o_ref, m_ref, acc_ref, st_ref = refs
    else:
        q_ref, kc_ref, vtc_ref, lp_ref, g_ref, o_ref, m_ref, acc_ref, st_ref = refs
    tq = q_ref.shape[0]
    q = q_ref[...].astype(F32)
    lane = lax.broadcasted_iota(I32, (1, LANES), 1)
    qt = jnp.concatenate([jnp.where(lane < DIFF_DH, q, 0.0).T, jnp.where(lane >= DIFF_DH, q, 0.0).T],
                         axis=1).astype(BF16)
    m_ref[...] = jnp.full(m_ref.shape, -jnp.inf, F32)
    acc_ref[...] = jnp.zeros_like(acc_ref)

    def scores(slot, kb):
        st_ref[slot, 0:kb.shape[0], :] = jnp.dot(kb, qt, preferred_element_type=F32)

    def absorb(slot, vtb):
        st = st_ref[slot, 0:vtb.shape[1], :]
        m_prev = m_ref[...]
        m_new = jnp.maximum(m_prev, jnp.max(st, axis=0, keepdims=True))
        alpha = jnp.exp(m_prev - m_new)
        p = jnp.exp(st - m_new).astype(BF16)
        acc_ref[...] = alpha * acc_ref[...] + jnp.dot(vtb, p, preferred_element_type=F32)
        m_ref[...] = m_new

    def k_lat(blk):
        return kl_ref[pl.ds(pl.multiple_of(blk * ATTN_TK, ATTN_TK), ATTN_TK), :]

    def vt_lat(blk):
        return vtl_ref[:, pl.ds(pl.multiple_of(blk * ATTN_TK, ATTN_TK), ATTN_TK)]

    scores(0, kc_ref[...])
    if not n_lat_blocks:
        absorb(0, vtc_ref[...])
    else:
        scores(1, k_lat(0))
        absorb(0, vtc_ref[...])
        n_pairs = (n_lat_blocks - 1) // 2

        def body(i, carry):
            scores(0, k_lat(2 * i + 1))
            absorb(1, vt_lat(2 * i))
            scores(1, k_lat(2 * i + 2))
            absorb(0, vt_lat(2 * i + 1))
            return carry
        lax.fori_loop(0, n_pairs, body, 0)
        done = 2 * n_pairs
        if n_lat_blocks - done == 2:
            scores(0, k_lat(done + 1))
            absorb(1, vt_lat(done))
            absorb(0, vt_lat(done + 1))
        else:
            absorb(1, vt_lat(done))

    lp = lp_ref[...]
    lam = (jnp.exp(jnp.sum(lp[0:1] * lp[1:2], axis=1, keepdims=True))
           - jnp.exp(jnp.sum(lp[2:3] * lp[3:4], axis=1, keepdims=True)) + lam_init)
    acc = acc_ref[...]
    o1 = acc[0:LANES, :tq] / acc[LANES:LANES + 1, :tq]
    o2 = acc[0:LANES, tq:] / acc[LANES:LANES + 1, tq:]
    ot = o1 - lam * o2
    ot = ot * lax.rsqrt(jnp.mean(ot * ot, axis=0, keepdims=True) + EPS) * g_ref[...][:, 0:1] * (1.0 - lam_init)
    o_ref[...] = ot.T.astype(o_ref.dtype)


def _diff_attention(dq, dk, vt, lp, g, lam_init, n_batch, seq, ctx_len, latent):
    ctx_blk0 = (n_batch * seq) // ctx_len
    vrows = LANES + ATTN_ONES
    kc_spec = pl.BlockSpec((ctx_len, LANES), lambda b, h, i: (ctx_blk0 + b, h))
    vtc_spec = pl.BlockSpec((vrows, ctx_len), lambda b, h, i: (h, ctx_blk0 + b))
    const = lambda b, h, i: (0, 0)
    if latent:
        tq = ATTN_TQ
        n_q = seq // tq
        q_map = lambda b, h, i: (b * n_q + i, h)
        in_specs = [pl.BlockSpec((tq, LANES), q_map), kc_spec, vtc_spec,
                    pl.BlockSpec((seq, LANES), lambda b, h, i: (b, h)),
                    pl.BlockSpec((vrows, seq), lambda b, h, i: (h, b))]
        args = (dq, dk, vt, dk, vt)
        n_rows, o_map, n_lat_blocks = n_batch * seq, q_map, seq // ATTN_TK
    else:
        tq, n_q = ctx_len, 1
        in_specs = [kc_spec, kc_spec, vtc_spec]
        args = (dq, dk, vt)
        n_rows, o_map, n_lat_blocks = n_batch * ctx_len, (lambda b, h, i: (b, h)), 0
    in_specs += [pl.BlockSpec(lp.shape, const), pl.BlockSpec((LANES, LANES), const)]
    return pl.pallas_call(
        functools.partial(_attn_kernel, n_lat_blocks, lam_init),
        out_shape=jax.ShapeDtypeStruct((n_rows, N_HEADS * LANES), BF16),
        grid=(n_batch, N_HEADS, n_q),
        in_specs=in_specs,
        out_specs=pl.BlockSpec((tq, LANES), o_map),
        scratch_shapes=[pltpu.VMEM((1, 2 * tq), F32), pltpu.VMEM((vrows, 2 * tq), F32),
                        pltpu.VMEM((2, max(ATTN_TK, ctx_len), 2 * tq), F32)],
        compiler_params=_cparams("parallel", "parallel", "parallel"),
        name="diff_attention_lat" if latent else "diff_attention_ctx",
    )(*args, lp, g)


def _merge_kernel(tiles_per_batch, n_batch, orf_ref, orb_ref, ogf_ref, ogb_ref, rg_ref, gr_ref, dif_ref,
                  x_ref, mod_ref, gpost_ref, gpre_ref, glag_ref, wout_ref, wrt_ref,
                  xo_ref, hp_ref, sc_ref):
    d = x_ref.shape[-1]
    r = jnp.minimum(pl.program_id(0) // tiles_per_batch, n_batch)
    gate1 = mod_ref[pl.ds(r, 1), 2 * d:3 * d]
    shift2 = mod_ref[pl.ds(r, 1), 3 * d:4 * d]
    scale2 = mod_ref[pl.ds(r, 1), 4 * d:5 * d]
    gi = lax.broadcasted_iota(I32, (256, 256), 0) // RET_DV
    gj = lax.broadcasted_iota(I32, (256, 256), 1) // RET_DV
    group_mean = jnp.where(gi == gj, 1.0 / RET_DV, 0.0).astype(BF16)

    def head_norm(o):
        ms = jnp.dot((o * o).astype(BF16), group_mean, preferred_element_type=F32)
        return o * lax.rsqrt(ms + EPS)

    ret = head_norm(orf_ref[...] + orb_ref[...]) * _silu(rg_ref[...].astype(F32))
    gla = head_norm(ogf_ref[...] + ogb_ref[...]) * glag_ref[...] * _silu(gr_ref[...].astype(F32))
    m = (jnp.dot(ret.astype(BF16), wout_ref[0:256, :], preferred_element_type=F32)
         + jnp.dot(gla.astype(BF16), wout_ref[256:512, :], preferred_element_type=F32)
         + jnp.dot(dif_ref[...], wout_ref[512:1024, :], preferred_element_type=F32))
    x_new = x_ref[...] + gate1 * (_rms(m) * gpost_ref[...])
    xo_ref[...] = x_new
    h2 = _rms(x_new) * gpre_ref[...] * (1.0 + scale2) + shift2
    hp_ref[...] = _pack_halves(h2[:, :d // 2], h2[:, d // 2:])
    logits = lax.dot_general(wrt_ref[...], h2, NT_DIMS, precision=HIGHEST, preferred_element_type=F32)
    sc_ref[...] = 1.0 / (1.0 + jnp.exp(-logits))


def _merge(o_rf, o_rb, o_gf, o_gb, ret, gla, dif, xs, mod, g_post, g_pre_ffn, gla_g, w_out, w_rt,
           n_rows, n_batch, seq):
    d = xs.shape[-1]
    tm = TOKEN_TILE
    row = lambda i: (i, 0)
    gate_col = lambda i: (i, 2)
    const = lambda i: (0, 0)
    return pl.pallas_call(
        functools.partial(_merge_kernel, seq // tm, n_batch),
        out_shape=(jax.ShapeDtypeStruct((n_rows, d), F32), jax.ShapeDtypeStruct((n_rows, d // 2), U32),
                   jax.ShapeDtypeStruct((N_EXPERTS, n_rows), F32)),
        grid=(n_rows // tm,),
        in_specs=[
            pl.BlockSpec((tm, 256), row), pl.BlockSpec((tm, 256), row),
            pl.BlockSpec((tm, 256), row), pl.BlockSpec((tm, 256), row),
            pl.BlockSpec((tm, 256), gate_col), pl.BlockSpec((tm, 256), gate_col),
            pl.BlockSpec((tm, 512), row), pl.BlockSpec((tm, d), row),
            pl.BlockSpec(mod.shape, const), pl.BlockSpec((1, d), const), pl.BlockSpec((1, d), const),
            pl.BlockSpec((1, 256), const), pl.BlockSpec(w_out.shape, const), pl.BlockSpec(w_rt.shape, const),
        ],
        out_specs=(pl.BlockSpec((tm, d), row), pl.BlockSpec((tm, d // 2), row),
                   pl.BlockSpec((N_EXPERTS, tm), lambda i: (0, i))),
        compiler_params=_cparams("parallel"),
        name="merge",
    )(o_rf, o_rb, o_gf, o_gb, ret, gla, dif, xs, mod, g_post, g_pre_ffn, gla_g, w_out, w_rt)


def _route_kernel(sc_ref, bias_ref, eidx_ref, wts_ref, pos_ref, meta_ref):
    tt = sc_ref.shape[-1]
    scores = sc_ref[...]
    sel = scores + bias_ref[...][:, 0:1]
    sub = lax.broadcasted_iota(I32, (GROUP_SIZE, tt), 0)
    neg = -jnp.inf
    gscore = []
    for g in range(N_GROUPS):
        xg = sel[g * GROUP_SIZE:(g + 1) * GROUP_SIZE]
        m1 = jnp.max(xg, axis=0, keepdims=True)
        i1 = jnp.min(jnp.where(xg == m1, sub, GROUP_SIZE), axis=0, keepdims=True)
        m2 = jnp.max(jnp.where(sub == i1, neg, xg), axis=0, keepdims=True)
        gscore.append(m1 + m2)
    rows = []
    for g in range(N_GROUPS):
        rank = jnp.zeros((1, tt), I32)
        for o in range(N_GROUPS):
            if o == g:
                continue
            ahead = (gscore[o] >= gscore[g]) if o < g else (gscore[o] > gscore[g])
            rank = rank + ahead.astype(I32)
        rows.append(jnp.where(rank < TOPK_GROUPS, sel[g * GROUP_SIZE:(g + 1) * GROUP_SIZE], neg))
    masked = jnp.concatenate(rows, axis=0)
    eio = lax.broadcasted_iota(I32, (N_EXPERTS, tt), 0)
    member = jnp.zeros((N_EXPERTS, tt), F32)
    idxs, ws = [], []
    for _ in range(TOP_K):
        m = jnp.max(masked, axis=0, keepdims=True)
        i = jnp.min(jnp.where(masked == m, eio, N_EXPERTS), axis=0, keepdims=True)
        hit = eio == i
        idxs.append(i)
        ws.append(jnp.sum(jnp.where(hit, scores, 0.0), axis=0, keepdims=True))
        member = jnp.where(hit, 1.0, member)
        masked = jnp.where(hit, neg, masked)
    wsum = ws[0]
    for w in ws[1:]:
        wsum = wsum + w
    ti = lax.broadcasted_iota(I32, (tt, tt), 0)
    tj = lax.broadcasted_iota(I32, (tt, tt), 1)
    before = jnp.where(ti < tj, 1.0, 0.0).astype(BF16)
    rank_in_e = jnp.dot(member.astype(BF16), before, preferred_element_type=F32)
    cnt = jnp.sum(member, axis=1, keepdims=True)
    padded = jnp.floor((cnt + (SUBLANES - 1)) * (1.0 / SUBLANES)) * SUBLANES
    ei = lax.broadcasted_iota(I32, (N_EXPERTS, N_EXPERTS), 0)
    ej = lax.broadcasted_iota(I32, (N_EXPERTS, N_EXPERTS), 1)
    lower = jnp.where(ej < ei, 1.0, 0.0)
    off = jnp.dot(lower, jnp.broadcast_to(padded, (N_EXPERTS, LANES)), precision=HIGHEST,
                  preferred_element_type=F32)
    slot = rank_in_e + off[:, 0:1]
    zrow_i = jnp.zeros((SUBLANES - TOP_K, tt), I32)
    zrow_f = jnp.zeros((SUBLANES - TOP_K, tt), F32)
    pos = [jnp.sum(jnp.where(eio == i, slot, 0.0), axis=0, keepdims=True).astype(I32) for i in idxs]
    eidx_ref[...] = jnp.concatenate(idxs + [zrow_i], axis=0)
    wts_ref[...] = jnp.concatenate([w / wsum * ROUTED_SCALE for w in ws] + [zrow_f], axis=0)
    pos_ref[...] = jnp.concatenate(pos + [zrow_i], axis=0)
    meta_ref[0] = jnp.concatenate([jnp.broadcast_to(cnt, (N_EXPERTS, LANES)), off], axis=1).astype(I32)


def _route(scores_t, bias, tt, row0, n):
    tile0 = row0 // tt
    tok = lambda i: (0, i)
    return pl.pallas_call(
        _route_kernel,
        out_shape=(jax.ShapeDtypeStruct((SUBLANES, n), I32), jax.ShapeDtypeStruct((SUBLANES, n), F32),
                   jax.ShapeDtypeStruct((SUBLANES, n), I32),
                   jax.ShapeDtypeStruct((n // tt, N_EXPERTS, 2 * LANES), I32)),
        grid=(n // tt,),
        in_specs=[pl.BlockSpec((N_EXPERTS, tt), lambda i: (0, i + tile0)),
                  pl.BlockSpec((N_EXPERTS, LANES), lambda i: (0, 0))],
        out_specs=(pl.BlockSpec((SUBLANES, tt), tok), pl.BlockSpec((SUBLANES, tt), tok),
                   pl.BlockSpec((SUBLANES, tt), tok),
                   pl.BlockSpec((1, N_EXPERTS, 2 * LANES), lambda i: (i, 0, 0))),
        compiler_params=_cparams("parallel"),
        name="route",
    )(scores_t, jnp.broadcast_to(bias[:, None], (N_EXPERTS, LANES)))


def _moe_kernel(group, n_first, has_prev, pos_ref, wts_ref, meta_ref, hp_ref, weg_ref, weu_ref, wed_ref, *refs):
    o_ref, xg_ref, y_ref, cg_ref, rt_ref = refs[1:] if has_prev else refs
    tt, half = hp_ref.shape
    d = 2 * half
    eg = pl.program_id(1)

    @pl.when((pl.program_id(0) == 0) & (eg == 0))
    def _():
        def zero(i, carry):
            for u in range(SUBLANES):
                rt_ref[i * SUBLANES + u] = 0
            return carry
        lax.fori_loop(0, rt_ref.shape[0] // SUBLANES, zero, 0)

    @pl.when(eg == 0)
    def _():
        def scatter(t, carry):
            for k in range(TOP_K):
                rt_ref[pos_ref[t * SUBLANES + k]] = t
            return carry
        lax.fori_loop(0, tt, scatter, 0)

    sub = lax.broadcasted_iota(I32, (SUBLANES, half), 0)

    def gather8(src_ref, row_of):
        buf = jnp.zeros((SUBLANES, half), U32)
        for u in range(SUBLANES):
            buf = jnp.where(sub == u, src_ref[pl.ds(row_of(u), 1), :], buf)
        return buf

    row = lax.broadcasted_iota(I32, (MOE_BLOCK, half), 0)

    def expert_block(ge, buf, base, n_valid):
        base = pl.multiple_of(base, SUBLANES)
        for i in range(MOE_BLOCK):
            xg_ref[buf, i:i + 1, :] = hp_ref[pl.ds(rt_ref[base + i], 1), :]
        yield
        xa, xb = _unpack_halves(xg_ref[buf])
        xa, xb = xa.astype(BF16), xb.astype(BF16)
        hg = (jnp.dot(xa, weg_ref[ge, 0:half, :], preferred_element_type=F32)
              + jnp.dot(xb, weg_ref[ge, half:d, :], preferred_element_type=F32))
        hu = (jnp.dot(xa, weu_ref[ge, 0:half, :], preferred_element_type=F32)
              + jnp.dot(xb, weu_ref[ge, half:d, :], preferred_element_type=F32))
        yield
        y = jnp.dot((_silu(hg) * hu).astype(BF16), wed_ref[ge], preferred_element_type=F32)
        yield
        pltpu.store(y_ref.at[pl.ds(base, MOE_BLOCK), :], _pack_halves(y[:, :half], y[:, half:]),
                    mask=row < n_valid)

    cnts = [meta_ref[0, eg * group + ge] for ge in range(group)]
    offs = [meta_ref[1, eg * group + ge] for ge in range(group)]
    chains = [expert_block(ge, ge * n_first + j, offs[ge] + j * MOE_BLOCK, cnts[ge] - j * MOE_BLOCK)
              for ge in range(group) for j in range(n_first)]
    while chains:
        chains = [ch for ch in chains if next(ch, True) is None]
    for ge in range(group):
        def more(j, carry, ge=ge):
            for _ in expert_block(ge, ge * n_first, offs[ge] + j * MOE_BLOCK, cnts[ge] - j * MOE_BLOCK):
                pass
            return carry
        lax.fori_loop(n_first, (cnts[ge] + (MOE_BLOCK - 1)) // MOE_BLOCK, more, 0)

    @pl.when(eg == N_EXPERTS // group - 1)
    def _():
        def tokens(g, carry):
            t0 = pl.multiple_of(g * SUBLANES, SUBLANES)
            wg = wts_ref[pl.ds(t0, SUBLANES), :]
            acc_a = jnp.zeros((SUBLANES, half), F32)
            acc_b = jnp.zeros((SUBLANES, half), F32)
            for k in range(TOP_K):
                slot_of = lambda u, k=k: pos_ref[(t0 + u) * SUBLANES + k]
                if k < COMBINE_SELECTS:
                    rows = gather8(y_ref, slot_of)
                else:
                    for u in range(SUBLANES):
                        cg_ref[k - COMBINE_SELECTS, u:u + 1, :] = y_ref[pl.ds(slot_of(u), 1), :]
                    rows = cg_ref[k - COMBINE_SELECTS]
                ya, yb = _unpack_halves(rows)
                acc_a = acc_a + wg[:, k:k + 1] * ya
                acc_b = acc_b + wg[:, k:k + 1] * yb
            o_ref[pl.ds(t0, SUBLANES), 0:half] = acc_a
            o_ref[pl.ds(t0, SUBLANES), half:d] = acc_b
            return carry

        lax.fori_loop(0, tt // SUBLANES, tokens, 0)


def _moe_segment(prev, scores_t, hp, router_bias, expert_w, layer, row0, n, tt):
    n_all, half = hp.shape
    d = 2 * half
    w_eg, w_eu, w_ed = expert_w
    de = w_eg.shape[-1]
    n_first = max(1, -(-tt * TOP_K // (N_EXPERTS * MOE_BLOCK)))
    group = max(1, MOE_CHAINS // n_first)
    group0 = layer * (N_EXPERTS // group)
    n_tiles, tile0 = n // tt, row0 // tt
    n_slots = -(-(tt * TOP_K + N_EXPERTS * (SUBLANES - 1) + n_first * MOE_BLOCK) // (SUBLANES * LANES)) * SUBLANES * LANES
    e_idx, wts, pos, meta = _route(scores_t, router_bias, tt, row0, n)
    cnt, off = meta[:, :, 0], meta[:, :, LANES]

    def per_token(a):
        return a.T.reshape(-1)

    meta_s = jnp.zeros((n_tiles, SUBLANES, LANES), I32)
    meta_s = meta_s.at[:, 0, :N_EXPERTS].set(cnt).at[:, 1, :N_EXPERTS].set(off).reshape(-1, LANES)

    tile = lambda i, e: (i + tile0, 0)
    expert = lambda i, e: (group0 + e, 0, 0)
    smem = functools.partial(pl.BlockSpec, memory_space=pltpu.SMEM)
    once = pl.Buffered(1)
    in_specs = [
        smem((SUBLANES * tt,), lambda i, e: (i,)), pl.BlockSpec((tt, SUBLANES), lambda i, e: (i, 0)),
        smem((SUBLANES, LANES), lambda i, e: (i, 0)),
        pl.BlockSpec((tt, half), tile, pipeline_mode=once),
        pl.BlockSpec((group, d, de), expert), pl.BlockSpec((group, d, de), expert),
        pl.BlockSpec((group, de, d), expert),
    ]
    args = [per_token(pos), wts.T, meta_s, hp, w_eg, w_eu, w_ed]
    aliases = {}
    if prev is not None:
        in_specs.append(pl.BlockSpec(memory_space=pl.ANY))
        args.append(prev)
        aliases = {len(args) - 1: 0}
    return pl.pallas_call(
        functools.partial(_moe_kernel, group, n_first, prev is not None),
        out_shape=jax.ShapeDtypeStruct((n_all, d), F32),
        grid=(n_tiles, N_EXPERTS // group),
        in_specs=in_specs,
        out_specs=pl.BlockSpec((tt, d), tile, pipeline_mode=once),
        scratch_shapes=[pltpu.VMEM((group * n_first, MOE_BLOCK, half), U32), pltpu.VMEM((n_slots, half), U32),
                        pltpu.VMEM((TOP_K - COMBINE_SELECTS, SUBLANES, half), U32),
                        pltpu.SMEM((n_slots,), I32)],
        input_output_aliases=aliases,
        compiler_params=_cparams("arbitrary", "arbitrary"),
        name="moe",
    )(*args)


def _moe_final_kernel(tiles_per_batch, n_batch, routed_ref, hp_ref, x_ref, mod_ref, gpost_ref,
                      wsg_ref, wsu_ref, wsd_ref, o_ref):
    half = hp_ref.shape[1]
    d = 2 * half
    r = jnp.minimum(pl.program_id(0) // tiles_per_batch, n_batch)
    gate2 = mod_ref[pl.ds(r, 1), 5 * d:6 * d]
    xa, xb = _unpack_halves(hp_ref[...])
    xa, xb = xa.astype(BF16), xb.astype(BF16)
    hg = (jnp.dot(xa, wsg_ref[0:half, :], preferred_element_type=F32)
          + jnp.dot(xb, wsg_ref[half:d, :], preferred_element_type=F32))
    hu = (jnp.dot(xa, wsu_ref[0:half, :], preferred_element_type=F32)
          + jnp.dot(xb, wsu_ref[half:d, :], preferred_element_type=F32))
    f = jnp.dot((_silu(hg) * hu).astype(BF16), wsd_ref[...], preferred_element_type=F32) + routed_ref[...]
    o_ref[...] = x_ref[...] + gate2 * (_rms(f) * gpost_ref[...])


def _moe_final(routed, hp, xs, mod, g_post, w_sg, w_su, w_sd, n_batch, seq):
    n, d = routed.shape
    tm = TOKEN_TILE
    row = lambda i: (i, 0)
    const = lambda i: (0, 0)
    return pl.pallas_call(
        functools.partial(_moe_final_kernel, seq // tm, n_batch),
        out_shape=jax.ShapeDtypeStruct((n, d), F32),
        grid=(n // tm,),
        in_specs=[pl.BlockSpec((tm, d), row), pl.BlockSpec((tm, d // 2), row), pl.BlockSpec((tm, d), row),
                  pl.BlockSpec(mod.shape, const), pl.BlockSpec((1, d), const),
                  pl.BlockSpec(w_sg.shape, const), pl.BlockSpec(w_su.shape, const), pl.BlockSpec(w_sd.shape, const)],
        out_specs=pl.BlockSpec((tm, d), row),
        compiler_params=_cparams("parallel"),
        name="moe_final",
    )(routed, hp, xs, mod, g_post, w_sg, w_su, w_sd)


def _moe_layer(scores_t, hp, xs, mod, router_bias, g_post, expert_w, shared_w, layer, n_lat, n_batch, seq):
    pick = lambda rows: next(t for t in MOE_TILES if rows % t == 0)
    routed = _moe_segment(None, scores_t, hp, router_bias, expert_w, layer, 0, n_lat, pick(n_lat))
    n_ctx = hp.shape[0] - n_lat
    if n_ctx:
        tt = next(t for t in MOE_TILES if n_ctx % t == 0 and n_lat % t == 0)
        routed = _moe_segment(routed, scores_t, hp, router_bias, expert_w, layer, n_lat, n_ctx, tt)
    return _moe_final(routed, hp, xs, mod, g_post, *shared_w, n_batch, seq)


def kernel(x, c, ctx, c_ctx, w_mod, b_mod, g_pre_mix, g_post_mix, g_pre_ffn, g_post_ffn, w_in, w_out,
           ret_decay_logit, gla_w_gate, gla_b_gate, gla_norm_g, diff_lambda, diff_norm_g,
           w_router, router_bias, w_exp_gate, w_exp_up, w_exp_down, w_sh_gate, w_sh_up, w_sh_down):
    n_batch, seq, d = x.shape
    ctx_len = ctx.shape[1]
    depth = w_mod.shape[0]
    n_lat = n_batch * seq
    assert seq % TOKEN_TILE == 0 and (n_batch * ctx_len) % TOKEN_TILE == 0 and n_batch < SUBLANES
    assert seq % ATTN_TK == 0 and seq % CHUNK == 0 and ctx_len % CHUNK == 0 and n_lat % ctx_len == 0

    xs = jnp.concatenate([x.reshape(n_lat, d), ctx.reshape(n_batch * ctx_len, d)], axis=0)
    cond = jnp.zeros((SUBLANES, d), F32).at[:n_batch].set(c).at[n_batch].set(c_ctx)
    mods = _modulation(cond, w_mod, b_mod)
    tables = _rope_tables(seq, RET_DK, TOKEN_TILE) + _rope_tables(seq, DIFF_DH, TOKEN_TILE)
    lr0 = N_HEADS * (2 * RET_DK + 2 * RET_DV + 2 * GLA_DK + 2 * GLA_DV)
    row = lambda a: a[None, :]
    expert_w = tuple(w.astype(BF16).reshape((depth * N_EXPERTS,) + w.shape[2:])
                     for w in (w_exp_gate, w_exp_up, w_exp_down))

    for layer in range(depth):
        need_ctx = layer < depth - 1
        lam_init = 0.8 - 0.6 * math.exp(-0.3 * layer)
        mod = mods[layer]
        wl = w_in[layer]
        w_r = jnp.concatenate([wl[:, :lr0], wl[:, lr0:lr0 + GLA_LOWRANK],
                               jnp.zeros((d, LANES - GLA_LOWRANK), F32), wl[:, lr0 + GLA_LOWRANK:]],
                              axis=1).astype(BF16)
        ret, gla, glr, dq, dk, vt = _in_projection(xs, mod, row(g_pre_mix[layer]), w_r, tables, n_batch, seq)

        o_rf, o_gf, o_rb, o_gb = _scan(ret, gla, glr, _scan_tables(ret_decay_logit[layer]), gla_w_gate[layer],
                                       gla_b_gate[layer], n_batch, seq, ctx_len)

        lp = jnp.zeros((SUBLANES, LANES), F32).at[:4, :DIFF_DH].set(diff_lambda[layer])
        g_col = jnp.broadcast_to(diff_norm_g[layer][:, None], (LANES, LANES))
        attn = functools.partial(_diff_attention, dq, dk, vt, lp, g_col, lam_init, n_batch, seq, ctx_len)
        dif = attn(True)
        n_rows = n_lat
        if need_ctx:
            dif = jnp.concatenate([dif, attn(False)], axis=0)
            n_rows = xs.shape[0]

        xs, hp, scores_t = _merge(o_rf, o_rb, o_gf, o_gb, ret, gla, dif, xs, mod, row(g_post_mix[layer]),
                                  row(g_pre_ffn[layer]), row(jnp.tile(gla_norm_g[layer], N_HEADS)),
                                  w_out[layer].astype(BF16), w_router[layer].T, n_rows, n_batch, seq)
        shared_w = tuple(w[layer].astype(BF16) for w in (w_sh_gate, w_sh_up, w_sh_down))
        xs = _moe_layer(scores_t, hp, xs, mod, router_bias[layer], row(g_post_ffn[layer]), expert_w, shared_w,
                        layer, n_lat, n_batch, seq)
    return xs[:n_lat].reshape(n_batch, seq, d)
```

```python
import functools
import math

import jax
import jax.numpy as jnp
import numpy as np
from jax import lax
from jax.experimental import pallas as pl
from jax.experimental.pallas import tpu as pltpu

F32 = jnp.float32
BF16 = jnp.bfloat16
I32 = jnp.int32
U32 = jnp.uint32

GRID_W = 64
CHUNK = 128
N_HEADS = 4
RET_DK, RET_DV = 32, 64
GLA_DK, GLA_DV = 32, 64
GLA_LOWRANK = 16
GLA_TAU = 16.0
DIFF_DH = 64
ROPE_BASE = 10000.0
N_EXPERTS = 64
TOP_K = 6
N_GROUPS = 8
TOPK_GROUPS = 4
GROUP_SIZE = N_EXPERTS // N_GROUPS
ROUTED_SCALE = 2.5
EPS = 1e-6
GLA_SUB = 16
SCAN_CHUNKS = 2

LANES = 128
SUBLANES = 8
VMEM_LIMIT_BYTES = 56 * 1024 * 1024

TOKEN_TILE = 512
ATTN_TQ = 512
ATTN_TK = 512
ATTN_ONES = 16
MOE_TILES = (2048, 1024, 512)
MOE_BLOCK = 128
MOE_CHAINS = 4
COMBINE_SELECTS = 3

HIGHEST = lax.Precision.HIGHEST
NT_DIMS = (((1,), (1,)), ((), ()))
TN_DIMS = (((0,), (0,)), ((), ()))


def _cparams(*sem):
    return pltpu.CompilerParams(dimension_semantics=sem, vmem_limit_bytes=VMEM_LIMIT_BYTES)


def _log_sigmoid(x):
    return jnp.minimum(x, 0.0) - jnp.log(1.0 + jnp.exp(-jnp.abs(x)))


def _silu(x):
    return x * (1.0 / (1.0 + jnp.exp(-x)))


def _rms(x):
    return x * lax.rsqrt(jnp.mean(x * x, axis=-1, keepdims=True) + EPS)


def _pack_halves(a, b):
    ua = lax.bitcast_convert_type(a.astype(BF16).astype(F32), U32)
    ub = lax.bitcast_convert_type(b.astype(BF16).astype(F32), U32)
    return (ua & jnp.uint32(0xFFFF0000)) | (ub >> 16)


def _unpack_halves(w):
    a = lax.bitcast_convert_type(w & jnp.uint32(0xFFFF0000), F32)
    b = lax.bitcast_convert_type(w << 16, F32)
    return a, b


def _mod_kernel(cond_ref, w_ref, b_ref, o_ref):
    a = _silu(cond_ref[...])
    o_ref[0] = jnp.dot(a, w_ref[0], precision=HIGHEST, preferred_element_type=F32) + b_ref[0]


def _modulation(cond, w_mod, b_mod):
    n_layers, d, d6 = w_mod.shape
    tn = 1024
    return pl.pallas_call(
        _mod_kernel,
        out_shape=jax.ShapeDtypeStruct((n_layers, SUBLANES, d6), F32),
        grid=(n_layers, d6 // tn),
        in_specs=[
            pl.BlockSpec((SUBLANES, d), lambda l, j: (0, 0)),
            pl.BlockSpec((1, d, tn), lambda l, j: (l, 0, j)),
            pl.BlockSpec((1, 1, tn), lambda l, j: (l, 0, j)),
        ],
        out_specs=pl.BlockSpec((1, SUBLANES, tn), lambda l, j: (l, 0, j)),
        compiler_params=_cparams("parallel", "parallel"),
        name="modulation",
    )(cond, w_mod, b_mod.reshape(n_layers, 1, d6))


def _rope(x, cos, sin, quarter):
    lane = lax.broadcasted_iota(I32, (1, LANES), 1)
    first = (lane % (2 * quarter)) < quarter
    outs = []
    for c in range(x.shape[-1] // LANES):
        xc = x[:, c * LANES:(c + 1) * LANES]
        partner = jnp.where(first, pltpu.roll(xc, LANES - quarter, 1), pltpu.roll(xc, quarter, 1))
        outs.append(xc * cos + partner * sin)
    return outs[0] if len(outs) == 1 else jnp.concatenate(outs, axis=-1)


def _split_rows(a, b, tm, width):
    n_a = a.shape[0] // tm
    return n_a, [pl.BlockSpec((tm, width), lambda i: (jnp.minimum(i, n_a - 1), 0)),
                 pl.BlockSpec((tm, width), lambda i: (jnp.maximum(i - n_a, 0), 0))]


def _pick_rows(n_a, a_ref, b_ref):
    return jnp.where(pl.program_id(0) < n_a, a_ref[...], b_ref[...])


def _inproj_kernel(tiles_per_batch, n_batch, n_a, *refs):
    n_x = 2 if n_a else 1
    (mod_ref, g_ref, w_ref, c32_ref, s32_ref, c64_ref, s64_ref,
     ret_ref, gla_ref, glr_ref, dq_ref, dk_ref, vt_ref) = refs[n_x:]
    x = _pick_rows(n_a, *refs[:2]) if n_a else refs[0][...]
    d = x.shape[-1]
    r = jnp.minimum(pl.program_id(0) // tiles_per_batch, n_batch)
    shift = mod_ref[pl.ds(r, 1), 0:d]
    scale = mod_ref[pl.ds(r, 1), d:2 * d]
    h = (_rms(x) * g_ref[...] * (1.0 + scale) + shift).astype(BF16)

    def proj(lo, hi):
        return jnp.dot(h, w_ref[:, lo:hi], preferred_element_type=F32)

    c32, s32 = c32_ref[...], s32_ref[...]
    c64, s64 = c64_ref[...], s64_ref[...]
    ret = proj(0, 768)
    ret_ref[:, 0:128] = _rope(ret[:, 0:128], c32, s32, RET_DK // 4).astype(BF16)
    ret_ref[:, 128:256] = (_rope(ret[:, 128:256], c32, s32, RET_DK // 4) * RET_DK ** -0.5).astype(BF16)
    ret_ref[:, 256:768] = ret[:, 256:768].astype(BF16)
    gla = proj(768, 1536)
    gla_ref[:, 0:128] = (gla[:, 0:128] * GLA_DK ** -0.5).astype(BF16)
    gla_ref[:, 128:768] = gla[:, 128:768].astype(BF16)
    glr_ref[...] = proj(1536, 1664)
    dq_ref[...] = (_rope(proj(1664, 2176), c64, s64, DIFF_DH // 4) * DIFF_DH ** -0.5).astype(BF16)
    dk_ref[...] = _rope(proj(2176, 2688), c64, s64, DIFF_DH // 4).astype(BF16)
    dv = proj(2688, 3200)
    vrows = LANES + ATTN_ONES
    for hd in range(N_HEADS):
        vt_ref[hd * vrows:hd * vrows + LANES, :] = dv[:, hd * LANES:(hd + 1) * LANES].T.astype(BF16)
        vt_ref[hd * vrows + LANES:(hd + 1) * vrows, :] = jnp.ones((ATTN_ONES, dv.shape[0]), BF16)


def _in_projection(xs, mod, g_pre, w_r, tables, n_batch, seq):
    tm = TOKEN_TILE
    if isinstance(xs, tuple):
        d = xs[0].shape[1]
        n = xs[0].shape[0] + xs[1].shape[0]
        n_a, x_specs = _split_rows(*xs, tm, d)
    else:
        n, d = xs.shape
        n_a, x_specs, xs = 0, [pl.BlockSpec((tm, d), lambda i: (i, 0))], (xs,)
    tiles_per_batch = seq // tm
    n_lat_tiles = n_batch * tiles_per_batch
    c32, s32, c64, s64 = tables

    def tab_map(i):
        return (jnp.where(i < n_lat_tiles, i % tiles_per_batch, tiles_per_batch), 0)

    row = lambda i: (i, 0)
    const = lambda i: (0, 0)
    tab_spec = pl.BlockSpec((tm, LANES), tab_map)
    return pl.pallas_call(
        functools.partial(_inproj_kernel, tiles_per_batch, n_batch, n_a),
        out_shape=(
            jax.ShapeDtypeStruct((n, 768), BF16), jax.ShapeDtypeStruct((n, 768), BF16),
            jax.ShapeDtypeStruct((n, LANES), F32),
            jax.ShapeDtypeStruct((n, 512), BF16), jax.ShapeDtypeStruct((n, 512), BF16),
            jax.ShapeDtypeStruct((N_HEADS * (LANES + ATTN_ONES), n), BF16)),
        grid=(n // tm,),
        in_specs=x_specs + [
            pl.BlockSpec(mod.shape, const),
            pl.BlockSpec((1, d), const),
            pl.BlockSpec(w_r.shape, const),
            tab_spec, tab_spec, tab_spec, tab_spec,
        ],
        out_specs=(
            pl.BlockSpec((tm, 768), row), pl.BlockSpec((tm, 768), row), pl.BlockSpec((tm, LANES), row),
            pl.BlockSpec((tm, 512), row), pl.BlockSpec((tm, 512), row),
            pl.BlockSpec((N_HEADS * (LANES + ATTN_ONES), tm), lambda i: (0, i))),
        compiler_params=_cparams("parallel"),
        name="in_projection",
    )(*xs, mod, g_pre, w_r, c32, s32, c64, s64)


def _rope_tables(seq, head_dim, extra_rows):
    half, quarter = head_dim // 2, head_dim // 4
    freqs = (ROPE_BASE ** (-np.arange(quarter, dtype=np.float32) / quarter)).astype(np.float32)
    t = np.arange(seq)
    row = (t // GRID_W).astype(np.float32)
    col = (t % GRID_W).astype(np.float32)
    j = np.arange(LANES) % head_dim
    jj = j % half
    pos = np.where((j < half)[None, :], row[:, None], col[:, None])
    ang = (pos * freqs[jj % quarter][None, :]).astype(np.float32)
    cos = np.cos(ang)
    sin = np.sin(ang) * np.where(jj < quarter, -1.0, 1.0)[None, :]
    cos = np.concatenate([cos, np.ones((extra_rows, LANES))], axis=0).astype(np.float32)
    sin = np.concatenate([sin, np.zeros((extra_rows, LANES))], axis=0).astype(np.float32)
    return jnp.asarray(cos), jnp.asarray(sin)


def _head_stack(x, width):
    lane = lax.broadcasted_iota(I32, (1, x.shape[-1]), 1)
    zero = jnp.zeros_like(x)
    return jnp.concatenate([jnp.where(lane // width == h, x, zero) for h in range(N_HEADS)], axis=0)


def _head_select(x4, rows):
    lane = lax.broadcasted_iota(I32, (1, x4.shape[-1]), 1)
    out = jnp.zeros((rows, x4.shape[-1]), F32)
    for h in range(N_HEADS):
        out = out + jnp.where(lane // RET_DV == h, x4[h * rows:(h + 1) * rows], 0.0)
    return out


def _scan_tables_kernel(rl_lane_ref, rl_rows_ref, rl_col_ref, dmat_ref, xi_ref, zeta_ref, gchunk_ref):
    c = CHUNK
    idx = lax.broadcasted_iota(I32, (c, 1), 0).astype(F32)
    ri4 = lax.broadcasted_iota(I32, (N_HEADS * c, c), 0) % c
    ci4 = lax.broadcasted_iota(I32, (N_HEADS * c, c), 1)
    dist = jnp.abs(ri4 - ci4).astype(F32)
    for d in range(2):
        lg_lane = _log_sigmoid(rl_lane_ref[d])
        lg_rows = _log_sigmoid(rl_rows_ref[d])
        lg_col = _log_sigmoid(rl_col_ref[d])
        att4 = (ri4 <= ci4) if d else (ri4 >= ci4)
        dmat_ref[d] = jnp.where(att4, jnp.exp(dist * lg_rows), 0.0)
        xi_ref[d] = jnp.exp(((c - idx) if d else (idx + 1.0)) * lg_lane)
        zeta_ref[d] = jnp.exp((idx if d else (c - 1.0 - idx)) * lg_lane)
        g_chunk = jnp.exp(float(c) * lg_col)
        gchunk_ref[d] = jnp.concatenate([g_chunk, g_chunk], axis=1)


def _scan_tables(ret_logit):
    c = CHUNK
    rl_lane = jnp.repeat(ret_logit, RET_DK, axis=1)[:, None, :]
    rl_rows = jnp.broadcast_to(jnp.repeat(ret_logit, c, axis=1)[:, :, None], (2, N_HEADS * c, c))
    rl_col = jnp.broadcast_to(jnp.repeat(ret_logit, RET_DK, axis=1)[:, :, None], (2, LANES, LANES))
    return pl.pallas_call(
        _scan_tables_kernel,
        out_shape=(jax.ShapeDtypeStruct((2, N_HEADS * c, c), F32), jax.ShapeDtypeStruct((2, c, LANES), F32),
                   jax.ShapeDtypeStruct((2, c, LANES), F32), jax.ShapeDtypeStruct((2, LANES, 2 * LANES), F32)),
        name="scan_tables",
    )(rl_lane, rl_rows, rl_col)


def _state_block_mask():
    return (lax.broadcasted_iota(I32, (LANES, 2 * LANES), 0) // RET_DK
            == lax.broadcasted_iota(I32, (LANES, 2 * LANES), 1) // RET_DV)


def _ret_chain(ret_ref, r0, dmat, xi, zeta, g_chunk, out):
    c = CHUNK
    q = ret_ref[r0:r0 + c, 0:128]
    k = ret_ref[r0:r0 + c, 128:256]
    v = ret_ref[r0:r0 + c, 256:512]
    s = lax.dot_general(_head_stack(q, RET_DK), k, NT_DIMS, preferred_element_type=F32)
    kz = (k.astype(F32) * zeta).astype(BF16)
    u = lax.dot_general(kz, v, TN_DIMS, preferred_element_type=F32)
    yield
    o4 = jnp.dot((s * dmat).astype(BF16), v, preferred_element_type=F32)
    out.update(qx=(q.astype(F32) * xi).astype(BF16), u=jnp.where(_state_block_mask(), u, 0.0), g=g_chunk)
    yield
    out.update(intra=_head_select(o4, c))


def _state_chain(parts, o_ref, s_ref):
    c = CHUNK
    s = s_ref[...]
    for r0, p in parts:
        o_ref[r0:r0 + c, :] = p["intra"] + jnp.dot(p["qx"], s.astype(BF16), preferred_element_type=F32)
        s = s * p["g"] + p["u"]
        yield
    s_ref[...] = s


def _gla_chain(rev, gla_ref, glr_ref, r0, gw, gb, out):
    c = CHUNK
    ri = lax.broadcasted_iota(I32, (c, c), 0)
    ci = lax.broadcasted_iota(I32, (c, c), 1)
    attends = (ri <= ci) if rev else (ri >= ci)
    bd = _state_block_mask()
    gq = gla_ref[r0:r0 + c, 0:128].astype(F32)
    gk = gla_ref[r0:r0 + c, 128:256].astype(F32)
    gv = gla_ref[r0:r0 + c, 256:512]
    z = jnp.dot(glr_ref[r0:r0 + c, :], gw, precision=HIGHEST, preferred_element_type=F32) + gb
    yield
    la = _log_sigmoid(z) * (1.0 / GLA_TAU)
    if rev:
        first = (ri // GLA_SUB) * GLA_SUB + (GLA_SUB - 1)
        ref_sel = ci >= first
    else:
        first = (ri // GLA_SUB) * GLA_SUB
        ref_sel = ci <= first
    sel = jnp.concatenate([jnp.where(attends, 1.0, 0.0), jnp.where(ref_sel, 1.0, 0.0)], axis=0).astype(BF16)
    la_hi = la.astype(BF16)
    la_lo = (la - la_hi.astype(F32)).astype(BF16)
    sums = (jnp.dot(sel, la_hi, preferred_element_type=F32) + jnp.dot(sel, la_lo, preferred_element_type=F32))
    b, refrow = sums[0:c], sums[c:2 * c]
    yield
    qs = gq * jnp.exp(b - refrow)
    b_last = b[0:1] if rev else b[c - 1:c]
    kz = (gk * jnp.exp(b_last - b)).astype(BF16)
    u = lax.dot_general(kz, gv, TN_DIMS, preferred_element_type=F32)
    eye = lax.broadcasted_iota(I32, (LANES, LANES), 0) == lax.broadcasted_iota(I32, (LANES, LANES), 1)
    g_col = jnp.sum(jnp.where(eye, jnp.exp(b_last), 0.0), axis=1, keepdims=True)
    jcol = lax.broadcasted_iota(I32, (c, 1), 0)
    rr = lax.broadcasted_iota(I32, (N_HEADS * GLA_SUB, c), 0) % GLA_SUB
    cc = lax.broadcasted_iota(I32, (N_HEADS * GLA_SUB, c), 1)
    pieces = []
    for blk in range(c // GLA_SUB):
        lo = blk * GLA_SUB
        ref_b = refrow[lo:lo + 1]
        seen = (jcol >= lo) if rev else (jcol < lo + GLA_SUB)
        ks = (gk * jnp.exp(jnp.where(seen, ref_b - b, -jnp.inf))).astype(BF16)
        qz = _head_stack(qs[lo:lo + GLA_SUB], GLA_DK).astype(BF16)
        att = lax.dot_general(qz, ks, NT_DIMS, preferred_element_type=F32)
        ok = (cc >= rr + lo) if rev else (cc <= rr + lo)
        att = jnp.where(ok, att, 0.0).astype(BF16)
        pieces.append(_head_select(jnp.dot(att, gv, preferred_element_type=F32), GLA_SUB))
        if blk % 2:
            yield
    out.update(intra=jnp.concatenate(pieces, axis=0), qx=(gq * jnp.exp(b)).astype(BF16),
               u=jnp.where(bd, u, 0.0), g=g_col)


def _round_robin(chains):
    while chains:
        chains = [ch for ch in chains if next(ch, True) is None]


def _scan_kernel(ret_f, gla_f, glr_f, ret_b, gla_b, glr_b, dmat_ref, xi_ref, zeta_ref, gchunk_ref, gw_ref, gb_ref,
                 orf_ref, ogf_ref, orb_ref, ogb_ref, sr_ref, sg_ref):
    @pl.when(pl.program_id(1) == 0)
    def _():
        sr_ref[...] = jnp.zeros_like(sr_ref)
        sg_ref[...] = jnp.zeros_like(sg_ref)

    chains, state_chains = [], []
    for d, (ret_ref, gla_ref, glr_ref, ore_ref, ogl_ref) in enumerate(
            ((ret_f, gla_f, glr_f, orf_ref, ogf_ref), (ret_b, gla_b, glr_b, orb_ref, ogb_ref))):
        order = range(SCAN_CHUNKS - 1, -1, -1) if d else range(SCAN_CHUNKS)
        gla_parts, ret_parts = [], []
        for j in order:
            r0 = j * CHUNK
            gla_parts.append((r0, {}))
            ret_parts.append((r0, {}))
            chains.append(_gla_chain(bool(d), gla_ref, glr_ref, r0, gw_ref[d], gb_ref[d], gla_parts[-1][1]))
            chains.append(_ret_chain(ret_ref, r0, dmat_ref[d], xi_ref[d], zeta_ref[d], gchunk_ref[d],
                                     ret_parts[-1][1]))
        state_chains.append(_state_chain(gla_parts, ogl_ref, sg_ref.at[d]))
        state_chains.append(_state_chain(ret_parts, ore_ref, sr_ref.at[d]))
    _round_robin(chains)
    _round_robin(state_chains)


def _scan(ret, gla, glr, tables, gla_w, gla_b, n_batch, seq, ctx_len):
    n = ret.shape[0]
    c = CHUNK * SCAN_CHUNKS
    nc_ctx, nc_lat = ctx_len // c, seq // c
    n_steps = nc_ctx + nc_lat
    ctx_base = n_batch * nc_lat

    def fwd(b, s):
        return (jnp.where(s < nc_ctx, ctx_base + b * nc_ctx + s, b * nc_lat + (s - nc_ctx)), 0)

    def bwd(b, s):
        return (jnp.where(s < nc_ctx, ctx_base + b * nc_ctx + (nc_ctx - 1 - s), b * nc_lat + (n_steps - 1 - s)), 0)

    const3 = lambda b, s: (0, 0, 0)
    gw = jnp.zeros((2, LANES, LANES), F32).at[:, :GLA_LOWRANK].set(gla_w)
    gb = gla_b[:, None, :]
    chain_in = lambda m: [pl.BlockSpec((c, 768), m), pl.BlockSpec((c, 768), m), pl.BlockSpec((c, LANES), m)]
    o_sds = jax.ShapeDtypeStruct((n, 256), F32)
    return pl.pallas_call(
        _scan_kernel,
        out_shape=(o_sds, o_sds, o_sds, o_sds),
        grid=(n_batch, n_steps),
        in_specs=chain_in(fwd) + chain_in(bwd) + [pl.BlockSpec(t.shape, const3) for t in tables]
        + [pl.BlockSpec(gw.shape, const3), pl.BlockSpec(gb.shape, const3)],
        out_specs=(pl.BlockSpec((c, 256), fwd), pl.BlockSpec((c, 256), fwd),
                   pl.BlockSpec((c, 256), bwd), pl.BlockSpec((c, 256), bwd)),
        scratch_shapes=[pltpu.VMEM((2, LANES, 2 * LANES), F32), pltpu.VMEM((2, LANES, 2 * LANES), F32)],
        compiler_params=_cparams("arbitrary", "arbitrary"),
        name="scan",
    )(ret, gla, glr, ret, gla, glr, *tables, gw, gb)


def _attn_kernel(n_lat_blocks, lam_init, *refs):
    if n_lat_blocks:
        q_ref, kc_ref, vtc_ref, kl_ref, vtl_ref, lp_ref, g_ref, o_ref, m_ref, acc_ref, st_ref = refs
    else:
        q_ref, kc_ref, vtc_ref, lp_ref, g_ref, o_ref, m_ref, acc_ref, st_ref = refs
    tq = q_ref.shape[0]
    q = q_ref[...].astype(F32)
    lane = lax.broadcasted_iota(I32, (1, LANES), 1)
    qt = jnp.concatenate([jnp.where(lane < DIFF_DH, q, 0.0).T, jnp.where(lane >= DIFF_DH, q, 0.0).T],
                         axis=1).astype(BF16)
    m_ref[...] = jnp.full(m_ref.shape, -jnp.inf, F32)
    acc_ref[...] = jnp.zeros_like(acc_ref)

    def scores(slot, kb):
        st_ref[slot, 0:kb.shape[0], :] = jnp.dot(kb, qt, preferred_element_type=F32)

    def absorb(slot, vtb):
        st = st_ref[slot, 0:vtb.shape[1], :]
        m_prev = m_ref[...]
        m_new = jnp.maximum(m_prev, jnp.max(st, axis=0, keepdims=True))
        alpha = jnp.exp(m_prev - m_new)
        p = jnp.exp(st - m_new).astype(BF16)
        acc_ref[...] = alpha * acc_ref[...] + jnp.dot(vtb, p, preferred_element_type=F32)
        m_ref[...] = m_new

    def k_lat(blk):
        return kl_ref[pl.ds(pl.multiple_of(blk * ATTN_TK, ATTN_TK), ATTN_TK), :]

    def vt_lat(blk):
        return vtl_ref[:, pl.ds(pl.multiple_of(blk * ATTN_TK, ATTN_TK), ATTN_TK)]

    scores(0, kc_ref[...])
    if not n_lat_blocks:
        absorb(0, vtc_ref[...])
    else:
        scores(1, k_lat(0))
        absorb(0, vtc_ref[...])
        n_pairs = (n_lat_blocks - 1) // 2

        def body(i, carry):
            scores(0, k_lat(2 * i + 1))
            absorb(1, vt_lat(2 * i))
            scores(1, k_lat(2 * i + 2))
            absorb(0, vt_lat(2 * i + 1))
            return carry
        lax.fori_loop(0, n_pairs, body, 0)
        done = 2 * n_pairs
        if n_lat_blocks - done == 2:
            scores(0, k_lat(done + 1))
            absorb(1, vt_lat(done))
            absorb(0, vt_lat(done + 1))
        else:
            absorb(1, vt_lat(done))

    lp = lp_ref[...]
    lam = (jnp.exp(jnp.sum(lp[0:1] * lp[1:2], axis=1, keepdims=True))
           - jnp.exp(jnp.sum(lp[2:3] * lp[3:4], axis=1, keepdims=True)) + lam_init)
    acc = acc_ref[...]
    o1 = acc[0:LANES, :tq] / acc[LANES:LANES + 1, :tq]
    o2 = acc[0:LANES, tq:] / acc[LANES:LANES + 1, tq:]
    ot = o1 - lam * o2
    ot = ot * lax.rsqrt(jnp.mean(ot * ot, axis=0, keepdims=True) + EPS) * g_ref[...][:, 0:1] * (1.0 - lam_init)
    o_ref[...] = ot.T.astype(o_ref.dtype)


def _diff_attention(dq, dk, vt, lp, g, lam_init, n_batch, seq, ctx_len, latent):
    ctx_blk0 = (n_batch * seq) // ctx_len
    vrows = LANES + ATTN_ONES
    kc_spec = pl.BlockSpec((ctx_len, LANES), lambda b, h, i: (ctx_blk0 + b, h))
    vtc_spec = pl.BlockSpec((vrows, ctx_len), lambda b, h, i: (h, ctx_blk0 + b))
    const = lambda b, h, i: (0, 0)
    if latent:
        tq = ATTN_TQ
        n_q = seq // tq
        q_map = lambda b, h, i: (b * n_q + i, h)
        in_specs = [pl.BlockSpec((tq, LANES), q_map), kc_spec, vtc_spec,
                    pl.BlockSpec((seq, LANES), lambda b, h, i: (b, h)),
                    pl.BlockSpec((vrows, seq), lambda b, h, i: (h, b))]
        args = (dq, dk, vt, dk, vt)
        n_rows, o_map, n_lat_blocks = n_batch * seq, q_map, seq // ATTN_TK
    else:
        tq, n_q = ctx_len, 1
        in_specs = [kc_spec, kc_spec, vtc_spec]
        args = (dq, dk, vt)
        n_rows, o_map, n_lat_blocks = n_batch * ctx_len, (lambda b, h, i: (b, h)), 0
    in_specs += [pl.BlockSpec(lp.shape, const), pl.BlockSpec((LANES, LANES), const)]
    return pl.pallas_call(
        functools.partial(_attn_kernel, n_lat_blocks, lam_init),
        out_shape=jax.ShapeDtypeStruct((n_rows, N_HEADS * LANES), BF16),
        grid=(n_batch, N_HEADS, n_q),
        in_specs=in_specs,
        out_specs=pl.BlockSpec((tq, LANES), o_map),
        scratch_shapes=[pltpu.VMEM((1, 2 * tq), F32), pltpu.VMEM((vrows, 2 * tq), F32),
                        pltpu.VMEM((2, max(ATTN_TK, ctx_len), 2 * tq), F32)],
        compiler_params=_cparams("parallel", "parallel", "parallel"),
        name="diff_attention_lat" if latent else "diff_attention_ctx",
    )(*args, lp, g)


def _merge_kernel(tiles_per_batch, n_batch, n_a, orf_ref, orb_ref, ogf_ref, ogb_ref, rg_ref, gr_ref, *refs):
    n_x = 2 if n_a else 1
    mod_ref, gpost_ref, gpre_ref, glag_ref, wout_ref, wrt_ref, xo_ref, hp_ref, sc_ref = refs[2 * n_x:]
    dif = _pick_rows(n_a, *refs[:2]) if n_a else refs[0][...]
    x = _pick_rows(n_a, *refs[2:4]) if n_a else refs[1][...]
    d = x.shape[-1]
    r = jnp.minimum(pl.program_id(0) // tiles_per_batch, n_batch)
    gate1 = mod_ref[pl.ds(r, 1), 2 * d:3 * d]
    shift2 = mod_ref[pl.ds(r, 1), 3 * d:4 * d]
    scale2 = mod_ref[pl.ds(r, 1), 4 * d:5 * d]
    gi = lax.broadcasted_iota(I32, (256, 256), 0) // RET_DV
    gj = lax.broadcasted_iota(I32, (256, 256), 1) // RET_DV
    group_mean = jnp.where(gi == gj, 1.0 / RET_DV, 0.0).astype(BF16)

    def head_norm(o):
        ms = jnp.dot((o * o).astype(BF16), group_mean, preferred_element_type=F32)
        return o * lax.rsqrt(ms + EPS)

    ret = head_norm(orf_ref[...] + orb_ref[...]) * _silu(rg_ref[...].astype(F32))
    gla = head_norm(ogf_ref[...] + ogb_ref[...]) * glag_ref[...] * _silu(gr_ref[...].astype(F32))
    m = (jnp.dot(ret.astype(BF16), wout_ref[0:256, :], preferred_element_type=F32)
         + jnp.dot(gla.astype(BF16), wout_ref[256:512, :], preferred_element_type=F32)
         + jnp.dot(dif, wout_ref[512:1024, :], preferred_element_type=F32))
    x_new = x + gate1 * (_rms(m) * gpost_ref[...])
    xo_ref[...] = x_new
    h2 = _rms(x_new) * gpre_ref[...] * (1.0 + scale2) + shift2
    hp_ref[...] = _pack_halves(h2[:, :d // 2], h2[:, d // 2:])
    def split(a):
        hi = a.astype(BF16)
        return hi, (a - hi.astype(F32)).astype(BF16)

    h_hi, h_lo = split(h2)
    w_hi, w_lo = split(wrt_ref[...])
    nt = functools.partial(lax.dot_general, dimension_numbers=NT_DIMS, preferred_element_type=F32)
    logits = nt(w_hi, h_hi) + (nt(w_hi, h_lo) + nt(w_lo, h_hi))
    sc_ref[...] = 1.0 / (1.0 + jnp.exp(-logits))


def _merge(o_rf, o_rb, o_gf, o_gb, ret, gla, dif, xs, mod, g_post, g_pre_ffn, gla_g, w_out, w_rt,
           n_rows, n_batch, seq):
    tm = TOKEN_TILE
    row = lambda i: (i, 0)
    gate_col = lambda i: (i, 2)
    const = lambda i: (0, 0)
    if isinstance(xs, tuple):
        d = xs[0].shape[1]
        n_a, x_specs = _split_rows(*xs, tm, d)
        _, dif_specs = _split_rows(*dif, tm, 512)
    else:
        d = xs.shape[1]
        n_a, x_specs, dif_specs = 0, [pl.BlockSpec((tm, d), row)], [pl.BlockSpec((tm, 512), row)]
        xs, dif = (xs,), (dif,)
    return pl.pallas_call(
        functools.partial(_merge_kernel, seq // tm, n_batch, n_a),
        out_shape=(jax.ShapeDtypeStruct((n_rows, d), F32), jax.ShapeDtypeStruct((n_rows, d // 2), U32),
                   jax.ShapeDtypeStruct((N_EXPERTS, n_rows), F32)),
        grid=(n_rows // tm,),
        in_specs=[
            pl.BlockSpec((tm, 256), row), pl.BlockSpec((tm, 256), row),
            pl.BlockSpec((tm, 256), row), pl.BlockSpec((tm, 256), row),
            pl.BlockSpec((tm, 256), gate_col), pl.BlockSpec((tm, 256), gate_col), *dif_specs, *x_specs,
            pl.BlockSpec(mod.shape, const), pl.BlockSpec((1, d), const), pl.BlockSpec((1, d), const),
            pl.BlockSpec((1, 256), const), pl.BlockSpec(w_out.shape, const), pl.BlockSpec(w_rt.shape, const),
        ],
        out_specs=(pl.BlockSpec((tm, d), row), pl.BlockSpec((tm, d // 2), row),
                   pl.BlockSpec((N_EXPERTS, tm), lambda i: (0, i))),
        compiler_params=_cparams("parallel"),
        name="merge",
    )(o_rf, o_rb, o_gf, o_gb, ret, gla, *dif, *xs, mod, g_post, g_pre_ffn, gla_g, w_out, w_rt)


def _route_kernel(sc_ref, bias_ref, eidx_ref, wts_ref, pos_ref, meta_ref):
    tt = sc_ref.shape[-1]
    scores = sc_ref[...]
    sel = scores + bias_ref[...][:, 0:1]
    sub = lax.broadcasted_iota(I32, (GROUP_SIZE, tt), 0)
    neg = -jnp.inf
    gscore = []
    for g in range(N_GROUPS):
        xg = sel[g * GROUP_SIZE:(g + 1) * GROUP_SIZE]
        m1 = jnp.max(xg, axis=0, keepdims=True)
        i1 = jnp.min(jnp.where(xg == m1, sub, GROUP_SIZE), axis=0, keepdims=True)
        m2 = jnp.max(jnp.where(sub == i1, neg, xg), axis=0, keepdims=True)
        gscore.append(m1 + m2)
    rows = []
    for g in range(N_GROUPS):
        rank = jnp.zeros((1, tt), I32)
        for o in range(N_GROUPS):
            if o == g:
                continue
            ahead = (gscore[o] >= gscore[g]) if o < g else (gscore[o] > gscore[g])
            rank = rank + ahead.astype(I32)
        rows.append(jnp.where(rank < TOPK_GROUPS, sel[g * GROUP_SIZE:(g + 1) * GROUP_SIZE], neg))
    masked = jnp.concatenate(rows, axis=0)
    eio = lax.broadcasted_iota(I32, (N_EXPERTS, tt), 0)
    member = jnp.zeros((N_EXPERTS, tt), F32)
    idxs, ws = [], []
    for _ in range(TOP_K):
        m = jnp.max(masked, axis=0, keepdims=True)
        i = jnp.min(jnp.where(masked == m, eio, N_EXPERTS), axis=0, keepdims=True)
        hit = eio == i
        idxs.append(i)
        ws.append(jnp.sum(jnp.where(hit, scores, 0.0), axis=0, keepdims=True))
        member = jnp.where(hit, 1.0, member)
        masked = jnp.where(hit, neg, masked)
    wsum = ws[0]
    for w in ws[1:]:
        wsum = wsum + w
    ti = lax.broadcasted_iota(I32, (tt, tt), 0)
    tj = lax.broadcasted_iota(I32, (tt, tt), 1)
    before = jnp.where(ti < tj, 1.0, 0.0).astype(BF16)
    rank_in_e = jnp.dot(member.astype(BF16), before, preferred_element_type=F32)
    cnt = jnp.sum(member, axis=1, keepdims=True)
    padded = jnp.floor((cnt + (SUBLANES - 1)) * (1.0 / SUBLANES)) * SUBLANES
    ei = lax.broadcasted_iota(I32, (N_EXPERTS, N_EXPERTS), 0)
    ej = lax.broadcasted_iota(I32, (N_EXPERTS, N_EXPERTS), 1)
    lower = jnp.where(ej < ei, 1.0, 0.0)
    off = jnp.dot(lower, jnp.broadcast_to(padded, (N_EXPERTS, LANES)), precision=HIGHEST,
                  preferred_element_type=F32)
    slot = rank_in_e + off[:, 0:1]
    zrow_i = jnp.zeros((SUBLANES - TOP_K, tt), I32)
    zrow_f = jnp.zeros((SUBLANES - TOP_K, tt), F32)
    pos = [jnp.sum(jnp.where(eio == i, slot, 0.0), axis=0, keepdims=True).astype(I32) for i in idxs]
    eidx_ref[...] = jnp.concatenate(idxs + [zrow_i], axis=0)
    wts_ref[...] = jnp.concatenate([w / wsum * ROUTED_SCALE for w in ws] + [zrow_f], axis=0)
    pos_ref[...] = jnp.concatenate(pos + [zrow_i], axis=0)
    meta_ref[0] = jnp.concatenate([jnp.broadcast_to(cnt, (N_EXPERTS, LANES)), off], axis=1).astype(I32)


def _route(scores_t, bias, tt, row0, n):
    tile0 = row0 // tt
    tok = lambda i: (0, i)
    return pl.pallas_call(
        _route_kernel,
        out_shape=(jax.ShapeDtypeStruct((SUBLANES, n), I32), jax.ShapeDtypeStruct((SUBLANES, n), F32),
                   jax.ShapeDtypeStruct((SUBLANES, n), I32),
                   jax.ShapeDtypeStruct((n // tt, N_EXPERTS, 2 * LANES), I32)),
        grid=(n // tt,),
        in_specs=[pl.BlockSpec((N_EXPERTS, tt), lambda i: (0, i + tile0)),
                  pl.BlockSpec((N_EXPERTS, LANES), lambda i: (0, 0))],
        out_specs=(pl.BlockSpec((SUBLANES, tt), tok), pl.BlockSpec((SUBLANES, tt), tok),
                   pl.BlockSpec((SUBLANES, tt), tok),
                   pl.BlockSpec((1, N_EXPERTS, 2 * LANES), lambda i: (i, 0, 0))),
        compiler_params=_cparams("parallel"),
        name="route",
    )(scores_t, jnp.broadcast_to(bias[:, None], (N_EXPERTS, LANES)))


def _moe_kernel(group, n_first, has_prev, pos_ref, wts_ref, meta_ref, hp_ref, weg_ref, weu_ref, wed_ref, *refs):
    o_ref, xg_ref, y_ref, cg_ref, rt_ref = refs[1:] if has_prev else refs
    tt, half = hp_ref.shape
    d = 2 * half
    eg = pl.program_id(1)

    @pl.when((pl.program_id(0) == 0) & (eg == 0))
    def _():
        def zero(i, carry):
            for u in range(SUBLANES):
                rt_ref[i * SUBLANES + u] = 0
            return carry
        lax.fori_loop(0, rt_ref.shape[0] // SUBLANES, zero, 0)

    @pl.when(eg == 0)
    def _():
        def scatter(t, carry):
            for k in range(TOP_K):
                rt_ref[pos_ref[t * SUBLANES + k]] = t
            return carry
        lax.fori_loop(0, tt, scatter, 0)

    sub = lax.broadcasted_iota(I32, (SUBLANES, half), 0)

    def gather8(src_ref, row_of):
        buf = jnp.zeros((SUBLANES, half), U32)
        for u in range(SUBLANES):
            buf = jnp.where(sub == u, src_ref[pl.ds(row_of(u), 1), :], buf)
        return buf

    row = lax.broadcasted_iota(I32, (MOE_BLOCK, half), 0)

    def expert_block(ge, buf, base, n_valid):
        base = pl.multiple_of(base, SUBLANES)
        for i in range(MOE_BLOCK):
            xg_ref[buf, i:i + 1, :] = hp_ref[pl.ds(rt_ref[base + i], 1), :]
        yield
        xa, xb = _unpack_halves(xg_ref[buf])
        xa, xb = xa.astype(BF16), xb.astype(BF16)
        hg = (jnp.dot(xa, weg_ref[ge, 0:half, :], preferred_element_type=F32)
              + jnp.dot(xb, weg_ref[ge, half:d, :], preferred_element_type=F32))
        hu = (jnp.dot(xa, weu_ref[ge, 0:half, :], preferred_element_type=F32)
              + jnp.dot(xb, weu_ref[ge, half:d, :], preferred_element_type=F32))
        yield
        y = jnp.dot((_silu(hg) * hu).astype(BF16), wed_ref[ge], preferred_element_type=F32)
        yield
        pltpu.store(y_ref.at[pl.ds(base, MOE_BLOCK), :], _pack_halves(y[:, :half], y[:, half:]),
                    mask=row < n_valid)

    cnts = [meta_ref[0, eg * group + ge] for ge in range(group)]
    offs = [meta_ref[1, eg * group + ge] for ge in range(group)]
    chains = [expert_block(ge, ge * n_first + j, offs[ge] + j * MOE_BLOCK, cnts[ge] - j * MOE_BLOCK)
              for ge in range(group) for j in range(n_first)]
    _round_robin(chains)
    for ge in range(group):
        def more(j, carry, ge=ge):
            for _ in expert_block(ge, ge * n_first, offs[ge] + j * MOE_BLOCK, cnts[ge] - j * MOE_BLOCK):
                pass
            return carry
        lax.fori_loop(n_first, (cnts[ge] + (MOE_BLOCK - 1)) // MOE_BLOCK, more, 0)

    @pl.when(eg == N_EXPERTS // group - 1)
    def _():
        def tokens(g, carry):
            t0 = pl.multiple_of(g * SUBLANES, SUBLANES)
            wg = wts_ref[pl.ds(t0, SUBLANES), :]
            acc_a = jnp.zeros((SUBLANES, half), F32)
            acc_b = jnp.zeros((SUBLANES, half), F32)
            for k in range(TOP_K):
                slot_of = lambda u, k=k: pos_ref[(t0 + u) * SUBLANES + k]
                if k < COMBINE_SELECTS:
                    rows = gather8(y_ref, slot_of)
                else:
                    for u in range(SUBLANES):
                        cg_ref[k - COMBINE_SELECTS, u:u + 1, :] = y_ref[pl.ds(slot_of(u), 1), :]
                    rows = cg_ref[k - COMBINE_SELECTS]
                ya, yb = _unpack_halves(rows)
                acc_a = acc_a + wg[:, k:k + 1] * ya
                acc_b = acc_b + wg[:, k:k + 1] * yb
            o_ref[pl.ds(t0, SUBLANES), 0:half] = acc_a
            o_ref[pl.ds(t0, SUBLANES), half:d] = acc_b
            return carry

        lax.fori_loop(0, tt // SUBLANES, tokens, 0)


def _moe_segment(prev, scores_t, hp, router_bias, expert_w, layer, row0, n, tt):
    n_all, half = hp.shape
    d = 2 * half
    w_eg, w_eu, w_ed = expert_w
    de = w_eg.shape[-1]
    n_first = max(1, -(-tt * TOP_K // (N_EXPERTS * MOE_BLOCK)))
    group = max(1, MOE_CHAINS // n_first)
    group0 = layer * (N_EXPERTS // group)
    n_tiles, tile0 = n // tt, row0 // tt
    n_slots = -(-(tt * TOP_K + N_EXPERTS * (SUBLANES - 1) + n_first * MOE_BLOCK) // (SUBLANES * LANES)) * SUBLANES * LANES
    e_idx, wts, pos, meta = _route(scores_t, router_bias, tt, row0, n)
    cnt, off = meta[:, :, 0], meta[:, :, LANES]

    def per_token(a):
        return a.T.reshape(-1)

    meta_s = jnp.zeros((n_tiles, SUBLANES, LANES), I32)
    meta_s = meta_s.at[:, 0, :N_EXPERTS].set(cnt).at[:, 1, :N_EXPERTS].set(off).reshape(-1, LANES)

    tile = lambda i, e: (i + tile0, 0)
    expert = lambda i, e: (group0 + e, 0, 0)
    smem = functools.partial(pl.BlockSpec, memory_space=pltpu.SMEM)
    once = pl.Buffered(1)
    in_specs = [
        smem((SUBLANES * tt,), lambda i, e: (i,)), pl.BlockSpec((tt, SUBLANES), lambda i, e: (i, 0)),
        smem((SUBLANES, LANES), lambda i, e: (i, 0)),
        pl.BlockSpec((tt, half), tile, pipeline_mode=once),
        pl.BlockSpec((group, d, de), expert), pl.BlockSpec((group, d, de), expert),
        pl.BlockSpec((group, de, d), expert),
    ]
    args = [per_token(pos), wts.T, meta_s, hp, w_eg, w_eu, w_ed]
    aliases = {}
    if prev is not None:
        in_specs.append(pl.BlockSpec(memory_space=pl.ANY))
        args.append(prev)
        aliases = {len(args) - 1: 0}
    return pl.pallas_call(
        functools.partial(_moe_kernel, group, n_first, prev is not None),
        out_shape=jax.ShapeDtypeStruct((n_all, d), F32),
        grid=(n_tiles, N_EXPERTS // group),
        in_specs=in_specs,
        out_specs=pl.BlockSpec((tt, d), tile, pipeline_mode=once),
        scratch_shapes=[pltpu.VMEM((group * n_first, MOE_BLOCK, half), U32), pltpu.VMEM((n_slots, half), U32),
                        pltpu.VMEM((TOP_K - COMBINE_SELECTS, SUBLANES, half), U32),
                        pltpu.SMEM((n_slots,), I32)],
        input_output_aliases=aliases,
        compiler_params=_cparams("arbitrary", "arbitrary"),
        name="moe",
    )(*args)


def _moe_final_kernel(tiles_per_batch, n_batch, routed_ref, hp_ref, x_ref, mod_ref, gpost_ref,
                      wsg_ref, wsu_ref, wsd_ref, o_ref):
    half = hp_ref.shape[1]
    d = 2 * half
    r = jnp.minimum(pl.program_id(0) // tiles_per_batch, n_batch)
    gate2 = mod_ref[pl.ds(r, 1), 5 * d:6 * d]
    xa, xb = _unpack_halves(hp_ref[...])
    xa, xb = xa.astype(BF16), xb.astype(BF16)
    hg = (jnp.dot(xa, wsg_ref[0:half, :], preferred_element_type=F32)
          + jnp.dot(xb, wsg_ref[half:d, :], preferred_element_type=F32))
    hu = (jnp.dot(xa, wsu_ref[0:half, :], preferred_element_type=F32)
          + jnp.dot(xb, wsu_ref[half:d, :], preferred_element_type=F32))
    f = jnp.dot((_silu(hg) * hu).astype(BF16), wsd_ref[...], preferred_element_type=F32) + routed_ref[...]
    o_ref[...] = x_ref[...] + gate2 * (_rms(f) * gpost_ref[...])


def _moe_final(routed, hp, xs, mod, g_post, w_sg, w_su, w_sd, n_batch, seq):
    n, d = routed.shape
    tm = TOKEN_TILE
    row = lambda i: (i, 0)
    const = lambda i: (0, 0)
    return pl.pallas_call(
        functools.partial(_moe_final_kernel, seq // tm, n_batch),
        out_shape=jax.ShapeDtypeStruct((n, d), F32),
        grid=(n // tm,),
        in_specs=[pl.BlockSpec((tm, d), row), pl.BlockSpec((tm, d // 2), row), pl.BlockSpec((tm, d), row),
                  pl.BlockSpec(mod.shape, const), pl.BlockSpec((1, d), const),
                  pl.BlockSpec(w_sg.shape, const), pl.BlockSpec(w_su.shape, const), pl.BlockSpec(w_sd.shape, const)],
        out_specs=pl.BlockSpec((tm, d), row),
        compiler_params=_cparams("parallel"),
        name="moe_final",
    )(routed, hp, xs, mod, g_post, w_sg, w_su, w_sd)


def _moe_layer(scores_t, hp, xs, mod, router_bias, g_post, expert_w, shared_w, layer, n_lat, n_batch, seq):
    pick = lambda rows: next(t for t in MOE_TILES if rows % t == 0)
    routed = _moe_segment(None, scores_t, hp, router_bias, expert_w, layer, 0, n_lat, pick(n_lat))
    n_ctx = hp.shape[0] - n_lat
    if n_ctx:
        tt = next(t for t in MOE_TILES if n_ctx % t == 0 and n_lat % t == 0)
        routed = _moe_segment(routed, scores_t, hp, router_bias, expert_w, layer, n_lat, n_ctx, tt)
    return _moe_final(routed, hp, xs, mod, g_post, *shared_w, n_batch, seq)


def kernel(x, c, ctx, c_ctx, w_mod, b_mod, g_pre_mix, g_post_mix, g_pre_ffn, g_post_ffn, w_in, w_out,
           ret_decay_logit, gla_w_gate, gla_b_gate, gla_norm_g, diff_lambda, diff_norm_g,
           w_router, router_bias, w_exp_gate, w_exp_up, w_exp_down, w_sh_gate, w_sh_up, w_sh_down):
    n_batch, seq, d = x.shape
    ctx_len = ctx.shape[1]
    depth = w_mod.shape[0]
    n_lat = n_batch * seq
    assert seq % TOKEN_TILE == 0 and (n_batch * ctx_len) % TOKEN_TILE == 0 and n_batch < SUBLANES
    scan_rows = CHUNK * SCAN_CHUNKS
    assert seq % ATTN_TK == 0 and seq % scan_rows == 0 and ctx_len % scan_rows == 0 and n_lat % ctx_len == 0

    xs = (x.reshape(n_lat, d), ctx.reshape(n_batch * ctx_len, d))
    cond = jnp.zeros((SUBLANES, d), F32).at[:n_batch].set(c).at[n_batch].set(c_ctx)
    mods = _modulation(cond, w_mod, b_mod)
    tables = _rope_tables(seq, RET_DK, TOKEN_TILE) + _rope_tables(seq, DIFF_DH, TOKEN_TILE)
    lr0 = N_HEADS * (2 * RET_DK + 2 * RET_DV + 2 * GLA_DK + 2 * GLA_DV)
    row = lambda a: a[None, :]
    expert_w = tuple(w.astype(BF16).reshape((depth * N_EXPERTS,) + w.shape[2:])
                     for w in (w_exp_gate, w_exp_up, w_exp_down))

    for layer in range(depth):
        need_ctx = layer < depth - 1
        lam_init = 0.8 - 0.6 * math.exp(-0.3 * layer)
        mod = mods[layer]
        wl = w_in[layer]
        w_r = jnp.concatenate([wl[:, :lr0], wl[:, lr0:lr0 + GLA_LOWRANK],
                               jnp.zeros((d, LANES - GLA_LOWRANK), F32), wl[:, lr0 + GLA_LOWRANK:]],
                              axis=1).astype(BF16)
        ret, gla, glr, dq, dk, vt = _in_projection(xs, mod, row(g_pre_mix[layer]), w_r, tables, n_batch, seq)

        o_rf, o_gf, o_rb, o_gb = _scan(ret, gla, glr, _scan_tables(ret_decay_logit[layer]), gla_w_gate[layer],
                                       gla_b_gate[layer], n_batch, seq, ctx_len)

        lp = jnp.zeros((SUBLANES, LANES), F32).at[:4, :DIFF_DH].set(diff_lambda[layer])
        g_col = jnp.broadcast_to(diff_norm_g[layer][:, None], (LANES, LANES))
        attn = functools.partial(_diff_attention, dq, dk, vt, lp, g_col, lam_init, n_batch, seq, ctx_len)
        dif = attn(True)
        n_rows = n_lat
        if need_ctx:
            n_rows = n_lat + n_batch * ctx_len
            if isinstance(xs, tuple):
                dif = (dif, attn(False))
            else:
                dif = jnp.concatenate([dif, attn(False)], axis=0)
        elif isinstance(xs, tuple):
            xs = xs[0]

        xs, hp, scores_t = _merge(o_rf, o_rb, o_gf, o_gb, ret, gla, dif, xs, mod, row(g_post_mix[layer]),
                                  row(g_pre_ffn[layer]), row(jnp.tile(gla_norm_g[layer], N_HEADS)),
                                  w_out[layer].astype(BF16), w_router[layer].T, n_rows, n_batch, seq)
        shared_w = tuple(w[layer].astype(BF16) for w in (w_sh_gate, w_sh_up, w_sh_down))
        xs = _moe_layer(scores_t, hp, xs, mod, router_bias[layer], row(g_post_ffn[layer]), expert_w, shared_w,
                        layer, n_lat, n_batch, seq)
    return xs[:n_lat].reshape(n_batch, seq, d)
```

```python
import functools
import math

import jax
import jax.numpy as jnp
import numpy as np
from jax import lax
from jax.experimental import pallas as pl
from jax.experimental.pallas import tpu as pltpu

F32 = jnp.float32
BF16 = jnp.bfloat16
I32 = jnp.int32
U32 = jnp.uint32

GRID_W = 64
CHUNK = 128
N_HEADS = 4
RET_DK, RET_DV = 32, 64
GLA_DK, GLA_DV = 32, 64
GLA_LOWRANK = 16
GLA_TAU = 16.0
DIFF_DH = 64
ROPE_BASE = 10000.0
N_EXPERTS = 64
TOP_K = 6
N_GROUPS = 8
TOPK_GROUPS = 4
GROUP_SIZE = N_EXPERTS // N_GROUPS
ROUTED_SCALE = 2.5
EPS = 1e-6
GLA_SUB = 16
SCAN_CHUNKS = 2

LANES = 128
SUBLANES = 8
VMEM_LIMIT_BYTES = 56 * 1024 * 1024

TOKEN_TILE = 512
ATTN_TQ = 512
ATTN_TK = 512
ATTN_ONES = 16
MOE_TILES = (2048, 1024, 512)
MOE_BLOCK = 128
MOE_CHAINS = 4
HP_CHUNKS = 4

HIGHEST = lax.Precision.HIGHEST
NT_DIMS = (((1,), (1,)), ((), ()))
TN_DIMS = (((0,), (0,)), ((), ()))


def _cparams(*sem):
    return pltpu.CompilerParams(dimension_semantics=sem, vmem_limit_bytes=VMEM_LIMIT_BYTES)


def _log_sigmoid(x):
    return jnp.minimum(x, 0.0) - jnp.log(1.0 + jnp.exp(-jnp.abs(x)))


def _silu(x):
    return x * (1.0 / (1.0 + jnp.exp(-x)))


def _rms(x):
    return x * lax.rsqrt(jnp.mean(x * x, axis=-1, keepdims=True) + EPS)


def _pack_halves(a, b):
    ua = lax.bitcast_convert_type(a.astype(BF16).astype(F32), U32)
    ub = lax.bitcast_convert_type(b.astype(BF16).astype(F32), U32)
    return (ua & jnp.uint32(0xFFFF0000)) | (ub >> 16)


def _unpack_halves(w):
    a = lax.bitcast_convert_type(w & jnp.uint32(0xFFFF0000), F32)
    b = lax.bitcast_convert_type(w << 16, F32)
    return a, b


def _tile_major_store(ref, v, lead=()):
    k = v.shape[1] // LANES
    for g in range(v.shape[0] // SUBLANES):
        for c in range(k):
            blk = g * k + c
            ref[lead + (slice(blk * SUBLANES, (blk + 1) * SUBLANES), slice(None))] = (
                v[g * SUBLANES:(g + 1) * SUBLANES, c * LANES:(c + 1) * LANES])


def _tile_major_columns(ref, rows, k, lead=()):
    return [jnp.concatenate([ref[lead + (slice((g * k + c) * SUBLANES, (g * k + c + 1) * SUBLANES), slice(None))]
                             for g in range(rows // SUBLANES)], axis=0) for c in range(k)]


def _tile_major_addr(t, k):
    return (t >> 3) * (SUBLANES * k) + (t & (SUBLANES - 1))


def _mod_kernel(cond_ref, w_ref, b_ref, o_ref):
    a = _silu(cond_ref[...])
    o_ref[0] = jnp.dot(a, w_ref[0], precision=HIGHEST, preferred_element_type=F32) + b_ref[0]


def _modulation(cond, w_mod, b_mod):
    n_layers, d, d6 = w_mod.shape
    tn = 1024
    return pl.pallas_call(
        _mod_kernel,
        out_shape=jax.ShapeDtypeStruct((n_layers, SUBLANES, d6), F32),
        grid=(n_layers, d6 // tn),
        in_specs=[
            pl.BlockSpec((SUBLANES, d), lambda l, j: (0, 0)),
            pl.BlockSpec((1, d, tn), lambda l, j: (l, 0, j)),
            pl.BlockSpec((1, 1, tn), lambda l, j: (l, 0, j)),
        ],
        out_specs=pl.BlockSpec((1, SUBLANES, tn), lambda l, j: (l, 0, j)),
        compiler_params=_cparams("parallel", "parallel"),
        name="modulation",
    )(cond, w_mod, b_mod.reshape(n_layers, 1, d6))


def _rope(x, cos, sin, quarter):
    lane = lax.broadcasted_iota(I32, (1, LANES), 1)
    first = (lane % (2 * quarter)) < quarter
    outs = []
    for c in range(x.shape[-1] // LANES):
        xc = x[:, c * LANES:(c + 1) * LANES]
        partner = jnp.where(first, pltpu.roll(xc, LANES - quarter, 1), pltpu.roll(xc, quarter, 1))
        outs.append(xc * cos + partner * sin)
    return outs[0] if len(outs) == 1 else jnp.concatenate(outs, axis=-1)


def _split_rows(a, b, tm, width):
    n_a = a.shape[0] // tm
    return n_a, [pl.BlockSpec((tm, width), lambda i: (jnp.minimum(i, n_a - 1), 0)),
                 pl.BlockSpec((tm, width), lambda i: (jnp.maximum(i - n_a, 0), 0))]


def _pick_rows(n_a, a_ref, b_ref):
    return jnp.where(pl.program_id(0) < n_a, a_ref[...], b_ref[...])


def _inproj_kernel(tiles_per_batch, n_batch, n_a, *refs):
    n_x = 2 if n_a else 1
    (mod_ref, g_ref, w_ref, c32_ref, s32_ref, c64_ref, s64_ref,
     ret_ref, gla_ref, glr_ref, dq_ref, dk_ref, vt_ref) = refs[n_x:]
    x = _pick_rows(n_a, *refs[:2]) if n_a else refs[0][...]
    d = x.shape[-1]
    r = jnp.minimum(pl.program_id(0) // tiles_per_batch, n_batch)
    shift = mod_ref[pl.ds(r, 1), 0:d]
    scale = mod_ref[pl.ds(r, 1), d:2 * d]
    h = (_rms(x) * g_ref[...] * (1.0 + scale) + shift).astype(BF16)

    def proj(lo, hi):
        return jnp.dot(h, w_ref[:, lo:hi], preferred_element_type=F32)

    c32, s32 = c32_ref[...], s32_ref[...]
    c64, s64 = c64_ref[...], s64_ref[...]
    ret = proj(0, 768)
    ret_ref[:, 0:128] = _rope(ret[:, 0:128], c32, s32, RET_DK // 4).astype(BF16)
    ret_ref[:, 128:256] = (_rope(ret[:, 128:256], c32, s32, RET_DK // 4) * RET_DK ** -0.5).astype(BF16)
    ret_ref[:, 256:768] = ret[:, 256:768].astype(BF16)
    gla = proj(768, 1536)
    gla_ref[:, 0:128] = (gla[:, 0:128] * GLA_DK ** -0.5).astype(BF16)
    gla_ref[:, 128:768] = gla[:, 128:768].astype(BF16)
    glr_ref[...] = proj(1536, 1664)
    dq_ref[...] = (_rope(proj(1664, 2176), c64, s64, DIFF_DH // 4) * DIFF_DH ** -0.5).astype(BF16)
    dk_ref[...] = _rope(proj(2176, 2688), c64, s64, DIFF_DH // 4).astype(BF16)
    dv = proj(2688, 3200)
    vrows = LANES + ATTN_ONES
    for hd in range(N_HEADS):
        vt_ref[hd * vrows:hd * vrows + LANES, :] = dv[:, hd * LANES:(hd + 1) * LANES].T.astype(BF16)
        vt_ref[hd * vrows + LANES:(hd + 1) * vrows, :] = jnp.ones((ATTN_ONES, dv.shape[0]), BF16)


def _in_projection(xs, mod, g_pre, w_r, tables, n_batch, seq):
    tm = TOKEN_TILE
    if isinstance(xs, tuple):
        d = xs[0].shape[1]
        n = xs[0].shape[0] + xs[1].shape[0]
        n_a, x_specs = _split_rows(*xs, tm, d)
    else:
        n, d = xs.shape
        n_a, x_specs, xs = 0, [pl.BlockSpec((tm, d), lambda i: (i, 0))], (xs,)
    tiles_per_batch = seq // tm
    n_lat_tiles = n_batch * tiles_per_batch
    c32, s32, c64, s64 = tables

    def tab_map(i):
        return (jnp.where(i < n_lat_tiles, i % tiles_per_batch, tiles_per_batch), 0)

    row = lambda i: (i, 0)
    const = lambda i: (0, 0)
    tab_spec = pl.BlockSpec((tm, LANES), tab_map)
    return pl.pallas_call(
        functools.partial(_inproj_kernel, tiles_per_batch, n_batch, n_a),
        out_shape=(
            jax.ShapeDtypeStruct((n, 768), BF16), jax.ShapeDtypeStruct((n, 768), BF16),
            jax.ShapeDtypeStruct((n, LANES), F32),
            jax.ShapeDtypeStruct((n, 512), BF16), jax.ShapeDtypeStruct((n, 512), BF16),
            jax.ShapeDtypeStruct((N_HEADS * (LANES + ATTN_ONES), n), BF16)),
        grid=(n // tm,),
        in_specs=x_specs + [
            pl.BlockSpec(mod.shape, const),
            pl.BlockSpec((1, d), const),
            pl.BlockSpec(w_r.shape, const),
            tab_spec, tab_spec, tab_spec, tab_spec,
        ],
        out_specs=(
            pl.BlockSpec((tm, 768), row), pl.BlockSpec((tm, 768), row), pl.BlockSpec((tm, LANES), row),
            pl.BlockSpec((tm, 512), row), pl.BlockSpec((tm, 512), row),
            pl.BlockSpec((N_HEADS * (LANES + ATTN_ONES), tm), lambda i: (0, i))),
        compiler_params=_cparams("parallel"),
        name="in_projection",
    )(*xs, mod, g_pre, w_r, c32, s32, c64, s64)


def _rope_tables(seq, head_dim, extra_rows):
    half, quarter = head_dim // 2, head_dim // 4
    freqs = (ROPE_BASE ** (-np.arange(quarter, dtype=np.float32) / quarter)).astype(np.float32)
    t = np.arange(seq)
    row = (t // GRID_W).astype(np.float32)
    col = (t % GRID_W).astype(np.float32)
    j = np.arange(LANES) % head_dim
    jj = j % half
    pos = np.where((j < half)[None, :], row[:, None], col[:, None])
    ang = (pos * freqs[jj % quarter][None, :]).astype(np.float32)
    cos = np.cos(ang)
    sin = np.sin(ang) * np.where(jj < quarter, -1.0, 1.0)[None, :]
    cos = np.concatenate([cos, np.ones((extra_rows, LANES))], axis=0).astype(np.float32)
    sin = np.concatenate([sin, np.zeros((extra_rows, LANES))], axis=0).astype(np.float32)
    return jnp.asarray(cos), jnp.asarray(sin)


def _head_stack(x, width):
    lane = lax.broadcasted_iota(I32, (1, x.shape[-1]), 1)
    zero = jnp.zeros_like(x)
    return jnp.concatenate([jnp.where(lane // width == h, x, zero) for h in range(N_HEADS)], axis=0)


def _head_select(x4, rows):
    lane = lax.broadcasted_iota(I32, (1, x4.shape[-1]), 1)
    out = jnp.zeros((rows, x4.shape[-1]), F32)
    for h in range(N_HEADS):
        out = out + jnp.where(lane // RET_DV == h, x4[h * rows:(h + 1) * rows], 0.0)
    return out


def _scan_tables_kernel(rl_lane_ref, rl_rows_ref, rl_col_ref, dmat_ref, xi_ref, zeta_ref, gchunk_ref):
    c = CHUNK
    idx = lax.broadcasted_iota(I32, (c, 1), 0).astype(F32)
    ri4 = lax.broadcasted_iota(I32, (N_HEADS * c, c), 0) % c
    ci4 = lax.broadcasted_iota(I32, (N_HEADS * c, c), 1)
    dist = jnp.abs(ri4 - ci4).astype(F32)
    for d in range(2):
        lg_lane = _log_sigmoid(rl_lane_ref[d])
        lg_rows = _log_sigmoid(rl_rows_ref[d])
        lg_col = _log_sigmoid(rl_col_ref[d])
        att4 = (ri4 <= ci4) if d else (ri4 >= ci4)
        dmat_ref[d] = jnp.where(att4, jnp.exp(dist * lg_rows), 0.0)
        xi_ref[d] = jnp.exp(((c - idx) if d else (idx + 1.0)) * lg_lane)
        zeta_ref[d] = jnp.exp((idx if d else (c - 1.0 - idx)) * lg_lane)
        g_chunk = jnp.exp(float(c) * lg_col)
        gchunk_ref[d] = jnp.concatenate([g_chunk, g_chunk], axis=1)


def _scan_tables(ret_logit):
    c = CHUNK
    rl_lane = jnp.repeat(ret_logit, RET_DK, axis=1)[:, None, :]
    rl_rows = jnp.broadcast_to(jnp.repeat(ret_logit, c, axis=1)[:, :, None], (2, N_HEADS * c, c))
    rl_col = jnp.broadcast_to(jnp.repeat(ret_logit, RET_DK, axis=1)[:, :, None], (2, LANES, LANES))
    return pl.pallas_call(
        _scan_tables_kernel,
        out_shape=(jax.ShapeDtypeStruct((2, N_HEADS * c, c), F32), jax.ShapeDtypeStruct((2, c, LANES), F32),
                   jax.ShapeDtypeStruct((2, c, LANES), F32), jax.ShapeDtypeStruct((2, LANES, 2 * LANES), F32)),
        name="scan_tables",
    )(rl_lane, rl_rows, rl_col)


def _state_block_mask():
    return (lax.broadcasted_iota(I32, (LANES, 2 * LANES), 0) // RET_DK
            == lax.broadcasted_iota(I32, (LANES, 2 * LANES), 1) // RET_DV)


def _ret_chain(ret_ref, r0, dmat, xi, zeta, g_chunk, out):
    c = CHUNK
    q = ret_ref[r0:r0 + c, 0:128]
    k = ret_ref[r0:r0 + c, 128:256]
    v = ret_ref[r0:r0 + c, 256:512]
    s = lax.dot_general(_head_stack(q, RET_DK), k, NT_DIMS, preferred_element_type=F32)
    kz = (k.astype(F32) * zeta).astype(BF16)
    u = lax.dot_general(kz, v, TN_DIMS, preferred_element_type=F32)
    yield
    o4 = jnp.dot((s * dmat).astype(BF16), v, preferred_element_type=F32)
    out.update(qx=(q.astype(F32) * xi).astype(BF16), u=jnp.where(_state_block_mask(), u, 0.0), g=g_chunk)
    yield
    out.update(intra=_head_select(o4, c))


def _state_chain(parts, o_ref, s_ref):
    c = CHUNK
    s = s_ref[...]
    for r0, p in parts:
        o_ref[r0:r0 + c, :] = p["intra"] + jnp.dot(p["qx"], s.astype(BF16), preferred_element_type=F32)
        s = s * p["g"] + p["u"]
        yield
    s_ref[...] = s


def _gla_chain(rev, gla_ref, glr_ref, r0, gw, gb, out):
    c = CHUNK
    ri = lax.broadcasted_iota(I32, (c, c), 0)
    ci = lax.broadcasted_iota(I32, (c, c), 1)
    attends = (ri <= ci) if rev else (ri >= ci)
    bd = _state_block_mask()
    gq = gla_ref[r0:r0 + c, 0:128].astype(F32)
    gk = gla_ref[r0:r0 + c, 128:256].astype(F32)
    gv = gla_ref[r0:r0 + c, 256:512]
    z = jnp.dot(glr_ref[r0:r0 + c, :], gw, precision=HIGHEST, preferred_element_type=F32) + gb
    yield
    la = _log_sigmoid(z) * (1.0 / GLA_TAU)
    if rev:
        first = (ri // GLA_SUB) * GLA_SUB + (GLA_SUB - 1)
        ref_sel = ci >= first
    else:
        first = (ri // GLA_SUB) * GLA_SUB
        ref_sel = ci <= first
    sel = jnp.concatenate([jnp.where(attends, 1.0, 0.0), jnp.where(ref_sel, 1.0, 0.0)], axis=0).astype(BF16)
    la_hi = la.astype(BF16)
    la_lo = (la - la_hi.astype(F32)).astype(BF16)
    sums = (jnp.dot(sel, la_hi, preferred_element_type=F32) + jnp.dot(sel, la_lo, preferred_element_type=F32))
    b, refrow = sums[0:c], sums[c:2 * c]
    yield
    qs = gq * jnp.exp(b - refrow)
    b_last = b[0:1] if rev else b[c - 1:c]
    kz = (gk * jnp.exp(b_last - b)).astype(BF16)
    u = lax.dot_general(kz, gv, TN_DIMS, preferred_element_type=F32)
    eye = lax.broadcasted_iota(I32, (LANES, LANES), 0) == lax.broadcasted_iota(I32, (LANES, LANES), 1)
    g_col = jnp.sum(jnp.where(eye, jnp.exp(b_last), 0.0), axis=1, keepdims=True)
    jcol = lax.broadcasted_iota(I32, (c, 1), 0)
    rr = lax.broadcasted_iota(I32, (N_HEADS * GLA_SUB, c), 0) % GLA_SUB
    cc = lax.broadcasted_iota(I32, (N_HEADS * GLA_SUB, c), 1)
    pieces = []
    for blk in range(c // GLA_SUB):
        lo = blk * GLA_SUB
        ref_b = refrow[lo:lo + 1]
        seen = (jcol >= lo) if rev else (jcol < lo + GLA_SUB)
        ks = (gk * jnp.exp(jnp.where(seen, ref_b - b, -jnp.inf))).astype(BF16)
        qz = _head_stack(qs[lo:lo + GLA_SUB], GLA_DK).astype(BF16)
        att = lax.dot_general(qz, ks, NT_DIMS, preferred_element_type=F32)
        ok = (cc >= rr + lo) if rev else (cc <= rr + lo)
        att = jnp.where(ok, att, 0.0).astype(BF16)
        pieces.append(_head_select(jnp.dot(att, gv, preferred_element_type=F32), GLA_SUB))
        if blk % 2:
            yield
    out.update(intra=jnp.concatenate(pieces, axis=0), qx=(gq * jnp.exp(b)).astype(BF16),
               u=jnp.where(bd, u, 0.0), g=g_col)


def _round_robin(chains):
    while chains:
        chains = [ch for ch in chains if next(ch, True) is None]


def _scan_kernel(ret_f, gla_f, glr_f, ret_b, gla_b, glr_b, dmat_ref, xi_ref, zeta_ref, gchunk_ref, gw_ref, gb_ref,
                 orf_ref, ogf_ref, orb_ref, ogb_ref, sr_ref, sg_ref):
    @pl.when(pl.program_id(1) == 0)
    def _():
        sr_ref[...] = jnp.zeros_like(sr_ref)
        sg_ref[...] = jnp.zeros_like(sg_ref)

    chains, state_chains = [], []
    for d, (ret_ref, gla_ref, glr_ref, ore_ref, ogl_ref) in enumerate(
            ((ret_f, gla_f, glr_f, orf_ref, ogf_ref), (ret_b, gla_b, glr_b, orb_ref, ogb_ref))):
        order = range(SCAN_CHUNKS - 1, -1, -1) if d else range(SCAN_CHUNKS)
        gla_parts, ret_parts = [], []
        for j in order:
            r0 = j * CHUNK
            gla_parts.append((r0, {}))
            ret_parts.append((r0, {}))
            chains.append(_gla_chain(bool(d), gla_ref, glr_ref, r0, gw_ref[d], gb_ref[d], gla_parts[-1][1]))
            chains.append(_ret_chain(ret_ref, r0, dmat_ref[d], xi_ref[d], zeta_ref[d], gchunk_ref[d],
                                     ret_parts[-1][1]))
        state_chains.append(_state_chain(gla_parts, ogl_ref, sg_ref.at[d]))
        state_chains.append(_state_chain(ret_parts, ore_ref, sr_ref.at[d]))
    _round_robin(chains)
    _round_robin(state_chains)


def _scan(ret, gla, glr, tables, gla_w, gla_b, n_batch, seq, ctx_len):
    n = ret.shape[0]
    c = CHUNK * SCAN_CHUNKS
    nc_ctx, nc_lat = ctx_len // c, seq // c
    n_steps = nc_ctx + nc_lat
    ctx_base = n_batch * nc_lat

    def fwd(b, s):
        return (jnp.where(s < nc_ctx, ctx_base + b * nc_ctx + s, b * nc_lat + (s - nc_ctx)), 0)

    def bwd(b, s):
        return (jnp.where(s < nc_ctx, ctx_base + b * nc_ctx + (nc_ctx - 1 - s), b * nc_lat + (n_steps - 1 - s)), 0)

    const3 = lambda b, s: (0, 0, 0)
    gw = jnp.zeros((2, LANES, LANES), F32).at[:, :GLA_LOWRANK].set(gla_w)
    gb = gla_b[:, None, :]
    chain_in = lambda m: [pl.BlockSpec((c, 768), m), pl.BlockSpec((c, 768), m), pl.BlockSpec((c, LANES), m)]
    o_sds = jax.ShapeDtypeStruct((n, 256), F32)
    return pl.pallas_call(
        _scan_kernel,
        out_shape=(o_sds, o_sds, o_sds, o_sds),
        grid=(n_batch, n_steps),
        in_specs=chain_in(fwd) + chain_in(bwd) + [pl.BlockSpec(t.shape, const3) for t in tables]
        + [pl.BlockSpec(gw.shape, const3), pl.BlockSpec(gb.shape, const3)],
        out_specs=(pl.BlockSpec((c, 256), fwd), pl.BlockSpec((c, 256), fwd),
                   pl.BlockSpec((c, 256), bwd), pl.BlockSpec((c, 256), bwd)),
        scratch_shapes=[pltpu.VMEM((2, LANES, 2 * LANES), F32), pltpu.VMEM((2, LANES, 2 * LANES), F32)],
        compiler_params=_cparams("arbitrary", "arbitrary"),
        name="scan",
    )(ret, gla, glr, ret, gla, glr, *tables, gw, gb)


def _attn_kernel(n_lat_blocks, lam_init, *refs):
    if n_lat_blocks:
        q_ref, kc_ref, vtc_ref, kl_ref, vtl_ref, lp_ref, g_ref, o_ref, m_ref, acc_ref, st_ref = refs
    else:
        q_ref, kc_ref, vtc_ref, lp_ref, g_ref, o_ref, m_ref, acc_ref, st_ref = refs
    tq = q_ref.shape[0]
    q = q_ref[...].astype(F32)
    lane = lax.broadcasted_iota(I32, (1, LANES), 1)
    qt = jnp.concatenate([jnp.where(lane < DIFF_DH, q, 0.0).T, jnp.where(lane >= DIFF_DH, q, 0.0).T],
                         axis=1).astype(BF16)
    m_ref[...] = jnp.full(m_ref.shape, -jnp.inf, F32)
    acc_ref[...] = jnp.zeros_like(acc_ref)

    def scores(slot, kb):
        st_ref[slot, 0:kb.shape[0], :] = jnp.dot(kb, qt, preferred_element_type=F32)

    def absorb(slot, vtb):
        st = st_ref[slot, 0:vtb.shape[1], :]
        m_prev = m_ref[...]
        m_new = jnp.maximum(m_prev, jnp.max(st, axis=0, keepdims=True))
        alpha = jnp.exp(m_prev - m_new)
        p = jnp.exp(st - m_new).astype(BF16)
        acc_ref[...] = alpha * acc_ref[...] + jnp.dot(vtb, p, preferred_element_type=F32)
        m_ref[...] = m_new

    def k_lat(blk):
        return kl_ref[pl.ds(pl.multiple_of(blk * ATTN_TK, ATTN_TK), ATTN_TK), :]

    def vt_lat(blk):
        return vtl_ref[:, pl.ds(pl.multiple_of(blk * ATTN_TK, ATTN_TK), ATTN_TK)]

    scores(0, kc_ref[...])
    if not n_lat_blocks:
        absorb(0, vtc_ref[...])
    else:
        scores(1, k_lat(0))
        absorb(0, vtc_ref[...])
        n_pairs = (n_lat_blocks - 1) // 2

        def body(i, carry):
            scores(0, k_lat(2 * i + 1))
            absorb(1, vt_lat(2 * i))
            scores(1, k_lat(2 * i + 2))
            absorb(0, vt_lat(2 * i + 1))
            return carry
        lax.fori_loop(0, n_pairs, body, 0)
        done = 2 * n_pairs
        if n_lat_blocks - done == 2:
            scores(0, k_lat(done + 1))
            absorb(1, vt_lat(done))
            absorb(0, vt_lat(done + 1))
        else:
            absorb(1, vt_lat(done))

    lp = lp_ref[...]
    lam = (jnp.exp(jnp.sum(lp[0:1] * lp[1:2], axis=1, keepdims=True))
           - jnp.exp(jnp.sum(lp[2:3] * lp[3:4], axis=1, keepdims=True)) + lam_init)
    acc = acc_ref[...]
    o1 = acc[0:LANES, :tq] / acc[LANES:LANES + 1, :tq]
    o2 = acc[0:LANES, tq:] / acc[LANES:LANES + 1, tq:]
    ot = o1 - lam * o2
    ot = ot * lax.rsqrt(jnp.mean(ot * ot, axis=0, keepdims=True) + EPS) * g_ref[...][:, 0:1] * (1.0 - lam_init)
    o_ref[...] = ot.T.astype(o_ref.dtype)


def _diff_attention(dq, dk, vt, lp, g, lam_init, n_batch, seq, ctx_len, latent):
    ctx_blk0 = (n_batch * seq) // ctx_len
    vrows = LANES + ATTN_ONES
    kc_spec = pl.BlockSpec((ctx_len, LANES), lambda b, h, i: (ctx_blk0 + b, h))
    vtc_spec = pl.BlockSpec((vrows, ctx_len), lambda b, h, i: (h, ctx_blk0 + b))
    const = lambda b, h, i: (0, 0)
    if latent:
        tq = ATTN_TQ
        n_q = seq // tq
        q_map = lambda b, h, i: (b * n_q + i, h)
        in_specs = [pl.BlockSpec((tq, LANES), q_map), kc_spec, vtc_spec,
                    pl.BlockSpec((seq, LANES), lambda b, h, i: (b, h)),
                    pl.BlockSpec((vrows, seq), lambda b, h, i: (h, b))]
        args = (dq, dk, vt, dk, vt)
        n_rows, o_map, n_lat_blocks = n_batch * seq, q_map, seq // ATTN_TK
    else:
        tq, n_q = ctx_len, 1
        in_specs = [kc_spec, kc_spec, vtc_spec]
        args = (dq, dk, vt)
        n_rows, o_map, n_lat_blocks = n_batch * ctx_len, (lambda b, h, i: (b, h)), 0
    in_specs += [pl.BlockSpec(lp.shape, const), pl.BlockSpec((LANES, LANES), const)]
    return pl.pallas_call(
        functools.partial(_attn_kernel, n_lat_blocks, lam_init),
        out_shape=jax.ShapeDtypeStruct((n_rows, N_HEADS * LANES), BF16),
        grid=(n_batch, N_HEADS, n_q),
        in_specs=in_specs,
        out_specs=pl.BlockSpec((tq, LANES), o_map),
        scratch_shapes=[pltpu.VMEM((1, 2 * tq), F32), pltpu.VMEM((vrows, 2 * tq), F32),
                        pltpu.VMEM((2, max(ATTN_TK, ctx_len), 2 * tq), F32)],
        compiler_params=_cparams("parallel", "parallel", "parallel"),
        name="diff_attention_lat" if latent else "diff_attention_ctx",
    )(*args, lp, g)


def _merge_kernel(tiles_per_batch, n_batch, n_a, orf_ref, orb_ref, ogf_ref, ogb_ref, rg_ref, gr_ref, *refs):
    n_x = 2 if n_a else 1
    mod_ref, gpost_ref, gpre_ref, glag_ref, wout_ref, wrt_ref, xo_ref, hp_ref, sc_ref = refs[2 * n_x:]
    dif = _pick_rows(n_a, *refs[:2]) if n_a else refs[0][...]
    x = _pick_rows(n_a, *refs[2:4]) if n_a else refs[1][...]
    d = x.shape[-1]
    r = jnp.minimum(pl.program_id(0) // tiles_per_batch, n_batch)
    gate1 = mod_ref[pl.ds(r, 1), 2 * d:3 * d]
    shift2 = mod_ref[pl.ds(r, 1), 3 * d:4 * d]
    scale2 = mod_ref[pl.ds(r, 1), 4 * d:5 * d]
    gi = lax.broadcasted_iota(I32, (256, 256), 0) // RET_DV
    gj = lax.broadcasted_iota(I32, (256, 256), 1) // RET_DV
    group_mean = jnp.where(gi == gj, 1.0 / RET_DV, 0.0).astype(BF16)

    def head_norm(o):
        ms = jnp.dot((o * o).astype(BF16), group_mean, preferred_element_type=F32)
        return o * lax.rsqrt(ms + EPS)

    ret = head_norm(orf_ref[...] + orb_ref[...]) * _silu(rg_ref[...].astype(F32))
    gla = head_norm(ogf_ref[...] + ogb_ref[...]) * glag_ref[...] * _silu(gr_ref[...].astype(F32))
    m = (jnp.dot(ret.astype(BF16), wout_ref[0:256, :], preferred_element_type=F32)
         + jnp.dot(gla.astype(BF16), wout_ref[256:512, :], preferred_element_type=F32)
         + jnp.dot(dif, wout_ref[512:1024, :], preferred_element_type=F32))
    x_new = x + gate1 * (_rms(m) * gpost_ref[...])
    xo_ref[...] = x_new
    h2 = _rms(x_new) * gpre_ref[...] * (1.0 + scale2) + shift2
    _tile_major_store(hp_ref, _pack_halves(h2[:, :d // 2], h2[:, d // 2:]))
    def split(a):
        hi = a.astype(BF16)
        return hi, (a - hi.astype(F32)).astype(BF16)

    h_hi, h_lo = split(h2)
    w_hi, w_lo = split(wrt_ref[...])
    nt = functools.partial(lax.dot_general, dimension_numbers=NT_DIMS, preferred_element_type=F32)
    logits = nt(w_hi, h_hi) + (nt(w_hi, h_lo) + nt(w_lo, h_hi))
    sc_ref[...] = 1.0 / (1.0 + jnp.exp(-logits))


def _merge(o_rf, o_rb, o_gf, o_gb, ret, gla, dif, xs, mod, g_post, g_pre_ffn, gla_g, w_out, w_rt,
           n_rows, n_batch, seq):
    tm = TOKEN_TILE
    row = lambda i: (i, 0)
    gate_col = lambda i: (i, 2)
    const = lambda i: (0, 0)
    if isinstance(xs, tuple):
        d = xs[0].shape[1]
        n_a, x_specs = _split_rows(*xs, tm, d)
        _, dif_specs = _split_rows(*dif, tm, 512)
    else:
        d = xs.shape[1]
        n_a, x_specs, dif_specs = 0, [pl.BlockSpec((tm, d), row)], [pl.BlockSpec((tm, 512), row)]
        xs, dif = (xs,), (dif,)
    return pl.pallas_call(
        functools.partial(_merge_kernel, seq // tm, n_batch, n_a),
        out_shape=(jax.ShapeDtypeStruct((n_rows, d), F32), jax.ShapeDtypeStruct((n_rows * HP_CHUNKS, LANES), U32),
                   jax.ShapeDtypeStruct((N_EXPERTS, n_rows), F32)),
        grid=(n_rows // tm,),
        in_specs=[
            pl.BlockSpec((tm, 256), row), pl.BlockSpec((tm, 256), row),
            pl.BlockSpec((tm, 256), row), pl.BlockSpec((tm, 256), row),
            pl.BlockSpec((tm, 256), gate_col), pl.BlockSpec((tm, 256), gate_col), *dif_specs, *x_specs,
            pl.BlockSpec(mod.shape, const), pl.BlockSpec((1, d), const), pl.BlockSpec((1, d), const),
            pl.BlockSpec((1, 256), const), pl.BlockSpec(w_out.shape, const), pl.BlockSpec(w_rt.shape, const),
        ],
        out_specs=(pl.BlockSpec((tm, d), row), pl.BlockSpec((tm * HP_CHUNKS, LANES), row),
                   pl.BlockSpec((N_EXPERTS, tm), lambda i: (0, i))),
        compiler_params=_cparams("parallel"),
        name="merge",
    )(o_rf, o_rb, o_gf, o_gb, ret, gla, *dif, *xs, mod, g_post, g_pre_ffn, gla_g, w_out, w_rt)


def _route_kernel(sc_ref, bias_ref, posa_ref, wts_ref, pos_ref, meta_ref):
    tt = sc_ref.shape[-1]
    scores = sc_ref[...]
    sel = scores + bias_ref[...][:, 0:1]
    sub = lax.broadcasted_iota(I32, (GROUP_SIZE, tt), 0)
    neg = -jnp.inf
    gscore = []
    for g in range(N_GROUPS):
        xg = sel[g * GROUP_SIZE:(g + 1) * GROUP_SIZE]
        m1 = jnp.max(xg, axis=0, keepdims=True)
        i1 = jnp.min(jnp.where(xg == m1, sub, GROUP_SIZE), axis=0, keepdims=True)
        m2 = jnp.max(jnp.where(sub == i1, neg, xg), axis=0, keepdims=True)
        gscore.append(m1 + m2)
    rows = []
    for g in range(N_GROUPS):
        rank = jnp.zeros((1, tt), I32)
        for o in range(N_GROUPS):
            if o == g:
                continue
            ahead = (gscore[o] >= gscore[g]) if o < g else (gscore[o] > gscore[g])
            rank = rank + ahead.astype(I32)
        rows.append(jnp.where(rank < TOPK_GROUPS, sel[g * GROUP_SIZE:(g + 1) * GROUP_SIZE], neg))
    masked = jnp.concatenate(rows, axis=0)
    eio = lax.broadcasted_iota(I32, (N_EXPERTS, tt), 0)
    member = jnp.zeros((N_EXPERTS, tt), F32)
    idxs, ws = [], []
    for _ in range(TOP_K):
        m = jnp.max(masked, axis=0, keepdims=True)
        i = jnp.min(jnp.where(masked == m, eio, N_EXPERTS), axis=0, keepdims=True)
        hit = eio == i
        idxs.append(i)
        ws.append(jnp.sum(jnp.where(hit, scores, 0.0), axis=0, keepdims=True))
        member = jnp.where(hit, 1.0, member)
        masked = jnp.where(hit, neg, masked)
    wsum = ws[0]
    for w in ws[1:]:
        wsum = wsum + w
    ti = lax.broadcasted_iota(I32, (tt, tt), 0)
    tj = lax.broadcasted_iota(I32, (tt, tt), 1)
    before = jnp.where(ti < tj, 1.0, 0.0).astype(BF16)
    rank_in_e = jnp.dot(member.astype(BF16), before, preferred_element_type=F32)
    cnt = jnp.sum(member, axis=1, keepdims=True)
    padded = jnp.floor((cnt + (SUBLANES - 1)) * (1.0 / SUBLANES)) * SUBLANES
    ei = lax.broadcasted_iota(I32, (N_EXPERTS, N_EXPERTS), 0)
    ej = lax.broadcasted_iota(I32, (N_EXPERTS, N_EXPERTS), 1)
    lower = jnp.where(ej < ei, 1.0, 0.0)
    off = jnp.dot(lower, jnp.broadcast_to(padded, (N_EXPERTS, LANES)), precision=HIGHEST,
                  preferred_element_type=F32)
    slot = rank_in_e + off[:, 0:1]
    zrow_i = jnp.zeros((SUBLANES - TOP_K, tt), I32)
    zrow_f = jnp.zeros((SUBLANES - TOP_K, tt), F32)
    pos = [jnp.sum(jnp.where(eio == i, slot, 0.0), axis=0, keepdims=True).astype(I32) for i in idxs]
    pos = jnp.concatenate(pos + [zrow_i], axis=0)
    wts_ref[...] = jnp.concatenate([w / wsum * ROUTED_SCALE for w in ws] + [zrow_f], axis=0)
    pos_ref[...] = pos
    posa_ref[...] = _tile_major_addr(pos, HP_CHUNKS)
    meta_ref[0] = jnp.concatenate([jnp.broadcast_to(cnt, (N_EXPERTS, LANES)), off], axis=1).astype(I32)


def _route(scores_t, bias, tt, row0, n):
    tile0 = row0 // tt
    tok = lambda i: (0, i)
    return pl.pallas_call(
        _route_kernel,
        out_shape=(jax.ShapeDtypeStruct((SUBLANES, n), I32), jax.ShapeDtypeStruct((SUBLANES, n), F32),
                   jax.ShapeDtypeStruct((SUBLANES, n), I32),
                   jax.ShapeDtypeStruct((n // tt, N_EXPERTS, 2 * LANES), I32)),
        grid=(n // tt,),
        in_specs=[pl.BlockSpec((N_EXPERTS, tt), lambda i: (0, i + tile0)),
                  pl.BlockSpec((N_EXPERTS, LANES), lambda i: (0, 0))],
        out_specs=(pl.BlockSpec((SUBLANES, tt), tok), pl.BlockSpec((SUBLANES, tt), tok),
                   pl.BlockSpec((SUBLANES, tt), tok),
                   pl.BlockSpec((1, N_EXPERTS, 2 * LANES), lambda i: (i, 0, 0))),
        compiler_params=_cparams("parallel"),
        name="route",
    )(scores_t, jnp.broadcast_to(bias[:, None], (N_EXPERTS, LANES)))


def _moe_kernel(group, n_first, has_prev, pos_ref, posa_ref, wts_ref, meta_ref, hp_ref, weg_ref, weu_ref, wed_ref,
                *refs):
    o_ref, xg_ref, y_ref, rt_ref = refs[1:] if has_prev else refs
    d = weg_ref.shape[1]
    half = d // 2
    tt = hp_ref.shape[0] // HP_CHUNKS
    eg = pl.program_id(1)

    @pl.when((pl.program_id(0) == 0) & (eg == 0))
    def _():
        def zero(i, carry):
            for u in range(SUBLANES):
                rt_ref[i * SUBLANES + u] = 0
            return carry
        lax.fori_loop(0, rt_ref.shape[0] // SUBLANES, zero, 0)

    @pl.when(eg == 0)
    def _():
        def scatter(t, carry):
            addr = _tile_major_addr(t, HP_CHUNKS)
            for k in range(TOP_K):
                rt_ref[pos_ref[t * SUBLANES + k]] = addr
            return carry
        lax.fori_loop(0, tt, scatter, 0)

    sub = lax.broadcasted_iota(I32, (SUBLANES, LANES), 0)

    def expert_block(ge, buf, base, n_valid):
        base = pl.multiple_of(base, SUBLANES)
        for i in range(MOE_BLOCK):
            xg_ref[buf, pl.ds(_tile_major_addr(i, HP_CHUNKS), HP_CHUNKS, stride=SUBLANES), :] = (
                hp_ref[pl.ds(rt_ref[base + i], HP_CHUNKS, stride=SUBLANES), :])
        yield
        halves = [_unpack_halves(col) for col in _tile_major_columns(xg_ref, MOE_BLOCK, HP_CHUNKS, (buf,))]
        xa = jnp.concatenate([h[0] for h in halves], axis=1).astype(BF16)
        xb = jnp.concatenate([h[1] for h in halves], axis=1).astype(BF16)
        hg = (jnp.dot(xa, weg_ref[ge, 0:half, :], preferred_element_type=F32)
              + jnp.dot(xb, weg_ref[ge, half:d, :], preferred_element_type=F32))
        hu = (jnp.dot(xa, weu_ref[ge, 0:half, :], preferred_element_type=F32)
              + jnp.dot(xb, weu_ref[ge, half:d, :], preferred_element_type=F32))
        yield
        y = jnp.dot((_silu(hg) * hu).astype(BF16), wed_ref[ge], preferred_element_type=F32)
        yield
        packed = _pack_halves(y[:, :half], y[:, half:])
        row0 = pl.multiple_of(base * HP_CHUNKS, SUBLANES * HP_CHUNKS)
        for g in range(MOE_BLOCK // SUBLANES):
            keep = sub < n_valid - g * SUBLANES
            for c in range(HP_CHUNKS):
                pltpu.store(y_ref.at[pl.ds(row0 + (g * HP_CHUNKS + c) * SUBLANES, SUBLANES), :],
                            packed[g * SUBLANES:(g + 1) * SUBLANES, c * LANES:(c + 1) * LANES], mask=keep)

    cnts = [meta_ref[0, eg * group + ge] for ge in range(group)]
    offs = [meta_ref[1, eg * group + ge] for ge in range(group)]
    chains = [expert_block(ge, ge * n_first + j, offs[ge] + j * MOE_BLOCK, cnts[ge] - j * MOE_BLOCK)
              for ge in range(group) for j in range(n_first)]
    _round_robin(chains)
    for ge in range(group):
        def more(j, carry, ge=ge):
            for _ in expert_block(ge, ge * n_first, offs[ge] + j * MOE_BLOCK, cnts[ge] - j * MOE_BLOCK):
                pass
            return carry
        lax.fori_loop(n_first, (cnts[ge] + (MOE_BLOCK - 1)) // MOE_BLOCK, more, 0)

    @pl.when(eg == N_EXPERTS // group - 1)
    def _():
        def tokens(g, carry):
            for u in range(SUBLANES):
                e0 = (g * SUBLANES + u) * SUBLANES
                acc_a = jnp.zeros((HP_CHUNKS, LANES), F32)
                acc_b = jnp.zeros((HP_CHUNKS, LANES), F32)
                for k in range(TOP_K):
                    ya, yb = _unpack_halves(y_ref[pl.ds(posa_ref[e0 + k], HP_CHUNKS, stride=SUBLANES), :])
                    w = wts_ref[e0 + k]
                    acc_a = acc_a + w * ya
                    acc_b = acc_b + w * yb
                out0 = pl.multiple_of(g * (2 * HP_CHUNKS * SUBLANES), 2 * HP_CHUNKS * SUBLANES) + u
                o_ref[pl.ds(out0, HP_CHUNKS, stride=SUBLANES), :] = acc_a
                o_ref[pl.ds(out0 + HP_CHUNKS * SUBLANES, HP_CHUNKS, stride=SUBLANES), :] = acc_b
            return carry

        lax.fori_loop(0, tt // SUBLANES, tokens, 0)


def _moe_segment(prev, scores_t, hp, router_bias, expert_w, layer, row0, n, tt):
    w_eg, w_eu, w_ed = expert_w
    d, de = w_eg.shape[1:]
    half = d // 2
    n_all = hp.shape[0] // HP_CHUNKS
    n_first = max(1, -(-tt * TOP_K // (N_EXPERTS * MOE_BLOCK)))
    group = max(1, MOE_CHAINS // n_first)
    group0 = layer * (N_EXPERTS // group)
    n_tiles, tile0 = n // tt, row0 // tt
    n_slots = -(-(tt * TOP_K + N_EXPERTS * (SUBLANES - 1) + n_first * MOE_BLOCK) // (SUBLANES * LANES)) * SUBLANES * LANES
    posa, wts, pos, meta = _route(scores_t, router_bias, tt, row0, n)
    cnt, off = meta[:, :, 0], meta[:, :, LANES]

    def per_token(a):
        return a.T.reshape(-1)

    meta_s = jnp.zeros((n_tiles, SUBLANES, LANES), I32)
    meta_s = meta_s.at[:, 0, :N_EXPERTS].set(cnt).at[:, 1, :N_EXPERTS].set(off).reshape(-1, LANES)

    tile = lambda i, e: (i + tile0, 0)
    expert = lambda i, e: (group0 + e, 0, 0)
    smem = functools.partial(pl.BlockSpec, memory_space=pltpu.SMEM)
    once = pl.Buffered(1)
    flat = smem((SUBLANES * tt,), lambda i, e: (i,))
    in_specs = [
        flat, flat, flat, smem((SUBLANES, LANES), lambda i, e: (i, 0)),
        pl.BlockSpec((tt * HP_CHUNKS, LANES), tile, pipeline_mode=once),
        pl.BlockSpec((group, d, de), expert), pl.BlockSpec((group, d, de), expert),
        pl.BlockSpec((group, de, d), expert),
    ]
    args = [per_token(pos), per_token(posa), per_token(wts), meta_s, hp, w_eg, w_eu, w_ed]
    aliases = {}
    if prev is not None:
        in_specs.append(pl.BlockSpec(memory_space=pl.ANY))
        args.append(prev)
        aliases = {len(args) - 1: 0}
    return pl.pallas_call(
        functools.partial(_moe_kernel, group, n_first, prev is not None),
        out_shape=jax.ShapeDtypeStruct((n_all * 2 * HP_CHUNKS, LANES), F32),
        grid=(n_tiles, N_EXPERTS // group),
        in_specs=in_specs,
        out_specs=pl.BlockSpec((tt * 2 * HP_CHUNKS, LANES), tile, pipeline_mode=once),
        scratch_shapes=[pltpu.VMEM((group * n_first, MOE_BLOCK * HP_CHUNKS, LANES), U32),
                        pltpu.VMEM((n_slots * HP_CHUNKS, LANES), U32),
                        pltpu.SMEM((n_slots,), I32)],
        input_output_aliases=aliases,
        compiler_params=_cparams("arbitrary", "arbitrary"),
        name="moe",
    )(*args)


def _moe_final_kernel(tiles_per_batch, n_batch, routed_ref, hp_ref, x_ref, mod_ref, gpost_ref,
                      wsg_ref, wsu_ref, wsd_ref, o_ref):
    tm, d = x_ref.shape
    half = d // 2
    r = jnp.minimum(pl.program_id(0) // tiles_per_batch, n_batch)
    gate2 = mod_ref[pl.ds(r, 1), 5 * d:6 * d]
    halves = [_unpack_halves(col) for col in _tile_major_columns(hp_ref, tm, HP_CHUNKS)]
    xa = jnp.concatenate([h[0] for h in halves], axis=1).astype(BF16)
    xb = jnp.concatenate([h[1] for h in halves], axis=1).astype(BF16)
    hg = (jnp.dot(xa, wsg_ref[0:half, :], preferred_element_type=F32)
          + jnp.dot(xb, wsg_ref[half:d, :], preferred_element_type=F32))
    hu = (jnp.dot(xa, wsu_ref[0:half, :], preferred_element_type=F32)
          + jnp.dot(xb, wsu_ref[half:d, :], preferred_element_type=F32))
    routed = jnp.concatenate(_tile_major_columns(routed_ref, tm, 2 * HP_CHUNKS), axis=1)
    f = jnp.dot((_silu(hg) * hu).astype(BF16), wsd_ref[...], preferred_element_type=F32) + routed
    o_ref[...] = x_ref[...] + gate2 * (_rms(f) * gpost_ref[...])


def _moe_final(routed, hp, xs, mod, g_post, w_sg, w_su, w_sd, n_batch, seq):
    n, d = xs.shape
    tm = TOKEN_TILE
    row = lambda i: (i, 0)
    const = lambda i: (0, 0)
    return pl.pallas_call(
        functools.partial(_moe_final_kernel, seq // tm, n_batch),
        out_shape=jax.ShapeDtypeStruct((n, d), F32),
        grid=(n // tm,),
        in_specs=[pl.BlockSpec((tm * 2 * HP_CHUNKS, LANES), row), pl.BlockSpec((tm * HP_CHUNKS, LANES), row),
                  pl.BlockSpec((tm, d), row),
                  pl.BlockSpec(mod.shape, const), pl.BlockSpec((1, d), const),
                  pl.BlockSpec(w_sg.shape, const), pl.BlockSpec(w_su.shape, const), pl.BlockSpec(w_sd.shape, const)],
        out_specs=pl.BlockSpec((tm, d), row),
        compiler_params=_cparams("parallel"),
        name="moe_final",
    )(routed, hp, xs, mod, g_post, w_sg, w_su, w_sd)


def _moe_layer(scores_t, hp, xs, mod, router_bias, g_post, expert_w, shared_w, layer, n_lat, n_batch, seq):
    pick = lambda rows: next(t for t in MOE_TILES if rows % t == 0)
    routed = _moe_segment(None, scores_t, hp, router_bias, expert_w, layer, 0, n_lat, pick(n_lat))
    n_ctx = hp.shape[0] // HP_CHUNKS - n_lat
    if n_ctx:
        tt = next(t for t in MOE_TILES if n_ctx % t == 0 and n_lat % t == 0)
        routed = _moe_segment(routed, scores_t, hp, router_bias, expert_w, layer, n_lat, n_ctx, tt)
    return _moe_final(routed, hp, xs, mod, g_post, *shared_w, n_batch, seq)


def kernel(x, c, ctx, c_ctx, w_mod, b_mod, g_pre_mix, g_post_mix, g_pre_ffn, g_post_ffn, w_in, w_out,
           ret_decay_logit, gla_w_gate, gla_b_gate, gla_norm_g, diff_lambda, diff_norm_g,
           w_router, router_bias, w_exp_gate, w_exp_up, w_exp_down, w_sh_gate, w_sh_up, w_sh_down):
    n_batch, seq, d = x.shape
    ctx_len = ctx.shape[1]
    depth = w_mod.shape[0]
    n_lat = n_batch * seq
    assert seq % TOKEN_TILE == 0 and (n_batch * ctx_len) % TOKEN_TILE == 0 and n_batch < SUBLANES
    scan_rows = CHUNK * SCAN_CHUNKS
    assert seq % ATTN_TK == 0 and seq % scan_rows == 0 and ctx_len % scan_rows == 0 and n_lat % ctx_len == 0

    xs = (x.reshape(n_lat, d), ctx.reshape(n_batch * ctx_len, d))
    cond = jnp.zeros((SUBLANES, d), F32).at[:n_batch].set(c).at[n_batch].set(c_ctx)
    mods = _modulation(cond, w_mod, b_mod)
    tables = _rope_tables(seq, RET_DK, TOKEN_TILE) + _rope_tables(seq, DIFF_DH, TOKEN_TILE)
    lr0 = N_HEADS * (2 * RET_DK + 2 * RET_DV + 2 * GLA_DK + 2 * GLA_DV)
    row = lambda a: a[None, :]
    expert_w = tuple(w.astype(BF16).reshape((depth * N_EXPERTS,) + w.shape[2:])
                     for w in (w_exp_gate, w_exp_up, w_exp_down))

    for layer in range(depth):
        need_ctx = layer < depth - 1
        lam_init = 0.8 - 0.6 * math.exp(-0.3 * layer)
        mod = mods[layer]
        wl = w_in[layer]
        w_r = jnp.concatenate([wl[:, :lr0], wl[:, lr0:lr0 + GLA_LOWRANK],
                               jnp.zeros((d, LANES - GLA_LOWRANK), F32), wl[:, lr0 + GLA_LOWRANK:]],
                              axis=1).astype(BF16)
        ret, gla, glr, dq, dk, vt = _in_projection(xs, mod, row(g_pre_mix[layer]), w_r, tables, n_batch, seq)

        o_rf, o_gf, o_rb, o_gb = _scan(ret, gla, glr, _scan_tables(ret_decay_logit[layer]), gla_w_gate[layer],
                                       gla_b_gate[layer], n_batch, seq, ctx_len)

        lp = jnp.zeros((SUBLANES, LANES), F32).at[:4, :DIFF_DH].set(diff_lambda[layer])
        g_col = jnp.broadcast_to(diff_norm_g[layer][:, None], (LANES, LANES))
        attn = functools.partial(_diff_attention, dq, dk, vt, lp, g_col, lam_init, n_batch, seq, ctx_len)
        dif = attn(True)
        n_rows = n_lat
        if need_ctx:
            n_rows = n_lat + n_batch * ctx_len
            if isinstance(xs, tuple):
                dif = (dif, attn(False))
            else:
                dif = jnp.concatenate([dif, attn(False)], axis=0)
        elif isinstance(xs, tuple):
            xs = xs[0]

        xs, hp, scores_t = _merge(o_rf, o_rb, o_gf, o_gb, ret, gla, dif, xs, mod, row(g_post_mix[layer]),
                                  row(g_pre_ffn[layer]), row(jnp.tile(gla_norm_g[layer], N_HEADS)),
                                  w_out[layer].astype(BF16), w_router[layer].T, n_rows, n_batch, seq)
        shared_w = tuple(w[layer].astype(BF16) for w in (w_sh_gate, w_sh_up, w_sh_down))
        xs = _moe_layer(scores_t, hp, xs, mod, router_bias[layer], row(g_post_ffn[layer]), expert_w, shared_w,
                        layer, n_lat, n_batch, seq)
    return xs[:n_lat].reshape(n_batch, seq, d)
```

```python
import functools
import math

import jax
import jax.numpy as jnp
import numpy as np
from jax import lax
from jax.experimental import pallas as pl
from jax.experimental.pallas import tpu as pltpu

F32 = jnp.float32
BF16 = jnp.bfloat16
I32 = jnp.int32
U32 = jnp.uint32

GRID_W = 64
CHUNK = 128
N_HEADS = 4
RET_DK, RET_DV = 32, 64
GLA_DK, GLA_DV = 32, 64
GLA_LOWRANK = 16
GLA_TAU = 16.0
DIFF_DH = 64
ROPE_BASE = 10000.0
N_EXPERTS = 64
TOP_K = 6
N_GROUPS = 8
TOPK_GROUPS = 4
GROUP_SIZE = N_EXPERTS // N_GROUPS
ROUTED_SCALE = 2.5
EPS = 1e-6
GLA_SUB = 16
SCAN_CHUNKS = 2

LANES = 128
SUBLANES = 8
VMEM_LIMIT_BYTES = 56 * 1024 * 1024

TOKEN_TILE = 512
ATTN_TQ = 512
ATTN_TK = 512
ATTN_ONES = 16
MOE_TILES = (2048, 1024, 512)
MOE_BLOCK = 128
MOE_CHAINS = 4
HP_CHUNKS = 4

HIGHEST = lax.Precision.HIGHEST
NT_DIMS = (((1,), (1,)), ((), ()))
TN_DIMS = (((0,), (0,)), ((), ()))


def _cparams(*sem):
    return pltpu.CompilerParams(dimension_semantics=sem, vmem_limit_bytes=VMEM_LIMIT_BYTES)


def _log_sigmoid(x):
    return jnp.minimum(x, 0.0) - jnp.log(1.0 + jnp.exp(-jnp.abs(x)))


def _silu(x):
    return x * (1.0 / (1.0 + jnp.exp(-x)))


def _rms(x):
    return x * lax.rsqrt(jnp.mean(x * x, axis=-1, keepdims=True) + EPS)


def _pack_halves(a, b):
    ua = lax.bitcast_convert_type(a.astype(BF16).astype(F32), U32)
    ub = lax.bitcast_convert_type(b.astype(BF16).astype(F32), U32)
    return (ua & jnp.uint32(0xFFFF0000)) | (ub >> 16)


def _unpack_halves(w):
    a = lax.bitcast_convert_type(w & jnp.uint32(0xFFFF0000), F32)
    b = lax.bitcast_convert_type(w << 16, F32)
    return a, b


def _tile_major_store(ref, v, lead=()):
    k = v.shape[1] // LANES
    for g in range(v.shape[0] // SUBLANES):
        for c in range(k):
            blk = g * k + c
            ref[lead + (slice(blk * SUBLANES, (blk + 1) * SUBLANES), slice(None))] = (
                v[g * SUBLANES:(g + 1) * SUBLANES, c * LANES:(c + 1) * LANES])


def _tile_major_columns(ref, rows, k, lead=()):
    return [jnp.concatenate([ref[lead + (slice((g * k + c) * SUBLANES, (g * k + c + 1) * SUBLANES), slice(None))]
                             for g in range(rows // SUBLANES)], axis=0) for c in range(k)]


def _tile_major_addr(t, k):
    return (t >> 3) * (SUBLANES * k) + (t & (SUBLANES - 1))


def _mod_kernel(cond_ref, w_ref, b_ref, o_ref):
    a = _silu(cond_ref[...])
    o_ref[0] = jnp.dot(a, w_ref[0], precision=HIGHEST, preferred_element_type=F32) + b_ref[0]


def _modulation(cond, w_mod, b_mod):
    n_layers, d, d6 = w_mod.shape
    tn = 1024
    return pl.pallas_call(
        _mod_kernel,
        out_shape=jax.ShapeDtypeStruct((n_layers, SUBLANES, d6), F32),
        grid=(n_layers, d6 // tn),
        in_specs=[
            pl.BlockSpec((SUBLANES, d), lambda l, j: (0, 0)),
            pl.BlockSpec((1, d, tn), lambda l, j: (l, 0, j)),
            pl.BlockSpec((1, 1, tn), lambda l, j: (l, 0, j)),
        ],
        out_specs=pl.BlockSpec((1, SUBLANES, tn), lambda l, j: (l, 0, j)),
        compiler_params=_cparams("parallel", "parallel"),
        name="modulation",
    )(cond, w_mod, b_mod.reshape(n_layers, 1, d6))


def _rope(x, cos, sin, quarter):
    lane = lax.broadcasted_iota(I32, (1, LANES), 1)
    first = (lane % (2 * quarter)) < quarter
    outs = []
    for c in range(x.shape[-1] // LANES):
        xc = x[:, c * LANES:(c + 1) * LANES]
        partner = jnp.where(first, pltpu.roll(xc, LANES - quarter, 1), pltpu.roll(xc, quarter, 1))
        outs.append(xc * cos + partner * sin)
    return outs[0] if len(outs) == 1 else jnp.concatenate(outs, axis=-1)


def _split_rows(a, b, tm, width):
    n_a = a.shape[0] // tm
    return n_a, [pl.BlockSpec((tm, width), lambda i: (jnp.minimum(i, n_a - 1), 0)),
                 pl.BlockSpec((tm, width), lambda i: (jnp.maximum(i - n_a, 0), 0))]


def _pick_rows(n_a, a_ref, b_ref):
    return jnp.where(pl.program_id(0) < n_a, a_ref[...], b_ref[...])


def _inproj_kernel(tiles_per_batch, n_batch, n_a, *refs):
    n_x = 2 if n_a else 1
    (mod_ref, g_ref, w_ref, c32_ref, s32_ref, c64_ref, s64_ref,
     ret_ref, gla_ref, glr_ref, dq_ref, dk_ref, vt_ref) = refs[n_x:]
    x = _pick_rows(n_a, *refs[:2]) if n_a else refs[0][...]
    d = x.shape[-1]
    r = jnp.minimum(pl.program_id(0) // tiles_per_batch, n_batch)
    shift = mod_ref[pl.ds(r, 1), 0:d]
    scale = mod_ref[pl.ds(r, 1), d:2 * d]
    h = (_rms(x) * g_ref[...] * (1.0 + scale) + shift).astype(BF16)

    def proj(lo, hi):
        return jnp.dot(h, w_ref[:, lo:hi], preferred_element_type=F32)

    c32, s32 = c32_ref[...], s32_ref[...]
    c64, s64 = c64_ref[...], s64_ref[...]
    ret = proj(0, 768)
    ret_ref[:, 0:128] = _rope(ret[:, 0:128], c32, s32, RET_DK // 4).astype(BF16)
    ret_ref[:, 128:256] = (_rope(ret[:, 128:256], c32, s32, RET_DK // 4) * RET_DK ** -0.5).astype(BF16)
    ret_ref[:, 256:768] = ret[:, 256:768].astype(BF16)
    gla = proj(768, 1536)
    gla_ref[:, 0:128] = (gla[:, 0:128] * GLA_DK ** -0.5).astype(BF16)
    gla_ref[:, 128:768] = gla[:, 128:768].astype(BF16)
    glr_ref[...] = proj(1536, 1664)
    dq_ref[...] = (_rope(proj(1664, 2176), c64, s64, DIFF_DH // 4) * DIFF_DH ** -0.5).astype(BF16)
    dk_ref[...] = _rope(proj(2176, 2688), c64, s64, DIFF_DH // 4).astype(BF16)
    dv = proj(2688, 3200)
    vrows = LANES + ATTN_ONES
    for hd in range(N_HEADS):
        vt_ref[hd * vrows:hd * vrows + LANES, :] = dv[:, hd * LANES:(hd + 1) * LANES].T.astype(BF16)
        vt_ref[hd * vrows + LANES:(hd + 1) * vrows, :] = jnp.ones((ATTN_ONES, dv.shape[0]), BF16)


def _in_projection(xs, mod, g_pre, w_r, tables, n_batch, seq):
    tm = TOKEN_TILE
    if isinstance(xs, tuple):
        d = xs[0].shape[1]
        n = xs[0].shape[0] + xs[1].shape[0]
        n_a, x_specs = _split_rows(*xs, tm, d)
    else:
        n, d = xs.shape
        n_a, x_specs, xs = 0, [pl.BlockSpec((tm, d), lambda i: (i, 0))], (xs,)
    tiles_per_batch = seq // tm
    n_lat_tiles = n_batch * tiles_per_batch
    c32, s32, c64, s64 = tables

    def tab_map(i):
        return (jnp.where(i < n_lat_tiles, i % tiles_per_batch, tiles_per_batch), 0)

    row = lambda i: (i, 0)
    const = lambda i: (0, 0)
    tab_spec = pl.BlockSpec((tm, LANES), tab_map)
    return pl.pallas_call(
        functools.partial(_inproj_kernel, tiles_per_batch, n_batch, n_a),
        out_shape=(
            jax.ShapeDtypeStruct((n, 768), BF16), jax.ShapeDtypeStruct((n, 768), BF16),
            jax.ShapeDtypeStruct((n, LANES), F32),
            jax.ShapeDtypeStruct((n, 512), BF16), jax.ShapeDtypeStruct((n, 512), BF16),
            jax.ShapeDtypeStruct((N_HEADS * (LANES + ATTN_ONES), n), BF16)),
        grid=(n // tm,),
        in_specs=x_specs + [
            pl.BlockSpec(mod.shape, const),
            pl.BlockSpec((1, d), const),
            pl.BlockSpec(w_r.shape, const),
            tab_spec, tab_spec, tab_spec, tab_spec,
        ],
        out_specs=(
            pl.BlockSpec((tm, 768), row), pl.BlockSpec((tm, 768), row), pl.BlockSpec((tm, LANES), row),
            pl.BlockSpec((tm, 512), row), pl.BlockSpec((tm, 512), row),
            pl.BlockSpec((N_HEADS * (LANES + ATTN_ONES), tm), lambda i: (0, i))),
        compiler_params=_cparams("parallel"),
        name="in_projection",
    )(*xs, mod, g_pre, w_r, c32, s32, c64, s64)


def _rope_tables(seq, head_dim, extra_rows):
    half, quarter = head_dim // 2, head_dim // 4
    freqs = (ROPE_BASE ** (-np.arange(quarter, dtype=np.float32) / quarter)).astype(np.float32)
    t = np.arange(seq)
    row = (t // GRID_W).astype(np.float32)
    col = (t % GRID_W).astype(np.float32)
    j = np.arange(LANES) % head_dim
    jj = j % half
    pos = np.where((j < half)[None, :], row[:, None], col[:, None])
    ang = (pos * freqs[jj % quarter][None, :]).astype(np.float32)
    cos = np.cos(ang)
    sin = np.sin(ang) * np.where(jj < quarter, -1.0, 1.0)[None, :]
    cos = np.concatenate([cos, np.ones((extra_rows, LANES))], axis=0).astype(np.float32)
    sin = np.concatenate([sin, np.zeros((extra_rows, LANES))], axis=0).astype(np.float32)
    return jnp.asarray(cos), jnp.asarray(sin)


def _head_stack(x, width):
    lane = lax.broadcasted_iota(I32, (1, x.shape[-1]), 1)
    zero = jnp.zeros_like(x)
    return jnp.concatenate([jnp.where(lane // width == h, x, zero) for h in range(N_HEADS)], axis=0)


def _head_select(x4, rows):
    lane = lax.broadcasted_iota(I32, (1, x4.shape[-1]), 1)
    out = jnp.zeros((rows, x4.shape[-1]), F32)
    for h in range(N_HEADS):
        out = out + jnp.where(lane // RET_DV == h, x4[h * rows:(h + 1) * rows], 0.0)
    return out


def _scan_tables_kernel(rl_lane_ref, rl_rows_ref, rl_col_ref, dmat_ref, xi_ref, zeta_ref, gchunk_ref):
    c = CHUNK
    idx = lax.broadcasted_iota(I32, (c, 1), 0).astype(F32)
    ri4 = lax.broadcasted_iota(I32, (N_HEADS * c, c), 0) % c
    ci4 = lax.broadcasted_iota(I32, (N_HEADS * c, c), 1)
    dist = jnp.abs(ri4 - ci4).astype(F32)
    for d in range(2):
        lg_lane = _log_sigmoid(rl_lane_ref[d])
        lg_rows = _log_sigmoid(rl_rows_ref[d])
        lg_col = _log_sigmoid(rl_col_ref[d])
        att4 = (ri4 <= ci4) if d else (ri4 >= ci4)
        dmat_ref[d] = jnp.where(att4, jnp.exp(dist * lg_rows), 0.0)
        xi_ref[d] = jnp.exp(((c - idx) if d else (idx + 1.0)) * lg_lane)
        zeta_ref[d] = jnp.exp((idx if d else (c - 1.0 - idx)) * lg_lane)
        g_chunk = jnp.exp(float(c) * lg_col)
        gchunk_ref[d] = jnp.concatenate([g_chunk, g_chunk], axis=1)


def _scan_tables(ret_logit):
    c = CHUNK
    rl_lane = jnp.repeat(ret_logit, RET_DK, axis=1)[:, None, :]
    rl_rows = jnp.broadcast_to(jnp.repeat(ret_logit, c, axis=1)[:, :, None], (2, N_HEADS * c, c))
    rl_col = jnp.broadcast_to(jnp.repeat(ret_logit, RET_DK, axis=1)[:, :, None], (2, LANES, LANES))
    return pl.pallas_call(
        _scan_tables_kernel,
        out_shape=(jax.ShapeDtypeStruct((2, N_HEADS * c, c), F32), jax.ShapeDtypeStruct((2, c, LANES), F32),
                   jax.ShapeDtypeStruct((2, c, LANES), F32), jax.ShapeDtypeStruct((2, LANES, 2 * LANES), F32)),
        name="scan_tables",
    )(rl_lane, rl_rows, rl_col)


def _state_block_mask():
    return (lax.broadcasted_iota(I32, (LANES, 2 * LANES), 0) // RET_DK
            == lax.broadcasted_iota(I32, (LANES, 2 * LANES), 1) // RET_DV)


def _ret_chain(ret_ref, r0, dmat, xi, zeta, g_chunk, out):
    c = CHUNK
    q = ret_ref[r0:r0 + c, 0:128]
    k = ret_ref[r0:r0 + c, 128:256]
    v = ret_ref[r0:r0 + c, 256:512]
    s = lax.dot_general(_head_stack(q, RET_DK), k, NT_DIMS, preferred_element_type=F32)
    kz = (k.astype(F32) * zeta).astype(BF16)
    u = lax.dot_general(kz, v, TN_DIMS, preferred_element_type=F32)
    yield
    o4 = jnp.dot((s * dmat).astype(BF16), v, preferred_element_type=F32)
    out.update(qx=(q.astype(F32) * xi).astype(BF16), u=jnp.where(_state_block_mask(), u, 0.0), g=g_chunk)
    yield
    out.update(intra=_head_select(o4, c))


def _state_chain(parts, o_ref, s_ref):
    c = CHUNK
    s = s_ref[...]
    for r0, p in parts:
        o_ref[r0:r0 + c, :] = p["intra"] + jnp.dot(p["qx"], s.astype(BF16), preferred_element_type=F32)
        s = s * p["g"] + p["u"]
        yield
    s_ref[...] = s


def _gla_chain(rev, gla_ref, glr_ref, r0, gw, gb, out):
    c = CHUNK
    ri = lax.broadcasted_iota(I32, (c, c), 0)
    ci = lax.broadcasted_iota(I32, (c, c), 1)
    attends = (ri <= ci) if rev else (ri >= ci)
    bd = _state_block_mask()
    gq = gla_ref[r0:r0 + c, 0:128].astype(F32)
    gk = gla_ref[r0:r0 + c, 128:256].astype(F32)
    gv = gla_ref[r0:r0 + c, 256:512]
    z = jnp.dot(glr_ref[r0:r0 + c, :], gw, precision=HIGHEST, preferred_element_type=F32) + gb
    yield
    la = _log_sigmoid(z) * (1.0 / GLA_TAU)
    if rev:
        first = (ri // GLA_SUB) * GLA_SUB + (GLA_SUB - 1)
        ref_sel = ci >= first
    else:
        first = (ri // GLA_SUB) * GLA_SUB
        ref_sel = ci <= first
    sel = jnp.concatenate([jnp.where(attends, 1.0, 0.0), jnp.where(ref_sel, 1.0, 0.0)], axis=0).astype(BF16)
    la_hi = la.astype(BF16)
    la_lo = (la - la_hi.astype(F32)).astype(BF16)
    sums = (jnp.dot(sel, la_hi, preferred_element_type=F32) + jnp.dot(sel, la_lo, preferred_element_type=F32))
    b, refrow = sums[0:c], sums[c:2 * c]
    yield
    qs = gq * jnp.exp(b - refrow)
    b_last = b[0:1] if rev else b[c - 1:c]
    kz = (gk * jnp.exp(b_last - b)).astype(BF16)
    u = lax.dot_general(kz, gv, TN_DIMS, preferred_element_type=F32)
    eye = lax.broadcasted_iota(I32, (LANES, LANES), 0) == lax.broadcasted_iota(I32, (LANES, LANES), 1)
    g_col = jnp.sum(jnp.where(eye, jnp.exp(b_last), 0.0), axis=1, keepdims=True)
    jcol = lax.broadcasted_iota(I32, (c, 1), 0)
    rr = lax.broadcasted_iota(I32, (N_HEADS * GLA_SUB, c), 0) % GLA_SUB
    cc = lax.broadcasted_iota(I32, (N_HEADS * GLA_SUB, c), 1)
    pieces = []
    for blk in range(c // GLA_SUB):
        lo = blk * GLA_SUB
        ref_b = refrow[lo:lo + 1]
        seen = (jcol >= lo) if rev else (jcol < lo + GLA_SUB)
        ks = (gk * jnp.exp(jnp.where(seen, ref_b - b, -jnp.inf))).astype(BF16)
        qz = _head_stack(qs[lo:lo + GLA_SUB], GLA_DK).astype(BF16)
        att = lax.dot_general(qz, ks, NT_DIMS, preferred_element_type=F32)
        ok = (cc >= rr + lo) if rev else (cc <= rr + lo)
        att = jnp.where(ok, att, 0.0).astype(BF16)
        pieces.append(_head_select(jnp.dot(att, gv, preferred_element_type=F32), GLA_SUB))
        if blk % 2:
            yield
    out.update(intra=jnp.concatenate(pieces, axis=0), qx=(gq * jnp.exp(b)).astype(BF16),
               u=jnp.where(bd, u, 0.0), g=g_col)


def _round_robin(chains):
    while chains:
        chains = [ch for ch in chains if next(ch, True) is None]


def _scan_kernel(ret_f, gla_f, glr_f, ret_b, gla_b, glr_b, dmat_ref, xi_ref, zeta_ref, gchunk_ref, gw_ref, gb_ref,
                 orf_ref, ogf_ref, orb_ref, ogb_ref, sr_ref, sg_ref):
    @pl.when(pl.program_id(1) == 0)
    def _():
        sr_ref[...] = jnp.zeros_like(sr_ref)
        sg_ref[...] = jnp.zeros_like(sg_ref)

    chains, state_chains = [], []
    for d, (ret_ref, gla_ref, glr_ref, ore_ref, ogl_ref) in enumerate(
            ((ret_f, gla_f, glr_f, orf_ref, ogf_ref), (ret_b, gla_b, glr_b, orb_ref, ogb_ref))):
        order = range(SCAN_CHUNKS - 1, -1, -1) if d else range(SCAN_CHUNKS)
        gla_parts, ret_parts = [], []
        for j in order:
            r0 = j * CHUNK
            gla_parts.append((r0, {}))
            ret_parts.append((r0, {}))
            chains.append(_gla_chain(bool(d), gla_ref, glr_ref, r0, gw_ref[d], gb_ref[d], gla_parts[-1][1]))
            chains.append(_ret_chain(ret_ref, r0, dmat_ref[d], xi_ref[d], zeta_ref[d], gchunk_ref[d],
                                     ret_parts[-1][1]))
        state_chains.append(_state_chain(gla_parts, ogl_ref, sg_ref.at[d]))
        state_chains.append(_state_chain(ret_parts, ore_ref, sr_ref.at[d]))
    _round_robin(chains)
    _round_robin(state_chains)


def _scan(ret, gla, glr, tables, gla_w, gla_b, n_batch, seq, ctx_len):
    n = ret.shape[0]
    c = CHUNK * SCAN_CHUNKS
    nc_ctx, nc_lat = ctx_len // c, seq // c
    n_steps = nc_ctx + nc_lat
    ctx_base = n_batch * nc_lat

    def fwd(b, s):
        return (jnp.where(s < nc_ctx, ctx_base + b * nc_ctx + s, b * nc_lat + (s - nc_ctx)), 0)

    def bwd(b, s):
        return (jnp.where(s < nc_ctx, ctx_base + b * nc_ctx + (nc_ctx - 1 - s), b * nc_lat + (n_steps - 1 - s)), 0)

    const3 = lambda b, s: (0, 0, 0)
    gw = jnp.zeros((2, LANES, LANES), F32).at[:, :GLA_LOWRANK].set(gla_w)
    gb = gla_b[:, None, :]
    chain_in = lambda m: [pl.BlockSpec((c, 768), m), pl.BlockSpec((c, 768), m), pl.BlockSpec((c, LANES), m)]
    o_sds = jax.ShapeDtypeStruct((n, 256), F32)
    return pl.pallas_call(
        _scan_kernel,
        out_shape=(o_sds, o_sds, o_sds, o_sds),
        grid=(n_batch, n_steps),
        in_specs=chain_in(fwd) + chain_in(bwd) + [pl.BlockSpec(t.shape, const3) for t in tables]
        + [pl.BlockSpec(gw.shape, const3), pl.BlockSpec(gb.shape, const3)],
        out_specs=(pl.BlockSpec((c, 256), fwd), pl.BlockSpec((c, 256), fwd),
                   pl.BlockSpec((c, 256), bwd), pl.BlockSpec((c, 256), bwd)),
        scratch_shapes=[pltpu.VMEM((2, LANES, 2 * LANES), F32), pltpu.VMEM((2, LANES, 2 * LANES), F32)],
        compiler_params=_cparams("arbitrary", "arbitrary"),
        name="scan",
    )(ret, gla, glr, ret, gla, glr, *tables, gw, gb)


def _attn_kernel(n_lat_blocks, lam_init, *refs):
    if n_lat_blocks:
        q_ref, kc_ref, vtc_ref, kl_ref, vtl_ref, lp_ref, g_ref, o_ref, m_ref, acc_ref, st_ref = refs
    else:
        q_ref, kc_ref, vtc_ref, lp_ref, g_ref, o_ref, m_ref, acc_ref, st_ref = refs
    tq = q_ref.shape[0]
    q = q_ref[...].astype(F32)
    lane = lax.broadcasted_iota(I32, (1, LANES), 1)
    qt = jnp.concatenate([jnp.where(lane < DIFF_DH, q, 0.0).T, jnp.where(lane >= DIFF_DH, q, 0.0).T],
                         axis=1).astype(BF16)
    m_ref[...] = jnp.full(m_ref.shape, -jnp.inf, F32)
    acc_ref[...] = jnp.zeros_like(acc_ref)

    def scores(slot, kb):
        st_ref[slot, 0:kb.shape[0], :] = jnp.dot(kb, qt, preferred_element_type=F32)

    def absorb(slot, vtb):
        st = st_ref[slot, 0:vtb.shape[1], :]
        m_prev = m_ref[...]
        m_new = jnp.maximum(m_prev, jnp.max(st, axis=0, keepdims=True))
        alpha = jnp.exp(m_prev - m_new)
        p = jnp.exp(st - m_new).astype(BF16)
        acc_ref[...] = alpha * acc_ref[...] + jnp.dot(vtb, p, preferred_element_type=F32)
        m_ref[...] = m_new

    def k_lat(blk):
        return kl_ref[pl.ds(pl.multiple_of(blk * ATTN_TK, ATTN_TK), ATTN_TK), :]

    def vt_lat(blk):
        return vtl_ref[:, pl.ds(pl.multiple_of(blk * ATTN_TK, ATTN_TK), ATTN_TK)]

    scores(0, kc_ref[...])
    if not n_lat_blocks:
        absorb(0, vtc_ref[...])
    else:
        scores(1, k_lat(0))
        absorb(0, vtc_ref[...])
        n_pairs = (n_lat_blocks - 1) // 2

        def body(i, carry):
            scores(0, k_lat(2 * i + 1))
            absorb(1, vt_lat(2 * i))
            scores(1, k_lat(2 * i + 2))
            absorb(0, vt_lat(2 * i + 1))
            return carry
        lax.fori_loop(0, n_pairs, body, 0)
        done = 2 * n_pairs
        if n_lat_blocks - done == 2:
            scores(0, k_lat(done + 1))
            absorb(1, vt_lat(done))
            absorb(0, vt_lat(done + 1))
        else:
            absorb(1, vt_lat(done))

    lp = lp_ref[...]
    lam = (jnp.exp(jnp.sum(lp[0:1] * lp[1:2], axis=1, keepdims=True))
           - jnp.exp(jnp.sum(lp[2:3] * lp[3:4], axis=1, keepdims=True)) + lam_init)
    acc = acc_ref[...]
    o1 = acc[0:LANES, :tq] / acc[LANES:LANES + 1, :tq]
    o2 = acc[0:LANES, tq:] / acc[LANES:LANES + 1, tq:]
    ot = o1 - lam * o2
    ot = ot * lax.rsqrt(jnp.mean(ot * ot, axis=0, keepdims=True) + EPS) * g_ref[...][:, 0:1] * (1.0 - lam_init)
    o_ref[...] = ot.T.astype(o_ref.dtype)


def _diff_attention(dq, dk, vt, lp, g, lam_init, n_batch, seq, ctx_len, latent):
    ctx_blk0 = (n_batch * seq) // ctx_len
    vrows = LANES + ATTN_ONES
    kc_spec = pl.BlockSpec((ctx_len, LANES), lambda b, h, i: (ctx_blk0 + b, h))
    vtc_spec = pl.BlockSpec((vrows, ctx_len), lambda b, h, i: (h, ctx_blk0 + b))
    const = lambda b, h, i: (0, 0)
    if latent:
        tq = ATTN_TQ
        n_q = seq // tq
        q_map = lambda b, h, i: (b * n_q + i, h)
        in_specs = [pl.BlockSpec((tq, LANES), q_map), kc_spec, vtc_spec,
                    pl.BlockSpec((seq, LANES), lambda b, h, i: (b, h)),
                    pl.BlockSpec((vrows, seq), lambda b, h, i: (h, b))]
        args = (dq, dk, vt, dk, vt)
        n_rows, o_map, n_lat_blocks = n_batch * seq, q_map, seq // ATTN_TK
    else:
        tq, n_q = ctx_len, 1
        in_specs = [kc_spec, kc_spec, vtc_spec]
        args = (dq, dk, vt)
        n_rows, o_map, n_lat_blocks = n_batch * ctx_len, (lambda b, h, i: (b, h)), 0
    in_specs += [pl.BlockSpec(lp.shape, const), pl.BlockSpec((LANES, LANES), const)]
    return pl.pallas_call(
        functools.partial(_attn_kernel, n_lat_blocks, lam_init),
        out_shape=jax.ShapeDtypeStruct((n_rows, N_HEADS * LANES), BF16),
        grid=(n_batch, N_HEADS, n_q),
        in_specs=in_specs,
        out_specs=pl.BlockSpec((tq, LANES), o_map),
        scratch_shapes=[pltpu.VMEM((1, 2 * tq), F32), pltpu.VMEM((vrows, 2 * tq), F32),
                        pltpu.VMEM((2, max(ATTN_TK, ctx_len), 2 * tq), F32)],
        compiler_params=_cparams("parallel", "parallel", "parallel"),
        name="diff_attention_lat" if latent else "diff_attention_ctx",
    )(*args, lp, g)


def _merge_kernel(tiles_per_batch, n_batch, n_a, orf_ref, orb_ref, ogf_ref, ogb_ref, rg_ref, gr_ref, *refs):
    n_x = 2 if n_a else 1
    mod_ref, gpost_ref, gpre_ref, glag_ref, wout_ref, wrt_ref, xo_ref, hp_ref, sc_ref = refs[2 * n_x:]
    dif = _pick_rows(n_a, *refs[:2]) if n_a else refs[0][...]
    x = _pick_rows(n_a, *refs[2:4]) if n_a else refs[1][...]
    d = x.shape[-1]
    r = jnp.minimum(pl.program_id(0) // tiles_per_batch, n_batch)
    gate1 = mod_ref[pl.ds(r, 1), 2 * d:3 * d]
    shift2 = mod_ref[pl.ds(r, 1), 3 * d:4 * d]
    scale2 = mod_ref[pl.ds(r, 1), 4 * d:5 * d]
    gi = lax.broadcasted_iota(I32, (256, 256), 0) // RET_DV
    gj = lax.broadcasted_iota(I32, (256, 256), 1) // RET_DV
    group_mean = jnp.where(gi == gj, 1.0 / RET_DV, 0.0).astype(BF16)

    def head_norm(o):
        ms = jnp.dot((o * o).astype(BF16), group_mean, preferred_element_type=F32)
        return o * lax.rsqrt(ms + EPS)

    ret = head_norm(orf_ref[...] + orb_ref[...]) * _silu(rg_ref[...].astype(F32))
    gla = head_norm(ogf_ref[...] + ogb_ref[...]) * glag_ref[...] * _silu(gr_ref[...].astype(F32))
    m = (jnp.dot(ret.astype(BF16), wout_ref[0:256, :], preferred_element_type=F32)
         + jnp.dot(gla.astype(BF16), wout_ref[256:512, :], preferred_element_type=F32)
         + jnp.dot(dif, wout_ref[512:1024, :], preferred_element_type=F32))
    x_new = x + gate1 * (_rms(m) * gpost_ref[...])
    xo_ref[...] = x_new
    h2 = _rms(x_new) * gpre_ref[...] * (1.0 + scale2) + shift2
    _tile_major_store(hp_ref, _pack_halves(h2[:, :d // 2], h2[:, d // 2:]))
    def split(a):
        hi = a.astype(BF16)
        return hi, (a - hi.astype(F32)).astype(BF16)

    h_hi, h_lo = split(h2)
    w_hi, w_lo = split(wrt_ref[...])
    nt = functools.partial(lax.dot_general, dimension_numbers=NT_DIMS, preferred_element_type=F32)
    logits = nt(w_hi, h_hi) + (nt(w_hi, h_lo) + nt(w_lo, h_hi))
    sc_ref[...] = 1.0 / (1.0 + jnp.exp(-logits))


def _merge(o_rf, o_rb, o_gf, o_gb, ret, gla, dif, xs, mod, g_post, g_pre_ffn, gla_g, w_out, w_rt,
           n_rows, n_batch, seq):
    tm = TOKEN_TILE
    row = lambda i: (i, 0)
    gate_col = lambda i: (i, 2)
    const = lambda i: (0, 0)
    if isinstance(xs, tuple):
        d = xs[0].shape[1]
        n_a, x_specs = _split_rows(*xs, tm, d)
        _, dif_specs = _split_rows(*dif, tm, 512)
    else:
        d = xs.shape[1]
        n_a, x_specs, dif_specs = 0, [pl.BlockSpec((tm, d), row)], [pl.BlockSpec((tm, 512), row)]
        xs, dif = (xs,), (dif,)
    return pl.pallas_call(
        functools.partial(_merge_kernel, seq // tm, n_batch, n_a),
        out_shape=(jax.ShapeDtypeStruct((n_rows, d), F32), jax.ShapeDtypeStruct((n_rows * HP_CHUNKS, LANES), U32),
                   jax.ShapeDtypeStruct((N_EXPERTS, n_rows), F32)),
        grid=(n_rows // tm,),
        in_specs=[
            pl.BlockSpec((tm, 256), row), pl.BlockSpec((tm, 256), row),
            pl.BlockSpec((tm, 256), row), pl.BlockSpec((tm, 256), row),
            pl.BlockSpec((tm, 256), gate_col), pl.BlockSpec((tm, 256), gate_col), *dif_specs, *x_specs,
            pl.BlockSpec(mod.shape, const), pl.BlockSpec((1, d), const), pl.BlockSpec((1, d), const),
            pl.BlockSpec((1, 256), const), pl.BlockSpec(w_out.shape, const), pl.BlockSpec(w_rt.shape, const),
        ],
        out_specs=(pl.BlockSpec((tm, d), row), pl.BlockSpec((tm * HP_CHUNKS, LANES), row),
                   pl.BlockSpec((N_EXPERTS, tm), lambda i: (0, i))),
        compiler_params=_cparams("parallel"),
        name="merge",
    )(o_rf, o_rb, o_gf, o_gb, ret, gla, *dif, *xs, mod, g_post, g_pre_ffn, gla_g, w_out, w_rt)


def _route_kernel(sc_ref, bias_ref, posa_ref, wts_ref, pos_ref, meta_ref):
    tt = sc_ref.shape[-1]
    scores = sc_ref[...]
    sel = scores + bias_ref[...][:, 0:1]
    sub = lax.broadcasted_iota(I32, (GROUP_SIZE, tt), 0)
    neg = -jnp.inf
    gscore = []
    for g in range(N_GROUPS):
        xg = sel[g * GROUP_SIZE:(g + 1) * GROUP_SIZE]
        m1 = jnp.max(xg, axis=0, keepdims=True)
        i1 = jnp.min(jnp.where(xg == m1, sub, GROUP_SIZE), axis=0, keepdims=True)
        m2 = jnp.max(jnp.where(sub == i1, neg, xg), axis=0, keepdims=True)
        gscore.append(m1 + m2)
    rows = []
    for g in range(N_GROUPS):
        rank = jnp.zeros((1, tt), I32)
        for o in range(N_GROUPS):
            if o == g:
                continue
            ahead = (gscore[o] >= gscore[g]) if o < g else (gscore[o] > gscore[g])
            rank = rank + ahead.astype(I32)
        rows.append(jnp.where(rank < TOPK_GROUPS, sel[g * GROUP_SIZE:(g + 1) * GROUP_SIZE], neg))
    masked = jnp.concatenate(rows, axis=0)
    eio = lax.broadcasted_iota(I32, (N_EXPERTS, tt), 0)
    member = jnp.zeros((N_EXPERTS, tt), F32)
    idxs, ws = [], []
    for _ in range(TOP_K):
        m = jnp.max(masked, axis=0, keepdims=True)
        i = jnp.min(jnp.where(masked == m, eio, N_EXPERTS), axis=0, keepdims=True)
        hit = eio == i
        idxs.append(i)
        ws.append(jnp.sum(jnp.where(hit, scores, 0.0), axis=0, keepdims=True))
        member = jnp.where(hit, 1.0, member)
        masked = jnp.where(hit, neg, masked)
    wsum = ws[0]
    for w in ws[1:]:
        wsum = wsum + w
    ti = lax.broadcasted_iota(I32, (tt, tt), 0)
    tj = lax.broadcasted_iota(I32, (tt, tt), 1)
    before = jnp.where(ti < tj, 1.0, 0.0).astype(BF16)
    rank_in_e = jnp.dot(member.astype(BF16), before, preferred_element_type=F32)
    cnt = jnp.sum(member, axis=1, keepdims=True)
    padded = jnp.floor((cnt + (SUBLANES - 1)) * (1.0 / SUBLANES)) * SUBLANES
    ei = lax.broadcasted_iota(I32, (N_EXPERTS, N_EXPERTS), 0)
    ej = lax.broadcasted_iota(I32, (N_EXPERTS, N_EXPERTS), 1)
    lower = jnp.where(ej < ei, 1.0, 0.0)
    off = jnp.dot(lower, jnp.broadcast_to(padded, (N_EXPERTS, LANES)), precision=HIGHEST,
                  preferred_element_type=F32)
    slot = rank_in_e + off[:, 0:1]
    zrow_i = jnp.zeros((SUBLANES - TOP_K, tt), I32)
    zrow_f = jnp.zeros((SUBLANES - TOP_K, tt), F32)
    pos = [jnp.sum(jnp.where(eio == i, slot, 0.0), axis=0, keepdims=True).astype(I32) for i in idxs]
    pos = jnp.concatenate(pos + [zrow_i], axis=0)
    wts_ref[...] = jnp.concatenate([w / wsum * ROUTED_SCALE for w in ws] + [zrow_f], axis=0)
    pos_ref[...] = pos
    posa_ref[...] = _tile_major_addr(pos, HP_CHUNKS)
    meta_ref[0] = jnp.concatenate([jnp.broadcast_to(cnt, (N_EXPERTS, LANES)), off], axis=1).astype(I32)


def _route(scores_t, bias, tt, row0, n):
    tile0 = row0 // tt
    tok = lambda i: (0, i)
    return pl.pallas_call(
        _route_kernel,
        out_shape=(jax.ShapeDtypeStruct((SUBLANES, n), I32), jax.ShapeDtypeStruct((SUBLANES, n), F32),
                   jax.ShapeDtypeStruct((SUBLANES, n), I32),
                   jax.ShapeDtypeStruct((n // tt, N_EXPERTS, 2 * LANES), I32)),
        grid=(n // tt,),
        in_specs=[pl.BlockSpec((N_EXPERTS, tt), lambda i: (0, i + tile0)),
                  pl.BlockSpec((N_EXPERTS, LANES), lambda i: (0, 0))],
        out_specs=(pl.BlockSpec((SUBLANES, tt), tok), pl.BlockSpec((SUBLANES, tt), tok),
                   pl.BlockSpec((SUBLANES, tt), tok),
                   pl.BlockSpec((1, N_EXPERTS, 2 * LANES), lambda i: (i, 0, 0))),
        compiler_params=_cparams("parallel"),
        name="route",
    )(scores_t, jnp.broadcast_to(bias[:, None], (N_EXPERTS, LANES)))


def _moe_kernel(group, n_first, has_prev, pos_ref, posa_ref, wts_ref, meta_ref, hp_ref, weg_ref, weu_ref, wed_ref,
                *refs):
    o_ref, xg_ref, y_ref, rt_ref = refs[1:] if has_prev else refs
    d = weg_ref.shape[1]
    half = d // 2
    tt = hp_ref.shape[0] // HP_CHUNKS
    eg = pl.program_id(1)

    @pl.when((pl.program_id(0) == 0) & (eg == 0))
    def _():
        def zero(i, carry):
            for u in range(SUBLANES):
                rt_ref[i * SUBLANES + u] = 0
            return carry
        lax.fori_loop(0, rt_ref.shape[0] // SUBLANES, zero, 0)

    @pl.when(eg == 0)
    def _():
        def scatter(g, carry):
            base = g * (SUBLANES * HP_CHUNKS)
            for u in range(SUBLANES):
                for k in range(TOP_K):
                    rt_ref[pos_ref[(g * SUBLANES + u) * SUBLANES + k]] = base + u
            return carry
        lax.fori_loop(0, tt // SUBLANES, scatter, 0)

    sub = lax.broadcasted_iota(I32, (SUBLANES, LANES), 0)

    def expert_block(ge, buf, base, n_valid):
        base = pl.multiple_of(base, SUBLANES)
        for i in range(MOE_BLOCK):
            xg_ref[buf, pl.ds(_tile_major_addr(i, HP_CHUNKS), HP_CHUNKS, stride=SUBLANES), :] = (
                hp_ref[pl.ds(rt_ref[base + i], HP_CHUNKS, stride=SUBLANES), :])
        yield
        halves = [_unpack_halves(col) for col in _tile_major_columns(xg_ref, MOE_BLOCK, HP_CHUNKS, (buf,))]
        xa = jnp.concatenate([h[0] for h in halves], axis=1).astype(BF16)
        xb = jnp.concatenate([h[1] for h in halves], axis=1).astype(BF16)
        hg = (jnp.dot(xa, weg_ref[ge, 0:half, :], preferred_element_type=F32)
              + jnp.dot(xb, weg_ref[ge, half:d, :], preferred_element_type=F32))
        hu = (jnp.dot(xa, weu_ref[ge, 0:half, :], preferred_element_type=F32)
              + jnp.dot(xb, weu_ref[ge, half:d, :], preferred_element_type=F32))
        yield
        y = jnp.dot((_silu(hg) * hu).astype(BF16), wed_ref[ge], preferred_element_type=F32)
        yield
        packed = _pack_halves(y[:, :half], y[:, half:])
        row0 = pl.multiple_of(base * HP_CHUNKS, SUBLANES * HP_CHUNKS)
        for g in range(MOE_BLOCK // SUBLANES):
            keep = sub < n_valid - g * SUBLANES
            for c in range(HP_CHUNKS):
                pltpu.store(y_ref.at[pl.ds(row0 + (g * HP_CHUNKS + c) * SUBLANES, SUBLANES), :],
                            packed[g * SUBLANES:(g + 1) * SUBLANES, c * LANES:(c + 1) * LANES], mask=keep)

    cnts = [meta_ref[0, eg * group + ge] for ge in range(group)]
    offs = [meta_ref[1, eg * group + ge] for ge in range(group)]
    chains = [expert_block(ge, ge * n_first + j, offs[ge] + j * MOE_BLOCK, cnts[ge] - j * MOE_BLOCK)
              for ge in range(group) for j in range(n_first)]
    _round_robin(chains)
    for ge in range(group):
        def more(j, carry, ge=ge):
            for _ in expert_block(ge, ge * n_first, offs[ge] + j * MOE_BLOCK, cnts[ge] - j * MOE_BLOCK):
                pass
            return carry
        lax.fori_loop(n_first, (cnts[ge] + (MOE_BLOCK - 1)) // MOE_BLOCK, more, 0)

    @pl.when(eg == N_EXPERTS // group - 1)
    def _():
        def tokens(g, carry):
            for u in range(SUBLANES):
                e0 = (g * SUBLANES + u) * SUBLANES
                acc_a = jnp.zeros((HP_CHUNKS, LANES), F32)
                acc_b = jnp.zeros((HP_CHUNKS, LANES), F32)
                for k in range(TOP_K):
                    ya, yb = _unpack_halves(y_ref[pl.ds(posa_ref[e0 + k], HP_CHUNKS, stride=SUBLANES), :])
                    w = wts_ref[e0 + k]
                    acc_a = acc_a + w * ya
                    acc_b = acc_b + w * yb
                out0 = pl.multiple_of(g * (2 * HP_CHUNKS * SUBLANES), 2 * HP_CHUNKS * SUBLANES) + u
                o_ref[pl.ds(out0, HP_CHUNKS, stride=SUBLANES), :] = acc_a
                o_ref[pl.ds(out0 + HP_CHUNKS * SUBLANES, HP_CHUNKS, stride=SUBLANES), :] = acc_b
            return carry

        lax.fori_loop(0, tt // SUBLANES, tokens, 0)


def _moe_segment(prev, scores_t, hp, router_bias, expert_w, layer, row0, n, tt):
    w_eg, w_eu, w_ed = expert_w
    d, de = w_eg.shape[1:]
    half = d // 2
    n_all = hp.shape[0] // HP_CHUNKS
    n_first = max(1, -(-tt * TOP_K // (N_EXPERTS * MOE_BLOCK)))
    group = max(1, MOE_CHAINS // n_first)
    group0 = layer * (N_EXPERTS // group)
    n_tiles, tile0 = n // tt, row0 // tt
    n_slots = -(-(tt * TOP_K + N_EXPERTS * (SUBLANES - 1) + n_first * MOE_BLOCK) // (SUBLANES * LANES)) * SUBLANES * LANES
    posa, wts, pos, meta = _route(scores_t, router_bias, tt, row0, n)
    cnt, off = meta[:, :, 0], meta[:, :, LANES]

    def per_token(a):
        return a.T.reshape(-1)

    meta_s = jnp.zeros((n_tiles, SUBLANES, LANES), I32)
    meta_s = meta_s.at[:, 0, :N_EXPERTS].set(cnt).at[:, 1, :N_EXPERTS].set(off).reshape(-1, LANES)

    tile = lambda i, e: (i + tile0, 0)
    expert = lambda i, e: (group0 + e, 0, 0)
    smem = functools.partial(pl.BlockSpec, memory_space=pltpu.SMEM)
    once = pl.Buffered(1)
    flat = smem((SUBLANES * tt,), lambda i, e: (i,))
    in_specs = [
        flat, flat, flat, smem((SUBLANES, LANES), lambda i, e: (i, 0)),
        pl.BlockSpec((tt * HP_CHUNKS, LANES), tile),
        pl.BlockSpec((group, d, de), expert), pl.BlockSpec((group, d, de), expert),
        pl.BlockSpec((group, de, d), expert),
    ]
    args = [per_token(pos), per_token(posa), per_token(wts), meta_s, hp, w_eg, w_eu, w_ed]
    aliases = {}
    if prev is not None:
        in_specs.append(pl.BlockSpec(memory_space=pl.ANY))
        args.append(prev)
        aliases = {len(args) - 1: 0}
    return pl.pallas_call(
        functools.partial(_moe_kernel, group, n_first, prev is not None),
        out_shape=jax.ShapeDtypeStruct((n_all * 2 * HP_CHUNKS, LANES), F32),
        grid=(n_tiles, N_EXPERTS // group),
        in_specs=in_specs,
        out_specs=pl.BlockSpec((tt * 2 * HP_CHUNKS, LANES), tile, pipeline_mode=once),
        scratch_shapes=[pltpu.VMEM((group * n_first, MOE_BLOCK * HP_CHUNKS, LANES), U32),
                        pltpu.VMEM((n_slots * HP_CHUNKS, LANES), U32),
                        pltpu.SMEM((n_slots,), I32)],
        input_output_aliases=aliases,
        compiler_params=_cparams("arbitrary", "arbitrary"),
        name="moe",
    )(*args)


def _moe_final_kernel(tiles_per_batch, n_batch, routed_ref, hp_ref, x_ref, mod_ref, gpost_ref,
                      wsg_ref, wsu_ref, wsd_ref, o_ref):
    tm, d = x_ref.shape
    half = d // 2
    r = jnp.minimum(pl.program_id(0) // tiles_per_batch, n_batch)
    gate2 = mod_ref[pl.ds(r, 1), 5 * d:6 * d]
    halves = [_unpack_halves(col) for col in _tile_major_columns(hp_ref, tm, HP_CHUNKS)]
    xa = jnp.concatenate([h[0] for h in halves], axis=1).astype(BF16)
    xb = jnp.concatenate([h[1] for h in halves], axis=1).astype(BF16)
    hg = (jnp.dot(xa, wsg_ref[0:half, :], preferred_element_type=F32)
          + jnp.dot(xb, wsg_ref[half:d, :], preferred_element_type=F32))
    hu = (jnp.dot(xa, wsu_ref[0:half, :], preferred_element_type=F32)
          + jnp.dot(xb, wsu_ref[half:d, :], preferred_element_type=F32))
    routed = jnp.concatenate(_tile_major_columns(routed_ref, tm, 2 * HP_CHUNKS), axis=1)
    f = jnp.dot((_silu(hg) * hu).astype(BF16), wsd_ref[...], preferred_element_type=F32) + routed
    o_ref[...] = x_ref[...] + gate2 * (_rms(f) * gpost_ref[...])


def _moe_final(routed, hp, xs, mod, g_post, w_sg, w_su, w_sd, n_batch, seq):
    n, d = xs.shape
    tm = TOKEN_TILE
    row = lambda i: (i, 0)
    const = lambda i: (0, 0)
    return pl.pallas_call(
        functools.partial(_moe_final_kernel, seq // tm, n_batch),
        out_shape=jax.ShapeDtypeStruct((n, d), F32),
        grid=(n // tm,),
        in_specs=[pl.BlockSpec((tm * 2 * HP_CHUNKS, LANES), row), pl.BlockSpec((tm * HP_CHUNKS, LANES), row),
                  pl.BlockSpec((tm, d), row),
                  pl.BlockSpec(mod.shape, const), pl.BlockSpec((1, d), const),
                  pl.BlockSpec(w_sg.shape, const), pl.BlockSpec(w_su.shape, const), pl.BlockSpec(w_sd.shape, const)],
        out_specs=pl.BlockSpec((tm, d), row),
        compiler_params=_cparams("parallel"),
        name="moe_final",
    )(routed, hp, xs, mod, g_post, w_sg, w_su, w_sd)


def _moe_layer(scores_t, hp, xs, mod, router_bias, g_post, expert_w, shared_w, layer, n_lat, n_batch, seq):
    pick = lambda rows: next(t for t in MOE_TILES if rows % t == 0)
    routed = _moe_segment(None, scores_t, hp, router_bias, expert_w, layer, 0, n_lat, pick(n_lat))
    n_ctx = hp.shape[0] // HP_CHUNKS - n_lat
    if n_ctx:
        tt = next(t for t in MOE_TILES if n_ctx % t == 0 and n_lat % t == 0)
        routed = _moe_segment(routed, scores_t, hp, router_bias, expert_w, layer, n_lat, n_ctx, tt)
    return _moe_final(routed, hp, xs, mod, g_post, *shared_w, n_batch, seq)


def kernel(x, c, ctx, c_ctx, w_mod, b_mod, g_pre_mix, g_post_mix, g_pre_ffn, g_post_ffn, w_in, w_out,
           ret_decay_logit, gla_w_gate, gla_b_gate, gla_norm_g, diff_lambda, diff_norm_g,
           w_router, router_bias, w_exp_gate, w_exp_up, w_exp_down, w_sh_gate, w_sh_up, w_sh_down):
    n_batch, seq, d = x.shape
    ctx_len = ctx.shape[1]
    depth = w_mod.shape[0]
    n_lat = n_batch * seq
    assert seq % TOKEN_TILE == 0 and (n_batch * ctx_len) % TOKEN_TILE == 0 and n_batch < SUBLANES
    scan_rows = CHUNK * SCAN_CHUNKS
    assert seq % ATTN_TK == 0 and seq % scan_rows == 0 and ctx_len % scan_rows == 0 and n_lat % ctx_len == 0

    xs = (x.reshape(n_lat, d), ctx.reshape(n_batch * ctx_len, d))
    cond = jnp.zeros((SUBLANES, d), F32).at[:n_batch].set(c).at[n_batch].set(c_ctx)
    mods = _modulation(cond, w_mod, b_mod)
    tables = _rope_tables(seq, RET_DK, TOKEN_TILE) + _rope_tables(seq, DIFF_DH, TOKEN_TILE)
    lr0 = N_HEADS * (2 * RET_DK + 2 * RET_DV + 2 * GLA_DK + 2 * GLA_DV)
    row = lambda a: a[None, :]
    expert_w = tuple(w.astype(BF16).reshape((depth * N_EXPERTS,) + w.shape[2:])
                     for w in (w_exp_gate, w_exp_up, w_exp_down))

    for layer in range(depth):
        need_ctx = layer < depth - 1
        lam_init = 0.8 - 0.6 * math.exp(-0.3 * layer)
        mod = mods[layer]
        wl = w_in[layer]
        w_r = jnp.concatenate([wl[:, :lr0], wl[:, lr0:lr0 + GLA_LOWRANK],
                               jnp.zeros((d, LANES - GLA_LOWRANK), F32), wl[:, lr0 + GLA_LOWRANK:]],
                              axis=1).astype(BF16)
        ret, gla, glr, dq, dk, vt = _in_projection(xs, mod, row(g_pre_mix[layer]), w_r, tables, n_batch, seq)

        o_rf, o_gf, o_rb, o_gb = _scan(ret, gla, glr, _scan_tables(ret_decay_logit[layer]), gla_w_gate[layer],
                                       gla_b_gate[layer], n_batch, seq, ctx_len)

        lp = jnp.zeros((SUBLANES, LANES), F32).at[:4, :DIFF_DH].set(diff_lambda[layer])
        g_col = jnp.broadcast_to(diff_norm_g[layer][:, None], (LANES, LANES))
        attn = functools.partial(_diff_attention, dq, dk, vt, lp, g_col, lam_init, n_batch, seq, ctx_len)
        dif = attn(True)
        n_rows = n_lat
        if need_ctx:
            n_rows = n_lat + n_batch * ctx_len
            if isinstance(xs, tuple):
                dif = (dif, attn(False))
            else:
                dif = jnp.concatenate([dif, attn(False)], axis=0)
        elif isinstance(xs, tuple):
            xs = xs[0]

        xs, hp, scores_t = _merge(o_rf, o_rb, o_gf, o_gb, ret, gla, dif, xs, mod, row(g_post_mix[layer]),
                                  row(g_pre_ffn[layer]), row(jnp.tile(gla_norm_g[layer], N_HEADS)),
                                  w_out[layer].astype(BF16), w_router[layer].T, n_rows, n_batch, seq)
        shared_w = tuple(w[layer].astype(BF16) for w in (w_sh_gate, w_sh_up, w_sh_down))
        xs = _moe_layer(scores_t, hp, xs, mod, router_bias[layer], row(g_post_ffn[layer]), expert_w, shared_w,
                        layer, n_lat, n_batch, seq)
    return xs[:n_lat].reshape(n_batch, seq, d)
```

```python
import functools
import math

import jax
import jax.numpy as jnp
import numpy as np
from jax import lax
from jax.experimental import pallas as pl
from jax.experimental.pallas import tpu as pltpu

F32 = jnp.float32
BF16 = jnp.bfloat16
I32 = jnp.int32
U32 = jnp.uint32

GRID_W = 64
CHUNK = 128
N_HEADS = 4
RET_DK, RET_DV = 32, 64
GLA_DK, GLA_DV = 32, 64
GLA_LOWRANK = 16
GLA_TAU = 16.0
DIFF_DH = 64
ROPE_BASE = 10000.0
N_EXPERTS = 64
TOP_K = 6
N_GROUPS = 8
TOPK_GROUPS = 4
GROUP_SIZE = N_EXPERTS // N_GROUPS
ROUTED_SCALE = 2.5
EPS = 1e-6
GLA_SUB = 16
SCAN_CHUNKS = 2

LANES = 128
SUBLANES = 8
VMEM_LIMIT_BYTES = 56 * 1024 * 1024

TOKEN_TILE = 512
ATTN_TQ = 512
ATTN_TK = 512
ATTN_ONES = 16
MOE_TILES = (2048, 1024, 512)
MOE_ROWS_IN_FLIGHT = 512
MOE_MAX_GROUP = 4
HP_CHUNKS = 4

HIGHEST = lax.Precision.HIGHEST
NT_DIMS = (((1,), (1,)), ((), ()))
TN_DIMS = (((0,), (0,)), ((), ()))


def _cparams(*sem):
    return pltpu.CompilerParams(dimension_semantics=sem, vmem_limit_bytes=VMEM_LIMIT_BYTES)


def _log_sigmoid(x):
    return jnp.minimum(x, 0.0) - jnp.log(1.0 + jnp.exp(-jnp.abs(x)))


def _silu(x):
    return x * (1.0 / (1.0 + jnp.exp(-x)))


def _rms(x):
    return x * lax.rsqrt(jnp.mean(x * x, axis=-1, keepdims=True) + EPS)


def _pack_halves(a, b):
    ua = lax.bitcast_convert_type(a.astype(BF16).astype(F32), U32)
    ub = lax.bitcast_convert_type(b.astype(BF16).astype(F32), U32)
    return (ua & jnp.uint32(0xFFFF0000)) | (ub >> 16)


def _unpack_halves(w):
    a = lax.bitcast_convert_type(w & jnp.uint32(0xFFFF0000), F32)
    b = lax.bitcast_convert_type(w << 16, F32)
    return a, b


def _tile_major_store(ref, v, lead=()):
    k = v.shape[1] // LANES
    for g in range(v.shape[0] // SUBLANES):
        for c in range(k):
            blk = g * k + c
            ref[lead + (slice(blk * SUBLANES, (blk + 1) * SUBLANES), slice(None))] = (
                v[g * SUBLANES:(g + 1) * SUBLANES, c * LANES:(c + 1) * LANES])


def _tile_major_columns(ref, rows, k, lead=()):
    return [jnp.concatenate([ref[lead + (slice((g * k + c) * SUBLANES, (g * k + c + 1) * SUBLANES), slice(None))]
                             for g in range(rows // SUBLANES)], axis=0) for c in range(k)]


def _tile_major_addr(t, k):
    return (t >> 3) * (SUBLANES * k) + (t & (SUBLANES - 1))


def _mod_kernel(cond_ref, w_ref, b_ref, o_ref):
    a = _silu(cond_ref[...])
    o_ref[0] = jnp.dot(a, w_ref[0], precision=HIGHEST, preferred_element_type=F32) + b_ref[0]


def _modulation(cond, w_mod, b_mod):
    n_layers, d, d6 = w_mod.shape
    tn = 1024
    return pl.pallas_call(
        _mod_kernel,
        out_shape=jax.ShapeDtypeStruct((n_layers, SUBLANES, d6), F32),
        grid=(n_layers, d6 // tn),
        in_specs=[
            pl.BlockSpec((SUBLANES, d), lambda l, j: (0, 0)),
            pl.BlockSpec((1, d, tn), lambda l, j: (l, 0, j)),
            pl.BlockSpec((1, 1, tn), lambda l, j: (l, 0, j)),
        ],
        out_specs=pl.BlockSpec((1, SUBLANES, tn), lambda l, j: (l, 0, j)),
        compiler_params=_cparams("parallel", "parallel"),
        name="modulation",
    )(cond, w_mod, b_mod.reshape(n_layers, 1, d6))


def _rope(x, cos, sin, quarter):
    lane = lax.broadcasted_iota(I32, (1, LANES), 1)
    first = (lane % (2 * quarter)) < quarter
    outs = []
    for c in range(x.shape[-1] // LANES):
        xc = x[:, c * LANES:(c + 1) * LANES]
        partner = jnp.where(first, pltpu.roll(xc, LANES - quarter, 1), pltpu.roll(xc, quarter, 1))
        outs.append(xc * cos + partner * sin)
    return outs[0] if len(outs) == 1 else jnp.concatenate(outs, axis=-1)


def _split_rows(a, b, tm, width):
    n_a = a.shape[0] // tm
    return n_a, [pl.BlockSpec((tm, width), lambda i: (jnp.minimum(i, n_a - 1), 0)),
                 pl.BlockSpec((tm, width), lambda i: (jnp.maximum(i - n_a, 0), 0))]


def _pick_rows(n_a, a_ref, b_ref):
    return jnp.where(pl.program_id(0) < n_a, a_ref[...], b_ref[...])


def _inproj_kernel(tiles_per_batch, n_batch, n_a, *refs):
    n_x = 2 if n_a else 1
    (mod_ref, g_ref, w_ref, c32_ref, s32_ref, c64_ref, s64_ref,
     ret_ref, gla_ref, glr_ref, dq_ref, dk_ref, vt_ref) = refs[n_x:]
    x = _pick_rows(n_a, *refs[:2]) if n_a else refs[0][...]
    d = x.shape[-1]
    r = jnp.minimum(pl.program_id(0) // tiles_per_batch, n_batch)
    shift = mod_ref[pl.ds(r, 1), 0:d]
    scale = mod_ref[pl.ds(r, 1), d:2 * d]
    h = (_rms(x) * g_ref[...] * (1.0 + scale) + shift).astype(BF16)

    def proj(lo, hi):
        return jnp.dot(h, w_ref[:, lo:hi], preferred_element_type=F32)

    c32, s32 = c32_ref[...], s32_ref[...]
    c64, s64 = c64_ref[...], s64_ref[...]
    ret = proj(0, 768)
    ret_ref[:, 0:128] = _rope(ret[:, 0:128], c32, s32, RET_DK // 4).astype(BF16)
    ret_ref[:, 128:256] = (_rope(ret[:, 128:256], c32, s32, RET_DK // 4) * RET_DK ** -0.5).astype(BF16)
    ret_ref[:, 256:768] = ret[:, 256:768].astype(BF16)
    gla = proj(768, 1536)
    gla_ref[:, 0:128] = (gla[:, 0:128] * GLA_DK ** -0.5).astype(BF16)
    gla_ref[:, 128:768] = gla[:, 128:768].astype(BF16)
    glr_ref[...] = proj(1536, 1664)
    dq_ref[...] = (_rope(proj(1664, 2176), c64, s64, DIFF_DH // 4) * DIFF_DH ** -0.5).astype(BF16)
    dk_ref[...] = _rope(proj(2176, 2688), c64, s64, DIFF_DH // 4).astype(BF16)
    dv = proj(2688, 3200)
    vrows = LANES + ATTN_ONES
    for hd in range(N_HEADS):
        vt_ref[hd * vrows:hd * vrows + LANES, :] = dv[:, hd * LANES:(hd + 1) * LANES].T.astype(BF16)
        vt_ref[hd * vrows + LANES:(hd + 1) * vrows, :] = jnp.ones((ATTN_ONES, dv.shape[0]), BF16)


def _in_projection(xs, mod, g_pre, w_r, tables, n_batch, seq):
    tm = TOKEN_TILE
    if isinstance(xs, tuple):
        d = xs[0].shape[1]
        n = xs[0].shape[0] + xs[1].shape[0]
        n_a, x_specs = _split_rows(*xs, tm, d)
    else:
        n, d = xs.shape
        n_a, x_specs, xs = 0, [pl.BlockSpec((tm, d), lambda i: (i, 0))], (xs,)
    tiles_per_batch = seq // tm
    n_lat_tiles = n_batch * tiles_per_batch
    c32, s32, c64, s64 = tables

    def tab_map(i):
        return (jnp.where(i < n_lat_tiles, i % tiles_per_batch, tiles_per_batch), 0)

    row = lambda i: (i, 0)
    const = lambda i: (0, 0)
    tab_spec = pl.BlockSpec((tm, LANES), tab_map)
    return pl.pallas_call(
        functools.partial(_inproj_kernel, tiles_per_batch, n_batch, n_a),
        out_shape=(
            jax.ShapeDtypeStruct((n, 768), BF16), jax.ShapeDtypeStruct((n, 768), BF16),
            jax.ShapeDtypeStruct((n, LANES), F32),
            jax.ShapeDtypeStruct((n, 512), BF16), jax.ShapeDtypeStruct((n, 512), BF16),
            jax.ShapeDtypeStruct((N_HEADS * (LANES + ATTN_ONES), n), BF16)),
        grid=(n // tm,),
        in_specs=x_specs + [
            pl.BlockSpec(mod.shape, const),
            pl.BlockSpec((1, d), const),
            pl.BlockSpec(w_r.shape, const),
            tab_spec, tab_spec, tab_spec, tab_spec,
        ],
        out_specs=(
            pl.BlockSpec((tm, 768), row), pl.BlockSpec((tm, 768), row), pl.BlockSpec((tm, LANES), row),
            pl.BlockSpec((tm, 512), row), pl.BlockSpec((tm, 512), row),
            pl.BlockSpec((N_HEADS * (LANES + ATTN_ONES), tm), lambda i: (0, i))),
        compiler_params=_cparams("parallel"),
        name="in_projection",
    )(*xs, mod, g_pre, w_r, c32, s32, c64, s64)


def _rope_tables(seq, head_dim, extra_rows):
    half, quarter = head_dim // 2, head_dim // 4
    freqs = (ROPE_BASE ** (-np.arange(quarter, dtype=np.float32) / quarter)).astype(np.float32)
    t = np.arange(seq)
    row = (t // GRID_W).astype(np.float32)
    col = (t % GRID_W).astype(np.float32)
    j = np.arange(LANES) % head_dim
    jj = j % half
    pos = np.where((j < half)[None, :], row[:, None], col[:, None])
    ang = (pos * freqs[jj % quarter][None, :]).astype(np.float32)
    cos = np.cos(ang)
    sin = np.sin(ang) * np.where(jj < quarter, -1.0, 1.0)[None, :]
    cos = np.concatenate([cos, np.ones((extra_rows, LANES))], axis=0).astype(np.float32)
    sin = np.concatenate([sin, np.zeros((extra_rows, LANES))], axis=0).astype(np.float32)
    return jnp.asarray(cos), jnp.asarray(sin)


def _head_stack(x, width):
    lane = lax.broadcasted_iota(I32, (1, x.shape[-1]), 1)
    zero = jnp.zeros_like(x)
    return jnp.concatenate([jnp.where(lane // width == h, x, zero) for h in range(N_HEADS)], axis=0)


def _head_select(x4, rows):
    lane = lax.broadcasted_iota(I32, (1, x4.shape[-1]), 1)
    out = jnp.zeros((rows, x4.shape[-1]), F32)
    for h in range(N_HEADS):
        out = out + jnp.where(lane // RET_DV == h, x4[h * rows:(h + 1) * rows], 0.0)
    return out


def _scan_tables_kernel(rl_lane_ref, rl_rows_ref, rl_col_ref, dmat_ref, xi_ref, zeta_ref, gchunk_ref):
    c = CHUNK
    idx = lax.broadcasted_iota(I32, (c, 1), 0).astype(F32)
    ri4 = lax.broadcasted_iota(I32, (N_HEADS * c, c), 0) % c
    ci4 = lax.broadcasted_iota(I32, (N_HEADS * c, c), 1)
    dist = jnp.abs(ri4 - ci4).astype(F32)
    for d in range(2):
        lg_lane = _log_sigmoid(rl_lane_ref[d])
        lg_rows = _log_sigmoid(rl_rows_ref[d])
        lg_col = _log_sigmoid(rl_col_ref[d])
        att4 = (ri4 <= ci4) if d else (ri4 >= ci4)
        dmat_ref[d] = jnp.where(att4, jnp.exp(dist * lg_rows), 0.0)
        xi_ref[d] = jnp.exp(((c - idx) if d else (idx + 1.0)) * lg_lane)
        zeta_ref[d] = jnp.exp((idx if d else (c - 1.0 - idx)) * lg_lane)
        g_chunk = jnp.exp(float(c) * lg_col)
        gchunk_ref[d] = jnp.concatenate([g_chunk, g_chunk], axis=1)


def _scan_tables(ret_logit):
    c = CHUNK
    rl_lane = jnp.repeat(ret_logit, RET_DK, axis=1)[:, None, :]
    rl_rows = jnp.broadcast_to(jnp.repeat(ret_logit, c, axis=1)[:, :, None], (2, N_HEADS * c, c))
    rl_col = jnp.broadcast_to(jnp.repeat(ret_logit, RET_DK, axis=1)[:, :, None], (2, LANES, LANES))
    return pl.pallas_call(
        _scan_tables_kernel,
        out_shape=(jax.ShapeDtypeStruct((2, N_HEADS * c, c), F32), jax.ShapeDtypeStruct((2, c, LANES), F32),
                   jax.ShapeDtypeStruct((2, c, LANES), F32), jax.ShapeDtypeStruct((2, LANES, 2 * LANES), F32)),
        name="scan_tables",
    )(rl_lane, rl_rows, rl_col)


def _state_block_mask():
    return (lax.broadcasted_iota(I32, (LANES, 2 * LANES), 0) // RET_DK
            == lax.broadcasted_iota(I32, (LANES, 2 * LANES), 1) // RET_DV)


def _ret_chain(ret_ref, r0, dmat, xi, zeta, g_chunk, out):
    c = CHUNK
    q = ret_ref[r0:r0 + c, 0:128]
    k = ret_ref[r0:r0 + c, 128:256]
    v = ret_ref[r0:r0 + c, 256:512]
    s = lax.dot_general(_head_stack(q, RET_DK), k, NT_DIMS, preferred_element_type=F32)
    kz = (k.astype(F32) * zeta).astype(BF16)
    u = lax.dot_general(kz, v, TN_DIMS, preferred_element_type=F32)
    yield
    o4 = jnp.dot((s * dmat).astype(BF16), v, preferred_element_type=F32)
    out.update(qx=(q.astype(F32) * xi).astype(BF16), u=jnp.where(_state_block_mask(), u, 0.0), g=g_chunk)
    yield
    out.update(intra=_head_select(o4, c))


def _state_chain(parts, o_ref, s_ref):
    c = CHUNK
    s = s_ref[...]
    for r0, p in parts:
        o_ref[r0:r0 + c, :] = p["intra"] + jnp.dot(p["qx"], s.astype(BF16), preferred_element_type=F32)
        s = s * p["g"] + p["u"]
        yield
    s_ref[...] = s


def _gla_chain(rev, gla_ref, glr_ref, r0, gw, gb, out):
    c = CHUNK
    ri = lax.broadcasted_iota(I32, (c, c), 0)
    ci = lax.broadcasted_iota(I32, (c, c), 1)
    attends = (ri <= ci) if rev else (ri >= ci)
    bd = _state_block_mask()
    gq = gla_ref[r0:r0 + c, 0:128].astype(F32)
    gk = gla_ref[r0:r0 + c, 128:256].astype(F32)
    gv = gla_ref[r0:r0 + c, 256:512]
    z = jnp.dot(glr_ref[r0:r0 + c, :], gw, precision=HIGHEST, preferred_element_type=F32) + gb
    yield
    la = _log_sigmoid(z) * (1.0 / GLA_TAU)
    if rev:
        first = (ri // GLA_SUB) * GLA_SUB + (GLA_SUB - 1)
        ref_sel = ci >= first
    else:
        first = (ri // GLA_SUB) * GLA_SUB
        ref_sel = ci <= first
    sel = jnp.concatenate([jnp.where(attends, 1.0, 0.0), jnp.where(ref_sel, 1.0, 0.0)], axis=0).astype(BF16)
    la_hi = la.astype(BF16)
    la_lo = (la - la_hi.astype(F32)).astype(BF16)
    sums = (jnp.dot(sel, la_hi, preferred_element_type=F32) + jnp.dot(sel, la_lo, preferred_element_type=F32))
    b, refrow = sums[0:c], sums[c:2 * c]
    yield
    qs = gq * jnp.exp(b - refrow)
    b_last = b[0:1] if rev else b[c - 1:c]
    kz = (gk * jnp.exp(b_last - b)).astype(BF16)
    u = lax.dot_general(kz, gv, TN_DIMS, preferred_element_type=F32)
    eye = lax.broadcasted_iota(I32, (LANES, LANES), 0) == lax.broadcasted_iota(I32, (LANES, LANES), 1)
    g_col = jnp.sum(jnp.where(eye, jnp.exp(b_last), 0.0), axis=1, keepdims=True)
    jcol = lax.broadcasted_iota(I32, (c, 1), 0)
    rr = lax.broadcasted_iota(I32, (N_HEADS * GLA_SUB, c), 0) % GLA_SUB
    cc = lax.broadcasted_iota(I32, (N_HEADS * GLA_SUB, c), 1)
    pieces = []
    for blk in range(c // GLA_SUB):
        lo = blk * GLA_SUB
        ref_b = refrow[lo:lo + 1]
        seen = (jcol >= lo) if rev else (jcol < lo + GLA_SUB)
        ks = (gk * jnp.exp(jnp.where(seen, ref_b - b, -jnp.inf))).astype(BF16)
        qz = _head_stack(qs[lo:lo + GLA_SUB], GLA_DK).astype(BF16)
        att = lax.dot_general(qz, ks, NT_DIMS, preferred_element_type=F32)
        ok = (cc >= rr + lo) if rev else (cc <= rr + lo)
        att = jnp.where(ok, att, 0.0).astype(BF16)
        pieces.append(_head_select(jnp.dot(att, gv, preferred_element_type=F32), GLA_SUB))
        if blk % 2:
            yield
    out.update(intra=jnp.concatenate(pieces, axis=0), qx=(gq * jnp.exp(b)).astype(BF16),
               u=jnp.where(bd, u, 0.0), g=g_col)


def _round_robin(chains):
    while chains:
        chains = [ch for ch in chains if next(ch, True) is None]


def _scan_kernel(ret_f, gla_f, glr_f, ret_b, gla_b, glr_b, dmat_ref, xi_ref, zeta_ref, gchunk_ref, gw_ref, gb_ref,
                 orf_ref, ogf_ref, orb_ref, ogb_ref, sr_ref, sg_ref):
    @pl.when(pl.program_id(1) == 0)
    def _():
        sr_ref[...] = jnp.zeros_like(sr_ref)
        sg_ref[...] = jnp.zeros_like(sg_ref)

    chains, state_chains = [], []
    for d, (ret_ref, gla_ref, glr_ref, ore_ref, ogl_ref) in enumerate(
            ((ret_f, gla_f, glr_f, orf_ref, ogf_ref), (ret_b, gla_b, glr_b, orb_ref, ogb_ref))):
        order = range(SCAN_CHUNKS - 1, -1, -1) if d else range(SCAN_CHUNKS)
        gla_parts, ret_parts = [], []
        for j in order:
            r0 = j * CHUNK
            gla_parts.append((r0, {}))
            ret_parts.append((r0, {}))
            chains.append(_gla_chain(bool(d), gla_ref, glr_ref, r0, gw_ref[d], gb_ref[d], gla_parts[-1][1]))
            chains.append(_ret_chain(ret_ref, r0, dmat_ref[d], xi_ref[d], zeta_ref[d], gchunk_ref[d],
                                     ret_parts[-1][1]))
        state_chains.append(_state_chain(gla_parts, ogl_ref, sg_ref.at[d]))
        state_chains.append(_state_chain(ret_parts, ore_ref, sr_ref.at[d]))
    _round_robin(chains)
    _round_robin(state_chains)


def _scan(ret, gla, glr, tables, gla_w, gla_b, n_batch, seq, ctx_len):
    n = ret.shape[0]
    c = CHUNK * SCAN_CHUNKS
    nc_ctx, nc_lat = ctx_len // c, seq // c
    n_steps = nc_ctx + nc_lat
    ctx_base = n_batch * nc_lat

    def fwd(b, s):
        return (jnp.where(s < nc_ctx, ctx_base + b * nc_ctx + s, b * nc_lat + (s - nc_ctx)), 0)

    def bwd(b, s):
        return (jnp.where(s < nc_ctx, ctx_base + b * nc_ctx + (nc_ctx - 1 - s), b * nc_lat + (n_steps - 1 - s)), 0)

    const3 = lambda b, s: (0, 0, 0)
    gw = jnp.zeros((2, LANES, LANES), F32).at[:, :GLA_LOWRANK].set(gla_w)
    gb = gla_b[:, None, :]
    chain_in = lambda m: [pl.BlockSpec((c, 768), m), pl.BlockSpec((c, 768), m), pl.BlockSpec((c, LANES), m)]
    o_sds = jax.ShapeDtypeStruct((n, 256), F32)
    return pl.pallas_call(
        _scan_kernel,
        out_shape=(o_sds, o_sds, o_sds, o_sds),
        grid=(n_batch, n_steps),
        in_specs=chain_in(fwd) + chain_in(bwd) + [pl.BlockSpec(t.shape, const3) for t in tables]
        + [pl.BlockSpec(gw.shape, const3), pl.BlockSpec(gb.shape, const3)],
        out_specs=(pl.BlockSpec((c, 256), fwd), pl.BlockSpec((c, 256), fwd),
                   pl.BlockSpec((c, 256), bwd), pl.BlockSpec((c, 256), bwd)),
        scratch_shapes=[pltpu.VMEM((2, LANES, 2 * LANES), F32), pltpu.VMEM((2, LANES, 2 * LANES), F32)],
        compiler_params=_cparams("arbitrary", "arbitrary"),
        name="scan",
    )(ret, gla, glr, ret, gla, glr, *tables, gw, gb)


def _attn_kernel(n_lat_blocks, lam_init, *refs):
    if n_lat_blocks:
        q_ref, kc_ref, vtc_ref, kl_ref, vtl_ref, lp_ref, g_ref, o_ref, m_ref, acc_ref, st_ref = refs
    else:
        q_ref, kc_ref, vtc_ref, lp_ref, g_ref, o_ref, m_ref, acc_ref, st_ref = refs
    tq = q_ref.shape[0]
    q = q_ref[...].astype(F32)
    lane = lax.broadcasted_iota(I32, (1, LANES), 1)
    qt = jnp.concatenate([jnp.where(lane < DIFF_DH, q, 0.0).T, jnp.where(lane >= DIFF_DH, q, 0.0).T],
                         axis=1).astype(BF16)
    m_ref[...] = jnp.full(m_ref.shape, -jnp.inf, F32)
    acc_ref[...] = jnp.zeros_like(acc_ref)

    def scores(slot, kb):
        st_ref[slot, 0:kb.shape[0], :] = jnp.dot(kb, qt, preferred_element_type=F32)

    def absorb(slot, vtb):
        st = st_ref[slot, 0:vtb.shape[1], :]
        m_prev = m_ref[...]
        m_new = jnp.maximum(m_prev, jnp.max(st, axis=0, keepdims=True))
        alpha = jnp.exp(m_prev - m_new)
        p = jnp.exp(st - m_new).astype(BF16)
        acc_ref[...] = alpha * acc_ref[...] + jnp.dot(vtb, p, preferred_element_type=F32)
        m_ref[...] = m_new

    def k_lat(blk):
        return kl_ref[pl.ds(pl.multiple_of(blk * ATTN_TK, ATTN_TK), ATTN_TK), :]

    def vt_lat(blk):
        return vtl_ref[:, pl.ds(pl.multiple_of(blk * ATTN_TK, ATTN_TK), ATTN_TK)]

    scores(0, kc_ref[...])
    if not n_lat_blocks:
        absorb(0, vtc_ref[...])
    else:
        scores(1, k_lat(0))
        absorb(0, vtc_ref[...])
        n_pairs = (n_lat_blocks - 1) // 2

        def body(i, carry):
            scores(0, k_lat(2 * i + 1))
            absorb(1, vt_lat(2 * i))
            scores(1, k_lat(2 * i + 2))
            absorb(0, vt_lat(2 * i + 1))
            return carry
        lax.fori_loop(0, n_pairs, body, 0)
        done = 2 * n_pairs
        if n_lat_blocks - done == 2:
            scores(0, k_lat(done + 1))
            absorb(1, vt_lat(done))
            absorb(0, vt_lat(done + 1))
        else:
            absorb(1, vt_lat(done))

    lp = lp_ref[...]
    lam = (jnp.exp(jnp.sum(lp[0:1] * lp[1:2], axis=1, keepdims=True))
           - jnp.exp(jnp.sum(lp[2:3] * lp[3:4], axis=1, keepdims=True)) + lam_init)
    acc = acc_ref[...]
    o1 = acc[0:LANES, :tq] / acc[LANES:LANES + 1, :tq]
    o2 = acc[0:LANES, tq:] / acc[LANES:LANES + 1, tq:]
    ot = o1 - lam * o2
    ot = ot * lax.rsqrt(jnp.mean(ot * ot, axis=0, keepdims=True) + EPS) * g_ref[...][:, 0:1] * (1.0 - lam_init)
    o_ref[...] = ot.T.astype(o_ref.dtype)


def _diff_attention(dq, dk, vt, lp, g, lam_init, n_batch, seq, ctx_len, latent):
    ctx_blk0 = (n_batch * seq) // ctx_len
    vrows = LANES + ATTN_ONES
    kc_spec = pl.BlockSpec((ctx_len, LANES), lambda b, h, i: (ctx_blk0 + b, h))
    vtc_spec = pl.BlockSpec((vrows, ctx_len), lambda b, h, i: (h, ctx_blk0 + b))
    const = lambda b, h, i: (0, 0)
    if latent:
        tq = ATTN_TQ
        n_q = seq // tq
        q_map = lambda b, h, i: (b * n_q + i, h)
        in_specs = [pl.BlockSpec((tq, LANES), q_map), kc_spec, vtc_spec,
                    pl.BlockSpec((seq, LANES), lambda b, h, i: (b, h)),
                    pl.BlockSpec((vrows, seq), lambda b, h, i: (h, b))]
        args = (dq, dk, vt, dk, vt)
        n_rows, o_map, n_lat_blocks = n_batch * seq, q_map, seq // ATTN_TK
    else:
        tq, n_q = ctx_len, 1
        in_specs = [kc_spec, kc_spec, vtc_spec]
        args = (dq, dk, vt)
        n_rows, o_map, n_lat_blocks = n_batch * ctx_len, (lambda b, h, i: (b, h)), 0
    in_specs += [pl.BlockSpec(lp.shape, const), pl.BlockSpec((LANES, LANES), const)]
    return pl.pallas_call(
        functools.partial(_attn_kernel, n_lat_blocks, lam_init),
        out_shape=jax.ShapeDtypeStruct((n_rows, N_HEADS * LANES), BF16),
        grid=(n_batch, N_HEADS, n_q),
        in_specs=in_specs,
        out_specs=pl.BlockSpec((tq, LANES), o_map),
        scratch_shapes=[pltpu.VMEM((1, 2 * tq), F32), pltpu.VMEM((vrows, 2 * tq), F32),
                        pltpu.VMEM((2, max(ATTN_TK, ctx_len), 2 * tq), F32)],
        compiler_params=_cparams("parallel", "parallel", "parallel"),
        name="diff_attention_lat" if latent else "diff_attention_ctx",
    )(*args, lp, g)


def _merge_kernel(tiles_per_batch, n_batch, n_a, orf_ref, orb_ref, ogf_ref, ogb_ref, rg_ref, gr_ref, *refs):
    n_x = 2 if n_a else 1
    mod_ref, gpost_ref, gpre_ref, glag_ref, wout_ref, wrt_ref, xo_ref, hp_ref, sc_ref = refs[2 * n_x:]
    dif = _pick_rows(n_a, *refs[:2]) if n_a else refs[0][...]
    x = _pick_rows(n_a, *refs[2:4]) if n_a else refs[1][...]
    d = x.shape[-1]
    r = jnp.minimum(pl.program_id(0) // tiles_per_batch, n_batch)
    gate1 = mod_ref[pl.ds(r, 1), 2 * d:3 * d]
    shift2 = mod_ref[pl.ds(r, 1), 3 * d:4 * d]
    scale2 = mod_ref[pl.ds(r, 1), 4 * d:5 * d]
    gi = lax.broadcasted_iota(I32, (256, 256), 0) // RET_DV
    gj = lax.broadcasted_iota(I32, (256, 256), 1) // RET_DV
    group_mean = jnp.where(gi == gj, 1.0 / RET_DV, 0.0).astype(BF16)

    def head_norm(o):
        ms = jnp.dot((o * o).astype(BF16), group_mean, preferred_element_type=F32)
        return o * lax.rsqrt(ms + EPS)

    ret = head_norm(orf_ref[...] + orb_ref[...]) * _silu(rg_ref[...].astype(F32))
    gla = head_norm(ogf_ref[...] + ogb_ref[...]) * glag_ref[...] * _silu(gr_ref[...].astype(F32))
    m = (jnp.dot(ret.astype(BF16), wout_ref[0:256, :], preferred_element_type=F32)
         + jnp.dot(gla.astype(BF16), wout_ref[256:512, :], preferred_element_type=F32)
         + jnp.dot(dif, wout_ref[512:1024, :], preferred_element_type=F32))
    x_new = x + gate1 * (_rms(m) * gpost_ref[...])
    xo_ref[...] = x_new
    h2 = _rms(x_new) * gpre_ref[...] * (1.0 + scale2) + shift2
    _tile_major_store(hp_ref, _pack_halves(h2[:, :d // 2], h2[:, d // 2:]))
    def split(a):
        hi = a.astype(BF16)
        return hi, (a - hi.astype(F32)).astype(BF16)

    h_hi, h_lo = split(h2)
    w_hi, w_lo = split(wrt_ref[...])
    nt = functools.partial(lax.dot_general, dimension_numbers=NT_DIMS, preferred_element_type=F32)
    logits = nt(w_hi, h_hi) + (nt(w_hi, h_lo) + nt(w_lo, h_hi))
    sc_ref[...] = 1.0 / (1.0 + jnp.exp(-logits))


def _merge(o_rf, o_rb, o_gf, o_gb, ret, gla, dif, xs, mod, g_post, g_pre_ffn, gla_g, w_out, w_rt,
           n_rows, n_batch, seq):
    tm = TOKEN_TILE
    row = lambda i: (i, 0)
    gate_col = lambda i: (i, 2)
    const = lambda i: (0, 0)
    if isinstance(xs, tuple):
        d = xs[0].shape[1]
        n_a, x_specs = _split_rows(*xs, tm, d)
        _, dif_specs = _split_rows(*dif, tm, 512)
    else:
        d = xs.shape[1]
        n_a, x_specs, dif_specs = 0, [pl.BlockSpec((tm, d), row)], [pl.BlockSpec((tm, 512), row)]
        xs, dif = (xs,), (dif,)
    return pl.pallas_call(
        functools.partial(_merge_kernel, seq // tm, n_batch, n_a),
        out_shape=(jax.ShapeDtypeStruct((n_rows, d), F32), jax.ShapeDtypeStruct((n_rows * HP_CHUNKS, LANES), U32),
                   jax.ShapeDtypeStruct((N_EXPERTS, n_rows), F32)),
        grid=(n_rows // tm,),
        in_specs=[
            pl.BlockSpec((tm, 256), row), pl.BlockSpec((tm, 256), row),
            pl.BlockSpec((tm, 256), row), pl.BlockSpec((tm, 256), row),
            pl.BlockSpec((tm, 256), gate_col), pl.BlockSpec((tm, 256), gate_col), *dif_specs, *x_specs,
            pl.BlockSpec(mod.shape, const), pl.BlockSpec((1, d), const), pl.BlockSpec((1, d), const),
            pl.BlockSpec((1, 256), const), pl.BlockSpec(w_out.shape, const), pl.BlockSpec(w_rt.shape, const),
        ],
        out_specs=(pl.BlockSpec((tm, d), row), pl.BlockSpec((tm * HP_CHUNKS, LANES), row),
                   pl.BlockSpec((N_EXPERTS, tm), lambda i: (0, i))),
        compiler_params=_cparams("parallel"),
        name="merge",
    )(o_rf, o_rb, o_gf, o_gb, ret, gla, *dif, *xs, mod, g_post, g_pre_ffn, gla_g, w_out, w_rt)


def _route_kernel(sc_ref, bias_ref, posa_ref, wts_ref, pos_ref, meta_ref):
    tt = sc_ref.shape[-1]
    scores = sc_ref[...]
    sel = scores + bias_ref[...][:, 0:1]
    sub = lax.broadcasted_iota(I32, (GROUP_SIZE, tt), 0)
    neg = -jnp.inf
    gscore = []
    for g in range(N_GROUPS):
        xg = sel[g * GROUP_SIZE:(g + 1) * GROUP_SIZE]
        m1 = jnp.max(xg, axis=0, keepdims=True)
        i1 = jnp.min(jnp.where(xg == m1, sub, GROUP_SIZE), axis=0, keepdims=True)
        m2 = jnp.max(jnp.where(sub == i1, neg, xg), axis=0, keepdims=True)
        gscore.append(m1 + m2)
    rows = []
    for g in range(N_GROUPS):
        rank = jnp.zeros((1, tt), I32)
        for o in range(N_GROUPS):
            if o == g:
                continue
            ahead = (gscore[o] >= gscore[g]) if o < g else (gscore[o] > gscore[g])
            rank = rank + ahead.astype(I32)
        rows.append(jnp.where(rank < TOPK_GROUPS, sel[g * GROUP_SIZE:(g + 1) * GROUP_SIZE], neg))
    masked = jnp.concatenate(rows, axis=0)
    eio = lax.broadcasted_iota(I32, (N_EXPERTS, tt), 0)
    member = jnp.zeros((N_EXPERTS, tt), F32)
    idxs, ws = [], []
    for _ in range(TOP_K):
        m = jnp.max(masked, axis=0, keepdims=True)
        i = jnp.min(jnp.where(masked == m, eio, N_EXPERTS), axis=0, keepdims=True)
        hit = eio == i
        idxs.append(i)
        ws.append(jnp.sum(jnp.where(hit, scores, 0.0), axis=0, keepdims=True))
        member = jnp.where(hit, 1.0, member)
        masked = jnp.where(hit, neg, masked)
    wsum = ws[0]
    for w in ws[1:]:
        wsum = wsum + w
    ti = lax.broadcasted_iota(I32, (tt, tt), 0)
    tj = lax.broadcasted_iota(I32, (tt, tt), 1)
    before = jnp.where(ti < tj, 1.0, 0.0).astype(BF16)
    rank_in_e = jnp.dot(member.astype(BF16), before, preferred_element_type=F32)
    cnt = jnp.sum(member, axis=1, keepdims=True)
    padded = jnp.floor((cnt + (SUBLANES - 1)) * (1.0 / SUBLANES)) * SUBLANES
    ei = lax.broadcasted_iota(I32, (N_EXPERTS, N_EXPERTS), 0)
    ej = lax.broadcasted_iota(I32, (N_EXPERTS, N_EXPERTS), 1)
    lower = jnp.where(ej < ei, 1.0, 0.0)
    off = jnp.dot(lower, jnp.broadcast_to(padded, (N_EXPERTS, LANES)), precision=HIGHEST,
                  preferred_element_type=F32)
    slot = rank_in_e + off[:, 0:1]
    zrow_i = jnp.zeros((SUBLANES - TOP_K, tt), I32)
    zrow_f = jnp.zeros((SUBLANES - TOP_K, tt), F32)
    pos = [jnp.sum(jnp.where(eio == i, slot, 0.0), axis=0, keepdims=True).astype(I32) for i in idxs]
    pos = jnp.concatenate(pos + [zrow_i], axis=0)
    wts_ref[...] = jnp.concatenate([w / wsum * ROUTED_SCALE for w in ws] + [zrow_f], axis=0)
    pos_ref[...] = pos
    posa_ref[...] = _tile_major_addr(pos, HP_CHUNKS)
    meta_ref[0] = jnp.concatenate([jnp.broadcast_to(cnt, (N_EXPERTS, LANES)), off], axis=1).astype(I32)


def _route(scores_t, bias, tt, row0, n):
    tile0 = row0 // tt
    tok = lambda i: (0, i)
    return pl.pallas_call(
        _route_kernel,
        out_shape=(jax.ShapeDtypeStruct((SUBLANES, n), I32), jax.ShapeDtypeStruct((SUBLANES, n), F32),
                   jax.ShapeDtypeStruct((SUBLANES, n), I32),
                   jax.ShapeDtypeStruct((n // tt, N_EXPERTS, 2 * LANES), I32)),
        grid=(n // tt,),
        in_specs=[pl.BlockSpec((N_EXPERTS, tt), lambda i: (0, i + tile0)),
                  pl.BlockSpec((N_EXPERTS, LANES), lambda i: (0, 0))],
        out_specs=(pl.BlockSpec((SUBLANES, tt), tok), pl.BlockSpec((SUBLANES, tt), tok),
                   pl.BlockSpec((SUBLANES, tt), tok),
                   pl.BlockSpec((1, N_EXPERTS, 2 * LANES), lambda i: (i, 0, 0))),
        compiler_params=_cparams("parallel"),
        name="route",
    )(scores_t, jnp.broadcast_to(bias[:, None], (N_EXPERTS, LANES)))


def _moe_kernel(group, block, has_prev, pos_ref, posa_ref, wts_ref, meta_ref, hp_ref, weg_ref, weu_ref, wed_ref,
                *refs):
    o_ref, xg_ref, y_ref, rt_ref = refs[1:] if has_prev else refs
    d = weg_ref.shape[1]
    half = d // 2
    tt = hp_ref.shape[0] // HP_CHUNKS
    eg = pl.program_id(1)

    @pl.when((pl.program_id(0) == 0) & (eg == 0))
    def _():
        def zero(i, carry):
            for u in range(SUBLANES):
                rt_ref[i * SUBLANES + u] = 0
            return carry
        lax.fori_loop(0, rt_ref.shape[0] // SUBLANES, zero, 0)

    @pl.when(eg == 0)
    def _():
        def scatter(g, carry):
            base = g * (SUBLANES * HP_CHUNKS)
            for u in range(SUBLANES):
                for k in range(TOP_K):
                    rt_ref[pos_ref[(g * SUBLANES + u) * SUBLANES + k]] = base + u
            return carry
        lax.fori_loop(0, tt // SUBLANES, scatter, 0)

    sub = lax.broadcasted_iota(I32, (SUBLANES, LANES), 0)

    def expert_block(ge, buf, base, n_valid):
        base = pl.multiple_of(base, SUBLANES)
        for i in range(block):
            xg_ref[buf, pl.ds(_tile_major_addr(i, HP_CHUNKS), HP_CHUNKS, stride=SUBLANES), :] = (
                hp_ref[pl.ds(rt_ref[base + i], HP_CHUNKS, stride=SUBLANES), :])
        yield
        halves = [_unpack_halves(col) for col in _tile_major_columns(xg_ref, block, HP_CHUNKS, (buf,))]
        xa = jnp.concatenate([h[0] for h in halves], axis=1).astype(BF16)
        xb = jnp.concatenate([h[1] for h in halves], axis=1).astype(BF16)
        hg = (jnp.dot(xa, weg_ref[ge, 0:half, :], preferred_element_type=F32)
              + jnp.dot(xb, weg_ref[ge, half:d, :], preferred_element_type=F32))
        hu = (jnp.dot(xa, weu_ref[ge, 0:half, :], preferred_element_type=F32)
              + jnp.dot(xb, weu_ref[ge, half:d, :], preferred_element_type=F32))
        yield
        y = jnp.dot((_silu(hg) * hu).astype(BF16), wed_ref[ge], preferred_element_type=F32)
        yield
        packed = _pack_halves(y[:, :half], y[:, half:])
        row0 = pl.multiple_of(base * HP_CHUNKS, SUBLANES * HP_CHUNKS)
        for g in range(block // SUBLANES):
            keep = sub < n_valid - g * SUBLANES
            for c in range(HP_CHUNKS):
                pltpu.store(y_ref.at[pl.ds(row0 + (g * HP_CHUNKS + c) * SUBLANES, SUBLANES), :],
                            packed[g * SUBLANES:(g + 1) * SUBLANES, c * LANES:(c + 1) * LANES], mask=keep)

    cnts = [meta_ref[0, eg * group + ge] for ge in range(group)]
    offs = [meta_ref[1, eg * group + ge] for ge in range(group)]
    _round_robin([expert_block(ge, ge, offs[ge], cnts[ge]) for ge in range(group)])
    for ge in range(group):
        def more(j, carry, ge=ge):
            for _ in expert_block(ge, ge, offs[ge] + j * block, cnts[ge] - j * block):
                pass
            return carry
        lax.fori_loop(1, (cnts[ge] + (block - 1)) // block, more, 0)

    @pl.when(eg == N_EXPERTS // group - 1)
    def _():
        def tokens(g, carry):
            for u in range(SUBLANES):
                e0 = (g * SUBLANES + u) * SUBLANES
                acc_a = jnp.zeros((HP_CHUNKS, LANES), F32)
                acc_b = jnp.zeros((HP_CHUNKS, LANES), F32)
                for k in range(TOP_K):
                    ya, yb = _unpack_halves(y_ref[pl.ds(posa_ref[e0 + k], HP_CHUNKS, stride=SUBLANES), :])
                    w = wts_ref[e0 + k]
                    acc_a = acc_a + w * ya
                    acc_b = acc_b + w * yb
                out0 = pl.multiple_of(g * (2 * HP_CHUNKS * SUBLANES), 2 * HP_CHUNKS * SUBLANES) + u
                o_ref[pl.ds(out0, HP_CHUNKS, stride=SUBLANES), :] = acc_a
                o_ref[pl.ds(out0 + HP_CHUNKS * SUBLANES, HP_CHUNKS, stride=SUBLANES), :] = acc_b
            return carry

        lax.fori_loop(0, tt // SUBLANES, tokens, 0)


def _moe_segment(prev, scores_t, hp, router_bias, expert_w, layer, row0, n, tt):
    w_eg, w_eu, w_ed = expert_w
    d, de = w_eg.shape[1:]
    half = d // 2
    n_all = hp.shape[0] // HP_CHUNKS
    mean = tt * TOP_K / N_EXPERTS
    block = int(-(-(mean + 2.0 * math.sqrt(mean * (1.0 - 1.0 / N_EXPERTS))) // 16) * 16)
    group = max(1, min(MOE_MAX_GROUP, MOE_ROWS_IN_FLIGHT // block))
    group0 = layer * (N_EXPERTS // group)
    n_tiles, tile0 = n // tt, row0 // tt
    n_slots = -(-(tt * TOP_K + N_EXPERTS * (SUBLANES - 1) + block) // (SUBLANES * LANES)) * SUBLANES * LANES
    posa, wts, pos, meta = _route(scores_t, router_bias, tt, row0, n)
    cnt, off = meta[:, :, 0], meta[:, :, LANES]

    def per_token(a):
        return a.T.reshape(-1)

    meta_s = jnp.zeros((n_tiles, SUBLANES, LANES), I32)
    meta_s = meta_s.at[:, 0, :N_EXPERTS].set(cnt).at[:, 1, :N_EXPERTS].set(off).reshape(-1, LANES)

    tile = lambda i, e: (i + tile0, 0)
    expert = lambda i, e: (group0 + e, 0, 0)
    smem = functools.partial(pl.BlockSpec, memory_space=pltpu.SMEM)
    once = pl.Buffered(1)
    flat = smem((SUBLANES * tt,), lambda i, e: (i,))
    in_specs = [
        flat, flat, flat, smem((SUBLANES, LANES), lambda i, e: (i, 0)),
        pl.BlockSpec((tt * HP_CHUNKS, LANES), tile),
        pl.BlockSpec((group, d, de), expert), pl.BlockSpec((group, d, de), expert),
        pl.BlockSpec((group, de, d), expert),
    ]
    args = [per_token(pos), per_token(posa), per_token(wts), meta_s, hp, w_eg, w_eu, w_ed]
    aliases = {}
    if prev is not None:
        in_specs.append(pl.BlockSpec(memory_space=pl.ANY))
        args.append(prev)
        aliases = {len(args) - 1: 0}
    return pl.pallas_call(
        functools.partial(_moe_kernel, group, block, prev is not None),
        out_shape=jax.ShapeDtypeStruct((n_all * 2 * HP_CHUNKS, LANES), F32),
        grid=(n_tiles, N_EXPERTS // group),
        in_specs=in_specs,
        out_specs=pl.BlockSpec((tt * 2 * HP_CHUNKS, LANES), tile, pipeline_mode=once),
        scratch_shapes=[pltpu.VMEM((group, block * HP_CHUNKS, LANES), U32),
                        pltpu.VMEM((n_slots * HP_CHUNKS, LANES), U32),
                        pltpu.SMEM((n_slots,), I32)],
        input_output_aliases=aliases,
        compiler_params=_cparams("arbitrary", "arbitrary"),
        name="moe",
    )(*args)


def _moe_final_kernel(tiles_per_batch, n_batch, routed_ref, hp_ref, x_ref, mod_ref, gpost_ref,
                      wsg_ref, wsu_ref, wsd_ref, o_ref):
    tm, d = x_ref.shape
    half = d // 2
    r = jnp.minimum(pl.program_id(0) // tiles_per_batch, n_batch)
    gate2 = mod_ref[pl.ds(r, 1), 5 * d:6 * d]
    halves = [_unpack_halves(col) for col in _tile_major_columns(hp_ref, tm, HP_CHUNKS)]
    xa = jnp.concatenate([h[0] for h in halves], axis=1).astype(BF16)
    xb = jnp.concatenate([h[1] for h in halves], axis=1).astype(BF16)
    hg = (jnp.dot(xa, wsg_ref[0:half, :], preferred_element_type=F32)
          + jnp.dot(xb, wsg_ref[half:d, :], preferred_element_type=F32))
    hu = (jnp.dot(xa, wsu_ref[0:half, :], preferred_element_type=F32)
          + jnp.dot(xb, wsu_ref[half:d, :], preferred_element_type=F32))
    routed = jnp.concatenate(_tile_major_columns(routed_ref, tm, 2 * HP_CHUNKS), axis=1)
    f = jnp.dot((_silu(hg) * hu).astype(BF16), wsd_ref[...], preferred_element_type=F32) + routed
    o_ref[...] = x_ref[...] + gate2 * (_rms(f) * gpost_ref[...])


def _moe_final(routed, hp, xs, mod, g_post, w_sg, w_su, w_sd, n_batch, seq):
    n, d = xs.shape
    tm = TOKEN_TILE
    row = lambda i: (i, 0)
    const = lambda i: (0, 0)
    return pl.pallas_call(
        functools.partial(_moe_final_kernel, seq // tm, n_batch),
        out_shape=jax.ShapeDtypeStruct((n, d), F32),
        grid=(n // tm,),
        in_specs=[pl.BlockSpec((tm * 2 * HP_CHUNKS, LANES), row), pl.BlockSpec((tm * HP_CHUNKS, LANES), row),
                  pl.BlockSpec((tm, d), row),
                  pl.BlockSpec(mod.shape, const), pl.BlockSpec((1, d), const),
                  pl.BlockSpec(w_sg.shape, const), pl.BlockSpec(w_su.shape, const), pl.BlockSpec(w_sd.shape, const)],
        out_specs=pl.BlockSpec((tm, d), row),
        compiler_params=_cparams("parallel"),
        name="moe_final",
    )(routed, hp, xs, mod, g_post, w_sg, w_su, w_sd)


def _moe_layer(scores_t, hp, xs, mod, router_bias, g_post, expert_w, shared_w, layer, n_lat, n_batch, seq):
    pick = lambda rows: next(t for t in MOE_TILES if rows % t == 0)
    routed = _moe_segment(None, scores_t, hp, router_bias, expert_w, layer, 0, n_lat, pick(n_lat))
    n_ctx = hp.shape[0] // HP_CHUNKS - n_lat
    if n_ctx:
        tt = next(t for t in MOE_TILES if n_ctx % t == 0 and n_lat % t == 0)
        routed = _moe_segment(routed, scores_t, hp, router_bias, expert_w, layer, n_lat, n_ctx, tt)
    return _moe_final(routed, hp, xs, mod, g_post, *shared_w, n_batch, seq)


def kernel(x, c, ctx, c_ctx, w_mod, b_mod, g_pre_mix, g_post_mix, g_pre_ffn, g_post_ffn, w_in, w_out,
           ret_decay_logit, gla_w_gate, gla_b_gate, gla_norm_g, diff_lambda, diff_norm_g,
           w_router, router_bias, w_exp_gate, w_exp_up, w_exp_down, w_sh_gate, w_sh_up, w_sh_down):
    n_batch, seq, d = x.shape
    ctx_len = ctx.shape[1]
    depth = w_mod.shape[0]
    n_lat = n_batch * seq
    assert seq % TOKEN_TILE == 0 and (n_batch * ctx_len) % TOKEN_TILE == 0 and n_batch < SUBLANES
    scan_rows = CHUNK * SCAN_CHUNKS
    assert seq % ATTN_TK == 0 and seq % scan_rows == 0 and ctx_len % scan_rows == 0 and n_lat % ctx_len == 0

    xs = (x.reshape(n_lat, d), ctx.reshape(n_batch * ctx_len, d))
    cond = jnp.zeros((SUBLANES, d), F32).at[:n_batch].set(c).at[n_batch].set(c_ctx)
    mods = _modulation(cond, w_mod, b_mod)
    tables = _rope_tables(seq, RET_DK, TOKEN_TILE) + _rope_tables(seq, DIFF_DH, TOKEN_TILE)
    lr0 = N_HEADS * (2 * RET_DK + 2 * RET_DV + 2 * GLA_DK + 2 * GLA_DV)
    row = lambda a: a[None, :]
    expert_w = tuple(w.astype(BF16).reshape((depth * N_EXPERTS,) + w.shape[2:])
                     for w in (w_exp_gate, w_exp_up, w_exp_down))

    for layer in range(depth):
        need_ctx = layer < depth - 1
        lam_init = 0.8 - 0.6 * math.exp(-0.3 * layer)
        mod = mods[layer]
        wl = w_in[layer]
        w_r = jnp.concatenate([wl[:, :lr0], wl[:, lr0:lr0 + GLA_LOWRANK],
                               jnp.zeros((d, LANES - GLA_LOWRANK), F32), wl[:, lr0 + GLA_LOWRANK:]],
                              axis=1).astype(BF16)
        ret, gla, glr, dq, dk, vt = _in_projection(xs, mod, row(g_pre_mix[layer]), w_r, tables, n_batch, seq)

        o_rf, o_gf, o_rb, o_gb = _scan(ret, gla, glr, _scan_tables(ret_decay_logit[layer]), gla_w_gate[layer],
                                       gla_b_gate[layer], n_batch, seq, ctx_len)

        lp = jnp.zeros((SUBLANES, LANES), F32).at[:4, :DIFF_DH].set(diff_lambda[layer])
        g_col = jnp.broadcast_to(diff_norm_g[layer][:, None], (LANES, LANES))
        attn = functools.partial(_diff_attention, dq, dk, vt, lp, g_col, lam_init, n_batch, seq, ctx_len)
        dif = attn(True)
        n_rows = n_lat
        if need_ctx:
            n_rows = n_lat + n_batch * ctx_len
            if isinstance(xs, tuple):
                dif = (dif, attn(False))
            else:
                dif = jnp.concatenate([dif, attn(False)], axis=0)
        elif isinstance(xs, tuple):
            xs = xs[0]

        xs, hp, scores_t = _merge(o_rf, o_rb, o_gf, o_gb, ret, gla, dif, xs, mod, row(g_post_mix[layer]),
                                  row(g_pre_ffn[layer]), row(jnp.tile(gla_norm_g[layer], N_HEADS)),
                                  w_out[layer].astype(BF16), w_router[layer].T, n_rows, n_batch, seq)
        shared_w = tuple(w[layer].astype(BF16) for w in (w_sh_gate, w_sh_up, w_sh_down))
        xs = _moe_layer(scores_t, hp, xs, mod, router_bias[layer], row(g_post_ffn[layer]), expert_w, shared_w,
                        layer, n_lat, n_batch, seq)
    return xs[:n_lat].reshape(n_batch, seq, d)
```

```python
import functools
import math

import jax
import jax.numpy as jnp
import numpy as np
from jax import lax
from jax.experimental import pallas as pl
from jax.experimental.pallas import tpu as pltpu

F32 = jnp.float32
BF16 = jnp.bfloat16
I32 = jnp.int32
U32 = jnp.uint32

GRID_W = 64
CHUNK = 128
N_HEADS = 4
RET_DK, RET_DV = 32, 64
GLA_DK, GLA_DV = 32, 64
GLA_LOWRANK = 16
GLA_TAU = 16.0
DIFF_DH = 64
ROPE_BASE = 10000.0
N_EXPERTS = 64
TOP_K = 6
N_GROUPS = 8
TOPK_GROUPS = 4
GROUP_SIZE = N_EXPERTS // N_GROUPS
ROUTED_SCALE = 2.5
EPS = 1e-6
GLA_SUB = 16
SCAN_CHUNKS = 2

LANES = 128
SUBLANES = 8
VMEM_LIMIT_BYTES = 56 * 1024 * 1024

TOKEN_TILE = 512
ATTN_TQ = 512
ATTN_TK = 512
ATTN_ONES = 16
MOE_TILES = (2048, 1024, 512)
MOE_ROWS_IN_FLIGHT = 512
MOE_MAX_GROUP = 4
MOE_BLOCK_SIGMAS = 4.0
HP_CHUNKS = 4

HIGHEST = lax.Precision.HIGHEST
NT_DIMS = (((1,), (1,)), ((), ()))
TN_DIMS = (((0,), (0,)), ((), ()))


def _cparams(*sem):
    return pltpu.CompilerParams(dimension_semantics=sem, vmem_limit_bytes=VMEM_LIMIT_BYTES)


def _log_sigmoid(x):
    return jnp.minimum(x, 0.0) - jnp.log(1.0 + jnp.exp(-jnp.abs(x)))


def _silu(x):
    return x * (1.0 / (1.0 + jnp.exp(-x)))


def _rms(x):
    return x * lax.rsqrt(jnp.mean(x * x, axis=-1, keepdims=True) + EPS)


def _pack_halves(a, b):
    ua = lax.bitcast_convert_type(a.astype(BF16).astype(F32), U32)
    ub = lax.bitcast_convert_type(b.astype(BF16).astype(F32), U32)
    return (ua & jnp.uint32(0xFFFF0000)) | (ub >> 16)


def _unpack_halves(w):
    a = lax.bitcast_convert_type(w & jnp.uint32(0xFFFF0000), F32)
    b = lax.bitcast_convert_type(w << 16, F32)
    return a, b


def _tile_major_store(ref, v, lead=()):
    k = v.shape[1] // LANES
    for g in range(v.shape[0] // SUBLANES):
        for c in range(k):
            blk = g * k + c
            ref[lead + (slice(blk * SUBLANES, (blk + 1) * SUBLANES), slice(None))] = (
                v[g * SUBLANES:(g + 1) * SUBLANES, c * LANES:(c + 1) * LANES])


def _tile_major_columns(ref, rows, k, lead=()):
    return [jnp.concatenate([ref[lead + (slice((g * k + c) * SUBLANES, (g * k + c + 1) * SUBLANES), slice(None))]
                             for g in range(rows // SUBLANES)], axis=0) for c in range(k)]


def _tile_major_addr(t, k):
    return (t >> 3) * (SUBLANES * k) + (t & (SUBLANES - 1))


def _mod_kernel(cond_ref, w_ref, b_ref, o_ref):
    a = _silu(cond_ref[...])
    o_ref[0] = jnp.dot(a, w_ref[0], precision=HIGHEST, preferred_element_type=F32) + b_ref[0]


def _modulation(cond, w_mod, b_mod):
    n_layers, d, d6 = w_mod.shape
    tn = 1024
    return pl.pallas_call(
        _mod_kernel,
        out_shape=jax.ShapeDtypeStruct((n_layers, SUBLANES, d6), F32),
        grid=(n_layers, d6 // tn),
        in_specs=[
            pl.BlockSpec((SUBLANES, d), lambda l, j: (0, 0)),
            pl.BlockSpec((1, d, tn), lambda l, j: (l, 0, j)),
            pl.BlockSpec((1, 1, tn), lambda l, j: (l, 0, j)),
        ],
        out_specs=pl.BlockSpec((1, SUBLANES, tn), lambda l, j: (l, 0, j)),
        compiler_params=_cparams("parallel", "parallel"),
        name="modulation",
    )(cond, w_mod, b_mod.reshape(n_layers, 1, d6))


def _rope(x, cos, sin, quarter):
    lane = lax.broadcasted_iota(I32, (1, LANES), 1)
    first = (lane % (2 * quarter)) < quarter
    outs = []
    for c in range(x.shape[-1] // LANES):
        xc = x[:, c * LANES:(c + 1) * LANES]
        partner = jnp.where(first, pltpu.roll(xc, LANES - quarter, 1), pltpu.roll(xc, quarter, 1))
        outs.append(xc * cos + partner * sin)
    return outs[0] if len(outs) == 1 else jnp.concatenate(outs, axis=-1)


def _split_rows(a, b, tm, width):
    n_a = a.shape[0] // tm
    return n_a, [pl.BlockSpec((tm, width), lambda i: (jnp.minimum(i, n_a - 1), 0)),
                 pl.BlockSpec((tm, width), lambda i: (jnp.maximum(i - n_a, 0), 0))]


def _pick_rows(n_a, a_ref, b_ref):
    return jnp.where(pl.program_id(0) < n_a, a_ref[...], b_ref[...])


def _inproj_kernel(tiles_per_batch, n_batch, n_a, *refs):
    n_x = 2 if n_a else 1
    (mod_ref, g_ref, w_ref, c32_ref, s32_ref, c64_ref, s64_ref,
     ret_ref, gla_ref, glr_ref, dqt_ref, dk_ref, vt_ref) = refs[n_x:]
    x = _pick_rows(n_a, *refs[:2]) if n_a else refs[0][...]
    d = x.shape[-1]
    r = jnp.minimum(pl.program_id(0) // tiles_per_batch, n_batch)
    shift = mod_ref[pl.ds(r, 1), 0:d]
    scale = mod_ref[pl.ds(r, 1), d:2 * d]
    h = (_rms(x) * g_ref[...] * (1.0 + scale) + shift).astype(BF16)

    def proj(lo, hi):
        return jnp.dot(h, w_ref[:, lo:hi], preferred_element_type=F32)

    c32, s32 = c32_ref[...], s32_ref[...]
    c64, s64 = c64_ref[...], s64_ref[...]
    ret = proj(0, 768)
    ret_ref[:, 0:128] = _rope(ret[:, 0:128], c32, s32, RET_DK // 4).astype(BF16)
    ret_ref[:, 128:256] = (_rope(ret[:, 128:256], c32, s32, RET_DK // 4) * RET_DK ** -0.5).astype(BF16)
    ret_ref[:, 256:768] = ret[:, 256:768].astype(BF16)
    gla = proj(768, 1536)
    gla_ref[:, 0:128] = (gla[:, 0:128] * GLA_DK ** -0.5).astype(BF16)
    gla_ref[:, 128:768] = gla[:, 128:768].astype(BF16)
    glr_ref[...] = proj(1536, 1664)
    dq = _rope(proj(1664, 2176), c64, s64, DIFF_DH // 4) * DIFF_DH ** -0.5
    for hd in range(N_HEADS):
        dqt_ref[hd * LANES:(hd + 1) * LANES, :] = dq[:, hd * LANES:(hd + 1) * LANES].T.astype(BF16)
    dk_ref[...] = _rope(proj(2176, 2688), c64, s64, DIFF_DH // 4).astype(BF16)
    dv = proj(2688, 3200)
    vrows = LANES + ATTN_ONES
    for hd in range(N_HEADS):
        vt_ref[hd * vrows:hd * vrows + LANES, :] = dv[:, hd * LANES:(hd + 1) * LANES].T.astype(BF16)
        vt_ref[hd * vrows + LANES:(hd + 1) * vrows, :] = jnp.ones((ATTN_ONES, dv.shape[0]), BF16)


def _in_projection(xs, mod, g_pre, w_r, tables, n_batch, seq):
    tm = TOKEN_TILE
    if isinstance(xs, tuple):
        d = xs[0].shape[1]
        n = xs[0].shape[0] + xs[1].shape[0]
        n_a, x_specs = _split_rows(*xs, tm, d)
    else:
        n, d = xs.shape
        n_a, x_specs, xs = 0, [pl.BlockSpec((tm, d), lambda i: (i, 0))], (xs,)
    tiles_per_batch = seq // tm
    n_lat_tiles = n_batch * tiles_per_batch
    c32, s32, c64, s64 = tables

    def tab_map(i):
        return (jnp.where(i < n_lat_tiles, i % tiles_per_batch, tiles_per_batch), 0)

    row = lambda i: (i, 0)
    const = lambda i: (0, 0)
    tab_spec = pl.BlockSpec((tm, LANES), tab_map)
    return pl.pallas_call(
        functools.partial(_inproj_kernel, tiles_per_batch, n_batch, n_a),
        out_shape=(
            jax.ShapeDtypeStruct((n, 768), BF16), jax.ShapeDtypeStruct((n, 768), BF16),
            jax.ShapeDtypeStruct((n, LANES), F32),
            jax.ShapeDtypeStruct((N_HEADS * LANES, n), BF16), jax.ShapeDtypeStruct((n, 512), BF16),
            jax.ShapeDtypeStruct((N_HEADS * (LANES + ATTN_ONES), n), BF16)),
        grid=(n // tm,),
        in_specs=x_specs + [
            pl.BlockSpec(mod.shape, const),
            pl.BlockSpec((1, d), const),
            pl.BlockSpec(w_r.shape, const),
            tab_spec, tab_spec, tab_spec, tab_spec,
        ],
        out_specs=(
            pl.BlockSpec((tm, 768), row), pl.BlockSpec((tm, 768), row), pl.BlockSpec((tm, LANES), row),
            pl.BlockSpec((N_HEADS * LANES, tm), lambda i: (0, i)), pl.BlockSpec((tm, 512), row),
            pl.BlockSpec((N_HEADS * (LANES + ATTN_ONES), tm), lambda i: (0, i))),
        compiler_params=_cparams("parallel"),
        name="in_projection",
    )(*xs, mod, g_pre, w_r, c32, s32, c64, s64)


def _rope_tables(seq, head_dim, extra_rows):
    half, quarter = head_dim // 2, head_dim // 4
    freqs = (ROPE_BASE ** (-np.arange(quarter, dtype=np.float32) / quarter)).astype(np.float32)
    t = np.arange(seq)
    row = (t // GRID_W).astype(np.float32)
    col = (t % GRID_W).astype(np.float32)
    j = np.arange(LANES) % head_dim
    jj = j % half
    pos = np.where((j < half)[None, :], row[:, None], col[:, None])
    ang = (pos * freqs[jj % quarter][None, :]).astype(np.float32)
    cos = np.cos(ang)
    sin = np.sin(ang) * np.where(jj < quarter, -1.0, 1.0)[None, :]
    cos = np.concatenate([cos, np.ones((extra_rows, LANES))], axis=0).astype(np.float32)
    sin = np.concatenate([sin, np.zeros((extra_rows, LANES))], axis=0).astype(np.float32)
    return jnp.asarray(cos), jnp.asarray(sin)


def _head_stack(x, width):
    lane = lax.broadcasted_iota(I32, (1, x.shape[-1]), 1)
    zero = jnp.zeros_like(x)
    return jnp.concatenate([jnp.where(lane // width == h, x, zero) for h in range(N_HEADS)], axis=0)


def _head_select(x4, rows):
    lane = lax.broadcasted_iota(I32, (1, x4.shape[-1]), 1)
    out = jnp.zeros((rows, x4.shape[-1]), F32)
    for h in range(N_HEADS):
        out = out + jnp.where(lane // RET_DV == h, x4[h * rows:(h + 1) * rows], 0.0)
    return out


def _scan_tables_kernel(rl_lane_ref, rl_rows_ref, rl_col_ref, dmat_ref, xi_ref, zeta_ref, gchunk_ref):
    c = CHUNK
    idx = lax.broadcasted_iota(I32, (c, 1), 0).astype(F32)
    ri4 = lax.broadcasted_iota(I32, (N_HEADS * c, c), 0) % c
    ci4 = lax.broadcasted_iota(I32, (N_HEADS * c, c), 1)
    dist = jnp.abs(ri4 - ci4).astype(F32)
    for d in range(2):
        lg_lane = _log_sigmoid(rl_lane_ref[d])
        lg_rows = _log_sigmoid(rl_rows_ref[d])
        lg_col = _log_sigmoid(rl_col_ref[d])
        att4 = (ri4 <= ci4) if d else (ri4 >= ci4)
        dmat_ref[d] = jnp.where(att4, jnp.exp(dist * lg_rows), 0.0)
        xi_ref[d] = jnp.exp(((c - idx) if d else (idx + 1.0)) * lg_lane)
        zeta_ref[d] = jnp.exp((idx if d else (c - 1.0 - idx)) * lg_lane)
        g_chunk = jnp.exp(float(c) * lg_col)
        gchunk_ref[d] = jnp.concatenate([g_chunk, g_chunk], axis=1)


def _scan_tables(ret_logit):
    c = CHUNK
    rl_lane = jnp.repeat(ret_logit, RET_DK, axis=1)[:, None, :]
    rl_rows = jnp.broadcast_to(jnp.repeat(ret_logit, c, axis=1)[:, :, None], (2, N_HEADS * c, c))
    rl_col = jnp.broadcast_to(jnp.repeat(ret_logit, RET_DK, axis=1)[:, :, None], (2, LANES, LANES))
    return pl.pallas_call(
        _scan_tables_kernel,
        out_shape=(jax.ShapeDtypeStruct((2, N_HEADS * c, c), F32), jax.ShapeDtypeStruct((2, c, LANES), F32),
                   jax.ShapeDtypeStruct((2, c, LANES), F32), jax.ShapeDtypeStruct((2, LANES, 2 * LANES), F32)),
        name="scan_tables",
    )(rl_lane, rl_rows, rl_col)


def _state_block_mask():
    return (lax.broadcasted_iota(I32, (LANES, 2 * LANES), 0) // RET_DK
            == lax.broadcasted_iota(I32, (LANES, 2 * LANES), 1) // RET_DV)


def _ret_chain(ret_ref, r0, dmat, xi, zeta, g_chunk, out):
    c = CHUNK
    q = ret_ref[r0:r0 + c, 0:128]
    k = ret_ref[r0:r0 + c, 128:256]
    v = ret_ref[r0:r0 + c, 256:512]
    s = lax.dot_general(_head_stack(q, RET_DK), k, NT_DIMS, preferred_element_type=F32)
    kz = (k.astype(F32) * zeta).astype(BF16)
    u = lax.dot_general(kz, v, TN_DIMS, preferred_element_type=F32)
    yield
    o4 = jnp.dot((s * dmat).astype(BF16), v, preferred_element_type=F32)
    out.update(qx=(q.astype(F32) * xi).astype(BF16), u=jnp.where(_state_block_mask(), u, 0.0), g=g_chunk)
    yield
    out.update(intra=_head_select(o4, c))


def _state_chain(parts, o_ref, s_ref):
    c = CHUNK
    s = s_ref[...]
    for r0, p in parts:
        o_ref[r0:r0 + c, :] = p["intra"] + jnp.dot(p["qx"], s.astype(BF16), preferred_element_type=F32)
        s = s * p["g"] + p["u"]
        yield
    s_ref[...] = s


def _gla_chain(rev, gla_ref, glr_ref, r0, gw, gb, out):
    c = CHUNK
    ri = lax.broadcasted_iota(I32, (c, c), 0)
    ci = lax.broadcasted_iota(I32, (c, c), 1)
    attends = (ri <= ci) if rev else (ri >= ci)
    bd = _state_block_mask()
    gq = gla_ref[r0:r0 + c, 0:128].astype(F32)
    gk = gla_ref[r0:r0 + c, 128:256].astype(F32)
    gv = gla_ref[r0:r0 + c, 256:512]
    z = jnp.dot(glr_ref[r0:r0 + c, :], gw, precision=HIGHEST, preferred_element_type=F32) + gb
    yield
    la = _log_sigmoid(z) * (1.0 / GLA_TAU)
    if rev:
        first = (ri // GLA_SUB) * GLA_SUB + (GLA_SUB - 1)
        ref_sel = ci >= first
    else:
        first = (ri // GLA_SUB) * GLA_SUB
        ref_sel = ci <= first
    sel = jnp.concatenate([jnp.where(attends, 1.0, 0.0), jnp.where(ref_sel, 1.0, 0.0)], axis=0).astype(BF16)
    la_hi = la.astype(BF16)
    la_lo = (la - la_hi.astype(F32)).astype(BF16)
    sums = (jnp.dot(sel, la_hi, preferred_element_type=F32) + jnp.dot(sel, la_lo, preferred_element_type=F32))
    b, refrow = sums[0:c], sums[c:2 * c]
    yield
    qs = gq * jnp.exp(b - refrow)
    b_last = b[0:1] if rev else b[c - 1:c]
    kz = (gk * jnp.exp(b_last - b)).astype(BF16)
    u = lax.dot_general(kz, gv, TN_DIMS, preferred_element_type=F32)
    eye = lax.broadcasted_iota(I32, (LANES, LANES), 0) == lax.broadcasted_iota(I32, (LANES, LANES), 1)
    g_col = jnp.sum(jnp.where(eye, jnp.exp(b_last), 0.0), axis=1, keepdims=True)
    jcol = lax.broadcasted_iota(I32, (c, 1), 0)
    rr = lax.broadcasted_iota(I32, (N_HEADS * GLA_SUB, c), 0) % GLA_SUB
    cc = lax.broadcasted_iota(I32, (N_HEADS * GLA_SUB, c), 1)
    pieces = []
    for blk in range(c // GLA_SUB):
        lo = blk * GLA_SUB
        ref_b = refrow[lo:lo + 1]
        seen = (jcol >= lo) if rev else (jcol < lo + GLA_SUB)
        ks = (gk * jnp.exp(jnp.where(seen, ref_b - b, -jnp.inf))).astype(BF16)
        qz = _head_stack(qs[lo:lo + GLA_SUB], GLA_DK).astype(BF16)
        att = lax.dot_general(qz, ks, NT_DIMS, preferred_element_type=F32)
        ok = (cc >= rr + lo) if rev else (cc <= rr + lo)
        att = jnp.where(ok, att, 0.0).astype(BF16)
        pieces.append(_head_select(jnp.dot(att, gv, preferred_element_type=F32), GLA_SUB))
        if blk % 2:
            yield
    out.update(intra=jnp.concatenate(pieces, axis=0), qx=(gq * jnp.exp(b)).astype(BF16),
               u=jnp.where(bd, u, 0.0), g=g_col)


def _round_robin(chains):
    while chains:
        chains = [ch for ch in chains if next(ch, True) is None]


def _scan_kernel(ret_f, gla_f, glr_f, ret_b, gla_b, glr_b, dmat_ref, xi_ref, zeta_ref, gchunk_ref, gw_ref, gb_ref,
                 orf_ref, ogf_ref, orb_ref, ogb_ref, sr_ref, sg_ref):
    @pl.when(pl.program_id(1) == 0)
    def _():
        sr_ref[...] = jnp.zeros_like(sr_ref)
        sg_ref[...] = jnp.zeros_like(sg_ref)

    chains, state_chains = [], []
    for d, (ret_ref, gla_ref, glr_ref, ore_ref, ogl_ref) in enumerate(
            ((ret_f, gla_f, glr_f, orf_ref, ogf_ref), (ret_b, gla_b, glr_b, orb_ref, ogb_ref))):
        order = range(SCAN_CHUNKS - 1, -1, -1) if d else range(SCAN_CHUNKS)
        gla_parts, ret_parts = [], []
        for j in order:
            r0 = j * CHUNK
            gla_parts.append((r0, {}))
            ret_parts.append((r0, {}))
            chains.append(_gla_chain(bool(d), gla_ref, glr_ref, r0, gw_ref[d], gb_ref[d], gla_parts[-1][1]))
            chains.append(_ret_chain(ret_ref, r0, dmat_ref[d], xi_ref[d], zeta_ref[d], gchunk_ref[d],
                                     ret_parts[-1][1]))
        state_chains.append(_state_chain(gla_parts, ogl_ref, sg_ref.at[d]))
        state_chains.append(_state_chain(ret_parts, ore_ref, sr_ref.at[d]))
    _round_robin(chains)
    _round_robin(state_chains)


def _scan(ret, gla, glr, tables, gla_w, gla_b, n_batch, seq, ctx_len):
    n = ret.shape[0]
    c = CHUNK * SCAN_CHUNKS
    nc_ctx, nc_lat = ctx_len // c, seq // c
    n_steps = nc_ctx + nc_lat
    ctx_base = n_batch * nc_lat

    def fwd(b, s):
        return (jnp.where(s < nc_ctx, ctx_base + b * nc_ctx + s, b * nc_lat + (s - nc_ctx)), 0)

    def bwd(b, s):
        return (jnp.where(s < nc_ctx, ctx_base + b * nc_ctx + (nc_ctx - 1 - s), b * nc_lat + (n_steps - 1 - s)), 0)

    const3 = lambda b, s: (0, 0, 0)
    gw = jnp.zeros((2, LANES, LANES), F32).at[:, :GLA_LOWRANK].set(gla_w)
    gb = gla_b[:, None, :]
    chain_in = lambda m: [pl.BlockSpec((c, 768), m), pl.BlockSpec((c, 768), m), pl.BlockSpec((c, LANES), m)]
    o_sds = jax.ShapeDtypeStruct((n, 256), F32)
    return pl.pallas_call(
        _scan_kernel,
        out_shape=(o_sds, o_sds, o_sds, o_sds),
        grid=(n_batch, n_steps),
        in_specs=chain_in(fwd) + chain_in(bwd) + [pl.BlockSpec(t.shape, const3) for t in tables]
        + [pl.BlockSpec(gw.shape, const3), pl.BlockSpec(gb.shape, const3)],
        out_specs=(pl.BlockSpec((c, 256), fwd), pl.BlockSpec((c, 256), fwd),
                   pl.BlockSpec((c, 256), bwd), pl.BlockSpec((c, 256), bwd)),
        scratch_shapes=[pltpu.VMEM((2, LANES, 2 * LANES), F32), pltpu.VMEM((2, LANES, 2 * LANES), F32)],
        compiler_params=_cparams("arbitrary", "arbitrary"),
        name="scan",
    )(ret, gla, glr, ret, gla, glr, *tables, gw, gb)


def _attn_kernel(n_lat_blocks, lam_init, *refs):
    if n_lat_blocks:
        q_ref, kc_ref, vtc_ref, kl_ref, vtl_ref, lp_ref, g_ref, o_ref, m_ref, acc_ref, st_ref = refs
    else:
        q_ref, kc_ref, vtc_ref, lp_ref, g_ref, o_ref, m_ref, acc_ref, st_ref = refs
    tq = q_ref.shape[1]
    q = q_ref[...]
    dim = lax.broadcasted_iota(I32, (LANES, 1), 0)
    zero = jnp.zeros_like(q)
    qt = jnp.concatenate([jnp.where(dim < DIFF_DH, q, zero), jnp.where(dim >= DIFF_DH, q, zero)], axis=1)
    m_ref[...] = jnp.full(m_ref.shape, -jnp.inf, F32)
    acc_ref[...] = jnp.zeros_like(acc_ref)

    def scores(slot, kb):
        st_ref[slot, 0:kb.shape[0], :] = jnp.dot(kb, qt, preferred_element_type=F32)

    def absorb(slot, vtb):
        st = st_ref[slot, 0:vtb.shape[1], :]
        m_prev = m_ref[...]
        m_new = jnp.maximum(m_prev, jnp.max(st, axis=0, keepdims=True))
        alpha = jnp.exp(m_prev - m_new)
        p = jnp.exp(st - m_new).astype(BF16)
        acc_ref[...] = alpha * acc_ref[...] + jnp.dot(vtb, p, preferred_element_type=F32)
        m_ref[...] = m_new

    def k_lat(blk):
        return kl_ref[pl.ds(pl.multiple_of(blk * ATTN_TK, ATTN_TK), ATTN_TK), :]

    def vt_lat(blk):
        return vtl_ref[:, pl.ds(pl.multiple_of(blk * ATTN_TK, ATTN_TK), ATTN_TK)]

    scores(0, kc_ref[...])
    if not n_lat_blocks:
        absorb(0, vtc_ref[...])
    else:
        scores(1, k_lat(0))
        absorb(0, vtc_ref[...])
        n_pairs = (n_lat_blocks - 1) // 2

        def body(i, carry):
            scores(0, k_lat(2 * i + 1))
            absorb(1, vt_lat(2 * i))
            scores(1, k_lat(2 * i + 2))
            absorb(0, vt_lat(2 * i + 1))
            return carry
        lax.fori_loop(0, n_pairs, body, 0)
        done = 2 * n_pairs
        if n_lat_blocks - done == 2:
            scores(0, k_lat(done + 1))
            absorb(1, vt_lat(done))
            absorb(0, vt_lat(done + 1))
        else:
            absorb(1, vt_lat(done))

    lp = lp_ref[...]
    lam = (jnp.exp(jnp.sum(lp[0:1] * lp[1:2], axis=1, keepdims=True))
           - jnp.exp(jnp.sum(lp[2:3] * lp[3:4], axis=1, keepdims=True)) + lam_init)
    acc = acc_ref[...]
    o1 = acc[0:LANES, :tq] / acc[LANES:LANES + 1, :tq]
    o2 = acc[0:LANES, tq:] / acc[LANES:LANES + 1, tq:]
    ot = o1 - lam * o2
    ot = ot * lax.rsqrt(jnp.mean(ot * ot, axis=0, keepdims=True) + EPS) * g_ref[...][:, 0:1] * (1.0 - lam_init)
    o_ref[...] = ot.T.astype(o_ref.dtype)


def _diff_attention(dqt, dk, vt, lp, g, lam_init, n_batch, seq, ctx_len, latent):
    ctx_blk0 = (n_batch * seq) // ctx_len
    vrows = LANES + ATTN_ONES
    kc_spec = pl.BlockSpec((ctx_len, LANES), lambda b, h, i: (ctx_blk0 + b, h))
    vtc_spec = pl.BlockSpec((vrows, ctx_len), lambda b, h, i: (h, ctx_blk0 + b))
    const = lambda b, h, i: (0, 0)
    if latent:
        tq = ATTN_TQ
        n_q = seq // tq
        o_map = lambda b, h, i: (b * n_q + i, h)
        in_specs = [pl.BlockSpec((LANES, tq), lambda b, h, i: (h, b * n_q + i)), kc_spec, vtc_spec,
                    pl.BlockSpec((seq, LANES), lambda b, h, i: (b, h)),
                    pl.BlockSpec((vrows, seq), lambda b, h, i: (h, b))]
        args = (dqt, dk, vt, dk, vt)
        n_rows, n_lat_blocks = n_batch * seq, seq // ATTN_TK
    else:
        tq, n_q = ctx_len, 1
        in_specs = [pl.BlockSpec((LANES, ctx_len), lambda b, h, i: (h, ctx_blk0 + b)), kc_spec, vtc_spec]
        args = (dqt, dk, vt)
        n_rows, o_map, n_lat_blocks = n_batch * ctx_len, (lambda b, h, i: (b, h)), 0
    in_specs += [pl.BlockSpec(lp.shape, const), pl.BlockSpec((LANES, LANES), const)]
    return pl.pallas_call(
        functools.partial(_attn_kernel, n_lat_blocks, lam_init),
        out_shape=jax.ShapeDtypeStruct((n_rows, N_HEADS * LANES), BF16),
        grid=(n_batch, N_HEADS, n_q),
        in_specs=in_specs,
        out_specs=pl.BlockSpec((tq, LANES), o_map),
        scratch_shapes=[pltpu.VMEM((1, 2 * tq), F32), pltpu.VMEM((vrows, 2 * tq), F32),
                        pltpu.VMEM((2, max(ATTN_TK, ctx_len), 2 * tq), F32)],
        compiler_params=_cparams("parallel", "parallel", "parallel"),
        name="diff_attention_lat" if latent else "diff_attention_ctx",
    )(*args, lp, g)


def _merge_kernel(tiles_per_batch, n_batch, n_a, orf_ref, orb_ref, ogf_ref, ogb_ref, rg_ref, gr_ref, *refs):
    n_x = 2 if n_a else 1
    mod_ref, gpost_ref, gpre_ref, glag_ref, wout_ref, wrt_ref, xo_ref, hp_ref, sc_ref = refs[2 * n_x:]
    dif = _pick_rows(n_a, *refs[:2]) if n_a else refs[0][...]
    x = _pick_rows(n_a, *refs[2:4]) if n_a else refs[1][...]
    d = x.shape[-1]
    r = jnp.minimum(pl.program_id(0) // tiles_per_batch, n_batch)
    gate1 = mod_ref[pl.ds(r, 1), 2 * d:3 * d]
    shift2 = mod_ref[pl.ds(r, 1), 3 * d:4 * d]
    scale2 = mod_ref[pl.ds(r, 1), 4 * d:5 * d]
    gi = lax.broadcasted_iota(I32, (256, 256), 0) // RET_DV
    gj = lax.broadcasted_iota(I32, (256, 256), 1) // RET_DV
    group_mean = jnp.where(gi == gj, 1.0 / RET_DV, 0.0).astype(BF16)

    def head_norm(o):
        ms = jnp.dot((o * o).astype(BF16), group_mean, preferred_element_type=F32)
        return o * lax.rsqrt(ms + EPS)

    ret = head_norm(orf_ref[...] + orb_ref[...]) * _silu(rg_ref[...].astype(F32))
    gla = head_norm(ogf_ref[...] + ogb_ref[...]) * glag_ref[...] * _silu(gr_ref[...].astype(F32))
    m = (jnp.dot(ret.astype(BF16), wout_ref[0:256, :], preferred_element_type=F32)
         + jnp.dot(gla.astype(BF16), wout_ref[256:512, :], preferred_element_type=F32)
         + jnp.dot(dif, wout_ref[512:1024, :], preferred_element_type=F32))
    x_new = x + gate1 * (_rms(m) * gpost_ref[...])
    xo_ref[...] = x_new
    h2 = _rms(x_new) * gpre_ref[...] * (1.0 + scale2) + shift2
    _tile_major_store(hp_ref, _pack_halves(h2[:, :d // 2], h2[:, d // 2:]))
    def split(a):
        hi = a.astype(BF16)
        return hi, (a - hi.astype(F32)).astype(BF16)

    h_hi, h_lo = split(h2)
    w_hi, w_lo = split(wrt_ref[...])
    nt = functools.partial(lax.dot_general, dimension_numbers=NT_DIMS, preferred_element_type=F32)
    logits = nt(w_hi, h_hi) + (nt(w_hi, h_lo) + nt(w_lo, h_hi))
    sc_ref[...] = 1.0 / (1.0 + jnp.exp(-logits))


def _merge(o_rf, o_rb, o_gf, o_gb, ret, gla, dif, xs, mod, g_post, g_pre_ffn, gla_g, w_out, w_rt,
           n_rows, n_batch, seq):
    tm = TOKEN_TILE
    row = lambda i: (i, 0)
    gate_col = lambda i: (i, 2)
    const = lambda i: (0, 0)
    if isinstance(xs, tuple):
        d = xs[0].shape[1]
        n_a, x_specs = _split_rows(*xs, tm, d)
        _, dif_specs = _split_rows(*dif, tm, 512)
    else:
        d = xs.shape[1]
        n_a, x_specs, dif_specs = 0, [pl.BlockSpec((tm, d), row)], [pl.BlockSpec((tm, 512), row)]
        xs, dif = (xs,), (dif,)
    return pl.pallas_call(
        functools.partial(_merge_kernel, seq // tm, n_batch, n_a),
        out_shape=(jax.ShapeDtypeStruct((n_rows, d), F32), jax.ShapeDtypeStruct((n_rows * HP_CHUNKS, LANES), U32),
                   jax.ShapeDtypeStruct((N_EXPERTS, n_rows), F32)),
        grid=(n_rows // tm,),
        in_specs=[
            pl.BlockSpec((tm, 256), row), pl.BlockSpec((tm, 256), row),
            pl.BlockSpec((tm, 256), row), pl.BlockSpec((tm, 256), row),
            pl.BlockSpec((tm, 256), gate_col), pl.BlockSpec((tm, 256), gate_col), *dif_specs, *x_specs,
            pl.BlockSpec(mod.shape, const), pl.BlockSpec((1, d), const), pl.BlockSpec((1, d), const),
            pl.BlockSpec((1, 256), const), pl.BlockSpec(w_out.shape, const), pl.BlockSpec(w_rt.shape, const),
        ],
        out_specs=(pl.BlockSpec((tm, d), row), pl.BlockSpec((tm * HP_CHUNKS, LANES), row),
                   pl.BlockSpec((N_EXPERTS, tm), lambda i: (0, i))),
        compiler_params=_cparams("parallel"),
        name="merge",
    )(o_rf, o_rb, o_gf, o_gb, ret, gla, *dif, *xs, mod, g_post, g_pre_ffn, gla_g, w_out, w_rt)


def _route_kernel(sc_ref, bias_ref, posa_ref, wts_ref, pos_ref, meta_ref):
    tt = sc_ref.shape[-1]
    scores = sc_ref[...]
    sel = scores + bias_ref[...][:, 0:1]
    sub = lax.broadcasted_iota(I32, (GROUP_SIZE, tt), 0)
    neg = -jnp.inf
    gscore = []
    for g in range(N_GROUPS):
        xg = sel[g * GROUP_SIZE:(g + 1) * GROUP_SIZE]
        m1 = jnp.max(xg, axis=0, keepdims=True)
        i1 = jnp.min(jnp.where(xg == m1, sub, GROUP_SIZE), axis=0, keepdims=True)
        m2 = jnp.max(jnp.where(sub == i1, neg, xg), axis=0, keepdims=True)
        gscore.append(m1 + m2)
    rows = []
    for g in range(N_GROUPS):
        rank = jnp.zeros((1, tt), I32)
        for o in range(N_GROUPS):
            if o == g:
                continue
            ahead = (gscore[o] >= gscore[g]) if o < g else (gscore[o] > gscore[g])
            rank = rank + ahead.astype(I32)
        rows.append(jnp.where(rank < TOPK_GROUPS, sel[g * GROUP_SIZE:(g + 1) * GROUP_SIZE], neg))
    masked = jnp.concatenate(rows, axis=0)
    eio = lax.broadcasted_iota(I32, (N_EXPERTS, tt), 0)
    member = jnp.zeros((N_EXPERTS, tt), F32)
    idxs, ws = [], []
    for _ in range(TOP_K):
        m = jnp.max(masked, axis=0, keepdims=True)
        i = jnp.min(jnp.where(masked == m, eio, N_EXPERTS), axis=0, keepdims=True)
        hit = eio == i
        idxs.append(i)
        ws.append(jnp.sum(jnp.where(hit, scores, 0.0), axis=0, keepdims=True))
        member = jnp.where(hit, 1.0, member)
        masked = jnp.where(hit, neg, masked)
    wsum = ws[0]
    for w in ws[1:]:
        wsum = wsum + w
    ti = lax.broadcasted_iota(I32, (tt, tt), 0)
    tj = lax.broadcasted_iota(I32, (tt, tt), 1)
    before = jnp.where(ti < tj, 1.0, 0.0).astype(BF16)
    rank_in_e = jnp.dot(member.astype(BF16), before, preferred_element_type=F32)
    cnt = jnp.sum(member, axis=1, keepdims=True)
    padded = jnp.floor((cnt + (SUBLANES - 1)) * (1.0 / SUBLANES)) * SUBLANES
    ei = lax.broadcasted_iota(I32, (N_EXPERTS, N_EXPERTS), 0)
    ej = lax.broadcasted_iota(I32, (N_EXPERTS, N_EXPERTS), 1)
    lower = jnp.where(ej < ei, 1.0, 0.0)
    off = jnp.dot(lower, jnp.broadcast_to(padded, (N_EXPERTS, LANES)), precision=HIGHEST,
                  preferred_element_type=F32)
    slot = rank_in_e + off[:, 0:1]
    zrow_i = jnp.zeros((SUBLANES - TOP_K, tt), I32)
    zrow_f = jnp.zeros((SUBLANES - TOP_K, tt), F32)
    pos = [jnp.sum(jnp.where(eio == i, slot, 0.0), axis=0, keepdims=True).astype(I32) for i in idxs]
    pos = jnp.concatenate(pos + [zrow_i], axis=0)
    wts_ref[...] = jnp.concatenate([w / wsum * ROUTED_SCALE for w in ws] + [zrow_f], axis=0)
    pos_ref[...] = pos
    posa_ref[...] = _tile_major_addr(pos, HP_CHUNKS)
    meta_ref[0] = jnp.concatenate([jnp.broadcast_to(cnt, (N_EXPERTS, LANES)), off], axis=1).astype(I32)


def _route(scores_t, bias, tt, row0, n):
    tile0 = row0 // tt
    tok = lambda i: (0, i)
    return pl.pallas_call(
        _route_kernel,
        out_shape=(jax.ShapeDtypeStruct((SUBLANES, n), I32), jax.ShapeDtypeStruct((SUBLANES, n), F32),
                   jax.ShapeDtypeStruct((SUBLANES, n), I32),
                   jax.ShapeDtypeStruct((n // tt, N_EXPERTS, 2 * LANES), I32)),
        grid=(n // tt,),
        in_specs=[pl.BlockSpec((N_EXPERTS, tt), lambda i: (0, i + tile0)),
                  pl.BlockSpec((N_EXPERTS, LANES), lambda i: (0, 0))],
        out_specs=(pl.BlockSpec((SUBLANES, tt), tok), pl.BlockSpec((SUBLANES, tt), tok),
                   pl.BlockSpec((SUBLANES, tt), tok),
                   pl.BlockSpec((1, N_EXPERTS, 2 * LANES), lambda i: (i, 0, 0))),
        compiler_params=_cparams("parallel"),
        name="route",
    )(scores_t, jnp.broadcast_to(bias[:, None], (N_EXPERTS, LANES)))


def _moe_kernel(group, block, has_prev, pos_ref, posa_ref, wts_ref, meta_ref, hp_ref, weg_ref, weu_ref, wed_ref,
                *refs):
    o_ref, xg_ref, y_ref, rt_ref = refs[1:] if has_prev else refs
    d = weg_ref.shape[1]
    half = d // 2
    tt = hp_ref.shape[0] // HP_CHUNKS
    eg = pl.program_id(1)

    @pl.when((pl.program_id(0) == 0) & (eg == 0))
    def _():
        def zero(i, carry):
            for u in range(SUBLANES):
                rt_ref[i * SUBLANES + u] = 0
            return carry
        lax.fori_loop(0, rt_ref.shape[0] // SUBLANES, zero, 0)

    @pl.when(eg == 0)
    def _():
        def scatter(g, carry):
            base = g * (SUBLANES * HP_CHUNKS)
            for u in range(SUBLANES):
                for k in range(TOP_K):
                    rt_ref[pos_ref[(g * SUBLANES + u) * SUBLANES + k]] = base + u
            return carry
        lax.fori_loop(0, tt // SUBLANES, scatter, 0)

    sub = lax.broadcasted_iota(I32, (SUBLANES, LANES), 0)

    def expert_block(ge, buf, base, n_valid):
        base = pl.multiple_of(base, SUBLANES)
        for i in range(block):
            xg_ref[buf, pl.ds(_tile_major_addr(i, HP_CHUNKS), HP_CHUNKS, stride=SUBLANES), :] = (
                hp_ref[pl.ds(rt_ref[base + i], HP_CHUNKS, stride=SUBLANES), :])
        yield
        halves = [_unpack_halves(col) for col in _tile_major_columns(xg_ref, block, HP_CHUNKS, (buf,))]
        xa = jnp.concatenate([h[0] for h in halves], axis=1).astype(BF16)
        xb = jnp.concatenate([h[1] for h in halves], axis=1).astype(BF16)
        hg = (jnp.dot(xa, weg_ref[ge, 0:half, :], preferred_element_type=F32)
              + jnp.dot(xb, weg_ref[ge, half:d, :], preferred_element_type=F32))
        hu = (jnp.dot(xa, weu_ref[ge, 0:half, :], preferred_element_type=F32)
              + jnp.dot(xb, weu_ref[ge, half:d, :], preferred_element_type=F32))
        yield
        y = jnp.dot((_silu(hg) * hu).astype(BF16), wed_ref[ge], preferred_element_type=F32)
        yield
        packed = _pack_halves(y[:, :half], y[:, half:])
        row0 = pl.multiple_of(base * HP_CHUNKS, SUBLANES * HP_CHUNKS)
        for g in range(block // SUBLANES):
            keep = sub < n_valid - g * SUBLANES
            for c in range(HP_CHUNKS):
                pltpu.store(y_ref.at[pl.ds(row0 + (g * HP_CHUNKS + c) * SUBLANES, SUBLANES), :],
                            packed[g * SUBLANES:(g + 1) * SUBLANES, c * LANES:(c + 1) * LANES], mask=keep)

    cnts = [meta_ref[0, eg * group + ge] for ge in range(group)]
    offs = [meta_ref[1, eg * group + ge] for ge in range(group)]
    _round_robin([expert_block(ge, ge, offs[ge], cnts[ge]) for ge in range(group)])
    for ge in range(group):
        def more(j, carry, ge=ge):
            for _ in expert_block(ge, ge, offs[ge] + j * block, cnts[ge] - j * block):
                pass
            return carry
        lax.fori_loop(1, (cnts[ge] + (block - 1)) // block, more, 0)

    @pl.when(eg == N_EXPERTS // group - 1)
    def _():
        def tokens(g, carry):
            for u in range(SUBLANES):
                e0 = (g * SUBLANES + u) * SUBLANES
                acc_a = jnp.zeros((HP_CHUNKS, LANES), F32)
                acc_b = jnp.zeros((HP_CHUNKS, LANES), F32)
                for k in range(TOP_K):
                    ya, yb = _unpack_halves(y_ref[pl.ds(posa_ref[e0 + k], HP_CHUNKS, stride=SUBLANES), :])
                    w = wts_ref[e0 + k]
                    acc_a = acc_a + w * ya
                    acc_b = acc_b + w * yb
                out0 = pl.multiple_of(g * (2 * HP_CHUNKS * SUBLANES), 2 * HP_CHUNKS * SUBLANES) + u
                o_ref[pl.ds(out0, HP_CHUNKS, stride=SUBLANES), :] = acc_a
                o_ref[pl.ds(out0 + HP_CHUNKS * SUBLANES, HP_CHUNKS, stride=SUBLANES), :] = acc_b
            return carry

        lax.fori_loop(0, tt // SUBLANES, tokens, 0)


def _moe_segment(prev, scores_t, hp, router_bias, expert_w, layer, row0, n, tt):
    w_eg, w_eu, w_ed = expert_w
    d, de = w_eg.shape[1:]
    half = d // 2
    n_all = hp.shape[0] // HP_CHUNKS
    mean = tt * TOP_K / N_EXPERTS
    block = int(-(-(mean + MOE_BLOCK_SIGMAS * math.sqrt(mean * (1.0 - 1.0 / N_EXPERTS))) // 16) * 16)
    group = 1
    while group * 2 <= min(MOE_MAX_GROUP, MOE_ROWS_IN_FLIGHT // block) and N_EXPERTS % (group * 2) == 0:
        group *= 2
    group0 = layer * (N_EXPERTS // group)
    n_tiles, tile0 = n // tt, row0 // tt
    n_slots = -(-(tt * TOP_K + N_EXPERTS * (SUBLANES - 1) + block) // (SUBLANES * LANES)) * SUBLANES * LANES
    posa, wts, pos, meta = _route(scores_t, router_bias, tt, row0, n)
    cnt, off = meta[:, :, 0], meta[:, :, LANES]

    def per_token(a):
        return a.T.reshape(-1)

    meta_s = jnp.zeros((n_tiles, SUBLANES, LANES), I32)
    meta_s = meta_s.at[:, 0, :N_EXPERTS].set(cnt).at[:, 1, :N_EXPERTS].set(off).reshape(-1, LANES)

    tile = lambda i, e: (i + tile0, 0)
    expert = lambda i, e: (group0 + e, 0, 0)
    smem = functools.partial(pl.BlockSpec, memory_space=pltpu.SMEM)
    once = pl.Buffered(1)
    flat = smem((SUBLANES * tt,), lambda i, e: (i,))
    in_specs = [
        flat, flat, flat, smem((SUBLANES, LANES), lambda i, e: (i, 0)),
        pl.BlockSpec((tt * HP_CHUNKS, LANES), tile),
        pl.BlockSpec((group, d, de), expert), pl.BlockSpec((group, d, de), expert),
        pl.BlockSpec((group, de, d), expert),
    ]
    args = [per_token(pos), per_token(posa), per_token(wts), meta_s, hp, w_eg, w_eu, w_ed]
    aliases = {}
    if prev is not None:
        in_specs.append(pl.BlockSpec(memory_space=pl.ANY))
        args.append(prev)
        aliases = {len(args) - 1: 0}
    return pl.pallas_call(
        functools.partial(_moe_kernel, group, block, prev is not None),
        out_shape=jax.ShapeDtypeStruct((n_all * 2 * HP_CHUNKS, LANES), F32),
        grid=(n_tiles, N_EXPERTS // group),
        in_specs=in_specs,
        out_specs=pl.BlockSpec((tt * 2 * HP_CHUNKS, LANES), tile, pipeline_mode=once),
        scratch_shapes=[pltpu.VMEM((group, block * HP_CHUNKS, LANES), U32),
                        pltpu.VMEM((n_slots * HP_CHUNKS, LANES), U32),
                        pltpu.SMEM((n_slots,), I32)],
        input_output_aliases=aliases,
        compiler_params=_cparams("arbitrary", "arbitrary"),
        name="moe",
    )(*args)


def _moe_final_kernel(tiles_per_batch, n_batch, routed_ref, hp_ref, x_ref, mod_ref, gpost_ref,
                      wsg_ref, wsu_ref, wsd_ref, o_ref):
    tm, d = x_ref.shape
    half = d // 2
    r = jnp.minimum(pl.program_id(0) // tiles_per_batch, n_batch)
    gate2 = mod_ref[pl.ds(r, 1), 5 * d:6 * d]
    halves = [_unpack_halves(col) for col in _tile_major_columns(hp_ref, tm, HP_CHUNKS)]
    xa = jnp.concatenate([h[0] for h in halves], axis=1).astype(BF16)
    xb = jnp.concatenate([h[1] for h in halves], axis=1).astype(BF16)
    hg = (jnp.dot(xa, wsg_ref[0:half, :], preferred_element_type=F32)
          + jnp.dot(xb, wsg_ref[half:d, :], preferred_element_type=F32))
    hu = (jnp.dot(xa, wsu_ref[0:half, :], preferred_element_type=F32)
          + jnp.dot(xb, wsu_ref[half:d, :], preferred_element_type=F32))
    routed = jnp.concatenate(_tile_major_columns(routed_ref, tm, 2 * HP_CHUNKS), axis=1)
    f = jnp.dot((_silu(hg) * hu).astype(BF16), wsd_ref[...], preferred_element_type=F32) + routed
    o_ref[...] = x_ref[...] + gate2 * (_rms(f) * gpost_ref[...])


def _moe_final(routed, hp, xs, mod, g_post, w_sg, w_su, w_sd, n_batch, seq):
    n, d = xs.shape
    tm = TOKEN_TILE
    row = lambda i: (i, 0)
    const = lambda i: (0, 0)
    return pl.pallas_call(
        functools.partial(_moe_final_kernel, seq // tm, n_batch),
        out_shape=jax.ShapeDtypeStruct((n, d), F32),
        grid=(n // tm,),
        in_specs=[pl.BlockSpec((tm * 2 * HP_CHUNKS, LANES), row), pl.BlockSpec((tm * HP_CHUNKS, LANES), row),
                  pl.BlockSpec((tm, d), row),
                  pl.BlockSpec(mod.shape, const), pl.BlockSpec((1, d), const),
                  pl.BlockSpec(w_sg.shape, const), pl.BlockSpec(w_su.shape, const), pl.BlockSpec(w_sd.shape, const)],
        out_specs=pl.BlockSpec((tm, d), row),
        compiler_params=_cparams("parallel"),
        name="moe_final",
    )(routed, hp, xs, mod, g_post, w_sg, w_su, w_sd)


def _moe_layer(scores_t, hp, xs, mod, router_bias, g_post, expert_w, shared_w, layer, n_lat, n_batch, seq):
    pick = lambda rows: next(t for t in MOE_TILES if rows % t == 0)
    routed = _moe_segment(None, scores_t, hp, router_bias, expert_w, layer, 0, n_lat, pick(n_lat))
    n_ctx = hp.shape[0] // HP_CHUNKS - n_lat
    if n_ctx:
        tt = next(t for t in MOE_TILES if n_ctx % t == 0 and n_lat % t == 0)
        routed = _moe_segment(routed, scores_t, hp, router_bias, expert_w, layer, n_lat, n_ctx, tt)
    return _moe_final(routed, hp, xs, mod, g_post, *shared_w, n_batch, seq)


def kernel(x, c, ctx, c_ctx, w_mod, b_mod, g_pre_mix, g_post_mix, g_pre_ffn, g_post_ffn, w_in, w_out,
           ret_decay_logit, gla_w_gate, gla_b_gate, gla_norm_g, diff_lambda, diff_norm_g,
           w_router, router_bias, w_exp_gate, w_exp_up, w_exp_down, w_sh_gate, w_sh_up, w_sh_down):
    n_batch, seq, d = x.shape
    ctx_len = ctx.shape[1]
    depth = w_mod.shape[0]
    n_lat = n_batch * seq
    assert seq % TOKEN_TILE == 0 and (n_batch * ctx_len) % TOKEN_TILE == 0 and n_batch < SUBLANES
    scan_rows = CHUNK * SCAN_CHUNKS
    assert seq % ATTN_TK == 0 and seq % scan_rows == 0 and ctx_len % scan_rows == 0 and n_lat % ctx_len == 0

    xs = (x.reshape(n_lat, d), ctx.reshape(n_batch * ctx_len, d))
    cond = jnp.zeros((SUBLANES, d), F32).at[:n_batch].set(c).at[n_batch].set(c_ctx)
    mods = _modulation(cond, w_mod, b_mod)
    tables = _rope_tables(seq, RET_DK, TOKEN_TILE) + _rope_tables(seq, DIFF_DH, TOKEN_TILE)
    lr0 = N_HEADS * (2 * RET_DK + 2 * RET_DV + 2 * GLA_DK + 2 * GLA_DV)
    row = lambda a: a[None, :]
    expert_w = tuple(w.astype(BF16).reshape((depth * N_EXPERTS,) + w.shape[2:])
                     for w in (w_exp_gate, w_exp_up, w_exp_down))

    for layer in range(depth):
        need_ctx = layer < depth - 1
        lam_init = 0.8 - 0.6 * math.exp(-0.3 * layer)
        mod = mods[layer]
        wl = w_in[layer]
        w_r = jnp.concatenate([wl[:, :lr0], wl[:, lr0:lr0 + GLA_LOWRANK],
                               jnp.zeros((d, LANES - GLA_LOWRANK), F32), wl[:, lr0 + GLA_LOWRANK:]],
                              axis=1).astype(BF16)
        ret, gla, glr, dqt, dk, vt = _in_projection(xs, mod, row(g_pre_mix[layer]), w_r, tables, n_batch, seq)

        o_rf, o_gf, o_rb, o_gb = _scan(ret, gla, glr, _scan_tables(ret_decay_logit[layer]), gla_w_gate[layer],
                                       gla_b_gate[layer], n_batch, seq, ctx_len)

        lp = jnp.zeros((SUBLANES, LANES), F32).at[:4, :DIFF_DH].set(diff_lambda[layer])
        g_col = jnp.broadcast_to(diff_norm_g[layer][:, None], (LANES, LANES))
        attn = functools.partial(_diff_attention, dqt, dk, vt, lp, g_col, lam_init, n_batch, seq, ctx_len)
        dif = attn(True)
        n_rows = n_lat
        if need_ctx:
            n_rows = n_lat + n_batch * ctx_len
            if isinstance(xs, tuple):
                dif = (dif, attn(False))
            else:
                dif = jnp.concatenate([dif, attn(False)], axis=0)
        elif isinstance(xs, tuple):
            xs = xs[0]

        xs, hp, scores_t = _merge(o_rf, o_rb, o_gf, o_gb, ret, gla, dif, xs, mod, row(g_post_mix[layer]),
                                  row(g_pre_ffn[layer]), row(jnp.tile(gla_norm_g[layer], N_HEADS)),
                                  w_out[layer].astype(BF16), w_router[layer].T, n_rows, n_batch, seq)
        shared_w = tuple(w[layer].astype(BF16) for w in (w_sh_gate, w_sh_up, w_sh_down))
        xs = _moe_layer(scores_t, hp, xs, mod, router_bias[layer], row(g_post_ffn[layer]), expert_w, shared_w,
                        layer, n_lat, n_batch, seq)
    return xs[:n_lat].reshape(n_batch, seq, d)
```

```python
import functools
import math

import jax
import jax.numpy as jnp
import numpy as np
from jax import lax
from jax.experimental import pallas as pl
from jax.experimental.pallas import tpu as pltpu

F32 = jnp.float32
BF16 = jnp.bfloat16
I32 = jnp.int32
U32 = jnp.uint32

GRID_W = 64
CHUNK = 128
N_HEADS = 4
RET_DK, RET_DV = 32, 64
GLA_DK, GLA_DV = 32, 64
GLA_LOWRANK = 16
GLA_TAU = 16.0
DIFF_DH = 64
ROPE_BASE = 10000.0
N_EXPERTS = 64
TOP_K = 6
N_GROUPS = 8
TOPK_GROUPS = 4
GROUP_SIZE = N_EXPERTS // N_GROUPS
ROUTED_SCALE = 2.5
EPS = 1e-6
GLA_SUB = 16
SCAN_CHUNKS = 2

LANES = 128
SUBLANES = 8
VMEM_LIMIT_BYTES = 56 * 1024 * 1024

TOKEN_TILE = 512
ATTN_TQ = 512
ATTN_TK = 512
ATTN_ONES = 16
MOE_TILES = (2048, 1024, 512)
MOE_ROWS_IN_FLIGHT = 512
MOE_MAX_GROUP = 4
MOE_BLOCK_SIGMAS = 4.0
HP_CHUNKS = 4

HIGHEST = lax.Precision.HIGHEST
NT_DIMS = (((1,), (1,)), ((), ()))
TN_DIMS = (((0,), (0,)), ((), ()))


def _cparams(*sem):
    return pltpu.CompilerParams(dimension_semantics=sem, vmem_limit_bytes=VMEM_LIMIT_BYTES)


def _log_sigmoid(x):
    return jnp.minimum(x, 0.0) - jnp.log(1.0 + jnp.exp(-jnp.abs(x)))


def _silu(x):
    return x * (1.0 / (1.0 + jnp.exp(-x)))


def _rms(x):
    return x * lax.rsqrt(jnp.mean(x * x, axis=-1, keepdims=True) + EPS)


def _pack_halves(a, b):
    ua = lax.bitcast_convert_type(a.astype(BF16).astype(F32), U32)
    ub = lax.bitcast_convert_type(b.astype(BF16).astype(F32), U32)
    return (ua & jnp.uint32(0xFFFF0000)) | (ub >> 16)


def _unpack_halves(w):
    a = lax.bitcast_convert_type(w & jnp.uint32(0xFFFF0000), F32)
    b = lax.bitcast_convert_type(w << 16, F32)
    return a, b


def _tile_major_store(ref, v, lead=()):
    k = v.shape[1] // LANES
    for g in range(v.shape[0] // SUBLANES):
        for c in range(k):
            blk = g * k + c
            ref[lead + (slice(blk * SUBLANES, (blk + 1) * SUBLANES), slice(None))] = (
                v[g * SUBLANES:(g + 1) * SUBLANES, c * LANES:(c + 1) * LANES])


def _tile_major_columns(ref, rows, k, lead=()):
    return [jnp.concatenate([ref[lead + (slice((g * k + c) * SUBLANES, (g * k + c + 1) * SUBLANES), slice(None))]
                             for g in range(rows // SUBLANES)], axis=0) for c in range(k)]


def _tile_major_addr(t, k):
    return (t >> 3) * (SUBLANES * k) + (t & (SUBLANES - 1))


def _mod_kernel(cond_ref, w_ref, b_ref, o_ref):
    a = _silu(cond_ref[...])
    o_ref[0] = jnp.dot(a, w_ref[0], precision=HIGHEST, preferred_element_type=F32) + b_ref[0]


def _modulation(cond, w_mod, b_mod):
    n_layers, d, d6 = w_mod.shape
    tn = 1024
    return pl.pallas_call(
        _mod_kernel,
        out_shape=jax.ShapeDtypeStruct((n_layers, SUBLANES, d6), F32),
        grid=(n_layers, d6 // tn),
        in_specs=[
            pl.BlockSpec((SUBLANES, d), lambda l, j: (0, 0)),
            pl.BlockSpec((1, d, tn), lambda l, j: (l, 0, j)),
            pl.BlockSpec((1, 1, tn), lambda l, j: (l, 0, j)),
        ],
        out_specs=pl.BlockSpec((1, SUBLANES, tn), lambda l, j: (l, 0, j)),
        compiler_params=_cparams("parallel", "parallel"),
        name="modulation",
    )(cond, w_mod, b_mod.reshape(n_layers, 1, d6))


def _rope(x, cos, sin, quarter):
    lane = lax.broadcasted_iota(I32, (1, LANES), 1)
    first = (lane % (2 * quarter)) < quarter
    outs = []
    for c in range(x.shape[-1] // LANES):
        xc = x[:, c * LANES:(c + 1) * LANES]
        partner = jnp.where(first, pltpu.roll(xc, LANES - quarter, 1), pltpu.roll(xc, quarter, 1))
        outs.append(xc * cos + partner * sin)
    return outs[0] if len(outs) == 1 else jnp.concatenate(outs, axis=-1)


def _split_rows(a, b, tm, width):
    n_a = a.shape[0] // tm
    return n_a, [pl.BlockSpec((tm, width), lambda i: (jnp.minimum(i, n_a - 1), 0)),
                 pl.BlockSpec((tm, width), lambda i: (jnp.maximum(i - n_a, 0), 0))]


def _pick_rows(n_a, a_ref, b_ref):
    return jnp.where(pl.program_id(0) < n_a, a_ref[...], b_ref[...])


def _inproj_kernel(tiles_per_batch, n_batch, n_a, *refs):
    n_x = 2 if n_a else 1
    (mod_ref, g_ref, w_ref, c32_ref, s32_ref, c64_ref, s64_ref,
     ret_ref, gla_ref, glr_ref, dqt_ref, dk_ref, vt_ref) = refs[n_x:]
    x = _pick_rows(n_a, *refs[:2]) if n_a else refs[0][...]
    d = x.shape[-1]
    r = jnp.minimum(pl.program_id(0) // tiles_per_batch, n_batch)
    shift = mod_ref[pl.ds(r, 1), 0:d]
    scale = mod_ref[pl.ds(r, 1), d:2 * d]
    h = (_rms(x) * g_ref[...] * (1.0 + scale) + shift).astype(BF16)

    def proj(lo, hi):
        return jnp.dot(h, w_ref[:, lo:hi], preferred_element_type=F32)

    c32, s32 = c32_ref[...], s32_ref[...]
    c64, s64 = c64_ref[...], s64_ref[...]
    ret = proj(0, 768)
    ret_ref[:, 0:128] = _rope(ret[:, 0:128], c32, s32, RET_DK // 4).astype(BF16)
    ret_ref[:, 128:256] = (_rope(ret[:, 128:256], c32, s32, RET_DK // 4) * RET_DK ** -0.5).astype(BF16)
    ret_ref[:, 256:768] = ret[:, 256:768].astype(BF16)
    gla = proj(768, 1536)
    gla_ref[:, 0:128] = (gla[:, 0:128] * GLA_DK ** -0.5).astype(BF16)
    gla_ref[:, 128:768] = gla[:, 128:768].astype(BF16)
    glr_ref[...] = proj(1536, 1664)
    dq = _rope(proj(1664, 2176), c64, s64, DIFF_DH // 4) * (DIFF_DH ** -0.5 * math.log2(math.e))
    for hd in range(N_HEADS):
        dqt_ref[hd * LANES:(hd + 1) * LANES, :] = dq[:, hd * LANES:(hd + 1) * LANES].T.astype(BF16)
    dk_ref[...] = _rope(proj(2176, 2688), c64, s64, DIFF_DH // 4).astype(BF16)
    dv = proj(2688, 3200)
    vrows = LANES + ATTN_ONES
    for hd in range(N_HEADS):
        vt_ref[hd * vrows:hd * vrows + LANES, :] = dv[:, hd * LANES:(hd + 1) * LANES].T.astype(BF16)
        vt_ref[hd * vrows + LANES:(hd + 1) * vrows, :] = jnp.ones((ATTN_ONES, dv.shape[0]), BF16)


def _in_projection(xs, mod, g_pre, w_r, tables, n_batch, seq):
    tm = TOKEN_TILE
    if isinstance(xs, tuple):
        d = xs[0].shape[1]
        n = xs[0].shape[0] + xs[1].shape[0]
        n_a, x_specs = _split_rows(*xs, tm, d)
    else:
        n, d = xs.shape
        n_a, x_specs, xs = 0, [pl.BlockSpec((tm, d), lambda i: (i, 0))], (xs,)
    tiles_per_batch = seq // tm
    n_lat_tiles = n_batch * tiles_per_batch
    c32, s32, c64, s64 = tables

    def tab_map(i):
        return (jnp.where(i < n_lat_tiles, i % tiles_per_batch, tiles_per_batch), 0)

    row = lambda i: (i, 0)
    const = lambda i: (0, 0)
    tab_spec = pl.BlockSpec((tm, LANES), tab_map)
    return pl.pallas_call(
        functools.partial(_inproj_kernel, tiles_per_batch, n_batch, n_a),
        out_shape=(
            jax.ShapeDtypeStruct((n, 768), BF16), jax.ShapeDtypeStruct((n, 768), BF16),
            jax.ShapeDtypeStruct((n, LANES), F32),
            jax.ShapeDtypeStruct((N_HEADS * LANES, n), BF16), jax.ShapeDtypeStruct((n, 512), BF16),
            jax.ShapeDtypeStruct((N_HEADS * (LANES + ATTN_ONES), n), BF16)),
        grid=(n // tm,),
        in_specs=x_specs + [
            pl.BlockSpec(mod.shape, const),
            pl.BlockSpec((1, d), const),
            pl.BlockSpec(w_r.shape, const),
            tab_spec, tab_spec, tab_spec, tab_spec,
        ],
        out_specs=(
            pl.BlockSpec((tm, 768), row), pl.BlockSpec((tm, 768), row), pl.BlockSpec((tm, LANES), row),
            pl.BlockSpec((N_HEADS * LANES, tm), lambda i: (0, i)), pl.BlockSpec((tm, 512), row),
            pl.BlockSpec((N_HEADS * (LANES + ATTN_ONES), tm), lambda i: (0, i))),
        compiler_params=_cparams("parallel"),
        name="in_projection",
    )(*xs, mod, g_pre, w_r, c32, s32, c64, s64)


def _rope_tables(seq, head_dim, extra_rows):
    half, quarter = head_dim // 2, head_dim // 4
    freqs = (ROPE_BASE ** (-np.arange(quarter, dtype=np.float32) / quarter)).astype(np.float32)
    t = np.arange(seq)
    row = (t // GRID_W).astype(np.float32)
    col = (t % GRID_W).astype(np.float32)
    j = np.arange(LANES) % head_dim
    jj = j % half
    pos = np.where((j < half)[None, :], row[:, None], col[:, None])
    ang = (pos * freqs[jj % quarter][None, :]).astype(np.float32)
    cos = np.cos(ang)
    sin = np.sin(ang) * np.where(jj < quarter, -1.0, 1.0)[None, :]
    cos = np.concatenate([cos, np.ones((extra_rows, LANES))], axis=0).astype(np.float32)
    sin = np.concatenate([sin, np.zeros((extra_rows, LANES))], axis=0).astype(np.float32)
    return jnp.asarray(cos), jnp.asarray(sin)


def _head_stack(x, width):
    lane = lax.broadcasted_iota(I32, (1, x.shape[-1]), 1)
    zero = jnp.zeros_like(x)
    return jnp.concatenate([jnp.where(lane // width == h, x, zero) for h in range(N_HEADS)], axis=0)


def _head_select(x4, rows):
    lane = lax.broadcasted_iota(I32, (1, x4.shape[-1]), 1)
    out = jnp.zeros((rows, x4.shape[-1]), F32)
    for h in range(N_HEADS):
        out = out + jnp.where(lane // RET_DV == h, x4[h * rows:(h + 1) * rows], 0.0)
    return out


def _scan_tables_kernel(rl_lane_ref, rl_rows_ref, rl_col_ref, dmat_ref, xi_ref, zeta_ref, gchunk_ref):
    c = CHUNK
    idx = lax.broadcasted_iota(I32, (c, 1), 0).astype(F32)
    ri4 = lax.broadcasted_iota(I32, (N_HEADS * c, c), 0) % c
    ci4 = lax.broadcasted_iota(I32, (N_HEADS * c, c), 1)
    dist = jnp.abs(ri4 - ci4).astype(F32)
    for d in range(2):
        lg_lane = _log_sigmoid(rl_lane_ref[d])
        lg_rows = _log_sigmoid(rl_rows_ref[d])
        lg_col = _log_sigmoid(rl_col_ref[d])
        att4 = (ri4 <= ci4) if d else (ri4 >= ci4)
        dmat_ref[d] = jnp.where(att4, jnp.exp(dist * lg_rows), 0.0)
        xi_ref[d] = jnp.exp(((c - idx) if d else (idx + 1.0)) * lg_lane)
        zeta_ref[d] = jnp.exp((idx if d else (c - 1.0 - idx)) * lg_lane)
        g_chunk = jnp.exp(float(c) * lg_col)
        gchunk_ref[d] = jnp.concatenate([g_chunk, g_chunk], axis=1)


def _scan_tables(ret_logit):
    c = CHUNK
    rl_lane = jnp.repeat(ret_logit, RET_DK, axis=1)[:, None, :]
    rl_rows = jnp.broadcast_to(jnp.repeat(ret_logit, c, axis=1)[:, :, None], (2, N_HEADS * c, c))
    rl_col = jnp.broadcast_to(jnp.repeat(ret_logit, RET_DK, axis=1)[:, :, None], (2, LANES, LANES))
    return pl.pallas_call(
        _scan_tables_kernel,
        out_shape=(jax.ShapeDtypeStruct((2, N_HEADS * c, c), F32), jax.ShapeDtypeStruct((2, c, LANES), F32),
                   jax.ShapeDtypeStruct((2, c, LANES), F32), jax.ShapeDtypeStruct((2, LANES, 2 * LANES), F32)),
        name="scan_tables",
    )(rl_lane, rl_rows, rl_col)


def _state_block_mask():
    return (lax.broadcasted_iota(I32, (LANES, 2 * LANES), 0) // RET_DK
            == lax.broadcasted_iota(I32, (LANES, 2 * LANES), 1) // RET_DV)


def _ret_chain(ret_ref, r0, dmat, xi, zeta, g_chunk, out):
    c = CHUNK
    q = ret_ref[r0:r0 + c, 0:128]
    k = ret_ref[r0:r0 + c, 128:256]
    v = ret_ref[r0:r0 + c, 256:512]
    s = lax.dot_general(_head_stack(q, RET_DK), k, NT_DIMS, preferred_element_type=F32)
    kz = (k.astype(F32) * zeta).astype(BF16)
    u = lax.dot_general(kz, v, TN_DIMS, preferred_element_type=F32)
    yield
    o4 = jnp.dot((s * dmat).astype(BF16), v, preferred_element_type=F32)
    out.update(qx=(q.astype(F32) * xi).astype(BF16), u=jnp.where(_state_block_mask(), u, 0.0), g=g_chunk)
    yield
    out.update(intra=_head_select(o4, c))


def _state_chain(parts, o_ref, s_ref):
    c = CHUNK
    s = s_ref[...]
    for r0, p in parts:
        o_ref[r0:r0 + c, :] = p["intra"] + jnp.dot(p["qx"], s.astype(BF16), preferred_element_type=F32)
        s = s * p["g"] + p["u"]
        yield
    s_ref[...] = s


def _gla_chain(rev, gla_ref, glr_ref, r0, gw, gb, out):
    c = CHUNK
    ri = lax.broadcasted_iota(I32, (c, c), 0)
    ci = lax.broadcasted_iota(I32, (c, c), 1)
    attends = (ri <= ci) if rev else (ri >= ci)
    bd = _state_block_mask()
    gq = gla_ref[r0:r0 + c, 0:128].astype(F32)
    gk = gla_ref[r0:r0 + c, 128:256].astype(F32)
    gv = gla_ref[r0:r0 + c, 256:512]
    z = jnp.dot(glr_ref[r0:r0 + c, :], gw, precision=HIGHEST, preferred_element_type=F32) + gb
    yield
    la = _log_sigmoid(z) * (1.0 / GLA_TAU)
    if rev:
        first = (ri // GLA_SUB) * GLA_SUB + (GLA_SUB - 1)
        ref_sel = ci >= first
    else:
        first = (ri // GLA_SUB) * GLA_SUB
        ref_sel = ci <= first
    sel = jnp.concatenate([jnp.where(attends, 1.0, 0.0), jnp.where(ref_sel, 1.0, 0.0)], axis=0).astype(BF16)
    la_hi = la.astype(BF16)
    la_lo = (la - la_hi.astype(F32)).astype(BF16)
    sums = (jnp.dot(sel, la_hi, preferred_element_type=F32) + jnp.dot(sel, la_lo, preferred_element_type=F32))
    b, refrow = sums[0:c], sums[c:2 * c]
    yield
    qs = gq * jnp.exp(b - refrow)
    b_last = b[0:1] if rev else b[c - 1:c]
    kz = (gk * jnp.exp(b_last - b)).astype(BF16)
    u = lax.dot_general(kz, gv, TN_DIMS, preferred_element_type=F32)
    eye = lax.broadcasted_iota(I32, (LANES, LANES), 0) == lax.broadcasted_iota(I32, (LANES, LANES), 1)
    g_col = jnp.sum(jnp.where(eye, jnp.exp(b_last), 0.0), axis=1, keepdims=True)
    jcol = lax.broadcasted_iota(I32, (c, 1), 0)
    rr = lax.broadcasted_iota(I32, (N_HEADS * GLA_SUB, c), 0) % GLA_SUB
    cc = lax.broadcasted_iota(I32, (N_HEADS * GLA_SUB, c), 1)
    pieces = []
    for blk in range(c // GLA_SUB):
        lo = blk * GLA_SUB
        ref_b = refrow[lo:lo + 1]
        seen = (jcol >= lo) if rev else (jcol < lo + GLA_SUB)
        ks = (gk * jnp.exp(jnp.where(seen, ref_b - b, -jnp.inf))).astype(BF16)
        qz = _head_stack(qs[lo:lo + GLA_SUB], GLA_DK).astype(BF16)
        att = lax.dot_general(qz, ks, NT_DIMS, preferred_element_type=F32)
        ok = (cc >= rr + lo) if rev else (cc <= rr + lo)
        att = jnp.where(ok, att, 0.0).astype(BF16)
        pieces.append(_head_select(jnp.dot(att, gv, preferred_element_type=F32), GLA_SUB))
        if blk % 2:
            yield
    out.update(intra=jnp.concatenate(pieces, axis=0), qx=(gq * jnp.exp(b)).astype(BF16),
               u=jnp.where(bd, u, 0.0), g=g_col)


def _round_robin(chains):
    while chains:
        chains = [ch for ch in chains if next(ch, True) is None]


def _scan_kernel(ret_f, gla_f, glr_f, ret_b, gla_b, glr_b, dmat_ref, xi_ref, zeta_ref, gchunk_ref, gw_ref, gb_ref,
                 orf_ref, ogf_ref, orb_ref, ogb_ref, sr_ref, sg_ref):
    @pl.when(pl.program_id(1) == 0)
    def _():
        sr_ref[...] = jnp.zeros_like(sr_ref)
        sg_ref[...] = jnp.zeros_like(sg_ref)

    chains, state_chains = [], []
    for d, (ret_ref, gla_ref, glr_ref, ore_ref, ogl_ref) in enumerate(
            ((ret_f, gla_f, glr_f, orf_ref, ogf_ref), (ret_b, gla_b, glr_b, orb_ref, ogb_ref))):
        order = range(SCAN_CHUNKS - 1, -1, -1) if d else range(SCAN_CHUNKS)
        gla_parts, ret_parts = [], []
        for j in order:
            r0 = j * CHUNK
            gla_parts.append((r0, {}))
            ret_parts.append((r0, {}))
            chains.append(_gla_chain(bool(d), gla_ref, glr_ref, r0, gw_ref[d], gb_ref[d], gla_parts[-1][1]))
            chains.append(_ret_chain(ret_ref, r0, dmat_ref[d], xi_ref[d], zeta_ref[d], gchunk_ref[d],
                                     ret_parts[-1][1]))
        state_chains.append(_state_chain(gla_parts, ogl_ref, sg_ref.at[d]))
        state_chains.append(_state_chain(ret_parts, ore_ref, sr_ref.at[d]))
    _round_robin(chains)
    _round_robin(state_chains)


def _scan(ret, gla, glr, tables, gla_w, gla_b, n_batch, seq, ctx_len):
    n = ret.shape[0]
    c = CHUNK * SCAN_CHUNKS
    nc_ctx, nc_lat = ctx_len // c, seq // c
    n_steps = nc_ctx + nc_lat
    ctx_base = n_batch * nc_lat

    def fwd(b, s):
        return (jnp.where(s < nc_ctx, ctx_base + b * nc_ctx + s, b * nc_lat + (s - nc_ctx)), 0)

    def bwd(b, s):
        return (jnp.where(s < nc_ctx, ctx_base + b * nc_ctx + (nc_ctx - 1 - s), b * nc_lat + (n_steps - 1 - s)), 0)

    const3 = lambda b, s: (0, 0, 0)
    gw = jnp.zeros((2, LANES, LANES), F32).at[:, :GLA_LOWRANK].set(gla_w)
    gb = gla_b[:, None, :]
    chain_in = lambda m: [pl.BlockSpec((c, 768), m), pl.BlockSpec((c, 768), m), pl.BlockSpec((c, LANES), m)]
    o_sds = jax.ShapeDtypeStruct((n, 256), F32)
    return pl.pallas_call(
        _scan_kernel,
        out_shape=(o_sds, o_sds, o_sds, o_sds),
        grid=(n_batch, n_steps),
        in_specs=chain_in(fwd) + chain_in(bwd) + [pl.BlockSpec(t.shape, const3) for t in tables]
        + [pl.BlockSpec(gw.shape, const3), pl.BlockSpec(gb.shape, const3)],
        out_specs=(pl.BlockSpec((c, 256), fwd), pl.BlockSpec((c, 256), fwd),
                   pl.BlockSpec((c, 256), bwd), pl.BlockSpec((c, 256), bwd)),
        scratch_shapes=[pltpu.VMEM((2, LANES, 2 * LANES), F32), pltpu.VMEM((2, LANES, 2 * LANES), F32)],
        compiler_params=_cparams("arbitrary", "arbitrary"),
        name="scan",
    )(ret, gla, glr, ret, gla, glr, *tables, gw, gb)


def _attn_kernel(n_lat_blocks, lam_init, *refs):
    if n_lat_blocks:
        q_ref, kc_ref, vtc_ref, kl_ref, vtl_ref, lp_ref, g_ref, o_ref, m_ref, acc_ref, st_ref = refs
    else:
        q_ref, kc_ref, vtc_ref, lp_ref, g_ref, o_ref, m_ref, acc_ref, st_ref = refs
    tq = q_ref.shape[1]
    q = q_ref[...]
    dim = lax.broadcasted_iota(I32, (LANES, 1), 0)
    zero = jnp.zeros_like(q)
    qt = jnp.concatenate([jnp.where(dim < DIFF_DH, q, zero), jnp.where(dim >= DIFF_DH, q, zero)], axis=1)
    m_ref[...] = jnp.full(m_ref.shape, -jnp.inf, F32)
    acc_ref[...] = jnp.zeros_like(acc_ref)

    def scores(slot, kb):
        st_ref[slot, 0:kb.shape[0], :] = jnp.dot(kb, qt, preferred_element_type=F32)

    def absorb(slot, vtb):
        st = st_ref[slot, 0:vtb.shape[1], :]
        m_prev = m_ref[...]
        m_new = jnp.maximum(m_prev, jnp.max(st, axis=0, keepdims=True))
        alpha = jnp.exp2(m_prev - m_new)
        p = jnp.exp2(st - m_new).astype(BF16)
        acc_ref[...] = alpha * acc_ref[...] + jnp.dot(vtb, p, preferred_element_type=F32)
        m_ref[...] = m_new

    def k_lat(blk):
        return kl_ref[pl.ds(pl.multiple_of(blk * ATTN_TK, ATTN_TK), ATTN_TK), :]

    def vt_lat(blk):
        return vtl_ref[:, pl.ds(pl.multiple_of(blk * ATTN_TK, ATTN_TK), ATTN_TK)]

    scores(0, kc_ref[...])
    if not n_lat_blocks:
        absorb(0, vtc_ref[...])
    else:
        scores(1, k_lat(0))
        absorb(0, vtc_ref[...])
        n_pairs = (n_lat_blocks - 1) // 2

        def body(i, carry):
            scores(0, k_lat(2 * i + 1))
            absorb(1, vt_lat(2 * i))
            scores(1, k_lat(2 * i + 2))
            absorb(0, vt_lat(2 * i + 1))
            return carry
        lax.fori_loop(0, n_pairs, body, 0)
        done = 2 * n_pairs
        if n_lat_blocks - done == 2:
            scores(0, k_lat(done + 1))
            absorb(1, vt_lat(done))
            absorb(0, vt_lat(done + 1))
        else:
            absorb(1, vt_lat(done))

    lp = lp_ref[...]
    lam = (jnp.exp(jnp.sum(lp[0:1] * lp[1:2], axis=1, keepdims=True))
           - jnp.exp(jnp.sum(lp[2:3] * lp[3:4], axis=1, keepdims=True)) + lam_init)
    acc = acc_ref[...]
    o1 = acc[0:LANES, :tq] / acc[LANES:LANES + 1, :tq]
    o2 = acc[0:LANES, tq:] / acc[LANES:LANES + 1, tq:]
    ot = o1 - lam * o2
    ot = ot * lax.rsqrt(jnp.mean(ot * ot, axis=0, keepdims=True) + EPS) * g_ref[...][:, 0:1] * (1.0 - lam_init)
    o_ref[...] = ot.T.astype(o_ref.dtype)


def _diff_attention(dqt, dk, vt, lp, g, lam_init, n_batch, seq, ctx_len, latent):
    ctx_blk0 = (n_batch * seq) // ctx_len
    vrows = LANES + ATTN_ONES
    kc_spec = pl.BlockSpec((ctx_len, LANES), lambda b, h, i: (ctx_blk0 + b, h))
    vtc_spec = pl.BlockSpec((vrows, ctx_len), lambda b, h, i: (h, ctx_blk0 + b))
    const = lambda b, h, i: (0, 0)
    if latent:
        tq = ATTN_TQ
        n_q = seq // tq
        o_map = lambda b, h, i: (b * n_q + i, h)
        in_specs = [pl.BlockSpec((LANES, tq), lambda b, h, i: (h, b * n_q + i)), kc_spec, vtc_spec,
                    pl.BlockSpec((seq, LANES), lambda b, h, i: (b, h)),
                    pl.BlockSpec((vrows, seq), lambda b, h, i: (h, b))]
        args = (dqt, dk, vt, dk, vt)
        n_rows, n_lat_blocks = n_batch * seq, seq // ATTN_TK
    else:
        tq, n_q = ctx_len, 1
        in_specs = [pl.BlockSpec((LANES, ctx_len), lambda b, h, i: (h, ctx_blk0 + b)), kc_spec, vtc_spec]
        args = (dqt, dk, vt)
        n_rows, o_map, n_lat_blocks = n_batch * ctx_len, (lambda b, h, i: (b, h)), 0
    in_specs += [pl.BlockSpec(lp.shape, const), pl.BlockSpec((LANES, LANES), const)]
    return pl.pallas_call(
        functools.partial(_attn_kernel, n_lat_blocks, lam_init),
        out_shape=jax.ShapeDtypeStruct((n_rows, N_HEADS * LANES), BF16),
        grid=(n_batch, N_HEADS, n_q),
        in_specs=in_specs,
        out_specs=pl.BlockSpec((tq, LANES), o_map),
        scratch_shapes=[pltpu.VMEM((1, 2 * tq), F32), pltpu.VMEM((vrows, 2 * tq), F32),
                        pltpu.VMEM((2, max(ATTN_TK, ctx_len), 2 * tq), F32)],
        compiler_params=_cparams("parallel", "parallel", "parallel"),
        name="diff_attention_lat" if latent else "diff_attention_ctx",
    )(*args, lp, g)


def _merge_kernel(tiles_per_batch, n_batch, n_a, orf_ref, orb_ref, ogf_ref, ogb_ref, rg_ref, gr_ref, *refs):
    n_x = 2 if n_a else 1
    mod_ref, gpost_ref, gpre_ref, glag_ref, wout_ref, wrt_ref, xo_ref, hp_ref, sc_ref = refs[2 * n_x:]
    dif = _pick_rows(n_a, *refs[:2]) if n_a else refs[0][...]
    x = _pick_rows(n_a, *refs[2:4]) if n_a else refs[1][...]
    d = x.shape[-1]
    r = jnp.minimum(pl.program_id(0) // tiles_per_batch, n_batch)
    gate1 = mod_ref[pl.ds(r, 1), 2 * d:3 * d]
    shift2 = mod_ref[pl.ds(r, 1), 3 * d:4 * d]
    scale2 = mod_ref[pl.ds(r, 1), 4 * d:5 * d]
    gi = lax.broadcasted_iota(I32, (256, 256), 0) // RET_DV
    gj = lax.broadcasted_iota(I32, (256, 256), 1) // RET_DV
    group_mean = jnp.where(gi == gj, 1.0 / RET_DV, 0.0).astype(BF16)

    def head_norm(o):
        ms = jnp.dot((o * o).astype(BF16), group_mean, preferred_element_type=F32)
        return o * lax.rsqrt(ms + EPS)

    ret = head_norm(orf_ref[...] + orb_ref[...]) * _silu(rg_ref[...].astype(F32))
    gla = head_norm(ogf_ref[...] + ogb_ref[...]) * glag_ref[...] * _silu(gr_ref[...].astype(F32))
    m = (jnp.dot(ret.astype(BF16), wout_ref[0:256, :], preferred_element_type=F32)
         + jnp.dot(gla.astype(BF16), wout_ref[256:512, :], preferred_element_type=F32)
         + jnp.dot(dif, wout_ref[512:1024, :], preferred_element_type=F32))
    x_new = x + gate1 * (_rms(m) * gpost_ref[...])
    xo_ref[...] = x_new
    h2 = _rms(x_new) * gpre_ref[...] * (1.0 + scale2) + shift2
    _tile_major_store(hp_ref, _pack_halves(h2[:, :d // 2], h2[:, d // 2:]))
    def split(a):
        hi = a.astype(BF16)
        return hi, (a - hi.astype(F32)).astype(BF16)

    h_hi, h_lo = split(h2)
    w_hi, w_lo = split(wrt_ref[...])
    nt = functools.partial(lax.dot_general, dimension_numbers=NT_DIMS, preferred_element_type=F32)
    logits = nt(w_hi, h_hi) + (nt(w_hi, h_lo) + nt(w_lo, h_hi))
    sc_ref[...] = 1.0 / (1.0 + jnp.exp(-logits))


def _merge(o_rf, o_rb, o_gf, o_gb, ret, gla, dif, xs, mod, g_post, g_pre_ffn, gla_g, w_out, w_rt,
           n_rows, n_batch, seq):
    tm = TOKEN_TILE
    row = lambda i: (i, 0)
    gate_col = lambda i: (i, 2)
    const = lambda i: (0, 0)
    if isinstance(xs, tuple):
        d = xs[0].shape[1]
        n_a, x_specs = _split_rows(*xs, tm, d)
        _, dif_specs = _split_rows(*dif, tm, 512)
    else:
        d = xs.shape[1]
        n_a, x_specs, dif_specs = 0, [pl.BlockSpec((tm, d), row)], [pl.BlockSpec((tm, 512), row)]
        xs, dif = (xs,), (dif,)
    return pl.pallas_call(
        functools.partial(_merge_kernel, seq // tm, n_batch, n_a),
        out_shape=(jax.ShapeDtypeStruct((n_rows, d), F32), jax.ShapeDtypeStruct((n_rows * HP_CHUNKS, LANES), U32),
                   jax.ShapeDtypeStruct((N_EXPERTS, n_rows), F32)),
        grid=(n_rows // tm,),
        in_specs=[
            pl.BlockSpec((tm, 256), row), pl.BlockSpec((tm, 256), row),
            pl.BlockSpec((tm, 256), row), pl.BlockSpec((tm, 256), row),
            pl.BlockSpec((tm, 256), gate_col), pl.BlockSpec((tm, 256), gate_col), *dif_specs, *x_specs,
            pl.BlockSpec(mod.shape, const), pl.BlockSpec((1, d), const), pl.BlockSpec((1, d), const),
            pl.BlockSpec((1, 256), const), pl.BlockSpec(w_out.shape, const), pl.BlockSpec(w_rt.shape, const),
        ],
        out_specs=(pl.BlockSpec((tm, d), row), pl.BlockSpec((tm * HP_CHUNKS, LANES), row),
                   pl.BlockSpec((N_EXPERTS, tm), lambda i: (0, i))),
        compiler_params=_cparams("parallel"),
        name="merge",
    )(o_rf, o_rb, o_gf, o_gb, ret, gla, *dif, *xs, mod, g_post, g_pre_ffn, gla_g, w_out, w_rt)


def _route_kernel(sc_ref, bias_ref, posa_ref, wts_ref, pos_ref, meta_ref):
    tt = sc_ref.shape[-1]
    scores = sc_ref[...]
    sel = scores + bias_ref[...][:, 0:1]
    sub = lax.broadcasted_iota(I32, (GROUP_SIZE, tt), 0)
    neg = -jnp.inf
    gscore = []
    for g in range(N_GROUPS):
        xg = sel[g * GROUP_SIZE:(g + 1) * GROUP_SIZE]
        m1 = jnp.max(xg, axis=0, keepdims=True)
        i1 = jnp.min(jnp.where(xg == m1, sub, GROUP_SIZE), axis=0, keepdims=True)
        m2 = jnp.max(jnp.where(sub == i1, neg, xg), axis=0, keepdims=True)
        gscore.append(m1 + m2)
    rows = []
    for g in range(N_GROUPS):
        rank = jnp.zeros((1, tt), I32)
        for o in range(N_GROUPS):
            if o == g:
                continue
            ahead = (gscore[o] >= gscore[g]) if o < g else (gscore[o] > gscore[g])
            rank = rank + ahead.astype(I32)
        rows.append(jnp.where(rank < TOPK_GROUPS, sel[g * GROUP_SIZE:(g + 1) * GROUP_SIZE], neg))
    masked = jnp.concatenate(rows, axis=0)
    eio = lax.broadcasted_iota(I32, (N_EXPERTS, tt), 0)
    member = jnp.zeros((N_EXPERTS, tt), F32)
    idxs, ws = [], []
    for _ in range(TOP_K):
        m = jnp.max(masked, axis=0, keepdims=True)
        i = jnp.min(jnp.where(masked == m, eio, N_EXPERTS), axis=0, keepdims=True)
        hit = eio == i
        idxs.append(i)
        ws.append(jnp.sum(jnp.where(hit, scores, 0.0), axis=0, keepdims=True))
        member = jnp.where(hit, 1.0, member)
        masked = jnp.where(hit, neg, masked)
    wsum = ws[0]
    for w in ws[1:]:
        wsum = wsum + w
    ti = lax.broadcasted_iota(I32, (tt, tt), 0)
    tj = lax.broadcasted_iota(I32, (tt, tt), 1)
    before = jnp.where(ti < tj, 1.0, 0.0).astype(BF16)
    rank_in_e = jnp.dot(member.astype(BF16), before, preferred_element_type=F32)
    cnt = jnp.sum(member, axis=1, keepdims=True)
    padded = jnp.floor((cnt + (SUBLANES - 1)) * (1.0 / SUBLANES)) * SUBLANES
    ei = lax.broadcasted_iota(I32, (N_EXPERTS, N_EXPERTS), 0)
    ej = lax.broadcasted_iota(I32, (N_EXPERTS, N_EXPERTS), 1)
    lower = jnp.where(ej < ei, 1.0, 0.0)
    off = jnp.dot(lower, jnp.broadcast_to(padded, (N_EXPERTS, LANES)), precision=HIGHEST,
                  preferred_element_type=F32)
    slot = rank_in_e + off[:, 0:1]
    zrow_i = jnp.zeros((SUBLANES - TOP_K, tt), I32)
    zrow_f = jnp.zeros((SUBLANES - TOP_K, tt), F32)
    pos = [jnp.sum(jnp.where(eio == i, slot, 0.0), axis=0, keepdims=True).astype(I32) for i in idxs]
    pos = jnp.concatenate(pos + [zrow_i], axis=0)
    wts_ref[...] = jnp.concatenate([w / wsum * ROUTED_SCALE for w in ws] + [zrow_f], axis=0)
    pos_ref[...] = pos
    posa_ref[...] = _tile_major_addr(pos, HP_CHUNKS)
    meta_ref[0] = jnp.concatenate([jnp.broadcast_to(cnt, (N_EXPERTS, LANES)), off], axis=1).astype(I32)


def _route(scores_t, bias, tt, row0, n):
    tile0 = row0 // tt
    tok = lambda i: (0, i)
    return pl.pallas_call(
        _route_kernel,
        out_shape=(jax.ShapeDtypeStruct((SUBLANES, n), I32), jax.ShapeDtypeStruct((SUBLANES, n), F32),
                   jax.ShapeDtypeStruct((SUBLANES, n), I32),
                   jax.ShapeDtypeStruct((n // tt, N_EXPERTS, 2 * LANES), I32)),
        grid=(n // tt,),
        in_specs=[pl.BlockSpec((N_EXPERTS, tt), lambda i: (0, i + tile0)),
                  pl.BlockSpec((N_EXPERTS, LANES), lambda i: (0, 0))],
        out_specs=(pl.BlockSpec((SUBLANES, tt), tok), pl.BlockSpec((SUBLANES, tt), tok),
                   pl.BlockSpec((SUBLANES, tt), tok),
                   pl.BlockSpec((1, N_EXPERTS, 2 * LANES), lambda i: (i, 0, 0))),
        compiler_params=_cparams("parallel"),
        name="route",
    )(scores_t, jnp.broadcast_to(bias[:, None], (N_EXPERTS, LANES)))


def _moe_kernel(group, block, has_prev, pos_ref, posa_ref, wts_ref, meta_ref, hp_ref, weg_ref, weu_ref, wed_ref,
                *refs):
    o_ref, xg_ref, y_ref, rt_ref = refs[1:] if has_prev else refs
    d = weg_ref.shape[1]
    half = d // 2
    tt = hp_ref.shape[0] // HP_CHUNKS
    eg = pl.program_id(1)

    @pl.when((pl.program_id(0) == 0) & (eg == 0))
    def _():
        def zero(i, carry):
            for u in range(SUBLANES):
                rt_ref[i * SUBLANES + u] = 0
            return carry
        lax.fori_loop(0, rt_ref.shape[0] // SUBLANES, zero, 0)

    @pl.when(eg == 0)
    def _():
        def scatter(g, carry):
            base = g * (SUBLANES * HP_CHUNKS)
            for u in range(SUBLANES):
                for k in range(TOP_K):
                    rt_ref[pos_ref[(g * SUBLANES + u) * SUBLANES + k]] = base + u
            return carry
        lax.fori_loop(0, tt // SUBLANES, scatter, 0)

    sub = lax.broadcasted_iota(I32, (SUBLANES, LANES), 0)

    def expert_block(ge, buf, base, n_valid):
        base = pl.multiple_of(base, SUBLANES)
        for i in range(block):
            xg_ref[buf, pl.ds(_tile_major_addr(i, HP_CHUNKS), HP_CHUNKS, stride=SUBLANES), :] = (
                hp_ref[pl.ds(rt_ref[base + i], HP_CHUNKS, stride=SUBLANES), :])
        yield
        halves = [_unpack_halves(col) for col in _tile_major_columns(xg_ref, block, HP_CHUNKS, (buf,))]
        xa = jnp.concatenate([h[0] for h in halves], axis=1).astype(BF16)
        xb = jnp.concatenate([h[1] for h in halves], axis=1).astype(BF16)
        hg = (jnp.dot(xa, weg_ref[ge, 0:half, :], preferred_element_type=F32)
              + jnp.dot(xb, weg_ref[ge, half:d, :], preferred_element_type=F32))
        hu = (jnp.dot(xa, weu_ref[ge, 0:half, :], preferred_element_type=F32)
              + jnp.dot(xb, weu_ref[ge, half:d, :], preferred_element_type=F32))
        yield
        y = jnp.dot((_silu(hg) * hu).astype(BF16), wed_ref[ge], preferred_element_type=F32)
        yield
        packed = _pack_halves(y[:, :half], y[:, half:])
        row0 = pl.multiple_of(base * HP_CHUNKS, SUBLANES * HP_CHUNKS)
        for g in range(block // SUBLANES):
            keep = sub < n_valid - g * SUBLANES
            for c in range(HP_CHUNKS):
                pltpu.store(y_ref.at[pl.ds(row0 + (g * HP_CHUNKS + c) * SUBLANES, SUBLANES), :],
                            packed[g * SUBLANES:(g + 1) * SUBLANES, c * LANES:(c + 1) * LANES], mask=keep)

    cnts = [meta_ref[0, eg * group + ge] for ge in range(group)]
    offs = [meta_ref[1, eg * group + ge] for ge in range(group)]
    _round_robin([expert_block(ge, ge, offs[ge], cnts[ge]) for ge in range(group)])
    for ge in range(group):
        def more(j, carry, ge=ge):
            for _ in expert_block(ge, ge, offs[ge] + j * block, cnts[ge] - j * block):
                pass
            return carry
        lax.fori_loop(1, (cnts[ge] + (block - 1)) // block, more, 0)

    @pl.when(eg == N_EXPERTS // group - 1)
    def _():
        def tokens(g, carry):
            for u in range(SUBLANES):
                e0 = (g * SUBLANES + u) * SUBLANES
                acc_a = jnp.zeros((HP_CHUNKS, LANES), F32)
                acc_b = jnp.zeros((HP_CHUNKS, LANES), F32)
                for k in range(TOP_K):
                    ya, yb = _unpack_halves(y_ref[pl.ds(posa_ref[e0 + k], HP_CHUNKS, stride=SUBLANES), :])
                    w = wts_ref[e0 + k]
                    acc_a = acc_a + w * ya
                    acc_b = acc_b + w * yb
                out0 = pl.multiple_of(g * (2 * HP_CHUNKS * SUBLANES), 2 * HP_CHUNKS * SUBLANES) + u
                o_ref[pl.ds(out0, HP_CHUNKS, stride=SUBLANES), :] = acc_a
                o_ref[pl.ds(out0 + HP_CHUNKS * SUBLANES, HP_CHUNKS, stride=SUBLANES), :] = acc_b
            return carry

        lax.fori_loop(0, tt // SUBLANES, tokens, 0)


def _moe_segment(prev, scores_t, hp, router_bias, expert_w, layer, row0, n, tt):
    w_eg, w_eu, w_ed = expert_w
    d, de = w_eg.shape[1:]
    half = d // 2
    n_all = hp.shape[0] // HP_CHUNKS
    mean = tt * TOP_K / N_EXPERTS
    block = int(-(-(mean + MOE_BLOCK_SIGMAS * math.sqrt(mean * (1.0 - 1.0 / N_EXPERTS))) // 16) * 16)
    group = 1
    while group * 2 <= min(MOE_MAX_GROUP, MOE_ROWS_IN_FLIGHT // block) and N_EXPERTS % (group * 2) == 0:
        group *= 2
    group0 = layer * (N_EXPERTS // group)
    n_tiles, tile0 = n // tt, row0 // tt
    n_slots = -(-(tt * TOP_K + N_EXPERTS * (SUBLANES - 1) + block) // (SUBLANES * LANES)) * SUBLANES * LANES
    posa, wts, pos, meta = _route(scores_t, router_bias, tt, row0, n)
    cnt, off = meta[:, :, 0], meta[:, :, LANES]

    def per_token(a):
        return a.T.reshape(-1)

    meta_s = jnp.zeros((n_tiles, SUBLANES, LANES), I32)
    meta_s = meta_s.at[:, 0, :N_EXPERTS].set(cnt).at[:, 1, :N_EXPERTS].set(off).reshape(-1, LANES)

    tile = lambda i, e: (i + tile0, 0)
    expert = lambda i, e: (group0 + e, 0, 0)
    smem = functools.partial(pl.BlockSpec, memory_space=pltpu.SMEM)
    once = pl.Buffered(1)
    flat = smem((SUBLANES * tt,), lambda i, e: (i,))
    in_specs = [
        flat, flat, flat, smem((SUBLANES, LANES), lambda i, e: (i, 0)),
        pl.BlockSpec((tt * HP_CHUNKS, LANES), tile),
        pl.BlockSpec((group, d, de), expert), pl.BlockSpec((group, d, de), expert),
        pl.BlockSpec((group, de, d), expert),
    ]
    args = [per_token(pos), per_token(posa), per_token(wts), meta_s, hp, w_eg, w_eu, w_ed]
    aliases = {}
    if prev is not None:
        in_specs.append(pl.BlockSpec(memory_space=pl.ANY))
        args.append(prev)
        aliases = {len(args) - 1: 0}
    return pl.pallas_call(
        functools.partial(_moe_kernel, group, block, prev is not None),
        out_shape=jax.ShapeDtypeStruct((n_all * 2 * HP_CHUNKS, LANES), F32),
        grid=(n_tiles, N_EXPERTS // group),
        in_specs=in_specs,
        out_specs=pl.BlockSpec((tt * 2 * HP_CHUNKS, LANES), tile, pipeline_mode=once),
        scratch_shapes=[pltpu.VMEM((group, block * HP_CHUNKS, LANES), U32),
                        pltpu.VMEM((n_slots * HP_CHUNKS, LANES), U32),
                        pltpu.SMEM((n_slots,), I32)],
        input_output_aliases=aliases,
        compiler_params=_cparams("arbitrary", "arbitrary"),
        name="moe",
    )(*args)


def _moe_final_kernel(tiles_per_batch, n_batch, routed_ref, hp_ref, x_ref, mod_ref, gpost_ref,
                      wsg_ref, wsu_ref, wsd_ref, o_ref):
    tm, d = x_ref.shape
    half = d // 2
    r = jnp.minimum(pl.program_id(0) // tiles_per_batch, n_batch)
    gate2 = mod_ref[pl.ds(r, 1), 5 * d:6 * d]
    halves = [_unpack_halves(col) for col in _tile_major_columns(hp_ref, tm, HP_CHUNKS)]
    xa = jnp.concatenate([h[0] for h in halves], axis=1).astype(BF16)
    xb = jnp.concatenate([h[1] for h in halves], axis=1).astype(BF16)
    hg = (jnp.dot(xa, wsg_ref[0:half, :], preferred_element_type=F32)
          + jnp.dot(xb, wsg_ref[half:d, :], preferred_element_type=F32))
    hu = (jnp.dot(xa, wsu_ref[0:half, :], preferred_element_type=F32)
          + jnp.dot(xb, wsu_ref[half:d, :], preferred_element_type=F32))
    routed = jnp.concatenate(_tile_major_columns(routed_ref, tm, 2 * HP_CHUNKS), axis=1)
    f = jnp.dot((_silu(hg) * hu).astype(BF16), wsd_ref[...], preferred_element_type=F32) + routed
    o_ref[...] = x_ref[...] + gate2 * (_rms(f) * gpost_ref[...])


def _moe_final(routed, hp, xs, mod, g_post, w_sg, w_su, w_sd, n_batch, seq):
    n, d = xs.shape
    tm = TOKEN_TILE
    row = lambda i: (i, 0)
    const = lambda i: (0, 0)
    return pl.pallas_call(
        functools.partial(_moe_final_kernel, seq // tm, n_batch),
        out_shape=jax.ShapeDtypeStruct((n, d), F32),
        grid=(n // tm,),
        in_specs=[pl.BlockSpec((tm * 2 * HP_CHUNKS, LANES), row), pl.BlockSpec((tm * HP_CHUNKS, LANES), row),
                  pl.BlockSpec((tm, d), row),
                  pl.BlockSpec(mod.shape, const), pl.BlockSpec((1, d), const),
                  pl.BlockSpec(w_sg.shape, const), pl.BlockSpec(w_su.shape, const), pl.BlockSpec(w_sd.shape, const)],
        out_specs=pl.BlockSpec((tm, d), row),
        compiler_params=_cparams("parallel"),
        name="moe_final",
    )(routed, hp, xs, mod, g_post, w_sg, w_su, w_sd)


def _moe_layer(scores_t, hp, xs, mod, router_bias, g_post, expert_w, shared_w, layer, n_lat, n_batch, seq):
    pick = lambda rows: next(t for t in MOE_TILES if rows % t == 0)
    routed = _moe_segment(None, scores_t, hp, router_bias, expert_w, layer, 0, n_lat, pick(n_lat))
    n_ctx = hp.shape[0] // HP_CHUNKS - n_lat
    if n_ctx:
        tt = next(t for t in MOE_TILES if n_ctx % t == 0 and n_lat % t == 0)
        routed = _moe_segment(routed, scores_t, hp, router_bias, expert_w, layer, n_lat, n_ctx, tt)
    return _moe_final(routed, hp, xs, mod, g_post, *shared_w, n_batch, seq)


def kernel(x, c, ctx, c_ctx, w_mod, b_mod, g_pre_mix, g_post_mix, g_pre_ffn, g_post_ffn, w_in, w_out,
           ret_decay_logit, gla_w_gate, gla_b_gate, gla_norm_g, diff_lambda, diff_norm_g,
           w_router, router_bias, w_exp_gate, w_exp_up, w_exp_down, w_sh_gate, w_sh_up, w_sh_down):
    n_batch, seq, d = x.shape
    ctx_len = ctx.shape[1]
    depth = w_mod.shape[0]
    n_lat = n_batch * seq
    assert seq % TOKEN_TILE == 0 and (n_batch * ctx_len) % TOKEN_TILE == 0 and n_batch < SUBLANES
    scan_rows = CHUNK * SCAN_CHUNKS
    assert seq % ATTN_TK == 0 and seq % scan_rows == 0 and ctx_len % scan_rows == 0 and n_lat % ctx_len == 0

    xs = (x.reshape(n_lat, d), ctx.reshape(n_batch * ctx_len, d))
    cond = jnp.zeros((SUBLANES, d), F32).at[:n_batch].set(c).at[n_batch].set(c_ctx)
    mods = _modulation(cond, w_mod, b_mod)
    tables = _rope_tables(seq, RET_DK, TOKEN_TILE) + _rope_tables(seq, DIFF_DH, TOKEN_TILE)
    lr0 = N_HEADS * (2 * RET_DK + 2 * RET_DV + 2 * GLA_DK + 2 * GLA_DV)
    row = lambda a: a[None, :]
    expert_w = tuple(w.astype(BF16).reshape((depth * N_EXPERTS,) + w.shape[2:])
                     for w in (w_exp_gate, w_exp_up, w_exp_down))

    for layer in range(depth):
        need_ctx = layer < depth - 1
        lam_init = 0.8 - 0.6 * math.exp(-0.3 * layer)
        mod = mods[layer]
        wl = w_in[layer]
        w_r = jnp.concatenate([wl[:, :lr0], wl[:, lr0:lr0 + GLA_LOWRANK],
                               jnp.zeros((d, LANES - GLA_LOWRANK), F32), wl[:, lr0 + GLA_LOWRANK:]],
                              axis=1).astype(BF16)
        ret, gla, glr, dqt, dk, vt = _in_projection(xs, mod, row(g_pre_mix[layer]), w_r, tables, n_batch, seq)

        o_rf, o_gf, o_rb, o_gb = _scan(ret, gla, glr, _scan_tables(ret_decay_logit[layer]), gla_w_gate[layer],
                                       gla_b_gate[layer], n_batch, seq, ctx_len)

        lp = jnp.zeros((SUBLANES, LANES), F32).at[:4, :DIFF_DH].set(diff_lambda[layer])
        g_col = jnp.broadcast_to(diff_norm_g[layer][:, None], (LANES, LANES))
        attn = functools.partial(_diff_attention, dqt, dk, vt, lp, g_col, lam_init, n_batch, seq, ctx_len)
        dif = attn(True)
        n_rows = n_lat
        if need_ctx:
            n_rows = n_lat + n_batch * ctx_len
            if isinstance(xs, tuple):
                dif = (dif, attn(False))
            else:
                dif = jnp.concatenate([dif, attn(False)], axis=0)
        elif isinstance(xs, tuple):
            xs = xs[0]

        xs, hp, scores_t = _merge(o_rf, o_rb, o_gf, o_gb, ret, gla, dif, xs, mod, row(g_post_mix[layer]),
                                  row(g_pre_ffn[layer]), row(jnp.tile(gla_norm_g[layer], N_HEADS)),
                                  w_out[layer].astype(BF16), w_router[layer].T, n_rows, n_batch, seq)
        shared_w = tuple(w[layer].astype(BF16) for w in (w_sh_gate, w_sh_up, w_sh_down))
        xs = _moe_layer(scores_t, hp, xs, mod, router_bias[layer], row(g_post_ffn[layer]), expert_w, shared_w,
                        layer, n_lat, n_batch, seq)
    return xs[:n_lat].reshape(n_batch, seq, d)
```

```python
import functools
import math

import jax
import jax.numpy as jnp
import numpy as np
from jax import lax
from jax.experimental import pallas as pl
from jax.experimental.pallas import tpu as pltpu

F32 = jnp.float32
BF16 = jnp.bfloat16
I32 = jnp.int32
U32 = jnp.uint32

GRID_W = 64
CHUNK = 128
N_HEADS = 4
RET_DK, RET_DV = 32, 64
GLA_DK, GLA_DV = 32, 64
GLA_LOWRANK = 16
GLA_TAU = 16.0
DIFF_DH = 64
ROPE_BASE = 10000.0
N_EXPERTS = 64
TOP_K = 6
N_GROUPS = 8
TOPK_GROUPS = 4
GROUP_SIZE = N_EXPERTS // N_GROUPS
ROUTED_SCALE = 2.5
EPS = 1e-6
GLA_SUB = 16
SCAN_CHUNKS = 2

LANES = 128
SUBLANES = 8
VMEM_LIMIT_BYTES = 56 * 1024 * 1024

TOKEN_TILE = 512
ATTN_TQ = 512
ATTN_TK = 512
ATTN_ONES = 16
MOE_TILES = (2048, 1024, 512)
MOE_ROWS_IN_FLIGHT = 512
MOE_MAX_GROUP = 4
MOE_BLOCK_SIGMAS = 4.0
HP_CHUNKS = 4

HIGHEST = lax.Precision.HIGHEST
NT_DIMS = (((1,), (1,)), ((), ()))
TN_DIMS = (((0,), (0,)), ((), ()))


def _cparams(*sem):
    return pltpu.CompilerParams(dimension_semantics=sem, vmem_limit_bytes=VMEM_LIMIT_BYTES)


def _log_sigmoid(x):
    return jnp.minimum(x, 0.0) - jnp.log(1.0 + jnp.exp(-jnp.abs(x)))


def _silu(x):
    return x * (1.0 / (1.0 + jnp.exp(-x)))


def _rms(x):
    return x * lax.rsqrt(jnp.mean(x * x, axis=-1, keepdims=True) + EPS)


def _pack_halves(a, b):
    ua = lax.bitcast_convert_type(a.astype(BF16).astype(F32), U32)
    ub = lax.bitcast_convert_type(b.astype(BF16).astype(F32), U32)
    return (ua & jnp.uint32(0xFFFF0000)) | (ub >> 16)


def _unpack_halves(w):
    a = lax.bitcast_convert_type(w & jnp.uint32(0xFFFF0000), F32)
    b = lax.bitcast_convert_type(w << 16, F32)
    return a, b


def _tile_major_store(ref, v, lead=()):
    k = v.shape[1] // LANES
    for g in range(v.shape[0] // SUBLANES):
        for c in range(k):
            blk = g * k + c
            ref[lead + (slice(blk * SUBLANES, (blk + 1) * SUBLANES), slice(None))] = (
                v[g * SUBLANES:(g + 1) * SUBLANES, c * LANES:(c + 1) * LANES])


def _tile_major_columns(ref, rows, k, lead=()):
    return [jnp.concatenate([ref[lead + (slice((g * k + c) * SUBLANES, (g * k + c + 1) * SUBLANES), slice(None))]
                             for g in range(rows // SUBLANES)], axis=0) for c in range(k)]


def _tile_major_addr(t, k):
    return (t >> 3) * (SUBLANES * k) + (t & (SUBLANES - 1))


def _mod_kernel(cond_ref, w_ref, b_ref, o_ref):
    a = _silu(cond_ref[...])
    o_ref[0] = jnp.dot(a, w_ref[0], precision=HIGHEST, preferred_element_type=F32) + b_ref[0]


def _modulation(cond, w_mod, b_mod):
    n_layers, d, d6 = w_mod.shape
    tn = 1024
    return pl.pallas_call(
        _mod_kernel,
        out_shape=jax.ShapeDtypeStruct((n_layers, SUBLANES, d6), F32),
        grid=(n_layers, d6 // tn),
        in_specs=[
            pl.BlockSpec((SUBLANES, d), lambda l, j: (0, 0)),
            pl.BlockSpec((1, d, tn), lambda l, j: (l, 0, j)),
            pl.BlockSpec((1, 1, tn), lambda l, j: (l, 0, j)),
        ],
        out_specs=pl.BlockSpec((1, SUBLANES, tn), lambda l, j: (l, 0, j)),
        compiler_params=_cparams("parallel", "parallel"),
        name="modulation",
    )(cond, w_mod, b_mod.reshape(n_layers, 1, d6))


def _rope(x, cos, sin, quarter):
    lane = lax.broadcasted_iota(I32, (1, LANES), 1)
    first = (lane % (2 * quarter)) < quarter
    outs = []
    for c in range(x.shape[-1] // LANES):
        xc = x[:, c * LANES:(c + 1) * LANES]
        partner = jnp.where(first, pltpu.roll(xc, LANES - quarter, 1), pltpu.roll(xc, quarter, 1))
        outs.append(xc * cos + partner * sin)
    return outs[0] if len(outs) == 1 else jnp.concatenate(outs, axis=-1)


def _split_rows(a, b, tm, width):
    n_a = a.shape[0] // tm
    return n_a, [pl.BlockSpec((tm, width), lambda i: (jnp.minimum(i, n_a - 1), 0)),
                 pl.BlockSpec((tm, width), lambda i: (jnp.maximum(i - n_a, 0), 0))]


def _pick_rows(n_a, a_ref, b_ref):
    return jnp.where(pl.program_id(0) < n_a, a_ref[...], b_ref[...])


def _inproj_kernel(tiles_per_batch, n_batch, n_a, *refs):
    n_x = 2 if n_a else 1
    (mod_ref, g_ref, w_ref, c32_ref, s32_ref, c64_ref, s64_ref,
     ret_ref, gla_ref, glr_ref, dqt_ref, dk_ref, vt_ref) = refs[n_x:]
    x = _pick_rows(n_a, *refs[:2]) if n_a else refs[0][...]
    d = x.shape[-1]
    r = jnp.minimum(pl.program_id(0) // tiles_per_batch, n_batch)
    shift = mod_ref[pl.ds(r, 1), 0:d]
    scale = mod_ref[pl.ds(r, 1), d:2 * d]
    h = (_rms(x) * g_ref[...] * (1.0 + scale) + shift).astype(BF16)

    def proj(lo, hi):
        return jnp.dot(h, w_ref[:, lo:hi], preferred_element_type=F32)

    c32, s32 = c32_ref[...], s32_ref[...]
    c64, s64 = c64_ref[...], s64_ref[...]
    ret = proj(0, 768)
    ret_ref[:, 0:128] = _rope(ret[:, 0:128], c32, s32, RET_DK // 4).astype(BF16)
    ret_ref[:, 128:256] = (_rope(ret[:, 128:256], c32, s32, RET_DK // 4) * RET_DK ** -0.5).astype(BF16)
    ret_ref[:, 256:768] = ret[:, 256:768].astype(BF16)
    gla = proj(768, 1536)
    gla_ref[:, 0:128] = (gla[:, 0:128] * GLA_DK ** -0.5).astype(BF16)
    gla_ref[:, 128:768] = gla[:, 128:768].astype(BF16)
    glr_ref[...] = proj(1536, 1664)
    dq = _rope(proj(1664, 2176), c64, s64, DIFF_DH // 4) * (DIFF_DH ** -0.5 * math.log2(math.e))
    for hd in range(N_HEADS):
        dqt_ref[hd * LANES:(hd + 1) * LANES, :] = dq[:, hd * LANES:(hd + 1) * LANES].T.astype(BF16)
    dk_ref[...] = _rope(proj(2176, 2688), c64, s64, DIFF_DH // 4).astype(BF16)
    dv = proj(2688, 3200)
    vrows = LANES + ATTN_ONES
    for hd in range(N_HEADS):
        vt_ref[hd * vrows:hd * vrows + LANES, :] = dv[:, hd * LANES:(hd + 1) * LANES].T.astype(BF16)
        vt_ref[hd * vrows + LANES:(hd + 1) * vrows, :] = jnp.ones((ATTN_ONES, dv.shape[0]), BF16)


def _in_projection(xs, mod, g_pre, w_r, tables, n_batch, seq):
    tm = TOKEN_TILE
    if isinstance(xs, tuple):
        d = xs[0].shape[1]
        n = xs[0].shape[0] + xs[1].shape[0]
        n_a, x_specs = _split_rows(*xs, tm, d)
    else:
        n, d = xs.shape
        n_a, x_specs, xs = 0, [pl.BlockSpec((tm, d), lambda i: (i, 0))], (xs,)
    tiles_per_batch = seq // tm
    n_lat_tiles = n_batch * tiles_per_batch
    c32, s32, c64, s64 = tables

    def tab_map(i):
        return (jnp.where(i < n_lat_tiles, i % tiles_per_batch, tiles_per_batch), 0)

    row = lambda i: (i, 0)
    const = lambda i: (0, 0)
    tab_spec = pl.BlockSpec((tm, LANES), tab_map)
    return pl.pallas_call(
        functools.partial(_inproj_kernel, tiles_per_batch, n_batch, n_a),
        out_shape=(
            jax.ShapeDtypeStruct((n, 768), BF16), jax.ShapeDtypeStruct((n, 768), BF16),
            jax.ShapeDtypeStruct((n, LANES), F32),
            jax.ShapeDtypeStruct((N_HEADS * LANES, n), BF16), jax.ShapeDtypeStruct((n, 512), BF16),
            jax.ShapeDtypeStruct((N_HEADS * (LANES + ATTN_ONES), n), BF16)),
        grid=(n // tm,),
        in_specs=x_specs + [
            pl.BlockSpec(mod.shape, const),
            pl.BlockSpec((1, d), const),
            pl.BlockSpec(w_r.shape, const),
            tab_spec, tab_spec, tab_spec, tab_spec,
        ],
        out_specs=(
            pl.BlockSpec((tm, 768), row), pl.BlockSpec((tm, 768), row), pl.BlockSpec((tm, LANES), row),
            pl.BlockSpec((N_HEADS * LANES, tm), lambda i: (0, i)), pl.BlockSpec((tm, 512), row),
            pl.BlockSpec((N_HEADS * (LANES + ATTN_ONES), tm), lambda i: (0, i))),
        compiler_params=_cparams("parallel"),
        name="in_projection",
    )(*xs, mod, g_pre, w_r, c32, s32, c64, s64)


def _rope_tables(seq, head_dim, extra_rows):
    half, quarter = head_dim // 2, head_dim // 4
    freqs = (ROPE_BASE ** (-np.arange(quarter, dtype=np.float32) / quarter)).astype(np.float32)
    t = np.arange(seq)
    row = (t // GRID_W).astype(np.float32)
    col = (t % GRID_W).astype(np.float32)
    j = np.arange(LANES) % head_dim
    jj = j % half
    pos = np.where((j < half)[None, :], row[:, None], col[:, None])
    ang = (pos * freqs[jj % quarter][None, :]).astype(np.float32)
    cos = np.cos(ang)
    sin = np.sin(ang) * np.where(jj < quarter, -1.0, 1.0)[None, :]
    cos = np.concatenate([cos, np.ones((extra_rows, LANES))], axis=0).astype(np.float32)
    sin = np.concatenate([sin, np.zeros((extra_rows, LANES))], axis=0).astype(np.float32)
    return jnp.asarray(cos), jnp.asarray(sin)


def _head_stack(x, width):
    lane = lax.broadcasted_iota(I32, (1, x.shape[-1]), 1)
    zero = jnp.zeros_like(x)
    return jnp.concatenate([jnp.where(lane // width == h, x, zero) for h in range(N_HEADS)], axis=0)


def _head_select(x4, rows):
    lane = lax.broadcasted_iota(I32, (1, x4.shape[-1]), 1)
    out = x4[(N_HEADS - 1) * rows:N_HEADS * rows]
    for h in range(N_HEADS - 2, -1, -1):
        out = jnp.where(lane < (h + 1) * RET_DV, x4[h * rows:(h + 1) * rows], out)
    return out


def _scan_tables_kernel(rl_lane_ref, rl_rows_ref, rl_col_ref, dmat_ref, xi_ref, zeta_ref, gchunk_ref):
    c = CHUNK
    idx = lax.broadcasted_iota(I32, (c, 1), 0).astype(F32)
    ri4 = lax.broadcasted_iota(I32, (N_HEADS * c, c), 0) % c
    ci4 = lax.broadcasted_iota(I32, (N_HEADS * c, c), 1)
    dist = jnp.abs(ri4 - ci4).astype(F32)
    for d in range(2):
        lg_lane = _log_sigmoid(rl_lane_ref[d])
        lg_rows = _log_sigmoid(rl_rows_ref[d])
        lg_col = _log_sigmoid(rl_col_ref[d])
        att4 = (ri4 <= ci4) if d else (ri4 >= ci4)
        dmat_ref[d] = jnp.where(att4, jnp.exp(dist * lg_rows), 0.0)
        xi_ref[d] = jnp.exp(((c - idx) if d else (idx + 1.0)) * lg_lane)
        zeta_ref[d] = jnp.exp((idx if d else (c - 1.0 - idx)) * lg_lane)
        g_chunk = jnp.exp(float(c) * lg_col)
        gchunk_ref[d] = jnp.concatenate([g_chunk, g_chunk], axis=1)


def _scan_tables(ret_logit):
    c = CHUNK
    rl_lane = jnp.repeat(ret_logit, RET_DK, axis=1)[:, None, :]
    rl_rows = jnp.broadcast_to(jnp.repeat(ret_logit, c, axis=1)[:, :, None], (2, N_HEADS * c, c))
    rl_col = jnp.broadcast_to(jnp.repeat(ret_logit, RET_DK, axis=1)[:, :, None], (2, LANES, LANES))
    return pl.pallas_call(
        _scan_tables_kernel,
        out_shape=(jax.ShapeDtypeStruct((2, N_HEADS * c, c), F32), jax.ShapeDtypeStruct((2, c, LANES), F32),
                   jax.ShapeDtypeStruct((2, c, LANES), F32), jax.ShapeDtypeStruct((2, LANES, 2 * LANES), F32)),
        name="scan_tables",
    )(rl_lane, rl_rows, rl_col)


def _state_block_mask():
    return (lax.broadcasted_iota(I32, (LANES, 2 * LANES), 0) // RET_DK
            == lax.broadcasted_iota(I32, (LANES, 2 * LANES), 1) // RET_DV)


def _ret_chain(ret_ref, r0, dmat, xi, zeta, g_chunk, out):
    c = CHUNK
    q = ret_ref[r0:r0 + c, 0:128]
    k = ret_ref[r0:r0 + c, 128:256]
    v = ret_ref[r0:r0 + c, 256:512]
    s = lax.dot_general(_head_stack(q, RET_DK), k, NT_DIMS, preferred_element_type=F32)
    kz = (k.astype(F32) * zeta).astype(BF16)
    u = lax.dot_general(kz, v, TN_DIMS, preferred_element_type=F32)
    yield
    o4 = jnp.dot((s * dmat).astype(BF16), v, preferred_element_type=F32)
    out.update(qx=(q.astype(F32) * xi).astype(BF16), u=jnp.where(_state_block_mask(), u, 0.0), g=g_chunk)
    yield
    out.update(intra=_head_select(o4, c))


def _state_chain(parts, o_ref, s_ref):
    c = CHUNK
    s = s_ref[...]
    for r0, p in parts:
        o_ref[r0:r0 + c, :] = p["intra"] + jnp.dot(p["qx"], s.astype(BF16), preferred_element_type=F32)
        s = s * p["g"] + p["u"]
        yield
    s_ref[...] = s


def _gla_chain(rev, gla_ref, glr_ref, r0, gw, gb, out):
    c = CHUNK
    ri = lax.broadcasted_iota(I32, (c, c), 0)
    ci = lax.broadcasted_iota(I32, (c, c), 1)
    attends = (ri <= ci) if rev else (ri >= ci)
    bd = _state_block_mask()
    gq = gla_ref[r0:r0 + c, 0:128].astype(F32)
    gk = gla_ref[r0:r0 + c, 128:256].astype(F32)
    gv = gla_ref[r0:r0 + c, 256:512]
    z = jnp.dot(glr_ref[r0:r0 + c, :], gw, precision=HIGHEST, preferred_element_type=F32) + gb
    yield
    la = _log_sigmoid(z) * (math.log2(math.e) / GLA_TAU)
    if rev:
        first = (ri // GLA_SUB) * GLA_SUB + (GLA_SUB - 1)
        ref_sel = ci >= first
    else:
        first = (ri // GLA_SUB) * GLA_SUB
        ref_sel = ci <= first
    sel = jnp.concatenate([jnp.where(attends, 1.0, 0.0), jnp.where(ref_sel, 1.0, 0.0)], axis=0).astype(BF16)
    la_hi = la.astype(BF16)
    la_lo = (la - la_hi.astype(F32)).astype(BF16)
    sums = (jnp.dot(sel, la_hi, preferred_element_type=F32) + jnp.dot(sel, la_lo, preferred_element_type=F32))
    b, refrow = sums[0:c], sums[c:2 * c]
    yield
    qs = gq * jnp.exp2(b - refrow)
    b_last = b[0:1] if rev else b[c - 1:c]
    kz = (gk * jnp.exp2(b_last - b)).astype(BF16)
    u = lax.dot_general(kz, gv, TN_DIMS, preferred_element_type=F32)
    eye = lax.broadcasted_iota(I32, (LANES, LANES), 0) == lax.broadcasted_iota(I32, (LANES, LANES), 1)
    g_col = jnp.sum(jnp.where(eye, jnp.exp2(b_last), 0.0), axis=1, keepdims=True)
    jcol = lax.broadcasted_iota(I32, (c, 1), 0)
    rr = lax.broadcasted_iota(I32, (N_HEADS * GLA_SUB, c), 0) % GLA_SUB
    cc = lax.broadcasted_iota(I32, (N_HEADS * GLA_SUB, c), 1)
    pieces = []
    for blk in range(c // GLA_SUB):
        lo = blk * GLA_SUB
        ref_b = refrow[lo:lo + 1]
        seen = (jcol >= lo) if rev else (jcol < lo + GLA_SUB)
        ks = (gk * jnp.exp2(jnp.where(seen, ref_b - b, -jnp.inf))).astype(BF16)
        qz = _head_stack(qs[lo:lo + GLA_SUB], GLA_DK).astype(BF16)
        att = lax.dot_general(qz, ks, NT_DIMS, preferred_element_type=F32)
        ok = (cc >= rr + lo) if rev else (cc <= rr + lo)
        att = jnp.where(ok, att, 0.0).astype(BF16)
        pieces.append(_head_select(jnp.dot(att, gv, preferred_element_type=F32), GLA_SUB))
        if blk % 2:
            yield
    out.update(intra=jnp.concatenate(pieces, axis=0), qx=(gq * jnp.exp2(b)).astype(BF16),
               u=jnp.where(bd, u, 0.0), g=g_col)


def _round_robin(chains):
    while chains:
        chains = [ch for ch in chains if next(ch, True) is None]


def _scan_kernel(ret_f, gla_f, glr_f, ret_b, gla_b, glr_b, dmat_ref, xi_ref, zeta_ref, gchunk_ref, gw_ref, gb_ref,
                 orf_ref, ogf_ref, orb_ref, ogb_ref, sr_ref, sg_ref):
    @pl.when(pl.program_id(1) == 0)
    def _():
        sr_ref[...] = jnp.zeros_like(sr_ref)
        sg_ref[...] = jnp.zeros_like(sg_ref)

    chains, state_chains = [], []
    for d, (ret_ref, gla_ref, glr_ref, ore_ref, ogl_ref) in enumerate(
            ((ret_f, gla_f, glr_f, orf_ref, ogf_ref), (ret_b, gla_b, glr_b, orb_ref, ogb_ref))):
        order = range(SCAN_CHUNKS - 1, -1, -1) if d else range(SCAN_CHUNKS)
        gla_parts, ret_parts = [], []
        for j in order:
            r0 = j * CHUNK
            gla_parts.append((r0, {}))
            ret_parts.append((r0, {}))
            chains.append(_gla_chain(bool(d), gla_ref, glr_ref, r0, gw_ref[d], gb_ref[d], gla_parts[-1][1]))
            chains.append(_ret_chain(ret_ref, r0, dmat_ref[d], xi_ref[d], zeta_ref[d], gchunk_ref[d],
                                     ret_parts[-1][1]))
        state_chains.append(_state_chain(gla_parts, ogl_ref, sg_ref.at[d]))
        state_chains.append(_state_chain(ret_parts, ore_ref, sr_ref.at[d]))
    _round_robin(chains)
    _round_robin(state_chains)


def _scan(ret, gla, glr, tables, gla_w, gla_b, n_batch, seq, ctx_len):
    n = ret.shape[0]
    c = CHUNK * SCAN_CHUNKS
    nc_ctx, nc_lat = ctx_len // c, seq // c
    n_steps = nc_ctx + nc_lat
    ctx_base = n_batch * nc_lat

    def fwd(b, s):
        return (jnp.where(s < nc_ctx, ctx_base + b * nc_ctx + s, b * nc_lat + (s - nc_ctx)), 0)

    def bwd(b, s):
        return (jnp.where(s < nc_ctx, ctx_base + b * nc_ctx + (nc_ctx - 1 - s), b * nc_lat + (n_steps - 1 - s)), 0)

    const3 = lambda b, s: (0, 0, 0)
    gw = jnp.zeros((2, LANES, LANES), F32).at[:, :GLA_LOWRANK].set(gla_w)
    gb = gla_b[:, None, :]
    chain_in = lambda m: [pl.BlockSpec((c, 768), m), pl.BlockSpec((c, 768), m), pl.BlockSpec((c, LANES), m)]
    o_sds = jax.ShapeDtypeStruct((n, 256), F32)
    return pl.pallas_call(
        _scan_kernel,
        out_shape=(o_sds, o_sds, o_sds, o_sds),
        grid=(n_batch, n_steps),
        in_specs=chain_in(fwd) + chain_in(bwd) + [pl.BlockSpec(t.shape, const3) for t in tables]
        + [pl.BlockSpec(gw.shape, const3), pl.BlockSpec(gb.shape, const3)],
        out_specs=(pl.BlockSpec((c, 256), fwd), pl.BlockSpec((c, 256), fwd),
                   pl.BlockSpec((c, 256), bwd), pl.BlockSpec((c, 256), bwd)),
        scratch_shapes=[pltpu.VMEM((2, LANES, 2 * LANES), F32), pltpu.VMEM((2, LANES, 2 * LANES), F32)],
        compiler_params=_cparams("arbitrary", "arbitrary"),
        name="scan",
    )(ret, gla, glr, ret, gla, glr, *tables, gw, gb)


def _attn_kernel(n_lat_blocks, lam_init, *refs):
    if n_lat_blocks:
        q_ref, kc_ref, vtc_ref, kl_ref, vtl_ref, lp_ref, g_ref, o_ref, m_ref, acc_ref, st_ref = refs
    else:
        q_ref, kc_ref, vtc_ref, lp_ref, g_ref, o_ref, m_ref, acc_ref, st_ref = refs
    tq = q_ref.shape[1]
    q = q_ref[...]
    dim = lax.broadcasted_iota(I32, (LANES, 1), 0)
    zero = jnp.zeros_like(q)
    qt = jnp.concatenate([jnp.where(dim < DIFF_DH, q, zero), jnp.where(dim >= DIFF_DH, q, zero)], axis=1)
    m_ref[...] = jnp.full(m_ref.shape, -jnp.inf, F32)
    acc_ref[...] = jnp.zeros_like(acc_ref)

    def scores(slot, kb):
        st_ref[slot, 0:kb.shape[0], :] = jnp.dot(kb, qt, preferred_element_type=F32)

    def absorb(slot, vtb):
        st = st_ref[slot, 0:vtb.shape[1], :]
        m_prev = m_ref[...]
        m_new = jnp.maximum(m_prev, jnp.max(st, axis=0, keepdims=True))
        alpha = jnp.exp2(m_prev - m_new)
        p = jnp.exp2(st - m_new).astype(BF16)
        acc_ref[...] = alpha * acc_ref[...] + jnp.dot(vtb, p, preferred_element_type=F32)
        m_ref[...] = m_new

    def k_lat(blk):
        return kl_ref[pl.ds(pl.multiple_of(blk * ATTN_TK, ATTN_TK), ATTN_TK), :]

    def vt_lat(blk):
        return vtl_ref[:, pl.ds(pl.multiple_of(blk * ATTN_TK, ATTN_TK), ATTN_TK)]

    scores(0, kc_ref[...])
    if not n_lat_blocks:
        absorb(0, vtc_ref[...])
    else:
        scores(1, k_lat(0))
        absorb(0, vtc_ref[...])
        n_pairs = (n_lat_blocks - 1) // 2

        def body(i, carry):
            scores(0, k_lat(2 * i + 1))
            absorb(1, vt_lat(2 * i))
            scores(1, k_lat(2 * i + 2))
            absorb(0, vt_lat(2 * i + 1))
            return carry
        lax.fori_loop(0, n_pairs, body, 0)
        done = 2 * n_pairs
        if n_lat_blocks - done == 2:
            scores(0, k_lat(done + 1))
            absorb(1, vt_lat(done))
            absorb(0, vt_lat(done + 1))
        else:
            absorb(1, vt_lat(done))

    lp = lp_ref[...]
    lam = (jnp.exp(jnp.sum(lp[0:1] * lp[1:2], axis=1, keepdims=True))
           - jnp.exp(jnp.sum(lp[2:3] * lp[3:4], axis=1, keepdims=True)) + lam_init)
    acc = acc_ref[...]
    o1 = acc[0:LANES, :tq] / acc[LANES:LANES + 1, :tq]
    o2 = acc[0:LANES, tq:] / acc[LANES:LANES + 1, tq:]
    ot = o1 - lam * o2
    ot = ot * lax.rsqrt(jnp.mean(ot * ot, axis=0, keepdims=True) + EPS) * g_ref[...][:, 0:1] * (1.0 - lam_init)
    o_ref[...] = ot.T.astype(o_ref.dtype)


def _diff_attention(dqt, dk, vt, lp, g, lam_init, n_batch, seq, ctx_len, latent):
    ctx_blk0 = (n_batch * seq) // ctx_len
    vrows = LANES + ATTN_ONES
    kc_spec = pl.BlockSpec((ctx_len, LANES), lambda b, h, i: (ctx_blk0 + b, h))
    vtc_spec = pl.BlockSpec((vrows, ctx_len), lambda b, h, i: (h, ctx_blk0 + b))
    const = lambda b, h, i: (0, 0)
    if latent:
        tq = ATTN_TQ
        n_q = seq // tq
        o_map = lambda b, h, i: (b * n_q + i, h)
        in_specs = [pl.BlockSpec((LANES, tq), lambda b, h, i: (h, b * n_q + i)), kc_spec, vtc_spec,
                    pl.BlockSpec((seq, LANES), lambda b, h, i: (b, h)),
                    pl.BlockSpec((vrows, seq), lambda b, h, i: (h, b))]
        args = (dqt, dk, vt, dk, vt)
        n_rows, n_lat_blocks = n_batch * seq, seq // ATTN_TK
    else:
        tq, n_q = ctx_len, 1
        in_specs = [pl.BlockSpec((LANES, ctx_len), lambda b, h, i: (h, ctx_blk0 + b)), kc_spec, vtc_spec]
        args = (dqt, dk, vt)
        n_rows, o_map, n_lat_blocks = n_batch * ctx_len, (lambda b, h, i: (b, h)), 0
    in_specs += [pl.BlockSpec(lp.shape, const), pl.BlockSpec((LANES, LANES), const)]
    return pl.pallas_call(
        functools.partial(_attn_kernel, n_lat_blocks, lam_init),
        out_shape=jax.ShapeDtypeStruct((n_rows, N_HEADS * LANES), BF16),
        grid=(n_batch, N_HEADS, n_q),
        in_specs=in_specs,
        out_specs=pl.BlockSpec((tq, LANES), o_map),
        scratch_shapes=[pltpu.VMEM((1, 2 * tq), F32), pltpu.VMEM((vrows, 2 * tq), F32),
                        pltpu.VMEM((2, max(ATTN_TK, ctx_len), 2 * tq), F32)],
        compiler_params=_cparams("parallel", "parallel", "parallel"),
        name="diff_attention_lat" if latent else "diff_attention_ctx",
    )(*args, lp, g)


def _merge_kernel(tiles_per_batch, n_batch, n_a, orf_ref, orb_ref, ogf_ref, ogb_ref, rg_ref, gr_ref, *refs):
    n_x = 2 if n_a else 1
    mod_ref, gpost_ref, gpre_ref, glag_ref, wout_ref, wrt_ref, xo_ref, hp_ref, sc_ref = refs[2 * n_x:]
    dif = _pick_rows(n_a, *refs[:2]) if n_a else refs[0][...]
    x = _pick_rows(n_a, *refs[2:4]) if n_a else refs[1][...]
    d = x.shape[-1]
    r = jnp.minimum(pl.program_id(0) // tiles_per_batch, n_batch)
    gate1 = mod_ref[pl.ds(r, 1), 2 * d:3 * d]
    shift2 = mod_ref[pl.ds(r, 1), 3 * d:4 * d]
    scale2 = mod_ref[pl.ds(r, 1), 4 * d:5 * d]
    gi = lax.broadcasted_iota(I32, (256, 256), 0) // RET_DV
    gj = lax.broadcasted_iota(I32, (256, 256), 1) // RET_DV
    group_mean = jnp.where(gi == gj, 1.0 / RET_DV, 0.0).astype(BF16)

    def head_norm(o):
        ms = jnp.dot((o * o).astype(BF16), group_mean, preferred_element_type=F32)
        return o * lax.rsqrt(ms + EPS)

    ret = head_norm(orf_ref[...] + orb_ref[...]) * _silu(rg_ref[...].astype(F32))
    gla = head_norm(ogf_ref[...] + ogb_ref[...]) * glag_ref[...] * _silu(gr_ref[...].astype(F32))
    m = (jnp.dot(ret.astype(BF16), wout_ref[0:256, :], preferred_element_type=F32)
         + jnp.dot(gla.astype(BF16), wout_ref[256:512, :], preferred_element_type=F32)
         + jnp.dot(dif, wout_ref[512:1024, :], preferred_element_type=F32))
    x_new = x + gate1 * (_rms(m) * gpost_ref[...])
    xo_ref[...] = x_new
    h2 = _rms(x_new) * gpre_ref[...] * (1.0 + scale2) + shift2
    _tile_major_store(hp_ref, _pack_halves(h2[:, :d // 2], h2[:, d // 2:]))
    def split(a):
        hi = a.astype(BF16)
        return hi, (a - hi.astype(F32)).astype(BF16)

    h_hi, h_lo = split(h2)
    w_hi, w_lo = split(wrt_ref[...])
    nt = functools.partial(lax.dot_general, dimension_numbers=NT_DIMS, preferred_element_type=F32)
    logits = nt(w_hi, h_hi) + (nt(w_hi, h_lo) + nt(w_lo, h_hi))
    sc_ref[...] = 1.0 / (1.0 + jnp.exp(-logits))


def _merge(o_rf, o_rb, o_gf, o_gb, ret, gla, dif, xs, mod, g_post, g_pre_ffn, gla_g, w_out, w_rt,
           n_rows, n_batch, seq):
    tm = TOKEN_TILE
    row = lambda i: (i, 0)
    gate_col = lambda i: (i, 2)
    const = lambda i: (0, 0)
    if isinstance(xs, tuple):
        d = xs[0].shape[1]
        n_a, x_specs = _split_rows(*xs, tm, d)
        _, dif_specs = _split_rows(*dif, tm, 512)
    else:
        d = xs.shape[1]
        n_a, x_specs, dif_specs = 0, [pl.BlockSpec((tm, d), row)], [pl.BlockSpec((tm, 512), row)]
        xs, dif = (xs,), (dif,)
    return pl.pallas_call(
        functools.partial(_merge_kernel, seq // tm, n_batch, n_a),
        out_shape=(jax.ShapeDtypeStruct((n_rows, d), F32), jax.ShapeDtypeStruct((n_rows * HP_CHUNKS, LANES), U32),
                   jax.ShapeDtypeStruct((N_EXPERTS, n_rows), F32)),
        grid=(n_rows // tm,),
        in_specs=[
            pl.BlockSpec((tm, 256), row), pl.BlockSpec((tm, 256), row),
            pl.BlockSpec((tm, 256), row), pl.BlockSpec((tm, 256), row),
            pl.BlockSpec((tm, 256), gate_col), pl.BlockSpec((tm, 256), gate_col), *dif_specs, *x_specs,
            pl.BlockSpec(mod.shape, const), pl.BlockSpec((1, d), const), pl.BlockSpec((1, d), const),
            pl.BlockSpec((1, 256), const), pl.BlockSpec(w_out.shape, const), pl.BlockSpec(w_rt.shape, const),
        ],
        out_specs=(pl.BlockSpec((tm, d), row), pl.BlockSpec((tm * HP_CHUNKS, LANES), row),
                   pl.BlockSpec((N_EXPERTS, tm), lambda i: (0, i))),
        compiler_params=_cparams("parallel"),
        name="merge",
    )(o_rf, o_rb, o_gf, o_gb, ret, gla, *dif, *xs, mod, g_post, g_pre_ffn, gla_g, w_out, w_rt)


def _route_kernel(sc_ref, bias_ref, posa_ref, wts_ref, pos_ref, meta_ref):
    tt = sc_ref.shape[-1]
    scores = sc_ref[...]
    sel = scores + bias_ref[...][:, 0:1]
    sub = lax.broadcasted_iota(I32, (GROUP_SIZE, tt), 0)
    neg = -jnp.inf
    gscore = []
    for g in range(N_GROUPS):
        xg = sel[g * GROUP_SIZE:(g + 1) * GROUP_SIZE]
        m1 = jnp.max(xg, axis=0, keepdims=True)
        i1 = jnp.min(jnp.where(xg == m1, sub, GROUP_SIZE), axis=0, keepdims=True)
        m2 = jnp.max(jnp.where(sub == i1, neg, xg), axis=0, keepdims=True)
        gscore.append(m1 + m2)
    rows = []
    for g in range(N_GROUPS):
        rank = jnp.zeros((1, tt), I32)
        for o in range(N_GROUPS):
            if o == g:
                continue
            ahead = (gscore[o] >= gscore[g]) if o < g else (gscore[o] > gscore[g])
            rank = rank + ahead.astype(I32)
        rows.append(jnp.where(rank < TOPK_GROUPS, sel[g * GROUP_SIZE:(g + 1) * GROUP_SIZE], neg))
    masked = jnp.concatenate(rows, axis=0)
    eio = lax.broadcasted_iota(I32, (N_EXPERTS, tt), 0)
    member = jnp.zeros((N_EXPERTS, tt), F32)
    idxs, ws = [], []
    for _ in range(TOP_K):
        m = jnp.max(masked, axis=0, keepdims=True)
        i = jnp.min(jnp.where(masked == m, eio, N_EXPERTS), axis=0, keepdims=True)
        hit = eio == i
        idxs.append(i)
        ws.append(jnp.sum(jnp.where(hit, scores, 0.0), axis=0, keepdims=True))
        member = jnp.where(hit, 1.0, member)
        masked = jnp.where(hit, neg, masked)
    wsum = ws[0]
    for w in ws[1:]:
        wsum = wsum + w
    ti = lax.broadcasted_iota(I32, (tt, tt), 0)
    tj = lax.broadcasted_iota(I32, (tt, tt), 1)
    before = jnp.where(ti < tj, 1.0, 0.0).astype(BF16)
    rank_in_e = jnp.dot(member.astype(BF16), before, preferred_element_type=F32)
    cnt = jnp.sum(member, axis=1, keepdims=True)
    padded = jnp.floor((cnt + (SUBLANES - 1)) * (1.0 / SUBLANES)) * SUBLANES
    ei = lax.broadcasted_iota(I32, (N_EXPERTS, N_EXPERTS), 0)
    ej = lax.broadcasted_iota(I32, (N_EXPERTS, N_EXPERTS), 1)
    lower = jnp.where(ej < ei, 1.0, 0.0)
    off = jnp.dot(lower, jnp.broadcast_to(padded, (N_EXPERTS, LANES)), precision=HIGHEST,
                  preferred_element_type=F32)
    slot = rank_in_e + off[:, 0:1]
    zrow_i = jnp.zeros((SUBLANES - TOP_K, tt), I32)
    zrow_f = jnp.zeros((SUBLANES - TOP_K, tt), F32)
    pos = [jnp.sum(jnp.where(eio == i, slot, 0.0), axis=0, keepdims=True).astype(I32) for i in idxs]
    pos = jnp.concatenate(pos + [zrow_i], axis=0)
    wts_ref[...] = jnp.concatenate([w / wsum * ROUTED_SCALE for w in ws] + [zrow_f], axis=0)
    pos_ref[...] = pos
    posa_ref[...] = _tile_major_addr(pos, HP_CHUNKS)
    meta_ref[0] = jnp.concatenate([jnp.broadcast_to(cnt, (N_EXPERTS, LANES)), off], axis=1).astype(I32)


def _route(scores_t, bias, tt, row0, n):
    tile0 = row0 // tt
    tok = lambda i: (0, i)
    return pl.pallas_call(
        _route_kernel,
        out_shape=(jax.ShapeDtypeStruct((SUBLANES, n), I32), jax.ShapeDtypeStruct((SUBLANES, n), F32),
                   jax.ShapeDtypeStruct((SUBLANES, n), I32),
                   jax.ShapeDtypeStruct((n // tt, N_EXPERTS, 2 * LANES), I32)),
        grid=(n // tt,),
        in_specs=[pl.BlockSpec((N_EXPERTS, tt), lambda i: (0, i + tile0)),
                  pl.BlockSpec((N_EXPERTS, LANES), lambda i: (0, 0))],
        out_specs=(pl.BlockSpec((SUBLANES, tt), tok), pl.BlockSpec((SUBLANES, tt), tok),
                   pl.BlockSpec((SUBLANES, tt), tok),
                   pl.BlockSpec((1, N_EXPERTS, 2 * LANES), lambda i: (i, 0, 0))),
        compiler_params=_cparams("parallel"),
        name="route",
    )(scores_t, jnp.broadcast_to(bias[:, None], (N_EXPERTS, LANES)))


def _moe_kernel(group, block, has_prev, pos_ref, posa_ref, wts_ref, meta_ref, hp_ref, weg_ref, weu_ref, wed_ref,
                *refs):
    o_ref, xg_ref, y_ref, rt_ref = refs[1:] if has_prev else refs
    d = weg_ref.shape[1]
    half = d // 2
    tt = hp_ref.shape[0] // HP_CHUNKS
    eg = pl.program_id(1)

    @pl.when((pl.program_id(0) == 0) & (eg == 0))
    def _():
        def zero(i, carry):
            for u in range(SUBLANES):
                rt_ref[i * SUBLANES + u] = 0
            return carry
        lax.fori_loop(0, rt_ref.shape[0] // SUBLANES, zero, 0)

    @pl.when(eg == 0)
    def _():
        def scatter(g, carry):
            base = g * (SUBLANES * HP_CHUNKS)
            for u in range(SUBLANES):
                for k in range(TOP_K):
                    rt_ref[pos_ref[(g * SUBLANES + u) * SUBLANES + k]] = base + u
            return carry
        lax.fori_loop(0, tt // SUBLANES, scatter, 0)

    sub = lax.broadcasted_iota(I32, (SUBLANES, LANES), 0)

    def expert_block(ge, buf, base, n_valid):
        base = pl.multiple_of(base, SUBLANES)
        for i in range(block):
            xg_ref[buf, pl.ds(_tile_major_addr(i, HP_CHUNKS), HP_CHUNKS, stride=SUBLANES), :] = (
                hp_ref[pl.ds(rt_ref[base + i], HP_CHUNKS, stride=SUBLANES), :])
        yield
        halves = [_unpack_halves(col) for col in _tile_major_columns(xg_ref, block, HP_CHUNKS, (buf,))]
        xa = jnp.concatenate([h[0] for h in halves], axis=1).astype(BF16)
        xb = jnp.concatenate([h[1] for h in halves], axis=1).astype(BF16)
        hg = (jnp.dot(xa, weg_ref[ge, 0:half, :], preferred_element_type=F32)
              + jnp.dot(xb, weg_ref[ge, half:d, :], preferred_element_type=F32))
        hu = (jnp.dot(xa, weu_ref[ge, 0:half, :], preferred_element_type=F32)
              + jnp.dot(xb, weu_ref[ge, half:d, :], preferred_element_type=F32))
        yield
        y = jnp.dot((_silu(hg) * hu).astype(BF16), wed_ref[ge], preferred_element_type=F32)
        yield
        packed = _pack_halves(y[:, :half], y[:, half:])
        row0 = pl.multiple_of(base * HP_CHUNKS, SUBLANES * HP_CHUNKS)
        for g in range(block // SUBLANES):
            keep = sub < n_valid - g * SUBLANES
            for c in range(HP_CHUNKS):
                pltpu.store(y_ref.at[pl.ds(row0 + (g * HP_CHUNKS + c) * SUBLANES, SUBLANES), :],
                            packed[g * SUBLANES:(g + 1) * SUBLANES, c * LANES:(c + 1) * LANES], mask=keep)

    cnts = [meta_ref[0, eg * group + ge] for ge in range(group)]
    offs = [meta_ref[1, eg * group + ge] for ge in range(group)]
    _round_robin([expert_block(ge, ge, offs[ge], cnts[ge]) for ge in range(group)])
    for ge in range(group):
        def more(j, carry, ge=ge):
            for _ in expert_block(ge, ge, offs[ge] + j * block, cnts[ge] - j * block):
                pass
            return carry
        lax.fori_loop(1, (cnts[ge] + (block - 1)) // block, more, 0)

    @pl.when(eg == N_EXPERTS // group - 1)
    def _():
        def tokens(g, carry):
            for u in range(SUBLANES):
                e0 = (g * SUBLANES + u) * SUBLANES
                acc_a = jnp.zeros((HP_CHUNKS, LANES), F32)
                acc_b = jnp.zeros((HP_CHUNKS, LANES), F32)
                for k in range(TOP_K):
                    ya, yb = _unpack_halves(y_ref[pl.ds(posa_ref[e0 + k], HP_CHUNKS, stride=SUBLANES), :])
                    w = wts_ref[e0 + k]
                    acc_a = acc_a + w * ya
                    acc_b = acc_b + w * yb
                out0 = pl.multiple_of(g * (2 * HP_CHUNKS * SUBLANES), 2 * HP_CHUNKS * SUBLANES) + u
                o_ref[pl.ds(out0, HP_CHUNKS, stride=SUBLANES), :] = acc_a
                o_ref[pl.ds(out0 + HP_CHUNKS * SUBLANES, HP_CHUNKS, stride=SUBLANES), :] = acc_b
            return carry

        lax.fori_loop(0, tt // SUBLANES, tokens, 0)


def _moe_segment(prev, scores_t, hp, router_bias, expert_w, layer, row0, n, tt):
    w_eg, w_eu, w_ed = expert_w
    d, de = w_eg.shape[1:]
    half = d // 2
    n_all = hp.shape[0] // HP_CHUNKS
    mean = tt * TOP_K / N_EXPERTS
    block = int(-(-(mean + MOE_BLOCK_SIGMAS * math.sqrt(mean * (1.0 - 1.0 / N_EXPERTS))) // 16) * 16)
    group = 1
    while group * 2 <= min(MOE_MAX_GROUP, MOE_ROWS_IN_FLIGHT // block) and N_EXPERTS % (group * 2) == 0:
        group *= 2
    group0 = layer * (N_EXPERTS // group)
    n_tiles, tile0 = n // tt, row0 // tt
    n_slots = -(-(tt * TOP_K + N_EXPERTS * (SUBLANES - 1) + block) // (SUBLANES * LANES)) * SUBLANES * LANES
    posa, wts, pos, meta = _route(scores_t, router_bias, tt, row0, n)
    cnt, off = meta[:, :, 0], meta[:, :, LANES]

    def per_token(a):
        return a.T.reshape(-1)

    meta_s = jnp.zeros((n_tiles, SUBLANES, LANES), I32)
    meta_s = meta_s.at[:, 0, :N_EXPERTS].set(cnt).at[:, 1, :N_EXPERTS].set(off).reshape(-1, LANES)

    tile = lambda i, e: (i + tile0, 0)
    expert = lambda i, e: (group0 + e, 0, 0)
    smem = functools.partial(pl.BlockSpec, memory_space=pltpu.SMEM)
    once = pl.Buffered(1)
    flat = smem((SUBLANES * tt,), lambda i, e: (i,))
    in_specs = [
        flat, flat, flat, smem((SUBLANES, LANES), lambda i, e: (i, 0)),
        pl.BlockSpec((tt * HP_CHUNKS, LANES), tile),
        pl.BlockSpec((group, d, de), expert), pl.BlockSpec((group, d, de), expert),
        pl.BlockSpec((group, de, d), expert),
    ]
    args = [per_token(pos), per_token(posa), per_token(wts), meta_s, hp, w_eg, w_eu, w_ed]
    aliases = {}
    if prev is not None:
        in_specs.append(pl.BlockSpec(memory_space=pl.ANY))
        args.append(prev)
        aliases = {len(args) - 1: 0}
    return pl.pallas_call(
        functools.partial(_moe_kernel, group, block, prev is not None),
        out_shape=jax.ShapeDtypeStruct((n_all * 2 * HP_CHUNKS, LANES), F32),
        grid=(n_tiles, N_EXPERTS // group),
        in_specs=in_specs,
        out_specs=pl.BlockSpec((tt * 2 * HP_CHUNKS, LANES), tile, pipeline_mode=once),
        scratch_shapes=[pltpu.VMEM((group, block * HP_CHUNKS, LANES), U32),
                        pltpu.VMEM((n_slots * HP_CHUNKS, LANES), U32),
                        pltpu.SMEM((n_slots,), I32)],
        input_output_aliases=aliases,
        compiler_params=_cparams("arbitrary", "arbitrary"),
        name="moe",
    )(*args)


def _moe_final_kernel(tiles_per_batch, n_batch, routed_ref, hp_ref, x_ref, mod_ref, gpost_ref,
                      wsg_ref, wsu_ref, wsd_ref, o_ref):
    tm, d = x_ref.shape
    half = d // 2
    r = jnp.minimum(pl.program_id(0) // tiles_per_batch, n_batch)
    gate2 = mod_ref[pl.ds(r, 1), 5 * d:6 * d]
    halves = [_unpack_halves(col) for col in _tile_major_columns(hp_ref, tm, HP_CHUNKS)]
    xa = jnp.concatenate([h[0] for h in halves], axis=1).astype(BF16)
    xb = jnp.concatenate([h[1] for h in halves], axis=1).astype(BF16)
    hg = (jnp.dot(xa, wsg_ref[0:half, :], preferred_element_type=F32)
          + jnp.dot(xb, wsg_ref[half:d, :], preferred_element_type=F32))
    hu = (jnp.dot(xa, wsu_ref[0:half, :], preferred_element_type=F32)
          + jnp.dot(xb, wsu_ref[half:d, :], preferred_element_type=F32))
    routed = jnp.concatenate(_tile_major_columns(routed_ref, tm, 2 * HP_CHUNKS), axis=1)
    f = jnp.dot((_silu(hg) * hu).astype(BF16), wsd_ref[...], preferred_element_type=F32) + routed
    o_ref[...] = x_ref[...] + gate2 * (_rms(f) * gpost_ref[...])


def _moe_final(routed, hp, xs, mod, g_post, w_sg, w_su, w_sd, n_batch, seq):
    n, d = xs.shape
    tm = TOKEN_TILE
    row = lambda i: (i, 0)
    const = lambda i: (0, 0)
    return pl.pallas_call(
        functools.partial(_moe_final_kernel, seq // tm, n_batch),
        out_shape=jax.ShapeDtypeStruct((n, d), F32),
        grid=(n // tm,),
        in_specs=[pl.BlockSpec((tm * 2 * HP_CHUNKS, LANES), row), pl.BlockSpec((tm * HP_CHUNKS, LANES), row),
                  pl.BlockSpec((tm, d), row),
                  pl.BlockSpec(mod.shape, const), pl.BlockSpec((1, d), const),
                  pl.BlockSpec(w_sg.shape, const), pl.BlockSpec(w_su.shape, const), pl.BlockSpec(w_sd.shape, const)],
        out_specs=pl.BlockSpec((tm, d), row),
        compiler_params=_cparams("parallel"),
        name="moe_final",
    )(routed, hp, xs, mod, g_post, w_sg, w_su, w_sd)


def _moe_layer(scores_t, hp, xs, mod, router_bias, g_post, expert_w, shared_w, layer, n_lat, n_batch, seq):
    pick = lambda rows: next(t for t in MOE_TILES if rows % t == 0)
    routed = _moe_segment(None, scores_t, hp, router_bias, expert_w, layer, 0, n_lat, pick(n_lat))
    n_ctx = hp.shape[0] // HP_CHUNKS - n_lat
    if n_ctx:
        tt = next(t for t in MOE_TILES if n_ctx % t == 0 and n_lat % t == 0)
        routed = _moe_segment(routed, scores_t, hp, router_bias, expert_w, layer, n_lat, n_ctx, tt)
    return _moe_final(routed, hp, xs, mod, g_post, *shared_w, n_batch, seq)


def kernel(x, c, ctx, c_ctx, w_mod, b_mod, g_pre_mix, g_post_mix, g_pre_ffn, g_post_ffn, w_in, w_out,
           ret_decay_logit, gla_w_gate, gla_b_gate, gla_norm_g, diff_lambda, diff_norm_g,
           w_router, router_bias, w_exp_gate, w_exp_up, w_exp_down, w_sh_gate, w_sh_up, w_sh_down):
    n_batch, seq, d = x.shape
    ctx_len = ctx.shape[1]
    depth = w_mod.shape[0]
    n_lat = n_batch * seq
    assert seq % TOKEN_TILE == 0 and (n_batch * ctx_len) % TOKEN_TILE == 0 and n_batch < SUBLANES
    scan_rows = CHUNK * SCAN_CHUNKS
    assert seq % ATTN_TK == 0 and seq % scan_rows == 0 and ctx_len % scan_rows == 0 and n_lat % ctx_len == 0

    xs = (x.reshape(n_lat, d), ctx.reshape(n_batch * ctx_len, d))
    cond = jnp.zeros((SUBLANES, d), F32).at[:n_batch].set(c).at[n_batch].set(c_ctx)
    mods = _modulation(cond, w_mod, b_mod)
    tables = _rope_tables(seq, RET_DK, TOKEN_TILE) + _rope_tables(seq, DIFF_DH, TOKEN_TILE)
    lr0 = N_HEADS * (2 * RET_DK + 2 * RET_DV + 2 * GLA_DK + 2 * GLA_DV)
    row = lambda a: a[None, :]
    expert_w = tuple(w.astype(BF16).reshape((depth * N_EXPERTS,) + w.shape[2:])
                     for w in (w_exp_gate, w_exp_up, w_exp_down))

    for layer in range(depth):
        need_ctx = layer < depth - 1
        lam_init = 0.8 - 0.6 * math.exp(-0.3 * layer)
        mod = mods[layer]
        wl = w_in[layer]
        w_r = jnp.concatenate([wl[:, :lr0], wl[:, lr0:lr0 + GLA_LOWRANK],
                               jnp.zeros((d, LANES - GLA_LOWRANK), F32), wl[:, lr0 + GLA_LOWRANK:]],
                              axis=1).astype(BF16)
        ret, gla, glr, dqt, dk, vt = _in_projection(xs, mod, row(g_pre_mix[layer]), w_r, tables, n_batch, seq)

        o_rf, o_gf, o_rb, o_gb = _scan(ret, gla, glr, _scan_tables(ret_decay_logit[layer]), gla_w_gate[layer],
                                       gla_b_gate[layer], n_batch, seq, ctx_len)

        lp = jnp.zeros((SUBLANES, LANES), F32).at[:4, :DIFF_DH].set(diff_lambda[layer])
        g_col = jnp.broadcast_to(diff_norm_g[layer][:, None], (LANES, LANES))
        attn = functools.partial(_diff_attention, dqt, dk, vt, lp, g_col, lam_init, n_batch, seq, ctx_len)
        dif = attn(True)
        n_rows = n_lat
        if need_ctx:
            n_rows = n_lat + n_batch * ctx_len
            if isinstance(xs, tuple):
                dif = (dif, attn(False))
            else:
                dif = jnp.concatenate([dif, attn(False)], axis=0)
        elif isinstance(xs, tuple):
            xs = xs[0]

        xs, hp, scores_t = _merge(o_rf, o_rb, o_gf, o_gb, ret, gla, dif, xs, mod, row(g_post_mix[layer]),
                                  row(g_pre_ffn[layer]), row(jnp.tile(gla_norm_g[layer], N_HEADS)),
                                  w_out[layer].astype(BF16), w_router[layer].T, n_rows, n_batch, seq)
        shared_w = tuple(w[layer].astype(BF16) for w in (w_sh_gate, w_sh_up, w_sh_down))
        xs = _moe_layer(scores_t, hp, xs, mod, router_bias[layer], row(g_post_ffn[layer]), expert_w, shared_w,
                        layer, n_lat, n_batch, seq)
    return xs[:n_lat].reshape(n_batch, seq, d)
```

```python
import functools
import math

import jax
import jax.numpy as jnp
import numpy as np
from jax import lax
from jax.experimental import pallas as pl
from jax.experimental.pallas import tpu as pltpu

F32 = jnp.float32
BF16 = jnp.bfloat16
I32 = jnp.int32
U32 = jnp.uint32

GRID_W = 64
CHUNK = 128
N_HEADS = 4
RET_DK, RET_DV = 32, 64
GLA_DK, GLA_DV = 32, 64
GLA_LOWRANK = 16
GLA_TAU = 16.0
DIFF_DH = 64
ROPE_BASE = 10000.0
N_EXPERTS = 64
TOP_K = 6
N_GROUPS = 8
TOPK_GROUPS = 4
GROUP_SIZE = N_EXPERTS // N_GROUPS
ROUTED_SCALE = 2.5
EPS = 1e-6
GLA_SUB = 16
SCAN_CHUNKS = 2

LANES = 128
SUBLANES = 8
VMEM_LIMIT_BYTES = 56 * 1024 * 1024

TOKEN_TILE = 512
ATTN_TQ = 512
ATTN_TK = 512
ATTN_ONES = 16
MOE_TILES = (2048, 1024, 512)
MOE_ROWS_IN_FLIGHT = 512
MOE_MAX_GROUP = 4
MOE_BLOCK_SIGMAS = 4.0
HP_CHUNKS = 4

HIGHEST = lax.Precision.HIGHEST
NT_DIMS = (((1,), (1,)), ((), ()))
TN_DIMS = (((0,), (0,)), ((), ()))


def _cparams(*sem):
    return pltpu.CompilerParams(dimension_semantics=sem, vmem_limit_bytes=VMEM_LIMIT_BYTES)


def _log_sigmoid(x):
    return jnp.minimum(x, 0.0) - jnp.log(1.0 + jnp.exp(-jnp.abs(x)))


def _silu(x):
    return x * (1.0 / (1.0 + jnp.exp(-x)))


def _rms(x):
    return x * lax.rsqrt(jnp.mean(x * x, axis=-1, keepdims=True) + EPS)


def _pack_halves(a, b):
    ua = lax.bitcast_convert_type(a.astype(BF16).astype(F32), U32)
    ub = lax.bitcast_convert_type(b.astype(BF16).astype(F32), U32)
    return (ua & jnp.uint32(0xFFFF0000)) | (ub >> 16)


def _unpack_halves(w):
    a = lax.bitcast_convert_type(w & jnp.uint32(0xFFFF0000), F32)
    b = lax.bitcast_convert_type(w << 16, F32)
    return a, b


def _tile_major_store(ref, v, lead=()):
    k = v.shape[1] // LANES
    for g in range(v.shape[0] // SUBLANES):
        for c in range(k):
            blk = g * k + c
            ref[lead + (slice(blk * SUBLANES, (blk + 1) * SUBLANES), slice(None))] = (
                v[g * SUBLANES:(g + 1) * SUBLANES, c * LANES:(c + 1) * LANES])


def _tile_major_columns(ref, rows, k, lead=()):
    return [jnp.concatenate([ref[lead + (slice((g * k + c) * SUBLANES, (g * k + c + 1) * SUBLANES), slice(None))]
                             for g in range(rows // SUBLANES)], axis=0) for c in range(k)]


def _tile_major_addr(t, k):
    return (t >> 3) * (SUBLANES * k) + (t & (SUBLANES - 1))


def _mod_kernel(cond_ref, w_ref, b_ref, o_ref):
    a = _silu(cond_ref[...])
    o_ref[0] = jnp.dot(a, w_ref[0], precision=HIGHEST, preferred_element_type=F32) + b_ref[0]


def _modulation(cond, w_mod, b_mod):
    n_layers, d, d6 = w_mod.shape
    tn = 1024
    return pl.pallas_call(
        _mod_kernel,
        out_shape=jax.ShapeDtypeStruct((n_layers, SUBLANES, d6), F32),
        grid=(n_layers, d6 // tn),
        in_specs=[
            pl.BlockSpec((SUBLANES, d), lambda l, j: (0, 0)),
            pl.BlockSpec((1, d, tn), lambda l, j: (l, 0, j)),
            pl.BlockSpec((1, 1, tn), lambda l, j: (l, 0, j)),
        ],
        out_specs=pl.BlockSpec((1, SUBLANES, tn), lambda l, j: (l, 0, j)),
        compiler_params=_cparams("parallel", "parallel"),
        name="modulation",
    )(cond, w_mod, b_mod.reshape(n_layers, 1, d6))


def _rope(x, cos, sin, quarter):
    lane = lax.broadcasted_iota(I32, (1, LANES), 1)
    first = (lane % (2 * quarter)) < quarter
    outs = []
    for c in range(x.shape[-1] // LANES):
        xc = x[:, c * LANES:(c + 1) * LANES]
        partner = jnp.where(first, pltpu.roll(xc, LANES - quarter, 1), pltpu.roll(xc, quarter, 1))
        outs.append(xc * cos + partner * sin)
    return outs[0] if len(outs) == 1 else jnp.concatenate(outs, axis=-1)


def _split_rows(a, b, tm, width):
    n_a = a.shape[0] // tm
    return n_a, [pl.BlockSpec((tm, width), lambda i: (jnp.minimum(i, n_a - 1), 0)),
                 pl.BlockSpec((tm, width), lambda i: (jnp.maximum(i - n_a, 0), 0))]


def _pick_rows(n_a, a_ref, b_ref):
    return jnp.where(pl.program_id(0) < n_a, a_ref[...], b_ref[...])


def _inproj_kernel(tiles_per_batch, n_batch, n_a, *refs):
    n_x = 2 if n_a else 1
    (mod_ref, g_ref, w_ref, c32_ref, s32_ref, c64_ref, s64_ref,
     ret_ref, gla_ref, glr_ref, dqt_ref, dk_ref, vt_ref) = refs[n_x:]
    x = _pick_rows(n_a, *refs[:2]) if n_a else refs[0][...]
    d = x.shape[-1]
    r = jnp.minimum(pl.program_id(0) // tiles_per_batch, n_batch)
    shift = mod_ref[pl.ds(r, 1), 0:d]
    scale = mod_ref[pl.ds(r, 1), d:2 * d]
    h = (_rms(x) * g_ref[...] * (1.0 + scale) + shift).astype(BF16)

    def proj(lo, hi):
        return jnp.dot(h, w_ref[:, lo:hi], preferred_element_type=F32)

    c32, s32 = c32_ref[...], s32_ref[...]
    c64, s64 = c64_ref[...], s64_ref[...]
    ret = proj(0, 768)
    ret_ref[:, 0:128] = _rope(ret[:, 0:128], c32, s32, RET_DK // 4).astype(BF16)
    ret_ref[:, 128:256] = (_rope(ret[:, 128:256], c32, s32, RET_DK // 4) * RET_DK ** -0.5).astype(BF16)
    ret_ref[:, 256:768] = ret[:, 256:768].astype(BF16)
    gla = proj(768, 1536)
    gla_ref[:, 0:128] = (gla[:, 0:128] * GLA_DK ** -0.5).astype(BF16)
    gla_ref[:, 128:768] = gla[:, 128:768].astype(BF16)
    glr_ref[...] = proj(1536, 1664)
    dq = _rope(proj(1664, 2176), c64, s64, DIFF_DH // 4) * (DIFF_DH ** -0.5 * math.log2(math.e))
    for hd in range(N_HEADS):
        dqt_ref[hd * LANES:(hd + 1) * LANES, :] = dq[:, hd * LANES:(hd + 1) * LANES].T.astype(BF16)
    dk_ref[...] = _rope(proj(2176, 2688), c64, s64, DIFF_DH // 4).astype(BF16)
    dv = proj(2688, 3200)
    vrows = LANES + ATTN_ONES
    for hd in range(N_HEADS):
        vt_ref[hd * vrows:hd * vrows + LANES, :] = dv[:, hd * LANES:(hd + 1) * LANES].T.astype(BF16)
        vt_ref[hd * vrows + LANES:(hd + 1) * vrows, :] = jnp.ones((ATTN_ONES, dv.shape[0]), BF16)


def _in_projection(xs, mod, g_pre, w_r, tables, n_batch, seq):
    tm = TOKEN_TILE
    if isinstance(xs, tuple):
        d = xs[0].shape[1]
        n = xs[0].shape[0] + xs[1].shape[0]
        n_a, x_specs = _split_rows(*xs, tm, d)
    else:
        n, d = xs.shape
        n_a, x_specs, xs = 0, [pl.BlockSpec((tm, d), lambda i: (i, 0))], (xs,)
    tiles_per_batch = seq // tm
    n_lat_tiles = n_batch * tiles_per_batch
    c32, s32, c64, s64 = tables

    def tab_map(i):
        return (jnp.where(i < n_lat_tiles, i % tiles_per_batch, tiles_per_batch), 0)

    row = lambda i: (i, 0)
    const = lambda i: (0, 0)
    tab_spec = pl.BlockSpec((tm, LANES), tab_map)
    return pl.pallas_call(
        functools.partial(_inproj_kernel, tiles_per_batch, n_batch, n_a),
        out_shape=(
            jax.ShapeDtypeStruct((n, 768), BF16), jax.ShapeDtypeStruct((n, 768), BF16),
            jax.ShapeDtypeStruct((n, LANES), F32),
            jax.ShapeDtypeStruct((N_HEADS * LANES, n), BF16), jax.ShapeDtypeStruct((n, 512), BF16),
            jax.ShapeDtypeStruct((N_HEADS * (LANES + ATTN_ONES), n), BF16)),
        grid=(n // tm,),
        in_specs=x_specs + [
            pl.BlockSpec(mod.shape, const),
            pl.BlockSpec((1, d), const),
            pl.BlockSpec(w_r.shape, const),
            tab_spec, tab_spec, tab_spec, tab_spec,
        ],
        out_specs=(
            pl.BlockSpec((tm, 768), row), pl.BlockSpec((tm, 768), row), pl.BlockSpec((tm, LANES), row),
            pl.BlockSpec((N_HEADS * LANES, tm), lambda i: (0, i)), pl.BlockSpec((tm, 512), row),
            pl.BlockSpec((N_HEADS * (LANES + ATTN_ONES), tm), lambda i: (0, i))),
        compiler_params=_cparams("parallel"),
        name="in_projection",
    )(*xs, mod, g_pre, w_r, c32, s32, c64, s64)


def _rope_tables(seq, head_dim, extra_rows):
    half, quarter = head_dim // 2, head_dim // 4
    freqs = (ROPE_BASE ** (-np.arange(quarter, dtype=np.float32) / quarter)).astype(np.float32)
    t = np.arange(seq)
    row = (t // GRID_W).astype(np.float32)
    col = (t % GRID_W).astype(np.float32)
    j = np.arange(LANES) % head_dim
    jj = j % half
    pos = np.where((j < half)[None, :], row[:, None], col[:, None])
    ang = (pos * freqs[jj % quarter][None, :]).astype(np.float32)
    cos = np.cos(ang)
    sin = np.sin(ang) * np.where(jj < quarter, -1.0, 1.0)[None, :]
    cos = np.concatenate([cos, np.ones((extra_rows, LANES))], axis=0).astype(np.float32)
    sin = np.concatenate([sin, np.zeros((extra_rows, LANES))], axis=0).astype(np.float32)
    return jnp.asarray(cos), jnp.asarray(sin)


def _head_stack(x, width):
    lane = lax.broadcasted_iota(I32, (1, x.shape[-1]), 1)
    zero = jnp.zeros_like(x)
    return jnp.concatenate([jnp.where(lane // width == h, x, zero) for h in range(N_HEADS)], axis=0)


def _head_select(x4, rows):
    lane = lax.broadcasted_iota(I32, (1, x4.shape[-1]), 1)
    out = x4[(N_HEADS - 1) * rows:N_HEADS * rows]
    for h in range(N_HEADS - 2, -1, -1):
        out = jnp.where(lane < (h + 1) * RET_DV, x4[h * rows:(h + 1) * rows], out)
    return out


def _scan_tables_kernel(rl_lane_ref, rl_rows_ref, rl_col_ref, dmat_ref, xi_ref, zeta_ref, gchunk_ref):
    c = CHUNK
    idx = lax.broadcasted_iota(I32, (c, 1), 0).astype(F32)
    ri4 = lax.broadcasted_iota(I32, (N_HEADS * c, c), 0) % c
    ci4 = lax.broadcasted_iota(I32, (N_HEADS * c, c), 1)
    dist = jnp.abs(ri4 - ci4).astype(F32)
    for d in range(2):
        lg_lane = _log_sigmoid(rl_lane_ref[d])
        lg_rows = _log_sigmoid(rl_rows_ref[d])
        lg_col = _log_sigmoid(rl_col_ref[d])
        att4 = (ri4 <= ci4) if d else (ri4 >= ci4)
        dmat_ref[d] = jnp.where(att4, jnp.exp(dist * lg_rows), 0.0)
        xi_ref[d] = jnp.exp(((c - idx) if d else (idx + 1.0)) * lg_lane)
        zeta_ref[d] = jnp.exp((idx if d else (c - 1.0 - idx)) * lg_lane)
        g_chunk = jnp.exp(float(c) * lg_col)
        gchunk_ref[d] = jnp.concatenate([g_chunk, g_chunk], axis=1)


def _scan_tables(ret_logit):
    c = CHUNK
    rl_lane = jnp.repeat(ret_logit, RET_DK, axis=1)[:, None, :]
    rl_rows = jnp.broadcast_to(jnp.repeat(ret_logit, c, axis=1)[:, :, None], (2, N_HEADS * c, c))
    rl_col = jnp.broadcast_to(jnp.repeat(ret_logit, RET_DK, axis=1)[:, :, None], (2, LANES, LANES))
    return pl.pallas_call(
        _scan_tables_kernel,
        out_shape=(jax.ShapeDtypeStruct((2, N_HEADS * c, c), F32), jax.ShapeDtypeStruct((2, c, LANES), F32),
                   jax.ShapeDtypeStruct((2, c, LANES), F32), jax.ShapeDtypeStruct((2, LANES, 2 * LANES), F32)),
        name="scan_tables",
    )(rl_lane, rl_rows, rl_col)


def _state_block_mask():
    return (lax.broadcasted_iota(I32, (LANES, 2 * LANES), 0) // RET_DK
            == lax.broadcasted_iota(I32, (LANES, 2 * LANES), 1) // RET_DV)


def _ret_chain(ret_ref, r0, dmat, xi, zeta, g_chunk, out):
    c = CHUNK
    q = ret_ref[r0:r0 + c, 0:128]
    k = ret_ref[r0:r0 + c, 128:256]
    v = ret_ref[r0:r0 + c, 256:512]
    s = lax.dot_general(_head_stack(q, RET_DK), k, NT_DIMS, preferred_element_type=F32)
    kz = (k.astype(F32) * zeta).astype(BF16)
    u = lax.dot_general(kz, v, TN_DIMS, preferred_element_type=F32)
    yield
    o4 = jnp.dot((s * dmat).astype(BF16), v, preferred_element_type=F32)
    out.update(qx=(q.astype(F32) * xi).astype(BF16), u=jnp.where(_state_block_mask(), u, 0.0), g=g_chunk)
    yield
    out.update(intra=_head_select(o4, c))


def _state_chain(parts, o_ref, s_ref):
    c = CHUNK
    s = s_ref[...]
    for r0, p in parts:
        o_ref[r0:r0 + c, :] = p["intra"] + jnp.dot(p["qx"], s.astype(BF16), preferred_element_type=F32)
        s = s * p["g"] + p["u"]
        yield
    s_ref[...] = s


def _gla_chain(rev, gla_ref, glr_ref, r0, gw, gb, out):
    c = CHUNK
    ri = lax.broadcasted_iota(I32, (c, c), 0)
    ci = lax.broadcasted_iota(I32, (c, c), 1)
    attends = (ri <= ci) if rev else (ri >= ci)
    bd = _state_block_mask()
    gq = gla_ref[r0:r0 + c, 0:128].astype(F32)
    gk = gla_ref[r0:r0 + c, 128:256].astype(F32)
    gv = gla_ref[r0:r0 + c, 256:512]
    z = jnp.dot(glr_ref[r0:r0 + c, :], gw, precision=HIGHEST, preferred_element_type=F32) + gb
    yield
    la = _log_sigmoid(z) * (math.log2(math.e) / GLA_TAU)
    if rev:
        first = (ri // GLA_SUB) * GLA_SUB + (GLA_SUB - 1)
        ref_sel = ci >= first
    else:
        first = (ri // GLA_SUB) * GLA_SUB
        ref_sel = ci <= first
    sel = jnp.concatenate([jnp.where(attends, 1.0, 0.0), jnp.where(ref_sel, 1.0, 0.0)], axis=0).astype(BF16)
    la_hi = la.astype(BF16)
    la_lo = (la - la_hi.astype(F32)).astype(BF16)
    sums = (jnp.dot(sel, la_hi, preferred_element_type=F32) + jnp.dot(sel, la_lo, preferred_element_type=F32))
    b, refrow = sums[0:c], sums[c:2 * c]
    yield
    qs = gq * jnp.exp2(b - refrow)
    b_last = b[0:1] if rev else b[c - 1:c]
    kz = (gk * jnp.exp2(b_last - b)).astype(BF16)
    u = lax.dot_general(kz, gv, TN_DIMS, preferred_element_type=F32)
    eye = lax.broadcasted_iota(I32, (LANES, LANES), 0) == lax.broadcasted_iota(I32, (LANES, LANES), 1)
    g_col = jnp.sum(jnp.where(eye, jnp.exp2(b_last), 0.0), axis=1, keepdims=True)
    jcol = lax.broadcasted_iota(I32, (c, 1), 0)
    rr = lax.broadcasted_iota(I32, (N_HEADS * GLA_SUB, c), 0) % GLA_SUB
    cc = lax.broadcasted_iota(I32, (N_HEADS * GLA_SUB, c), 1)
    pieces = []
    for blk in range(c // GLA_SUB):
        lo = blk * GLA_SUB
        ref_b = refrow[lo:lo + 1]
        seen = (jcol >= lo) if rev else (jcol < lo + GLA_SUB)
        ks = (gk * jnp.exp2(jnp.where(seen, ref_b - b, -jnp.inf))).astype(BF16)
        qz = _head_stack(qs[lo:lo + GLA_SUB], GLA_DK).astype(BF16)
        att = lax.dot_general(qz, ks, NT_DIMS, preferred_element_type=F32)
        ok = (cc >= rr + lo) if rev else (cc <= rr + lo)
        att = jnp.where(ok, att, 0.0).astype(BF16)
        pieces.append(_head_select(jnp.dot(att, gv, preferred_element_type=F32), GLA_SUB))
    out.update(intra=jnp.concatenate(pieces, axis=0), qx=(gq * jnp.exp2(b)).astype(BF16),
               u=jnp.where(bd, u, 0.0), g=g_col)


def _round_robin(chains):
    while chains:
        chains = [ch for ch in chains if next(ch, True) is None]


def _scan_kernel(ret_f, gla_f, glr_f, ret_b, gla_b, glr_b, dmat_ref, xi_ref, zeta_ref, gchunk_ref, gw_ref, gb_ref,
                 orf_ref, ogf_ref, orb_ref, ogb_ref, sr_ref, sg_ref):
    @pl.when(pl.program_id(1) == 0)
    def _():
        sr_ref[...] = jnp.zeros_like(sr_ref)
        sg_ref[...] = jnp.zeros_like(sg_ref)

    chains, state_chains = [], []
    for d, (ret_ref, gla_ref, glr_ref, ore_ref, ogl_ref) in enumerate(
            ((ret_f, gla_f, glr_f, orf_ref, ogf_ref), (ret_b, gla_b, glr_b, orb_ref, ogb_ref))):
        order = range(SCAN_CHUNKS - 1, -1, -1) if d else range(SCAN_CHUNKS)
        gla_parts, ret_parts = [], []
        for j in order:
            r0 = j * CHUNK
            gla_parts.append((r0, {}))
            ret_parts.append((r0, {}))
            chains.append(_gla_chain(bool(d), gla_ref, glr_ref, r0, gw_ref[d], gb_ref[d], gla_parts[-1][1]))
            chains.append(_ret_chain(ret_ref, r0, dmat_ref[d], xi_ref[d], zeta_ref[d], gchunk_ref[d],
                                     ret_parts[-1][1]))
        state_chains.append(_state_chain(gla_parts, ogl_ref, sg_ref.at[d]))
        state_chains.append(_state_chain(ret_parts, ore_ref, sr_ref.at[d]))
    _round_robin(chains)
    _round_robin(state_chains)


def _scan(ret, gla, glr, tables, gla_w, gla_b, n_batch, seq, ctx_len):
    n = ret.shape[0]
    c = CHUNK * SCAN_CHUNKS
    nc_ctx, nc_lat = ctx_len // c, seq // c
    n_steps = nc_ctx + nc_lat
    ctx_base = n_batch * nc_lat

    def fwd(b, s):
        return (jnp.where(s < nc_ctx, ctx_base + b * nc_ctx + s, b * nc_lat + (s - nc_ctx)), 0)

    def bwd(b, s):
        return (jnp.where(s < nc_ctx, ctx_base + b * nc_ctx + (nc_ctx - 1 - s), b * nc_lat + (n_steps - 1 - s)), 0)

    const3 = lambda b, s: (0, 0, 0)
    gw = jnp.zeros((2, LANES, LANES), F32).at[:, :GLA_LOWRANK].set(gla_w)
    gb = gla_b[:, None, :]
    chain_in = lambda m: [pl.BlockSpec((c, 768), m), pl.BlockSpec((c, 768), m), pl.BlockSpec((c, LANES), m)]
    o_sds = jax.ShapeDtypeStruct((n, 256), F32)
    return pl.pallas_call(
        _scan_kernel,
        out_shape=(o_sds, o_sds, o_sds, o_sds),
        grid=(n_batch, n_steps),
        in_specs=chain_in(fwd) + chain_in(bwd) + [pl.BlockSpec(t.shape, const3) for t in tables]
        + [pl.BlockSpec(gw.shape, const3), pl.BlockSpec(gb.shape, const3)],
        out_specs=(pl.BlockSpec((c, 256), fwd), pl.BlockSpec((c, 256), fwd),
                   pl.BlockSpec((c, 256), bwd), pl.BlockSpec((c, 256), bwd)),
        scratch_shapes=[pltpu.VMEM((2, LANES, 2 * LANES), F32), pltpu.VMEM((2, LANES, 2 * LANES), F32)],
        compiler_params=_cparams("arbitrary", "arbitrary"),
        name="scan",
    )(ret, gla, glr, ret, gla, glr, *tables, gw, gb)


def _attn_kernel(n_lat_blocks, lam_init, *refs):
    if n_lat_blocks:
        q_ref, kc_ref, vtc_ref, kl_ref, vtl_ref, lp_ref, g_ref, o_ref, m_ref, acc_ref, st_ref = refs
    else:
        q_ref, kc_ref, vtc_ref, lp_ref, g_ref, o_ref, m_ref, acc_ref, st_ref = refs
    tq = q_ref.shape[1]
    q = q_ref[...]
    dim = lax.broadcasted_iota(I32, (LANES, 1), 0)
    zero = jnp.zeros_like(q)
    qt = jnp.concatenate([jnp.where(dim < DIFF_DH, q, zero), jnp.where(dim >= DIFF_DH, q, zero)], axis=1)
    m_ref[...] = jnp.full(m_ref.shape, -jnp.inf, F32)
    acc_ref[...] = jnp.zeros_like(acc_ref)

    def scores(slot, kb):
        st_ref[slot, 0:kb.shape[0], :] = jnp.dot(kb, qt, preferred_element_type=F32)

    def absorb(slot, vtb):
        st = st_ref[slot, 0:vtb.shape[1], :]
        m_prev = m_ref[...]
        m_new = jnp.maximum(m_prev, jnp.max(st, axis=0, keepdims=True))
        alpha = jnp.exp2(m_prev - m_new)
        p = jnp.exp2(st - m_new).astype(BF16)
        acc_ref[...] = alpha * acc_ref[...] + jnp.dot(vtb, p, preferred_element_type=F32)
        m_ref[...] = m_new

    def k_lat(blk):
        return kl_ref[pl.ds(pl.multiple_of(blk * ATTN_TK, ATTN_TK), ATTN_TK), :]

    def vt_lat(blk):
        return vtl_ref[:, pl.ds(pl.multiple_of(blk * ATTN_TK, ATTN_TK), ATTN_TK)]

    scores(0, kc_ref[...])
    if not n_lat_blocks:
        absorb(0, vtc_ref[...])
    else:
        scores(1, k_lat(0))
        absorb(0, vtc_ref[...])
        n_pairs = (n_lat_blocks - 1) // 2

        def body(i, carry):
            scores(0, k_lat(2 * i + 1))
            absorb(1, vt_lat(2 * i))
            scores(1, k_lat(2 * i + 2))
            absorb(0, vt_lat(2 * i + 1))
            return carry
        lax.fori_loop(0, n_pairs, body, 0)
        done = 2 * n_pairs
        if n_lat_blocks - done == 2:
            scores(0, k_lat(done + 1))
            absorb(1, vt_lat(done))
            absorb(0, vt_lat(done + 1))
        else:
            absorb(1, vt_lat(done))

    lp = lp_ref[...]
    lam = (jnp.exp(jnp.sum(lp[0:1] * lp[1:2], axis=1, keepdims=True))
           - jnp.exp(jnp.sum(lp[2:3] * lp[3:4], axis=1, keepdims=True)) + lam_init)
    acc = acc_ref[...]
    o1 = acc[0:LANES, :tq] / acc[LANES:LANES + 1, :tq]
    o2 = acc[0:LANES, tq:] / acc[LANES:LANES + 1, tq:]
    ot = o1 - lam * o2
    ot = ot * lax.rsqrt(jnp.mean(ot * ot, axis=0, keepdims=True) + EPS) * g_ref[...][:, 0:1] * (1.0 - lam_init)
    o_ref[...] = ot.T.astype(o_ref.dtype)


def _diff_attention(dqt, dk, vt, lp, g, lam_init, n_batch, seq, ctx_len, latent):
    ctx_blk0 = (n_batch * seq) // ctx_len
    vrows = LANES + ATTN_ONES
    kc_spec = pl.BlockSpec((ctx_len, LANES), lambda b, h, i: (ctx_blk0 + b, h))
    vtc_spec = pl.BlockSpec((vrows, ctx_len), lambda b, h, i: (h, ctx_blk0 + b))
    const = lambda b, h, i: (0, 0)
    if latent:
        tq = ATTN_TQ
        n_q = seq // tq
        o_map = lambda b, h, i: (b * n_q + i, h)
        in_specs = [pl.BlockSpec((LANES, tq), lambda b, h, i: (h, b * n_q + i)), kc_spec, vtc_spec,
                    pl.BlockSpec((seq, LANES), lambda b, h, i: (b, h)),
                    pl.BlockSpec((vrows, seq), lambda b, h, i: (h, b))]
        args = (dqt, dk, vt, dk, vt)
        n_rows, n_lat_blocks = n_batch * seq, seq // ATTN_TK
    else:
        tq, n_q = ctx_len, 1
        in_specs = [pl.BlockSpec((LANES, ctx_len), lambda b, h, i: (h, ctx_blk0 + b)), kc_spec, vtc_spec]
        args = (dqt, dk, vt)
        n_rows, o_map, n_lat_blocks = n_batch * ctx_len, (lambda b, h, i: (b, h)), 0
    in_specs += [pl.BlockSpec(lp.shape, const), pl.BlockSpec((LANES, LANES), const)]
    return pl.pallas_call(
        functools.partial(_attn_kernel, n_lat_blocks, lam_init),
        out_shape=jax.ShapeDtypeStruct((n_rows, N_HEADS * LANES), BF16),
        grid=(n_batch, N_HEADS, n_q),
        in_specs=in_specs,
        out_specs=pl.BlockSpec((tq, LANES), o_map),
        scratch_shapes=[pltpu.VMEM((1, 2 * tq), F32), pltpu.VMEM((vrows, 2 * tq), F32),
                        pltpu.VMEM((2, max(ATTN_TK, ctx_len), 2 * tq), F32)],
        compiler_params=_cparams("parallel", "parallel", "parallel"),
        name="diff_attention_lat" if latent else "diff_attention_ctx",
    )(*args, lp, g)


def _merge_kernel(tiles_per_batch, n_batch, n_a, orf_ref, orb_ref, ogf_ref, ogb_ref, rg_ref, gr_ref, *refs):
    n_x = 2 if n_a else 1
    mod_ref, gpost_ref, gpre_ref, glag_ref, wout_ref, wrt_ref, xo_ref, hp_ref, sc_ref = refs[2 * n_x:]
    dif = _pick_rows(n_a, *refs[:2]) if n_a else refs[0][...]
    x = _pick_rows(n_a, *refs[2:4]) if n_a else refs[1][...]
    d = x.shape[-1]
    r = jnp.minimum(pl.program_id(0) // tiles_per_batch, n_batch)
    gate1 = mod_ref[pl.ds(r, 1), 2 * d:3 * d]
    shift2 = mod_ref[pl.ds(r, 1), 3 * d:4 * d]
    scale2 = mod_ref[pl.ds(r, 1), 4 * d:5 * d]
    gi = lax.broadcasted_iota(I32, (256, 256), 0) // RET_DV
    gj = lax.broadcasted_iota(I32, (256, 256), 1) // RET_DV
    group_mean = jnp.where(gi == gj, 1.0 / RET_DV, 0.0).astype(BF16)

    def head_norm(o):
        ms = jnp.dot((o * o).astype(BF16), group_mean, preferred_element_type=F32)
        return o * lax.rsqrt(ms + EPS)

    ret = head_norm(orf_ref[...] + orb_ref[...]) * _silu(rg_ref[...].astype(F32))
    gla = head_norm(ogf_ref[...] + ogb_ref[...]) * glag_ref[...] * _silu(gr_ref[...].astype(F32))
    m = (jnp.dot(ret.astype(BF16), wout_ref[0:256, :], preferred_element_type=F32)
         + jnp.dot(gla.astype(BF16), wout_ref[256:512, :], preferred_element_type=F32)
         + jnp.dot(dif, wout_ref[512:1024, :], preferred_element_type=F32))
    x_new = x + gate1 * (_rms(m) * gpost_ref[...])
    xo_ref[...] = x_new
    h2 = _rms(x_new) * gpre_ref[...] * (1.0 + scale2) + shift2
    _tile_major_store(hp_ref, _pack_halves(h2[:, :d // 2], h2[:, d // 2:]))
    def split(a):
        hi = a.astype(BF16)
        return hi, (a - hi.astype(F32)).astype(BF16)

    h_hi, h_lo = split(h2)
    w_hi, w_lo = split(wrt_ref[...])
    nt = functools.partial(lax.dot_general, dimension_numbers=NT_DIMS, preferred_element_type=F32)
    logits = nt(w_hi, h_hi) + (nt(w_hi, h_lo) + nt(w_lo, h_hi))
    sc_ref[...] = 1.0 / (1.0 + jnp.exp(-logits))


def _merge(o_rf, o_rb, o_gf, o_gb, ret, gla, dif, xs, mod, g_post, g_pre_ffn, gla_g, w_out, w_rt,
           n_rows, n_batch, seq):
    tm = TOKEN_TILE
    row = lambda i: (i, 0)
    gate_col = lambda i: (i, 2)
    const = lambda i: (0, 0)
    if isinstance(xs, tuple):
        d = xs[0].shape[1]
        n_a, x_specs = _split_rows(*xs, tm, d)
        _, dif_specs = _split_rows(*dif, tm, 512)
    else:
        d = xs.shape[1]
        n_a, x_specs, dif_specs = 0, [pl.BlockSpec((tm, d), row)], [pl.BlockSpec((tm, 512), row)]
        xs, dif = (xs,), (dif,)
    return pl.pallas_call(
        functools.partial(_merge_kernel, seq // tm, n_batch, n_a),
        out_shape=(jax.ShapeDtypeStruct((n_rows, d), F32), jax.ShapeDtypeStruct((n_rows * HP_CHUNKS, LANES), U32),
                   jax.ShapeDtypeStruct((N_EXPERTS, n_rows), F32)),
        grid=(n_rows // tm,),
        in_specs=[
            pl.BlockSpec((tm, 256), row), pl.BlockSpec((tm, 256), row),
            pl.BlockSpec((tm, 256), row), pl.BlockSpec((tm, 256), row),
            pl.BlockSpec((tm, 256), gate_col), pl.BlockSpec((tm, 256), gate_col), *dif_specs, *x_specs,
            pl.BlockSpec(mod.shape, const), pl.BlockSpec((1, d), const), pl.BlockSpec((1, d), const),
            pl.BlockSpec((1, 256), const), pl.BlockSpec(w_out.shape, const), pl.BlockSpec(w_rt.shape, const),
        ],
        out_specs=(pl.BlockSpec((tm, d), row), pl.BlockSpec((tm * HP_CHUNKS, LANES), row),
                   pl.BlockSpec((N_EXPERTS, tm), lambda i: (0, i))),
        compiler_params=_cparams("parallel"),
        name="merge",
    )(o_rf, o_rb, o_gf, o_gb, ret, gla, *dif, *xs, mod, g_post, g_pre_ffn, gla_g, w_out, w_rt)


def _route_kernel(sc_ref, bias_ref, posa_ref, wts_ref, pos_ref, meta_ref):
    tt = sc_ref.shape[-1]
    scores = sc_ref[...]
    sel = scores + bias_ref[...][:, 0:1]
    sub = lax.broadcasted_iota(I32, (GROUP_SIZE, tt), 0)
    neg = -jnp.inf
    gscore = []
    for g in range(N_GROUPS):
        xg = sel[g * GROUP_SIZE:(g + 1) * GROUP_SIZE]
        m1 = jnp.max(xg, axis=0, keepdims=True)
        i1 = jnp.min(jnp.where(xg == m1, sub, GROUP_SIZE), axis=0, keepdims=True)
        m2 = jnp.max(jnp.where(sub == i1, neg, xg), axis=0, keepdims=True)
        gscore.append(m1 + m2)
    rows = []
    for g in range(N_GROUPS):
        rank = jnp.zeros((1, tt), I32)
        for o in range(N_GROUPS):
            if o == g:
                continue
            ahead = (gscore[o] >= gscore[g]) if o < g else (gscore[o] > gscore[g])
            rank = rank + ahead.astype(I32)
        rows.append(jnp.where(rank < TOPK_GROUPS, sel[g * GROUP_SIZE:(g + 1) * GROUP_SIZE], neg))
    masked = jnp.concatenate(rows, axis=0)
    eio = lax.broadcasted_iota(I32, (N_EXPERTS, tt), 0)
    member = jnp.zeros((N_EXPERTS, tt), F32)
    idxs, ws = [], []
    for _ in range(TOP_K):
        m = jnp.max(masked, axis=0, keepdims=True)
        i = jnp.min(jnp.where(masked == m, eio, N_EXPERTS), axis=0, keepdims=True)
        hit = eio == i
        idxs.append(i)
        ws.append(jnp.sum(jnp.where(hit, scores, 0.0), axis=0, keepdims=True))
        member = jnp.where(hit, 1.0, member)
        masked = jnp.where(hit, neg, masked)
    wsum = ws[0]
    for w in ws[1:]:
        wsum = wsum + w
    ti = lax.broadcasted_iota(I32, (tt, tt), 0)
    tj = lax.broadcasted_iota(I32, (tt, tt), 1)
    before = jnp.where(ti < tj, 1.0, 0.0).astype(BF16)
    rank_in_e = jnp.dot(member.astype(BF16), before, preferred_element_type=F32)
    cnt = jnp.sum(member, axis=1, keepdims=True)
    padded = jnp.floor((cnt + (SUBLANES - 1)) * (1.0 / SUBLANES)) * SUBLANES
    ei = lax.broadcasted_iota(I32, (N_EXPERTS, N_EXPERTS), 0)
    ej = lax.broadcasted_iota(I32, (N_EXPERTS, N_EXPERTS), 1)
    lower = jnp.where(ej < ei, 1.0, 0.0)
    off = jnp.dot(lower, jnp.broadcast_to(padded, (N_EXPERTS, LANES)), precision=HIGHEST,
                  preferred_element_type=F32)
    slot = rank_in_e + off[:, 0:1]
    zrow_i = jnp.zeros((SUBLANES - TOP_K, tt), I32)
    zrow_f = jnp.zeros((SUBLANES - TOP_K, tt), F32)
    pos = [jnp.sum(jnp.where(eio == i, slot, 0.0), axis=0, keepdims=True).astype(I32) for i in idxs]
    pos = jnp.concatenate(pos + [zrow_i], axis=0)
    wts_ref[...] = jnp.concatenate([w / wsum * ROUTED_SCALE for w in ws] + [zrow_f], axis=0)
    pos_ref[...] = pos
    posa_ref[...] = _tile_major_addr(pos, HP_CHUNKS)
    meta_ref[0] = jnp.concatenate([jnp.broadcast_to(cnt, (N_EXPERTS, LANES)), off], axis=1).astype(I32)


def _route(scores_t, bias, tt, row0, n):
    tile0 = row0 // tt
    tok = lambda i: (0, i)
    return pl.pallas_call(
        _route_kernel,
        out_shape=(jax.ShapeDtypeStruct((SUBLANES, n), I32), jax.ShapeDtypeStruct((SUBLANES, n), F32),
                   jax.ShapeDtypeStruct((SUBLANES, n), I32),
                   jax.ShapeDtypeStruct((n // tt, N_EXPERTS, 2 * LANES), I32)),
        grid=(n // tt,),
        in_specs=[pl.BlockSpec((N_EXPERTS, tt), lambda i: (0, i + tile0)),
                  pl.BlockSpec((N_EXPERTS, LANES), lambda i: (0, 0))],
        out_specs=(pl.BlockSpec((SUBLANES, tt), tok), pl.BlockSpec((SUBLANES, tt), tok),
                   pl.BlockSpec((SUBLANES, tt), tok),
                   pl.BlockSpec((1, N_EXPERTS, 2 * LANES), lambda i: (i, 0, 0))),
        compiler_params=_cparams("parallel"),
        name="route",
    )(scores_t, jnp.broadcast_to(bias[:, None], (N_EXPERTS, LANES)))


def _moe_kernel(group, block, has_prev, pos_ref, posa_ref, wts_ref, meta_ref, hp_ref, weg_ref, weu_ref, wed_ref,
                *refs):
    o_ref, xg_ref, y_ref, rt_ref = refs[1:] if has_prev else refs
    d = weg_ref.shape[1]
    half = d // 2
    tt = hp_ref.shape[0] // HP_CHUNKS
    eg = pl.program_id(1)

    @pl.when((pl.program_id(0) == 0) & (eg == 0))
    def _():
        def zero(i, carry):
            for u in range(SUBLANES):
                rt_ref[i * SUBLANES + u] = 0
            return carry
        lax.fori_loop(0, rt_ref.shape[0] // SUBLANES, zero, 0)

    @pl.when(eg == 0)
    def _():
        def scatter(g, carry):
            base = g * (SUBLANES * HP_CHUNKS)
            for u in range(SUBLANES):
                for k in range(TOP_K):
                    rt_ref[pos_ref[(g * SUBLANES + u) * SUBLANES + k]] = base + u
            return carry
        lax.fori_loop(0, tt // SUBLANES, scatter, 0)

    sub = lax.broadcasted_iota(I32, (SUBLANES, LANES), 0)

    def expert_block(ge, buf, base, n_valid):
        base = pl.multiple_of(base, SUBLANES)
        for i in range(block):
            xg_ref[buf, pl.ds(_tile_major_addr(i, HP_CHUNKS), HP_CHUNKS, stride=SUBLANES), :] = (
                hp_ref[pl.ds(rt_ref[base + i], HP_CHUNKS, stride=SUBLANES), :])
        yield
        halves = [_unpack_halves(col) for col in _tile_major_columns(xg_ref, block, HP_CHUNKS, (buf,))]
        xa = jnp.concatenate([h[0] for h in halves], axis=1).astype(BF16)
        xb = jnp.concatenate([h[1] for h in halves], axis=1).astype(BF16)
        hg = (jnp.dot(xa, weg_ref[ge, 0:half, :], preferred_element_type=F32)
              + jnp.dot(xb, weg_ref[ge, half:d, :], preferred_element_type=F32))
        hu = (jnp.dot(xa, weu_ref[ge, 0:half, :], preferred_element_type=F32)
              + jnp.dot(xb, weu_ref[ge, half:d, :], preferred_element_type=F32))
        yield
        y = jnp.dot((_silu(hg) * hu).astype(BF16), wed_ref[ge], preferred_element_type=F32)
        packed = _pack_halves(y[:, :half], y[:, half:])
        row0 = pl.multiple_of(base * HP_CHUNKS, SUBLANES * HP_CHUNKS)
        for g in range(block // SUBLANES):
            keep = sub < n_valid - g * SUBLANES
            for c in range(HP_CHUNKS):
                pltpu.store(y_ref.at[pl.ds(row0 + (g * HP_CHUNKS + c) * SUBLANES, SUBLANES), :],
                            packed[g * SUBLANES:(g + 1) * SUBLANES, c * LANES:(c + 1) * LANES], mask=keep)

    cnts = [meta_ref[0, eg * group + ge] for ge in range(group)]
    offs = [meta_ref[1, eg * group + ge] for ge in range(group)]
    _round_robin([expert_block(ge, ge, offs[ge], cnts[ge]) for ge in range(group)])
    for ge in range(group):
        def more(j, carry, ge=ge):
            for _ in expert_block(ge, ge, offs[ge] + j * block, cnts[ge] - j * block):
                pass
            return carry
        lax.fori_loop(1, (cnts[ge] + (block - 1)) // block, more, 0)

    @pl.when(eg == N_EXPERTS // group - 1)
    def _():
        def tokens(g, carry):
            for u in range(SUBLANES):
                e0 = (g * SUBLANES + u) * SUBLANES
                acc_a = jnp.zeros((HP_CHUNKS, LANES), F32)
                acc_b = jnp.zeros((HP_CHUNKS, LANES), F32)
                for k in range(TOP_K):
                    ya, yb = _unpack_halves(y_ref[pl.ds(posa_ref[e0 + k], HP_CHUNKS, stride=SUBLANES), :])
                    w = wts_ref[e0 + k]
                    acc_a = acc_a + w * ya
                    acc_b = acc_b + w * yb
                out0 = pl.multiple_of(g * (2 * HP_CHUNKS * SUBLANES), 2 * HP_CHUNKS * SUBLANES) + u
                o_ref[pl.ds(out0, HP_CHUNKS, stride=SUBLANES), :] = acc_a
                o_ref[pl.ds(out0 + HP_CHUNKS * SUBLANES, HP_CHUNKS, stride=SUBLANES), :] = acc_b
            return carry

        lax.fori_loop(0, tt // SUBLANES, tokens, 0)


def _moe_segment(prev, scores_t, hp, router_bias, expert_w, layer, row0, n, tt):
    w_eg, w_eu, w_ed = expert_w
    d, de = w_eg.shape[1:]
    half = d // 2
    n_all = hp.shape[0] // HP_CHUNKS
    mean = tt * TOP_K / N_EXPERTS
    block = int(-(-(mean + MOE_BLOCK_SIGMAS * math.sqrt(mean * (1.0 - 1.0 / N_EXPERTS))) // 16) * 16)
    group = 1
    while group * 2 <= min(MOE_MAX_GROUP, MOE_ROWS_IN_FLIGHT // block) and N_EXPERTS % (group * 2) == 0:
        group *= 2
    group0 = layer * (N_EXPERTS // group)
    n_tiles, tile0 = n // tt, row0 // tt
    n_slots = -(-(tt * TOP_K + N_EXPERTS * (SUBLANES - 1) + block) // (SUBLANES * LANES)) * SUBLANES * LANES
    posa, wts, pos, meta = _route(scores_t, router_bias, tt, row0, n)
    cnt, off = meta[:, :, 0], meta[:, :, LANES]

    def per_token(a):
        return a.T.reshape(-1)

    meta_s = jnp.zeros((n_tiles, SUBLANES, LANES), I32)
    meta_s = meta_s.at[:, 0, :N_EXPERTS].set(cnt).at[:, 1, :N_EXPERTS].set(off).reshape(-1, LANES)

    tile = lambda i, e: (i + tile0, 0)
    expert = lambda i, e: (group0 + e, 0, 0)
    smem = functools.partial(pl.BlockSpec, memory_space=pltpu.SMEM)
    once = pl.Buffered(1)
    flat = smem((SUBLANES * tt,), lambda i, e: (i,))
    in_specs = [
        flat, flat, flat, smem((SUBLANES, LANES), lambda i, e: (i, 0)),
        pl.BlockSpec((tt * HP_CHUNKS, LANES), tile),
        pl.BlockSpec((group, d, de), expert), pl.BlockSpec((group, d, de), expert),
        pl.BlockSpec((group, de, d), expert),
    ]
    args = [per_token(pos), per_token(posa), per_token(wts), meta_s, hp, w_eg, w_eu, w_ed]
    aliases = {}
    if prev is not None:
        in_specs.append(pl.BlockSpec(memory_space=pl.ANY))
        args.append(prev)
        aliases = {len(args) - 1: 0}
    return pl.pallas_call(
        functools.partial(_moe_kernel, group, block, prev is not None),
        out_shape=jax.ShapeDtypeStruct((n_all * 2 * HP_CHUNKS, LANES), F32),
        grid=(n_tiles, N_EXPERTS // group),
        in_specs=in_specs,
        out_specs=pl.BlockSpec((tt * 2 * HP_CHUNKS, LANES), tile, pipeline_mode=once),
        scratch_shapes=[pltpu.VMEM((group, block * HP_CHUNKS, LANES), U32),
                        pltpu.VMEM((n_slots * HP_CHUNKS, LANES), U32),
                        pltpu.SMEM((n_slots,), I32)],
        input_output_aliases=aliases,
        compiler_params=_cparams("arbitrary", "arbitrary"),
        name="moe",
    )(*args)


def _moe_final_kernel(tiles_per_batch, n_batch, routed_ref, hp_ref, x_ref, mod_ref, gpost_ref,
                      wsg_ref, wsu_ref, wsd_ref, o_ref):
    tm, d = x_ref.shape
    half = d // 2
    r = jnp.minimum(pl.program_id(0) // tiles_per_batch, n_batch)
    gate2 = mod_ref[pl.ds(r, 1), 5 * d:6 * d]
    halves = [_unpack_halves(col) for col in _tile_major_columns(hp_ref, tm, HP_CHUNKS)]
    xa = jnp.concatenate([h[0] for h in halves], axis=1).astype(BF16)
    xb = jnp.concatenate([h[1] for h in halves], axis=1).astype(BF16)
    hg = (jnp.dot(xa, wsg_ref[0:half, :], preferred_element_type=F32)
          + jnp.dot(xb, wsg_ref[half:d, :], preferred_element_type=F32))
    hu = (jnp.dot(xa, wsu_ref[0:half, :], preferred_element_type=F32)
          + jnp.dot(xb, wsu_ref[half:d, :], preferred_element_type=F32))
    routed = jnp.concatenate(_tile_major_columns(routed_ref, tm, 2 * HP_CHUNKS), axis=1)
    f = jnp.dot((_silu(hg) * hu).astype(BF16), wsd_ref[...], preferred_element_type=F32) + routed
    o_ref[...] = x_ref[...] + gate2 * (_rms(f) * gpost_ref[...])


def _moe_final(routed, hp, xs, mod, g_post, w_sg, w_su, w_sd, n_batch, seq):
    n, d = xs.shape
    tm = TOKEN_TILE
    row = lambda i: (i, 0)
    const = lambda i: (0, 0)
    return pl.pallas_call(
        functools.partial(_moe_final_kernel, seq // tm, n_batch),
        out_shape=jax.ShapeDtypeStruct((n, d), F32),
        grid=(n // tm,),
        in_specs=[pl.BlockSpec((tm * 2 * HP_CHUNKS, LANES), row), pl.BlockSpec((tm * HP_CHUNKS, LANES), row),
                  pl.BlockSpec((tm, d), row),
                  pl.BlockSpec(mod.shape, const), pl.BlockSpec((1, d), const),
                  pl.BlockSpec(w_sg.shape, const), pl.BlockSpec(w_su.shape, const), pl.BlockSpec(w_sd.shape, const)],
        out_specs=pl.BlockSpec((tm, d), row),
        compiler_params=_cparams("parallel"),
        name="moe_final",
    )(routed, hp, xs, mod, g_post, w_sg, w_su, w_sd)


def _moe_layer(scores_t, hp, xs, mod, router_bias, g_post, expert_w, shared_w, layer, n_lat, n_batch, seq):
    pick = lambda rows: next(t for t in MOE_TILES if rows % t == 0)
    routed = _moe_segment(None, scores_t, hp, router_bias, expert_w, layer, 0, n_lat, pick(n_lat))
    n_ctx = hp.shape[0] // HP_CHUNKS - n_lat
    if n_ctx:
        tt = next(t for t in MOE_TILES if n_ctx % t == 0 and n_lat % t == 0)
        routed = _moe_segment(routed, scores_t, hp, router_bias, expert_w, layer, n_lat, n_ctx, tt)
    return _moe_final(routed, hp, xs, mod, g_post, *shared_w, n_batch, seq)


def kernel(x, c, ctx, c_ctx, w_mod, b_mod, g_pre_mix, g_post_mix, g_pre_ffn, g_post_ffn, w_in, w_out,
           ret_decay_logit, gla_w_gate, gla_b_gate, gla_norm_g, diff_lambda, diff_norm_g,
           w_router, router_bias, w_exp_gate, w_exp_up, w_exp_down, w_sh_gate, w_sh_up, w_sh_down):
    n_batch, seq, d = x.shape
    ctx_len = ctx.shape[1]
    depth = w_mod.shape[0]
    n_lat = n_batch * seq
    assert seq % TOKEN_TILE == 0 and (n_batch * ctx_len) % TOKEN_TILE == 0 and n_batch < SUBLANES
    scan_rows = CHUNK * SCAN_CHUNKS
    assert seq % ATTN_TK == 0 and seq % scan_rows == 0 and ctx_len % scan_rows == 0 and n_lat % ctx_len == 0

    xs = (x.reshape(n_lat, d), ctx.reshape(n_batch * ctx_len, d))
    cond = jnp.zeros((SUBLANES, d), F32).at[:n_batch].set(c).at[n_batch].set(c_ctx)
    mods = _modulation(cond, w_mod, b_mod)
    tables = _rope_tables(seq, RET_DK, TOKEN_TILE) + _rope_tables(seq, DIFF_DH, TOKEN_TILE)
    lr0 = N_HEADS * (2 * RET_DK + 2 * RET_DV + 2 * GLA_DK + 2 * GLA_DV)
    row = lambda a: a[None, :]
    expert_w = tuple(w.astype(BF16).reshape((depth * N_EXPERTS,) + w.shape[2:])
                     for w in (w_exp_gate, w_exp_up, w_exp_down))

    for layer in range(depth):
        need_ctx = layer < depth - 1
        lam_init = 0.8 - 0.6 * math.exp(-0.3 * layer)
        mod = mods[layer]
        wl = w_in[layer]
        w_r = jnp.concatenate([wl[:, :lr0], wl[:, lr0:lr0 + GLA_LOWRANK],
                               jnp.zeros((d, LANES - GLA_LOWRANK), F32), wl[:, lr0 + GLA_LOWRANK:]],
                              axis=1).astype(BF16)
        ret, gla, glr, dqt, dk, vt = _in_projection(xs, mod, row(g_pre_mix[layer]), w_r, tables, n_batch, seq)

        o_rf, o_gf, o_rb, o_gb = _scan(ret, gla, glr, _scan_tables(ret_decay_logit[layer]), gla_w_gate[layer],
                                       gla_b_gate[layer], n_batch, seq, ctx_len)

        lp = jnp.zeros((SUBLANES, LANES), F32).at[:4, :DIFF_DH].set(diff_lambda[layer])
        g_col = jnp.broadcast_to(diff_norm_g[layer][:, None], (LANES, LANES))
        attn = functools.partial(_diff_attention, dqt, dk, vt, lp, g_col, lam_init, n_batch, seq, ctx_len)
        dif = attn(True)
        n_rows = n_lat
        if need_ctx:
            n_rows = n_lat + n_batch * ctx_len
            if isinstance(xs, tuple):
                dif = (dif, attn(False))
            else:
                dif = jnp.concatenate([dif, attn(False)], axis=0)
        elif isinstance(xs, tuple):
            xs = xs[0]

        xs, hp, scores_t = _merge(o_rf, o_rb, o_gf, o_gb, ret, gla, dif, xs, mod, row(g_post_mix[layer]),
                                  row(g_pre_ffn[layer]), row(jnp.tile(gla_norm_g[layer], N_HEADS)),
                                  w_out[layer].astype(BF16), w_router[layer].T, n_rows, n_batch, seq)
        shared_w = tuple(w[layer].astype(BF16) for w in (w_sh_gate, w_sh_up, w_sh_down))
        xs = _moe_layer(scores_t, hp, xs, mod, router_bias[layer], row(g_post_ffn[layer]), expert_w, shared_w,
                        layer, n_lat, n_batch, seq)
    return xs[:n_lat].reshape(n_batch, seq, d)
```

```python
import functools
import math

import jax
import jax.numpy as jnp
import numpy as np
from jax import lax
from jax.experimental import pallas as pl
from jax.experimental.pallas import tpu as pltpu

F32 = jnp.float32
BF16 = jnp.bfloat16
I32 = jnp.int32
U32 = jnp.uint32

GRID_W = 64
CHUNK = 128
N_HEADS = 4
RET_DK, RET_DV = 32, 64
GLA_DK, GLA_DV = 32, 64
GLA_LOWRANK = 16
GLA_TAU = 16.0
DIFF_DH = 64
ROPE_BASE = 10000.0
N_EXPERTS = 64
TOP_K = 6
N_GROUPS = 8
TOPK_GROUPS = 4
GROUP_SIZE = N_EXPERTS // N_GROUPS
ROUTED_SCALE = 2.5
EPS = 1e-6
GLA_SUB = 16
SCAN_CHUNKS = 2

LANES = 128
SUBLANES = 8
VMEM_LIMIT_BYTES = 56 * 1024 * 1024

TOKEN_TILE = 512
ATTN_TQ = 512
ATTN_TK = 512
ATTN_ONES = 16
MOE_TILES = (2048, 1024, 512)
MOE_ROWS_IN_FLIGHT = 512
MOE_MAX_GROUP = 4
MOE_BLOCK_SIGMAS = 4.0
HP_CHUNKS = 4

HIGHEST = lax.Precision.HIGHEST
NT_DIMS = (((1,), (1,)), ((), ()))
TN_DIMS = (((0,), (0,)), ((), ()))


def _cparams(*sem):
    return pltpu.CompilerParams(dimension_semantics=sem, vmem_limit_bytes=VMEM_LIMIT_BYTES)


def _log_sigmoid(x):
    return jnp.minimum(x, 0.0) - jnp.log(1.0 + jnp.exp(-jnp.abs(x)))


def _silu(x):
    return x * (1.0 / (1.0 + jnp.exp(-x)))


def _rms(x):
    return x * lax.rsqrt(jnp.mean(x * x, axis=-1, keepdims=True) + EPS)


def _pack_halves(a, b):
    ua = lax.bitcast_convert_type(a.astype(BF16).astype(F32), U32)
    ub = lax.bitcast_convert_type(b.astype(BF16).astype(F32), U32)
    return (ua & jnp.uint32(0xFFFF0000)) | (ub >> 16)


def _unpack_halves(w):
    a = lax.bitcast_convert_type(w & jnp.uint32(0xFFFF0000), F32)
    b = lax.bitcast_convert_type(w << 16, F32)
    return a, b


def _tile_major_store(ref, v, lead=()):
    k = v.shape[1] // LANES
    for g in range(v.shape[0] // SUBLANES):
        for c in range(k):
            blk = g * k + c
            ref[lead + (slice(blk * SUBLANES, (blk + 1) * SUBLANES), slice(None))] = (
                v[g * SUBLANES:(g + 1) * SUBLANES, c * LANES:(c + 1) * LANES])


def _tile_major_columns(ref, rows, k, lead=()):
    return [jnp.concatenate([ref[lead + (slice((g * k + c) * SUBLANES, (g * k + c + 1) * SUBLANES), slice(None))]
                             for g in range(rows // SUBLANES)], axis=0) for c in range(k)]


def _tile_major_addr(t, k):
    return (t >> 3) * (SUBLANES * k) + (t & (SUBLANES - 1))


def _mod_kernel(cond_ref, w_ref, b_ref, o_ref):
    a = _silu(cond_ref[...])
    o_ref[0] = jnp.dot(a, w_ref[0], precision=HIGHEST, preferred_element_type=F32) + b_ref[0]


def _modulation(cond, w_mod, b_mod):
    n_layers, d, d6 = w_mod.shape
    tn = 1024
    return pl.pallas_call(
        _mod_kernel,
        out_shape=jax.ShapeDtypeStruct((n_layers, SUBLANES, d6), F32),
        grid=(n_layers, d6 // tn),
        in_specs=[
            pl.BlockSpec((SUBLANES, d), lambda l, j: (0, 0)),
            pl.BlockSpec((1, d, tn), lambda l, j: (l, 0, j)),
            pl.BlockSpec((1, 1, tn), lambda l, j: (l, 0, j)),
        ],
        out_specs=pl.BlockSpec((1, SUBLANES, tn), lambda l, j: (l, 0, j)),
        compiler_params=_cparams("parallel", "parallel"),
        name="modulation",
    )(cond, w_mod, b_mod.reshape(n_layers, 1, d6))


def _rope(x, cos, sin, quarter):
    lane = lax.broadcasted_iota(I32, (1, LANES), 1)
    first = (lane % (2 * quarter)) < quarter
    outs = []
    for c in range(x.shape[-1] // LANES):
        xc = x[:, c * LANES:(c + 1) * LANES]
        partner = jnp.where(first, pltpu.roll(xc, LANES - quarter, 1), pltpu.roll(xc, quarter, 1))
        outs.append(xc * cos + partner * sin)
    return outs[0] if len(outs) == 1 else jnp.concatenate(outs, axis=-1)


def _split_rows(a, b, tm, width):
    n_a = a.shape[0] // tm
    return n_a, [pl.BlockSpec((tm, width), lambda i: (jnp.minimum(i, n_a - 1), 0)),
                 pl.BlockSpec((tm, width), lambda i: (jnp.maximum(i - n_a, 0), 0))]


def _pick_rows(n_a, a_ref, b_ref):
    return jnp.where(pl.program_id(0) < n_a, a_ref[...], b_ref[...])


def _inproj_kernel(tiles_per_batch, n_batch, n_a, *refs):
    n_x = 2 if n_a else 1
    (mod_ref, g_ref, w_ref, c32_ref, s32_ref, c64_ref, s64_ref,
     ret_ref, gla_ref, glr_ref, dqt_ref, dk_ref, vt_ref) = refs[n_x:]
    x = _pick_rows(n_a, *refs[:2]) if n_a else refs[0][...]
    d = x.shape[-1]
    r = jnp.minimum(pl.program_id(0) // tiles_per_batch, n_batch)
    shift = mod_ref[pl.ds(r, 1), 0:d]
    scale = mod_ref[pl.ds(r, 1), d:2 * d]
    h = (_rms(x) * g_ref[...] * (1.0 + scale) + shift).astype(BF16)

    def proj(lo, hi):
        return jnp.dot(h, w_ref[:, lo:hi], preferred_element_type=F32)

    c32, s32 = c32_ref[...], s32_ref[...]
    c64, s64 = c64_ref[...], s64_ref[...]
    ret = proj(0, 768)
    ret_ref[:, 0:128] = _rope(ret[:, 0:128], c32, s32, RET_DK // 4).astype(BF16)
    ret_ref[:, 128:256] = (_rope(ret[:, 128:256], c32, s32, RET_DK // 4) * RET_DK ** -0.5).astype(BF16)
    ret_ref[:, 256:768] = ret[:, 256:768].astype(BF16)
    gla = proj(768, 1536)
    gla_ref[:, 0:128] = (gla[:, 0:128] * GLA_DK ** -0.5).astype(BF16)
    gla_ref[:, 128:768] = gla[:, 128:768].astype(BF16)
    glr_ref[...] = proj(1536, 1664)
    dq = _rope(proj(1664, 2176), c64, s64, DIFF_DH // 4) * (DIFF_DH ** -0.5 * math.log2(math.e))
    for hd in range(N_HEADS):
        dqt_ref[hd * LANES:(hd + 1) * LANES, :] = dq[:, hd * LANES:(hd + 1) * LANES].T.astype(BF16)
    dk_ref[...] = _rope(proj(2176, 2688), c64, s64, DIFF_DH // 4).astype(BF16)
    dv = proj(2688, 3200)
    vrows = LANES + ATTN_ONES
    for hd in range(N_HEADS):
        vt_ref[hd * vrows:hd * vrows + LANES, :] = dv[:, hd * LANES:(hd + 1) * LANES].T.astype(BF16)
        vt_ref[hd * vrows + LANES:(hd + 1) * vrows, :] = jnp.ones((ATTN_ONES, dv.shape[0]), BF16)


def _in_projection(xs, mod, g_pre, w_r, tables, n_batch, seq):
    tm = TOKEN_TILE
    if isinstance(xs, tuple):
        d = xs[0].shape[1]
        n = xs[0].shape[0] + xs[1].shape[0]
        n_a, x_specs = _split_rows(*xs, tm, d)
    else:
        n, d = xs.shape
        n_a, x_specs, xs = 0, [pl.BlockSpec((tm, d), lambda i: (i, 0))], (xs,)
    tiles_per_batch = seq // tm
    n_lat_tiles = n_batch * tiles_per_batch
    c32, s32, c64, s64 = tables

    def tab_map(i):
        return (jnp.where(i < n_lat_tiles, i % tiles_per_batch, tiles_per_batch), 0)

    row = lambda i: (i, 0)
    const = lambda i: (0, 0)
    tab_spec = pl.BlockSpec((tm, LANES), tab_map)
    return pl.pallas_call(
        functools.partial(_inproj_kernel, tiles_per_batch, n_batch, n_a),
        out_shape=(
            jax.ShapeDtypeStruct((n, 768), BF16), jax.ShapeDtypeStruct((n, 768), BF16),
            jax.ShapeDtypeStruct((n, LANES), F32),
            jax.ShapeDtypeStruct((N_HEADS * LANES, n), BF16), jax.ShapeDtypeStruct((n, 512), BF16),
            jax.ShapeDtypeStruct((N_HEADS * (LANES + ATTN_ONES), n), BF16)),
        grid=(n // tm,),
        in_specs=x_specs + [
            pl.BlockSpec(mod.shape, const),
            pl.BlockSpec((1, d), const),
            pl.BlockSpec(w_r.shape, const),
            tab_spec, tab_spec, tab_spec, tab_spec,
        ],
        out_specs=(
            pl.BlockSpec((tm, 768), row), pl.BlockSpec((tm, 768), row), pl.BlockSpec((tm, LANES), row),
            pl.BlockSpec((N_HEADS * LANES, tm), lambda i: (0, i)), pl.BlockSpec((tm, 512), row),
            pl.BlockSpec((N_HEADS * (LANES + ATTN_ONES), tm), lambda i: (0, i))),
        compiler_params=_cparams("parallel"),
        name="in_projection",
    )(*xs, mod, g_pre, w_r, c32, s32, c64, s64)


def _rope_tables(seq, head_dim, extra_rows):
    half, quarter = head_dim // 2, head_dim // 4
    freqs = (ROPE_BASE ** (-np.arange(quarter, dtype=np.float32) / quarter)).astype(np.float32)
    t = np.arange(seq)
    row = (t // GRID_W).astype(np.float32)
    col = (t % GRID_W).astype(np.float32)
    j = np.arange(LANES) % head_dim
    jj = j % half
    pos = np.where((j < half)[None, :], row[:, None], col[:, None])
    ang = (pos * freqs[jj % quarter][None, :]).astype(np.float32)
    cos = np.cos(ang)
    sin = np.sin(ang) * np.where(jj < quarter, -1.0, 1.0)[None, :]
    cos = np.concatenate([cos, np.ones((extra_rows, LANES))], axis=0).astype(np.float32)
    sin = np.concatenate([sin, np.zeros((extra_rows, LANES))], axis=0).astype(np.float32)
    return jnp.asarray(cos), jnp.asarray(sin)


def _head_stack(x, width):
    lane = lax.broadcasted_iota(I32, (1, x.shape[-1]), 1)
    zero = jnp.zeros_like(x)
    return jnp.concatenate([jnp.where(lane // width == h, x, zero) for h in range(N_HEADS)], axis=0)


def _head_select(x4, rows):
    lane = lax.broadcasted_iota(I32, (1, x4.shape[-1]), 1)
    out = x4[(N_HEADS - 1) * rows:N_HEADS * rows]
    for h in range(N_HEADS - 2, -1, -1):
        out = jnp.where(lane < (h + 1) * RET_DV, x4[h * rows:(h + 1) * rows], out)
    return out


def _scan_tables_kernel(rl_lane_ref, rl_rows_ref, rl_col_ref, dmat_ref, xi_ref, zeta_ref, gchunk_ref):
    c = CHUNK
    idx = lax.broadcasted_iota(I32, (c, 1), 0).astype(F32)
    ri4 = lax.broadcasted_iota(I32, (N_HEADS * c, c), 0) % c
    ci4 = lax.broadcasted_iota(I32, (N_HEADS * c, c), 1)
    dist = jnp.abs(ri4 - ci4).astype(F32)
    for d in range(2):
        lg_lane = _log_sigmoid(rl_lane_ref[d])
        lg_rows = _log_sigmoid(rl_rows_ref[d])
        lg_col = _log_sigmoid(rl_col_ref[d])
        att4 = (ri4 <= ci4) if d else (ri4 >= ci4)
        dmat_ref[d] = jnp.where(att4, jnp.exp(dist * lg_rows), 0.0)
        xi_ref[d] = jnp.exp(((c - idx) if d else (idx + 1.0)) * lg_lane)
        zeta_ref[d] = jnp.exp((idx if d else (c - 1.0 - idx)) * lg_lane)
        g_chunk = jnp.exp(float(c) * lg_col)
        gchunk_ref[d] = jnp.concatenate([g_chunk, g_chunk], axis=1)


def _scan_tables(ret_logit):
    c = CHUNK
    rl_lane = jnp.repeat(ret_logit, RET_DK, axis=1)[:, None, :]
    rl_rows = jnp.broadcast_to(jnp.repeat(ret_logit, c, axis=1)[:, :, None], (2, N_HEADS * c, c))
    rl_col = jnp.broadcast_to(jnp.repeat(ret_logit, RET_DK, axis=1)[:, :, None], (2, LANES, LANES))
    return pl.pallas_call(
        _scan_tables_kernel,
        out_shape=(jax.ShapeDtypeStruct((2, N_HEADS * c, c), F32), jax.ShapeDtypeStruct((2, c, LANES), F32),
                   jax.ShapeDtypeStruct((2, c, LANES), F32), jax.ShapeDtypeStruct((2, LANES, 2 * LANES), F32)),
        name="scan_tables",
    )(rl_lane, rl_rows, rl_col)


def _state_block_mask():
    return (lax.broadcasted_iota(I32, (LANES, 2 * LANES), 0) // RET_DK
            == lax.broadcasted_iota(I32, (LANES, 2 * LANES), 1) // RET_DV)


def _ret_chain(ret_ref, r0, dmat, xi, zeta, g_chunk, out):
    c = CHUNK
    q = ret_ref[r0:r0 + c, 0:128]
    k = ret_ref[r0:r0 + c, 128:256]
    v = ret_ref[r0:r0 + c, 256:512]
    s = lax.dot_general(_head_stack(q, RET_DK), k, NT_DIMS, preferred_element_type=F32)
    kz = (k.astype(F32) * zeta).astype(BF16)
    u = lax.dot_general(kz, v, TN_DIMS, preferred_element_type=F32)
    yield
    o4 = jnp.dot((s * dmat).astype(BF16), v, preferred_element_type=F32)
    out.update(qx=(q.astype(F32) * xi).astype(BF16), u=jnp.where(_state_block_mask(), u, 0.0), g=g_chunk)
    yield
    out.update(intra=_head_select(o4, c))


def _state_chain(parts, o_ref, s_ref):
    c = CHUNK
    s = s_ref[...]
    for r0, p in parts:
        o_ref[r0:r0 + c, :] = p["intra"] + jnp.dot(p["qx"], s.astype(BF16), preferred_element_type=F32)
        s = s * p["g"] + p["u"]
        yield
    s_ref[...] = s


def _gla_chain(rev, gla_ref, glr_ref, r0, gw, gb, out):
    c = CHUNK
    ri = lax.broadcasted_iota(I32, (c, c), 0)
    ci = lax.broadcasted_iota(I32, (c, c), 1)
    attends = (ri <= ci) if rev else (ri >= ci)
    bd = _state_block_mask()
    gq = gla_ref[r0:r0 + c, 0:128].astype(F32)
    gk = gla_ref[r0:r0 + c, 128:256].astype(F32)
    gv = gla_ref[r0:r0 + c, 256:512]
    z = jnp.dot(glr_ref[r0:r0 + c, :], gw, precision=HIGHEST, preferred_element_type=F32) + gb
    yield
    la = _log_sigmoid(z) * (math.log2(math.e) / GLA_TAU)
    if rev:
        first = (ri // GLA_SUB) * GLA_SUB + (GLA_SUB - 1)
        ref_sel = ci >= first
    else:
        first = (ri // GLA_SUB) * GLA_SUB
        ref_sel = ci <= first
    sel = jnp.concatenate([jnp.where(attends, 1.0, 0.0), jnp.where(ref_sel, 1.0, 0.0)], axis=0).astype(BF16)
    la_hi = la.astype(BF16)
    la_lo = (la - la_hi.astype(F32)).astype(BF16)
    sums = (jnp.dot(sel, la_hi, preferred_element_type=F32) + jnp.dot(sel, la_lo, preferred_element_type=F32))
    b, refrow = sums[0:c], sums[c:2 * c]
    yield
    qs = gq * jnp.exp2(b - refrow)
    b_last = b[0:1] if rev else b[c - 1:c]
    kz = (gk * jnp.exp2(b_last - b)).astype(BF16)
    u = lax.dot_general(kz, gv, TN_DIMS, preferred_element_type=F32)
    eye = lax.broadcasted_iota(I32, (LANES, LANES), 0) == lax.broadcasted_iota(I32, (LANES, LANES), 1)
    g_col = jnp.sum(jnp.where(eye, jnp.exp2(b_last), 0.0), axis=1, keepdims=True)
    jcol = lax.broadcasted_iota(I32, (c, 1), 0)
    rr = lax.broadcasted_iota(I32, (N_HEADS * GLA_SUB, c), 0) % GLA_SUB
    cc = lax.broadcasted_iota(I32, (N_HEADS * GLA_SUB, c), 1)
    pieces = []
    for blk in range(c // GLA_SUB):
        lo = blk * GLA_SUB
        ref_b = refrow[lo:lo + 1]
        seen = (jcol >= lo) if rev else (jcol < lo + GLA_SUB)
        ks = (gk * jnp.exp2(jnp.where(seen, ref_b - b, -jnp.inf))).astype(BF16)
        qz = _head_stack(qs[lo:lo + GLA_SUB], GLA_DK).astype(BF16)
        att = lax.dot_general(qz, ks, NT_DIMS, preferred_element_type=F32)
        ok = (cc >= rr + lo) if rev else (cc <= rr + lo)
        att = jnp.where(ok, att, 0.0).astype(BF16)
        pieces.append(_head_select(jnp.dot(att, gv, preferred_element_type=F32), GLA_SUB))
    out.update(intra=jnp.concatenate(pieces, axis=0), qx=(gq * jnp.exp2(b)).astype(BF16),
               u=jnp.where(bd, u, 0.0), g=g_col)


def _round_robin(chains):
    while chains:
        chains = [ch for ch in chains if next(ch, True) is None]


def _scan_kernel(ret_f, gla_f, glr_f, ret_b, gla_b, glr_b, dmat_ref, xi_ref, zeta_ref, gchunk_ref, gw_ref, gb_ref,
                 orf_ref, ogf_ref, orb_ref, ogb_ref, sr_ref, sg_ref):
    @pl.when(pl.program_id(1) == 0)
    def _():
        sr_ref[...] = jnp.zeros_like(sr_ref)
        sg_ref[...] = jnp.zeros_like(sg_ref)

    chains, state_chains = [], []
    for d, (ret_ref, gla_ref, glr_ref, ore_ref, ogl_ref) in enumerate(
            ((ret_f, gla_f, glr_f, orf_ref, ogf_ref), (ret_b, gla_b, glr_b, orb_ref, ogb_ref))):
        order = range(SCAN_CHUNKS - 1, -1, -1) if d else range(SCAN_CHUNKS)
        gla_parts, ret_parts = [], []
        for j in order:
            r0 = j * CHUNK
            gla_parts.append((r0, {}))
            ret_parts.append((r0, {}))
            chains.append(_gla_chain(bool(d), gla_ref, glr_ref, r0, gw_ref[d], gb_ref[d], gla_parts[-1][1]))
            chains.append(_ret_chain(ret_ref, r0, dmat_ref[d], xi_ref[d], zeta_ref[d], gchunk_ref[d],
                                     ret_parts[-1][1]))
        state_chains.append(_state_chain(gla_parts, ogl_ref, sg_ref.at[d]))
        state_chains.append(_state_chain(ret_parts, ore_ref, sr_ref.at[d]))
    _round_robin(chains)
    _round_robin(state_chains)


def _scan(ret, gla, glr, tables, gla_w, gla_b, n_batch, seq, ctx_len):
    n = ret.shape[0]
    c = CHUNK * SCAN_CHUNKS
    nc_ctx, nc_lat = ctx_len // c, seq // c
    n_steps = nc_ctx + nc_lat
    ctx_base = n_batch * nc_lat

    def fwd(b, s):
        return (jnp.where(s < nc_ctx, ctx_base + b * nc_ctx + s, b * nc_lat + (s - nc_ctx)), 0)

    def bwd(b, s):
        return (jnp.where(s < nc_ctx, ctx_base + b * nc_ctx + (nc_ctx - 1 - s), b * nc_lat + (n_steps - 1 - s)), 0)

    const3 = lambda b, s: (0, 0, 0)
    gw = jnp.zeros((2, LANES, LANES), F32).at[:, :GLA_LOWRANK].set(gla_w)
    gb = gla_b[:, None, :]
    chain_in = lambda m: [pl.BlockSpec((c, 768), m), pl.BlockSpec((c, 768), m), pl.BlockSpec((c, LANES), m)]
    o_sds = jax.ShapeDtypeStruct((n, 256), F32)
    return pl.pallas_call(
        _scan_kernel,
        out_shape=(o_sds, o_sds, o_sds, o_sds),
        grid=(n_batch, n_steps),
        in_specs=chain_in(fwd) + chain_in(bwd) + [pl.BlockSpec(t.shape, const3) for t in tables]
        + [pl.BlockSpec(gw.shape, const3), pl.BlockSpec(gb.shape, const3)],
        out_specs=(pl.BlockSpec((c, 256), fwd), pl.BlockSpec((c, 256), fwd),
                   pl.BlockSpec((c, 256), bwd), pl.BlockSpec((c, 256), bwd)),
        scratch_shapes=[pltpu.VMEM((2, LANES, 2 * LANES), F32), pltpu.VMEM((2, LANES, 2 * LANES), F32)],
        compiler_params=_cparams("arbitrary", "arbitrary"),
        name="scan",
    )(ret, gla, glr, ret, gla, glr, *tables, gw, gb)


def _attn_kernel(n_lat_blocks, lam_init, *refs):
    if n_lat_blocks:
        q_ref, kc_ref, vtc_ref, kl_ref, vtl_ref, lp_ref, g_ref, o_ref, m_ref, acc_ref, st_ref = refs
    else:
        q_ref, kc_ref, vtc_ref, lp_ref, g_ref, o_ref, m_ref, acc_ref, st_ref = refs
    tq = q_ref.shape[1]
    q = q_ref[...]
    dim = lax.broadcasted_iota(I32, (LANES, 1), 0)
    zero = jnp.zeros_like(q)
    qt = jnp.concatenate([jnp.where(dim < DIFF_DH, q, zero), jnp.where(dim >= DIFF_DH, q, zero)], axis=1)
    m_ref[...] = jnp.full(m_ref.shape, -jnp.inf, F32)
    acc_ref[...] = jnp.zeros_like(acc_ref)

    def scores(slot, kb):
        st_ref[slot, 0:kb.shape[0], :] = jnp.dot(kb, qt, preferred_element_type=F32)

    def absorb(slot, vtb):
        st = st_ref[slot, 0:vtb.shape[1], :]
        m_prev = m_ref[...]
        m_new = jnp.maximum(m_prev, jnp.max(st, axis=0, keepdims=True))
        alpha = jnp.exp2(m_prev - m_new)
        p = jnp.exp2(st - m_new).astype(BF16)
        acc_ref[...] = alpha * acc_ref[...] + jnp.dot(vtb, p, preferred_element_type=F32)
        m_ref[...] = m_new

    def k_lat(blk):
        return kl_ref[pl.ds(pl.multiple_of(blk * ATTN_TK, ATTN_TK), ATTN_TK), :]

    def vt_lat(blk):
        return vtl_ref[:, pl.ds(pl.multiple_of(blk * ATTN_TK, ATTN_TK), ATTN_TK)]

    scores(0, kc_ref[...])
    if not n_lat_blocks:
        absorb(0, vtc_ref[...])
    else:
        scores(1, k_lat(0))
        absorb(0, vtc_ref[...])
        n_pairs = (n_lat_blocks - 1) // 2

        def body(i, carry):
            scores(0, k_lat(2 * i + 1))
            absorb(1, vt_lat(2 * i))
            scores(1, k_lat(2 * i + 2))
            absorb(0, vt_lat(2 * i + 1))
            return carry
        lax.fori_loop(0, n_pairs, body, 0)
        done = 2 * n_pairs
        if n_lat_blocks - done == 2:
            scores(0, k_lat(done + 1))
            absorb(1, vt_lat(done))
            absorb(0, vt_lat(done + 1))
        else:
            absorb(1, vt_lat(done))

    lp = lp_ref[...]
    lam = (jnp.exp(jnp.sum(lp[0:1] * lp[1:2], axis=1, keepdims=True))
           - jnp.exp(jnp.sum(lp[2:3] * lp[3:4], axis=1, keepdims=True)) + lam_init)
    acc = acc_ref[...]
    o1 = acc[0:LANES, :tq] / acc[LANES:LANES + 1, :tq]
    o2 = acc[0:LANES, tq:] / acc[LANES:LANES + 1, tq:]
    ot = o1 - lam * o2
    ot = ot * lax.rsqrt(jnp.mean(ot * ot, axis=0, keepdims=True) + EPS) * g_ref[...][:, 0:1] * (1.0 - lam_init)
    o_ref[...] = ot.T.astype(o_ref.dtype)


def _diff_attention(dqt, dk, vt, lp, g, lam_init, n_batch, seq, ctx_len, latent):
    ctx_blk0 = (n_batch * seq) // ctx_len
    vrows = LANES + ATTN_ONES
    kc_spec = pl.BlockSpec((ctx_len, LANES), lambda b, h, i: (ctx_blk0 + b, h))
    vtc_spec = pl.BlockSpec((vrows, ctx_len), lambda b, h, i: (h, ctx_blk0 + b))
    const = lambda b, h, i: (0, 0)
    if latent:
        tq = ATTN_TQ
        n_q = seq // tq
        o_map = lambda b, h, i: (b * n_q + i, h)
        in_specs = [pl.BlockSpec((LANES, tq), lambda b, h, i: (h, b * n_q + i)), kc_spec, vtc_spec,
                    pl.BlockSpec((seq, LANES), lambda b, h, i: (b, h)),
                    pl.BlockSpec((vrows, seq), lambda b, h, i: (h, b))]
        args = (dqt, dk, vt, dk, vt)
        n_rows, n_lat_blocks = n_batch * seq, seq // ATTN_TK
    else:
        tq, n_q = ctx_len, 1
        in_specs = [pl.BlockSpec((LANES, ctx_len), lambda b, h, i: (h, ctx_blk0 + b)), kc_spec, vtc_spec]
        args = (dqt, dk, vt)
        n_rows, o_map, n_lat_blocks = n_batch * ctx_len, (lambda b, h, i: (b, h)), 0
    in_specs += [pl.BlockSpec(lp.shape, const), pl.BlockSpec((LANES, LANES), const)]
    return pl.pallas_call(
        functools.partial(_attn_kernel, n_lat_blocks, lam_init),
        out_shape=jax.ShapeDtypeStruct((n_rows, N_HEADS * LANES), BF16),
        grid=(n_batch, N_HEADS, n_q),
        in_specs=in_specs,
        out_specs=pl.BlockSpec((tq, LANES), o_map),
        scratch_shapes=[pltpu.VMEM((1, 2 * tq), F32), pltpu.VMEM((vrows, 2 * tq), F32),
                        pltpu.VMEM((2, max(ATTN_TK, ctx_len), 2 * tq), F32)],
        compiler_params=_cparams("parallel", "parallel", "parallel"),
        name="diff_attention_lat" if latent else "diff_attention_ctx",
    )(*args, lp, g)


def _merge_kernel(tiles_per_batch, n_batch, n_a, orf_ref, orb_ref, ogf_ref, ogb_ref, rg_ref, gr_ref, *refs):
    n_x = 2 if n_a else 1
    mod_ref, gpost_ref, gpre_ref, glag_ref, wout_ref, wrt_ref, xo_ref, hp_ref, sc_ref = refs[2 * n_x:]
    dif = _pick_rows(n_a, *refs[:2]) if n_a else refs[0][...]
    x = _pick_rows(n_a, *refs[2:4]) if n_a else refs[1][...]
    d = x.shape[-1]
    r = jnp.minimum(pl.program_id(0) // tiles_per_batch, n_batch)
    gate1 = mod_ref[pl.ds(r, 1), 2 * d:3 * d]
    shift2 = mod_ref[pl.ds(r, 1), 3 * d:4 * d]
    scale2 = mod_ref[pl.ds(r, 1), 4 * d:5 * d]
    gi = lax.broadcasted_iota(I32, (256, 256), 0) // RET_DV
    gj = lax.broadcasted_iota(I32, (256, 256), 1) // RET_DV
    group_mean = jnp.where(gi == gj, 1.0 / RET_DV, 0.0).astype(BF16)

    def head_norm(o):
        ms = jnp.dot((o * o).astype(BF16), group_mean, preferred_element_type=F32)
        return o * lax.rsqrt(ms + EPS)

    ret = head_norm(orf_ref[...] + orb_ref[...]) * _silu(rg_ref[...].astype(F32))
    gla = head_norm(ogf_ref[...] + ogb_ref[...]) * glag_ref[...] * _silu(gr_ref[...].astype(F32))
    m = (jnp.dot(ret.astype(BF16), wout_ref[0:256, :], preferred_element_type=F32)
         + jnp.dot(gla.astype(BF16), wout_ref[256:512, :], preferred_element_type=F32)
         + jnp.dot(dif, wout_ref[512:1024, :], preferred_element_type=F32))
    x_new = x + gate1 * (_rms(m) * gpost_ref[...])
    xo_ref[...] = x_new
    h2 = _rms(x_new) * gpre_ref[...] * (1.0 + scale2) + shift2
    _tile_major_store(hp_ref, _pack_halves(h2[:, :d // 2], h2[:, d // 2:]))
    def split(a):
        hi = a.astype(BF16)
        return hi, (a - hi.astype(F32)).astype(BF16)

    h_hi, h_lo = split(h2)
    w_hi, w_lo = split(wrt_ref[...])
    nt = functools.partial(lax.dot_general, dimension_numbers=NT_DIMS, preferred_element_type=F32)
    logits = nt(w_hi, h_hi) + (nt(w_hi, h_lo) + nt(w_lo, h_hi))
    sc_ref[...] = 1.0 / (1.0 + jnp.exp(-logits))


def _merge(o_rf, o_rb, o_gf, o_gb, ret, gla, dif, xs, mod, g_post, g_pre_ffn, gla_g, w_out, w_rt,
           n_rows, n_batch, seq):
    tm = TOKEN_TILE
    row = lambda i: (i, 0)
    gate_col = lambda i: (i, 2)
    const = lambda i: (0, 0)
    if isinstance(xs, tuple):
        d = xs[0].shape[1]
        n_a, x_specs = _split_rows(*xs, tm, d)
        _, dif_specs = _split_rows(*dif, tm, 512)
    else:
        d = xs.shape[1]
        n_a, x_specs, dif_specs = 0, [pl.BlockSpec((tm, d), row)], [pl.BlockSpec((tm, 512), row)]
        xs, dif = (xs,), (dif,)
    return pl.pallas_call(
        functools.partial(_merge_kernel, seq // tm, n_batch, n_a),
        out_shape=(jax.ShapeDtypeStruct((n_rows, d), F32), jax.ShapeDtypeStruct((n_rows * HP_CHUNKS, LANES), U32),
                   jax.ShapeDtypeStruct((N_EXPERTS, n_rows), F32)),
        grid=(n_rows // tm,),
        in_specs=[
            pl.BlockSpec((tm, 256), row), pl.BlockSpec((tm, 256), row),
            pl.BlockSpec((tm, 256), row), pl.BlockSpec((tm, 256), row),
            pl.BlockSpec((tm, 256), gate_col), pl.BlockSpec((tm, 256), gate_col), *dif_specs, *x_specs,
            pl.BlockSpec(mod.shape, const), pl.BlockSpec((1, d), const), pl.BlockSpec((1, d), const),
            pl.BlockSpec((1, 256), const), pl.BlockSpec(w_out.shape, const), pl.BlockSpec(w_rt.shape, const),
        ],
        out_specs=(pl.BlockSpec((tm, d), row), pl.BlockSpec((tm * HP_CHUNKS, LANES), row),
                   pl.BlockSpec((N_EXPERTS, tm), lambda i: (0, i))),
        compiler_params=_cparams("parallel"),
        name="merge",
    )(o_rf, o_rb, o_gf, o_gb, ret, gla, *dif, *xs, mod, g_post, g_pre_ffn, gla_g, w_out, w_rt)


def _route_kernel(sc_ref, bias_ref, posa_ref, wts_ref, pos_ref, meta_ref):
    tt = sc_ref.shape[-1]
    scores = sc_ref[...]
    sel = scores + bias_ref[...][:, 0:1]
    sub = lax.broadcasted_iota(I32, (GROUP_SIZE, tt), 0)
    neg = -jnp.inf
    gscore = []
    for g in range(N_GROUPS):
        xg = sel[g * GROUP_SIZE:(g + 1) * GROUP_SIZE]
        m1 = jnp.max(xg, axis=0, keepdims=True)
        i1 = jnp.min(jnp.where(xg == m1, sub, GROUP_SIZE), axis=0, keepdims=True)
        m2 = jnp.max(jnp.where(sub == i1, neg, xg), axis=0, keepdims=True)
        gscore.append(m1 + m2)
    rows = []
    for g in range(N_GROUPS):
        rank = jnp.zeros((1, tt), I32)
        for o in range(N_GROUPS):
            if o == g:
                continue
            ahead = (gscore[o] >= gscore[g]) if o < g else (gscore[o] > gscore[g])
            rank = rank + ahead.astype(I32)
        rows.append(jnp.where(rank < TOPK_GROUPS, sel[g * GROUP_SIZE:(g + 1) * GROUP_SIZE], neg))
    masked = jnp.concatenate(rows, axis=0)
    eio = lax.broadcasted_iota(I32, (N_EXPERTS, tt), 0)
    member = jnp.zeros((N_EXPERTS, tt), F32)
    idxs, ws = [], []
    for _ in range(TOP_K):
        m = jnp.max(masked, axis=0, keepdims=True)
        i = jnp.min(jnp.where(masked == m, eio, N_EXPERTS), axis=0, keepdims=True)
        hit = eio == i
        idxs.append(i)
        ws.append(jnp.sum(jnp.where(hit, scores, 0.0), axis=0, keepdims=True))
        member = jnp.where(hit, 1.0, member)
        masked = jnp.where(hit, neg, masked)
    wsum = ws[0]
    for w in ws[1:]:
        wsum = wsum + w
    ti = lax.broadcasted_iota(I32, (tt, tt), 0)
    tj = lax.broadcasted_iota(I32, (tt, tt), 1)
    before = jnp.where(ti < tj, 1.0, 0.0).astype(BF16)
    rank_in_e = jnp.dot(member.astype(BF16), before, preferred_element_type=F32)
    cnt = jnp.sum(member, axis=1, keepdims=True)
    padded = jnp.floor((cnt + (SUBLANES - 1)) * (1.0 / SUBLANES)) * SUBLANES
    ei = lax.broadcasted_iota(I32, (N_EXPERTS, N_EXPERTS), 0)
    ej = lax.broadcasted_iota(I32, (N_EXPERTS, N_EXPERTS), 1)
    lower = jnp.where(ej < ei, 1.0, 0.0)
    off = jnp.dot(lower, jnp.broadcast_to(padded, (N_EXPERTS, LANES)), precision=HIGHEST,
                  preferred_element_type=F32)
    slot = rank_in_e + off[:, 0:1]
    zrow_i = jnp.zeros((SUBLANES - TOP_K, tt), I32)
    zrow_f = jnp.zeros((SUBLANES - TOP_K, tt), F32)
    pos = [jnp.sum(jnp.where(eio == i, slot, 0.0), axis=0, keepdims=True).astype(I32) for i in idxs]
    pos = jnp.concatenate(pos + [zrow_i], axis=0)
    wts_ref[...] = jnp.concatenate([w / wsum * ROUTED_SCALE for w in ws] + [zrow_f], axis=0)
    pos_ref[...] = pos
    posa_ref[...] = _tile_major_addr(pos, HP_CHUNKS)
    meta_ref[0] = jnp.concatenate([jnp.broadcast_to(cnt, (N_EXPERTS, LANES)), off], axis=1).astype(I32)


def _route(scores_t, bias, tt, row0, n):
    tile0 = row0 // tt
    tok = lambda i: (0, i)
    return pl.pallas_call(
        _route_kernel,
        out_shape=(jax.ShapeDtypeStruct((SUBLANES, n), I32), jax.ShapeDtypeStruct((SUBLANES, n), F32),
                   jax.ShapeDtypeStruct((SUBLANES, n), I32),
                   jax.ShapeDtypeStruct((n // tt, N_EXPERTS, 2 * LANES), I32)),
        grid=(n // tt,),
        in_specs=[pl.BlockSpec((N_EXPERTS, tt), lambda i: (0, i + tile0)),
                  pl.BlockSpec((N_EXPERTS, LANES), lambda i: (0, 0))],
        out_specs=(pl.BlockSpec((SUBLANES, tt), tok), pl.BlockSpec((SUBLANES, tt), tok),
                   pl.BlockSpec((SUBLANES, tt), tok),
                   pl.BlockSpec((1, N_EXPERTS, 2 * LANES), lambda i: (i, 0, 0))),
        compiler_params=_cparams("parallel"),
        name="route",
    )(scores_t, jnp.broadcast_to(bias[:, None], (N_EXPERTS, LANES)))


def _moe_kernel(group, block, pos_ref, posa_ref, wts_ref, meta_ref, hp_ref, weg_ref, weu_ref, wed_ref,
                o_ref, xg_ref, y_ref, rt_ref):
    d = weg_ref.shape[1]
    half = d // 2
    tt = hp_ref.shape[0] // HP_CHUNKS
    eg = pl.program_id(1)

    @pl.when((pl.program_id(0) == 0) & (eg == 0))
    def _():
        def zero(i, carry):
            for u in range(SUBLANES):
                rt_ref[i * SUBLANES + u] = 0
            return carry
        lax.fori_loop(0, rt_ref.shape[0] // SUBLANES, zero, 0)

    @pl.when(eg == 0)
    def _():
        def scatter(g, carry):
            base = g * (SUBLANES * HP_CHUNKS)
            for u in range(SUBLANES):
                for k in range(TOP_K):
                    rt_ref[pos_ref[(g * SUBLANES + u) * SUBLANES + k]] = base + u
            return carry
        lax.fori_loop(0, tt // SUBLANES, scatter, 0)

    sub = lax.broadcasted_iota(I32, (SUBLANES, LANES), 0)

    def expert_block(ge, buf, base, n_valid):
        base = pl.multiple_of(base, SUBLANES)
        for i in range(block):
            xg_ref[buf, pl.ds(_tile_major_addr(i, HP_CHUNKS), HP_CHUNKS, stride=SUBLANES), :] = (
                hp_ref[pl.ds(rt_ref[base + i], HP_CHUNKS, stride=SUBLANES), :])
        yield
        halves = [_unpack_halves(col) for col in _tile_major_columns(xg_ref, block, HP_CHUNKS, (buf,))]
        xa = jnp.concatenate([h[0] for h in halves], axis=1).astype(BF16)
        xb = jnp.concatenate([h[1] for h in halves], axis=1).astype(BF16)
        hg = (jnp.dot(xa, weg_ref[ge, 0:half, :], preferred_element_type=F32)
              + jnp.dot(xb, weg_ref[ge, half:d, :], preferred_element_type=F32))
        hu = (jnp.dot(xa, weu_ref[ge, 0:half, :], preferred_element_type=F32)
              + jnp.dot(xb, weu_ref[ge, half:d, :], preferred_element_type=F32))
        yield
        y = jnp.dot((_silu(hg) * hu).astype(BF16), wed_ref[ge], preferred_element_type=F32)
        packed = _pack_halves(y[:, :half], y[:, half:])
        row0 = pl.multiple_of(base * HP_CHUNKS, SUBLANES * HP_CHUNKS)
        for g in range(block // SUBLANES):
            keep = sub < n_valid - g * SUBLANES
            for c in range(HP_CHUNKS):
                pltpu.store(y_ref.at[pl.ds(row0 + (g * HP_CHUNKS + c) * SUBLANES, SUBLANES), :],
                            packed[g * SUBLANES:(g + 1) * SUBLANES, c * LANES:(c + 1) * LANES], mask=keep)

    cnts = [meta_ref[0, eg * group + ge] for ge in range(group)]
    offs = [meta_ref[1, eg * group + ge] for ge in range(group)]
    _round_robin([expert_block(ge, ge, offs[ge], cnts[ge]) for ge in range(group)])
    for ge in range(group):
        def more(j, carry, ge=ge):
            for _ in expert_block(ge, ge, offs[ge] + j * block, cnts[ge] - j * block):
                pass
            return carry
        lax.fori_loop(1, (cnts[ge] + (block - 1)) // block, more, 0)

    @pl.when(eg == N_EXPERTS // group - 1)
    def _():
        def tokens(g, carry):
            for u in range(SUBLANES):
                e0 = (g * SUBLANES + u) * SUBLANES
                acc_a = jnp.zeros((HP_CHUNKS, LANES), F32)
                acc_b = jnp.zeros((HP_CHUNKS, LANES), F32)
                for k in range(TOP_K):
                    ya, yb = _unpack_halves(y_ref[pl.ds(posa_ref[e0 + k], HP_CHUNKS, stride=SUBLANES), :])
                    w = wts_ref[e0 + k]
                    acc_a = acc_a + w * ya
                    acc_b = acc_b + w * yb
                out0 = pl.multiple_of(g * (2 * HP_CHUNKS * SUBLANES), 2 * HP_CHUNKS * SUBLANES) + u
                o_ref[pl.ds(out0, HP_CHUNKS, stride=SUBLANES), :] = acc_a
                o_ref[pl.ds(out0 + HP_CHUNKS * SUBLANES, HP_CHUNKS, stride=SUBLANES), :] = acc_b
            return carry

        lax.fori_loop(0, tt // SUBLANES, tokens, 0)


def _moe_segment(scores_t, hp, router_bias, expert_w, layer, row0, n, tt):
    w_eg, w_eu, w_ed = expert_w
    d, de = w_eg.shape[1:]
    mean = tt * TOP_K / N_EXPERTS
    block = int(-(-(mean + MOE_BLOCK_SIGMAS * math.sqrt(mean * (1.0 - 1.0 / N_EXPERTS))) // 16) * 16)
    group = 1
    while group * 2 <= min(MOE_MAX_GROUP, MOE_ROWS_IN_FLIGHT // block) and N_EXPERTS % (group * 2) == 0:
        group *= 2
    group0 = layer * (N_EXPERTS // group)
    n_tiles, tile0 = n // tt, row0 // tt
    n_slots = -(-(tt * TOP_K + N_EXPERTS * (SUBLANES - 1) + block) // (SUBLANES * LANES)) * SUBLANES * LANES
    posa, wts, pos, meta = _route(scores_t, router_bias, tt, row0, n)
    cnt, off = meta[:, :, 0], meta[:, :, LANES]

    def per_token(a):
        return a.T.reshape(-1)

    meta_s = jnp.zeros((n_tiles, SUBLANES, LANES), I32)
    meta_s = meta_s.at[:, 0, :N_EXPERTS].set(cnt).at[:, 1, :N_EXPERTS].set(off).reshape(-1, LANES)

    tile = lambda i, e: (i + tile0, 0)
    expert = lambda i, e: (group0 + e, 0, 0)
    smem = functools.partial(pl.BlockSpec, memory_space=pltpu.SMEM)
    once = pl.Buffered(1)
    flat = smem((SUBLANES * tt,), lambda i, e: (i,))
    in_specs = [
        flat, flat, flat, smem((SUBLANES, LANES), lambda i, e: (i, 0)),
        pl.BlockSpec((tt * HP_CHUNKS, LANES), tile),
        pl.BlockSpec((group, d, de), expert), pl.BlockSpec((group, d, de), expert),
        pl.BlockSpec((group, de, d), expert),
    ]
    args = [per_token(pos), per_token(posa), per_token(wts), meta_s, hp, w_eg, w_eu, w_ed]
    return pl.pallas_call(
        functools.partial(_moe_kernel, group, block),
        out_shape=jax.ShapeDtypeStruct((n * 2 * HP_CHUNKS, LANES), F32),
        grid=(n_tiles, N_EXPERTS // group),
        in_specs=in_specs,
        out_specs=pl.BlockSpec((tt * 2 * HP_CHUNKS, LANES), lambda i, e: (i, 0), pipeline_mode=once),
        scratch_shapes=[pltpu.VMEM((group, block * HP_CHUNKS, LANES), U32),
                        pltpu.VMEM((n_slots * HP_CHUNKS, LANES), U32),
                        pltpu.SMEM((n_slots,), I32)],
        compiler_params=_cparams("arbitrary", "arbitrary"),
        name="moe",
    )(*args)


def _moe_final_kernel(tiles_per_batch, n_batch, n_a, *refs):
    hp_ref, x_ref, mod_ref, gpost_ref, wsg_ref, wsu_ref, wsd_ref, o_ref = refs[2 if n_a else 1:]
    routed_tm = _pick_rows(n_a, *refs[:2]) if n_a else refs[0][...]
    tm, d = x_ref.shape
    half = d // 2
    r = jnp.minimum(pl.program_id(0) // tiles_per_batch, n_batch)
    gate2 = mod_ref[pl.ds(r, 1), 5 * d:6 * d]
    halves = [_unpack_halves(col) for col in _tile_major_columns(hp_ref, tm, HP_CHUNKS)]
    xa = jnp.concatenate([h[0] for h in halves], axis=1).astype(BF16)
    xb = jnp.concatenate([h[1] for h in halves], axis=1).astype(BF16)
    hg = (jnp.dot(xa, wsg_ref[0:half, :], preferred_element_type=F32)
          + jnp.dot(xb, wsg_ref[half:d, :], preferred_element_type=F32))
    hu = (jnp.dot(xa, wsu_ref[0:half, :], preferred_element_type=F32)
          + jnp.dot(xb, wsu_ref[half:d, :], preferred_element_type=F32))
    routed = jnp.concatenate(_tile_major_columns(routed_tm, tm, 2 * HP_CHUNKS), axis=1)
    f = jnp.dot((_silu(hg) * hu).astype(BF16), wsd_ref[...], preferred_element_type=F32) + routed
    o_ref[...] = x_ref[...] + gate2 * (_rms(f) * gpost_ref[...])


def _moe_final(routed, hp, xs, mod, g_post, w_sg, w_su, w_sd, n_batch, seq):
    n, d = xs.shape
    tm = TOKEN_TILE
    row = lambda i: (i, 0)
    const = lambda i: (0, 0)
    if isinstance(routed, tuple):
        n_a, routed_specs = _split_rows(*routed, tm * 2 * HP_CHUNKS, LANES)
    else:
        n_a, routed_specs, routed = 0, [pl.BlockSpec((tm * 2 * HP_CHUNKS, LANES), row)], (routed,)
    return pl.pallas_call(
        functools.partial(_moe_final_kernel, seq // tm, n_batch, n_a),
        out_shape=jax.ShapeDtypeStruct((n, d), F32),
        grid=(n // tm,),
        in_specs=routed_specs + [pl.BlockSpec((tm * HP_CHUNKS, LANES), row),
                  pl.BlockSpec((tm, d), row),
                  pl.BlockSpec(mod.shape, const), pl.BlockSpec((1, d), const),
                  pl.BlockSpec(w_sg.shape, const), pl.BlockSpec(w_su.shape, const), pl.BlockSpec(w_sd.shape, const)],
        out_specs=pl.BlockSpec((tm, d), row),
        compiler_params=_cparams("parallel"),
        name="moe_final",
    )(*routed, hp, xs, mod, g_post, w_sg, w_su, w_sd)


def _moe_layer(scores_t, hp, xs, mod, router_bias, g_post, expert_w, shared_w, layer, n_lat, n_batch, seq):
    pick = lambda rows: next(t for t in MOE_TILES if rows % t == 0)
    routed = _moe_segment(scores_t, hp, router_bias, expert_w, layer, 0, n_lat, pick(n_lat))
    n_ctx = hp.shape[0] // HP_CHUNKS - n_lat
    if n_ctx:
        tt = next(t for t in MOE_TILES if n_ctx % t == 0 and n_lat % t == 0)
        routed = (routed, _moe_segment(scores_t, hp, router_bias, expert_w, layer, n_lat, n_ctx, tt))
    return _moe_final(routed, hp, xs, mod, g_post, *shared_w, n_batch, seq)


def kernel(x, c, ctx, c_ctx, w_mod, b_mod, g_pre_mix, g_post_mix, g_pre_ffn, g_post_ffn, w_in, w_out,
           ret_decay_logit, gla_w_gate, gla_b_gate, gla_norm_g, diff_lambda, diff_norm_g,
           w_router, router_bias, w_exp_gate, w_exp_up, w_exp_down, w_sh_gate, w_sh_up, w_sh_down):
    n_batch, seq, d = x.shape
    ctx_len = ctx.shape[1]
    depth = w_mod.shape[0]
    n_lat = n_batch * seq
    assert seq % TOKEN_TILE == 0 and (n_batch * ctx_len) % TOKEN_TILE == 0 and n_batch < SUBLANES
    scan_rows = CHUNK * SCAN_CHUNKS
    assert seq % ATTN_TK == 0 and seq % scan_rows == 0 and ctx_len % scan_rows == 0 and n_lat % ctx_len == 0

    xs = (x.reshape(n_lat, d), ctx.reshape(n_batch * ctx_len, d))
    cond = jnp.zeros((SUBLANES, d), F32).at[:n_batch].set(c).at[n_batch].set(c_ctx)
    mods = _modulation(cond, w_mod, b_mod)
    tables = _rope_tables(seq, RET_DK, TOKEN_TILE) + _rope_tables(seq, DIFF_DH, TOKEN_TILE)
    lr0 = N_HEADS * (2 * RET_DK + 2 * RET_DV + 2 * GLA_DK + 2 * GLA_DV)
    row = lambda a: a[None, :]
    expert_w = tuple(w.astype(BF16).reshape((depth * N_EXPERTS,) + w.shape[2:])
                     for w in (w_exp_gate, w_exp_up, w_exp_down))

    for layer in range(depth):
        need_ctx = layer < depth - 1
        lam_init = 0.8 - 0.6 * math.exp(-0.3 * layer)
        mod = mods[layer]
        wl = w_in[layer]
        w_r = jnp.concatenate([wl[:, :lr0], wl[:, lr0:lr0 + GLA_LOWRANK],
                               jnp.zeros((d, LANES - GLA_LOWRANK), F32), wl[:, lr0 + GLA_LOWRANK:]],
                              axis=1).astype(BF16)
        ret, gla, glr, dqt, dk, vt = _in_projection(xs, mod, row(g_pre_mix[layer]), w_r, tables, n_batch, seq)

        o_rf, o_gf, o_rb, o_gb = _scan(ret, gla, glr, _scan_tables(ret_decay_logit[layer]), gla_w_gate[layer],
                                       gla_b_gate[layer], n_batch, seq, ctx_len)

        lp = jnp.zeros((SUBLANES, LANES), F32).at[:4, :DIFF_DH].set(diff_lambda[layer])
        g_col = jnp.broadcast_to(diff_norm_g[layer][:, None], (LANES, LANES))
        attn = functools.partial(_diff_attention, dqt, dk, vt, lp, g_col, lam_init, n_batch, seq, ctx_len)
        dif = attn(True)
        n_rows = n_lat
        if need_ctx:
            n_rows = n_lat + n_batch * ctx_len
            if isinstance(xs, tuple):
                dif = (dif, attn(False))
            else:
                dif = jnp.concatenate([dif, attn(False)], axis=0)
        elif isinstance(xs, tuple):
            xs = xs[0]

        xs, hp, scores_t = _merge(o_rf, o_rb, o_gf, o_gb, ret, gla, dif, xs, mod, row(g_post_mix[layer]),
                                  row(g_pre_ffn[layer]), row(jnp.tile(gla_norm_g[layer], N_HEADS)),
                                  w_out[layer].astype(BF16), w_router[layer].T, n_rows, n_batch, seq)
        shared_w = tuple(w[layer].astype(BF16) for w in (w_sh_gate, w_sh_up, w_sh_down))
        xs = _moe_layer(scores_t, hp, xs, mod, router_bias[layer], row(g_post_ffn[layer]), expert_w, shared_w,
                        layer, n_lat, n_batch, seq)
    return xs[:n_lat].reshape(n_batch, seq, d)
```
